```python
import math
import jax, jax.numpy as jnp
from jax import lax
import numpy as np


D_MODEL = 1024
BATCH = 4
SEQ = 4096
DEPTH = 4

HEAD_DIM = 64
N_HEADS_SB = 8
N_HEADS_FOX = 8
N_HEADS_MOBA = 8
BRANCH_WIDTH = 8 * HEAD_DIM
N_BRANCH = 3
D_FF = 4 * D_MODEL
Q_BLOCK = 128
MOBA_BLOCK = 256
MOBA_TOPK = 3
MOBA_Q_CHUNK = 32
RMS_EPS = 1e-6
NEG_INF = -1e30
FORGET_BIAS_INIT = 2.0
IN_COLS = 9 * BRANCH_WIDTH + N_HEADS_FOX + N_BRANCH * D_MODEL

kernel_name = "hybrid_stickbreak_fox_moba_gated"


def rmsnorm(x, g):
    xf = x.astype(jnp.float32)
    y = xf * lax.rsqrt(jnp.mean(xf * xf, axis=-1, keepdims=True) + RMS_EPS)
    return (y * g.astype(jnp.float32)).astype(x.dtype)


def split_in_proj(proj):
    sizes = [BRANCH_WIDTH] * 6 + [N_HEADS_FOX] + [BRANCH_WIDTH] * 3 + [N_BRANCH * D_MODEL]
    points = [int(p) for p in np.cumsum(sizes)[:-1]]
    return jnp.split(proj, points, axis=-1)


def to_heads(t, n_heads):
    B, S, _ = t.shape
    return t.reshape(B, S, n_heads, HEAD_DIM).transpose(0, 2, 1, 3)


def merge_heads(t):
    B, H, S, d = t.shape
    return t.transpose(0, 2, 1, 3).reshape(B, S, H * d)


def alibi_slopes(n):
    return 2.0 ** (-8.0 * jnp.arange(1, n + 1, dtype=jnp.float32) / n)


def stick_breaking_attention(q, k, v):
    B, H, S, d = q.shape
    scale = 1.0 / math.sqrt(d)
    kf = k.astype(jnp.float32)
    vf = v.astype(jnp.float32)
    key_pos = jnp.arange(S)

    def block(i):
        t0 = i * Q_BLOCK
        qb = lax.dynamic_slice_in_dim(q, t0, Q_BLOCK, axis=2).astype(jnp.float32)
        z = jnp.einsum('bhqd,bhkd->bhqk', qb, kf) * scale
        q_pos = t0 + jnp.arange(Q_BLOCK)
        past = key_pos[None, :] < q_pos[:, None]
        log_1m = jnp.where(past, -jax.nn.softplus(z), 0.0)
        tail = lax.cumsum(log_1m, axis=3, reverse=True) - log_1m
        w = jnp.where(past, jnp.exp(jax.nn.log_sigmoid(z) + tail), 0.0)
        return jnp.einsum('bhqk,bhkd->bhqd', w, vf)

    out = lax.map(block, jnp.arange(S // Q_BLOCK))
    out = out.transpose(1, 2, 0, 3, 4).reshape(B, H, S, d)
    return out.astype(q.dtype)


def forgetting_attention(q, k, v, f_logit):
    B, H, S, d = q.shape
    scale = 1.0 / math.sqrt(d)
    kf = k.astype(jnp.float32)
    vf = v.astype(jnp.float32)
    c = lax.cumsum(jax.nn.log_sigmoid(f_logit.astype(jnp.float32)), axis=2)
    key_pos = jnp.arange(S)

    def block(i):
        t0 = i * Q_BLOCK
        qb = lax.dynamic_slice_in_dim(q, t0, Q_BLOCK, axis=2).astype(jnp.float32)
        cq = lax.dynamic_slice_in_dim(c, t0, Q_BLOCK, axis=2)
        logits = jnp.einsum('bhqd,bhkd->bhqk', qb, kf) * scale + (cq[..., :, None] - c[..., None, :])
        q_pos = t0 + jnp.arange(Q_BLOCK)
        causal = key_pos[None, :] <= q_pos[:, None]
        p = jax.nn.softmax(jnp.where(causal, logits, NEG_INF), axis=-1)
        return jnp.einsum('bhqk,bhkd->bhqd', p, vf)

    out = lax.map(block, jnp.arange(S // Q_BLOCK))
    out = out.transpose(1, 2, 0, 3, 4).reshape(B, H, S, d)
    return out.astype(q.dtype)


def moba_attention(q, k, v, slopes):
    B, H, S, d = q.shape
    scale = 1.0 / math.sqrt(d)
    n_kb = -(-S // MOBA_BLOCK)
    n_top = min(MOBA_TOPK, n_kb)
    pad = n_kb * MOBA_BLOCK - S
    kf = jnp.pad(k.astype(jnp.float32), ((0, 0), (0, 0), (0, pad), (0, 0)))
    vf = jnp.pad(v.astype(jnp.float32), ((0, 0), (0, 0), (0, pad), (0, 0)))
    k_blocks = kf.reshape(B, H, n_kb, MOBA_BLOCK, d)
    v_blocks = vf.reshape(B, H, n_kb, MOBA_BLOCK, d)
    k_mean = jnp.mean(k_blocks, axis=3)
    b_idx = jnp.arange(B)[:, None, None, None]
    h_idx = jnp.arange(H)[None, :, None, None]
    blk_ids = jnp.arange(n_kb)
    offs = jnp.arange(MOBA_BLOCK)
    slope = slopes[None, :, None, None]
    n_sel = n_top * MOBA_BLOCK

    def chunk(c):
        t0 = c * MOBA_Q_CHUNK
        own = t0 // MOBA_BLOCK
        qc = lax.dynamic_slice_in_dim(q, t0, MOBA_Q_CHUNK, axis=2).astype(jnp.float32)
        q_pos = t0 + jnp.arange(MOBA_Q_CHUNK)
        route = jnp.einsum('bhqd,bhnd->bhqn', qc, k_mean)
        route = jnp.where(blk_ids < own, route, NEG_INF)
        _, sel = lax.top_k(route, n_top)
        sel_valid = sel < own
        k_sel = k_blocks[b_idx, h_idx, sel]
        v_sel = v_blocks[b_idx, h_idx, sel]
        s_sel = jnp.einsum('bhqd,bhqrnd->bhqrn', qc, k_sel) * scale
        pos_sel = sel[..., None] * MOBA_BLOCK + offs
        s_sel = s_sel - slope[..., None] * (q_pos[:, None, None] - pos_sel)
        s_sel = jnp.where(sel_valid[..., None], s_sel, NEG_INF).reshape(B, H, MOBA_Q_CHUNK, n_sel)
        k_own = lax.dynamic_slice_in_dim(kf, own * MOBA_BLOCK, MOBA_BLOCK, axis=2)
        v_own = lax.dynamic_slice_in_dim(vf, own * MOBA_BLOCK, MOBA_BLOCK, axis=2)
        rel = q_pos[:, None] - (own * MOBA_BLOCK + offs)[None, :]
        s_own = jnp.einsum('bhqd,bhnd->bhqn', qc, k_own) * scale - slope * rel
        s_own = jnp.where(rel >= 0, s_own, NEG_INF)
        p = jax.nn.softmax(jnp.concatenate([s_sel, s_own], axis=-1), axis=-1)
        p_sel = p[..., :n_sel].reshape(B, H, MOBA_Q_CHUNK, n_top, MOBA_BLOCK)
        p_own = p[..., n_sel:]
        return (jnp.einsum('bhqrn,bhqrnd->bhqd', p_sel, v_sel)
                + jnp.einsum('bhqn,bhnd->bhqd', p_own, v_own))

    out = lax.map(chunk, jnp.arange(S // MOBA_Q_CHUNK))
    out = out.transpose(1, 2, 0, 3, 4).reshape(B, H, S, d)
    return out.astype(q.dtype)


def setup_inputs(seed: int = 0) -> dict:
    key = jax.random.key(seed)
    ks = jax.random.split(key, 11)
    f32 = jnp.float32
    x = jax.random.normal(ks[0], (BATCH, SEQ, D_MODEL), f32)
    norm_mix = 1.0 + 0.02 * jax.random.normal(ks[1], (DEPTH, D_MODEL), f32)
    w_in = jax.random.normal(ks[2], (DEPTH, D_MODEL, IN_COLS), f32) * D_MODEL ** -0.5
    b_forget = FORGET_BIAS_INIT + 0.5 * jax.random.normal(ks[3], (DEPTH, N_HEADS_FOX), f32)
    w_branch = jax.random.normal(ks[4], (DEPTH, N_BRANCH, BRANCH_WIDTH, D_MODEL), f32) * BRANCH_WIDTH ** -0.5
    w_out = jax.random.normal(ks[5], (DEPTH, D_MODEL, D_MODEL), f32) * D_MODEL ** -0.5
    norm_mlp = 1.0 + 0.02 * jax.random.normal(ks[6], (DEPTH, D_MODEL), f32)
    w_up = jax.random.normal(ks[7], (DEPTH, D_MODEL, D_FF), f32) * D_MODEL ** -0.5
    w_down = jax.random.normal(ks[8], (DEPTH, D_FF, D_MODEL), f32) * D_FF ** -0.5
    norm_final = 1.0 + 0.02 * jax.random.normal(ks[9], (D_MODEL,), f32)
    return {"x": x, "norm_mix": norm_mix, "w_in": w_in, "b_forget": b_forget,
            "w_branch": w_branch, "w_out": w_out, "norm_mlp": norm_mlp,
            "w_up": w_up, "w_down": w_down, "norm_final": norm_final}


def reference(x, norm_mix, w_in, b_forget, w_branch, w_out, norm_mlp, w_up, w_down, norm_final):
    B, S, _ = x.shape
    slopes = alibi_slopes(N_HEADS_MOBA)
    for l in range(DEPTH):
        h = rmsnorm(x, norm_mix[l])
        proj = jnp.einsum('bsd,de->bse', h, w_in[l])
        q_a, k_a, v_a, q_b, k_b, v_b, f_b, q_c, k_c, v_c, g = split_in_proj(proj)
        o_a = stick_breaking_attention(to_heads(q_a, N_HEADS_SB), to_heads(k_a, N_HEADS_SB),
                                       to_heads(v_a, N_HEADS_SB))
        f_logit = (f_b + b_forget[l]).transpose(0, 2, 1)
        o_b = forgetting_attention(to_heads(q_b, N_HEADS_FOX), to_heads(k_b, N_HEADS_FOX),
                                   to_heads(v_b, N_HEADS_FOX), f_logit)
        o_c = moba_attention(to_heads(q_c, N_HEADS_MOBA), to_heads(k_c, N_HEADS_MOBA),
                             to_heads(v_c, N_HEADS_MOBA), slopes)
        branches = jnp.stack([merge_heads(o_a), merge_heads(o_b), merge_heads(o_c)], axis=2)
        lifted = jnp.einsum('bsnw,nwd->bsnd', branches, w_branch[l])
        gates = jax.nn.sigmoid(g.reshape(B, S, N_BRANCH, D_MODEL))
        mixed = jnp.sum(gates * lifted, axis=2)
        x = x + jnp.einsum('bsd,de->bse', mixed, w_out[l])
        h = rmsnorm(x, norm_mlp[l])
        hid = jnp.square(jax.nn.relu(jnp.einsum('bsd,df->bsf', h, w_up[l])))
        x = x + jnp.einsum('bsf,fd->bsd', hid, w_down[l])
    return rmsnorm(x, norm_final)
```

```python
import functools

import jax
import jax.numpy as jnp
import numpy as np
from jax import lax
from jax.experimental import pallas as pl
from jax.experimental.pallas import tpu as pltpu

F32 = jnp.float32
BF16 = jnp.bfloat16

HEAD_DIM = 64
N_HEADS = 8
BRANCH_WIDTH = N_HEADS * HEAD_DIM
N_BRANCH = 3
MOBA_BLOCK = 256
MOBA_TOPK = 3
RMS_EPS = 1e-6
NEG_INF = -1e30
LANES = 128
HEADS_PER_BLOCK = LANES // HEAD_DIM
N_HEAD_BLOCKS = BRANCH_WIDTH // LANES
ATT_TILE = 256
SB_EXIT = 110.0
VMEM_LIMIT = 56 * 1024 * 1024


def _cparams(sem):
    return pltpu.CompilerParams(dimension_semantics=sem, vmem_limit_bytes=VMEM_LIMIT)


def _rms(x, g):
    ms = jnp.mean(x * x, axis=-1, keepdims=True)
    return x * lax.rsqrt(ms + RMS_EPS) * g


def _dot(a, b):
    return jnp.dot(a, b, preferred_element_type=F32)


def _dot_nt(a, b):
    return lax.dot_general(a, b, (((1,), (1,)), ((), ())), preferred_element_type=F32)


def _split2(x):
    hi = x.astype(BF16)
    lo = (x - hi.astype(F32)).astype(BF16)
    return hi, lo


def _split3(x):
    hi = x.astype(BF16)
    r = x - hi.astype(F32)
    mid = r.astype(BF16)
    lo = (r - mid.astype(F32)).astype(BF16)
    return hi, mid, lo


def _norm_matmul_kernel(x_ref, g_ref, w_ref, o_ref, h_ref):
    @pl.when(pl.program_id(1) == 0)
    def _():
        h_ref[...] = _rms(x_ref[...], g_ref[...]).astype(BF16)

    o_ref[...] = _dot(h_ref[...], w_ref[...]).astype(o_ref.dtype)


def _norm_matmul(x, g, w, tm, tn):
    n, d = x.shape
    cols = w.shape[1]
    return pl.pallas_call(
        _norm_matmul_kernel,
        grid=(n // tm, cols // tn),
        in_specs=[
            pl.BlockSpec((tm, d), lambda i, j: (i, 0)),
            pl.BlockSpec((1, d), lambda i, j: (0, 0)),
            pl.BlockSpec((d, tn), lambda i, j: (0, j)),
        ],
        out_specs=pl.BlockSpec((tm, tn), lambda i, j: (i, j)),
        out_shape=jax.ShapeDtypeStruct((n, cols), BF16),
        scratch_shapes=[pltpu.VMEM((tm, d), BF16)],
        compiler_params=_cparams(("parallel", "arbitrary")),
        name="norm_in_proj",
    )(x, g, w)


def _decay_kernel(x_ref, g_ref, wf_hi_ref, wf_lo_ref, b_ref, tri_ref, c_ref, carry_ref):
    @pl.when(pl.program_id(1) == 0)
    def _():
        carry_ref[...] = jnp.zeros_like(carry_ref)

    h_hi, h_lo = _split2(_rms(x_ref[0], g_ref[...]))
    f = (_dot_nt(wf_hi_ref[...], h_hi) + _dot_nt(wf_hi_ref[...], h_lo)
         + _dot_nt(wf_lo_ref[...], h_hi))
    y = f + b_ref[...]
    logf = jnp.minimum(y, 0.0) - jnp.log(1.0 + jnp.exp(-jnp.abs(y)))
    p0, p1, p2 = _split3(logf)
    tri = tri_ref[...]
    cum = _dot(p0, tri) + _dot(p1, tri) + _dot(p2, tri) + carry_ref[...]
    carry_ref[...] = cum[:, -1:]
    c_ref[0] = cum[:N_HEADS]


def _decay(x3, g, wf_hi, wf_lo, b_f, tri, tc):
    b, s, d = x3.shape
    rows = wf_hi.shape[0]
    return pl.pallas_call(
        _decay_kernel,
        grid=(b, s // tc),
        in_specs=[
            pl.BlockSpec((1, tc, d), lambda i, j: (i, j, 0)),
            pl.BlockSpec((1, d), lambda i, j: (0, 0)),
            pl.BlockSpec((rows, d), lambda i, j: (0, 0)),
            pl.BlockSpec((rows, d), lambda i, j: (0, 0)),
            pl.BlockSpec((rows, 1), lambda i, j: (0, 0)),
            pl.BlockSpec((tc, tc), lambda i, j: (0, 0)),
        ],
        out_specs=pl.BlockSpec((1, N_HEADS, tc), lambda i, j: (i, 0, j)),
        out_shape=jax.ShapeDtypeStruct((b, N_HEADS, s), F32),
        scratch_shapes=[pltpu.VMEM((rows, 1), F32)],
        compiler_params=_cparams(("parallel", "arbitrary")),
        name="fox_decay",
    )(x3, g, wf_hi, wf_lo, b_f, tri)


def _head_lane_mask(hh):
    lane = lax.broadcasted_iota(jnp.int32, (1, LANES), 1)
    return (lane >= hh * HEAD_DIM) & (lane < (hh + 1) * HEAD_DIM)


def _tile_rel(t):
    return (lax.broadcasted_iota(jnp.int32, (t, t), 0)
            - lax.broadcasted_iota(jnp.int32, (t, t), 1))


def _attn_specs(mixer, t, s, d):
    nq = s // t
    qoff = N_BRANCH * d // LANES + (3 * mixer) * N_HEAD_BLOCKS
    koff = qoff + N_HEAD_BLOCKS
    voff = koff + N_HEAD_BLOCKS
    return [
        pl.BlockSpec((t, LANES), lambda b, hp, qi: (b * nq + qi, qoff + hp)),
        pl.BlockSpec((s, LANES), lambda b, hp, qi: (b, koff + hp)),
        pl.BlockSpec((s, LANES), lambda b, hp, qi: (b, voff + hp)),
    ], pl.BlockSpec((t, LANES), lambda b, hp, qi: (b * nq + qi, hp))


def _sb_kernel(q_ref, k_ref, v_ref, o_ref, acc_ref, cs_ref):
    t = q_ref.shape[0]
    qi = pl.program_id(2)
    q2 = q_ref[...]
    rel = _tile_rel(t)
    tril = jnp.where(rel >= 0, 1.0, 0.0).astype(BF16)

    def tile(hh, qh, j, diag):
        start = pl.multiple_of(j * t, t)
        z = _dot_nt(qh, k_ref[pl.ds(start, t), :])
        sp = jnp.maximum(z, 0.0) + jnp.log(1.0 + jnp.exp(-jnp.abs(z)))
        if diag:
            sp = jnp.where(rel > 0, sp, 0.0)
        hi, lo = _split2(sp)
        r = _dot(hi, tril) + _dot(lo, tril) + cs_ref[hh]
        w = jnp.exp(z - r)
        if diag:
            w = jnp.where(rel > 0, w, 0.0)
        acc_ref[hh] += _dot(w.astype(BF16), v_ref[pl.ds(start, t), :])
        cs_ref[hh] = r[:, 0:1]

    for hh in range(HEADS_PER_BLOCK):
        qh = jnp.where(_head_lane_mask(hh), q2, jnp.zeros_like(q2))
        acc_ref[hh] = jnp.zeros((t, LANES), F32)
        cs_ref[hh] = jnp.zeros((t, 1), F32)
        tile(hh, qh, qi, True)

        def cond(j):
            return j >= 0

        def body(j, hh=hh, qh=qh):
            tile(hh, qh, j, False)
            done = jnp.min(cs_ref[hh]) > SB_EXIT
            return jnp.where(done, -1, j - 1)

        lax.while_loop(cond, body, qi - 1)

    o_ref[...] = jnp.where(_head_lane_mask(0), acc_ref[0], acc_ref[1]).astype(o_ref.dtype)


def _sb_attention(proj, b, s, d):
    t = ATT_TILE
    in_specs, out_spec = _attn_specs(0, t, s, d)
    return pl.pallas_call(
        _sb_kernel,
        grid=(b, N_HEAD_BLOCKS, s // t),
        in_specs=in_specs,
        out_specs=out_spec,
        out_shape=jax.ShapeDtypeStruct((b * s, BRANCH_WIDTH), BF16),
        scratch_shapes=[pltpu.VMEM((HEADS_PER_BLOCK, t, LANES), F32),
                        pltpu.VMEM((HEADS_PER_BLOCK, t, 1), F32)],
        compiler_params=_cparams(("parallel", "parallel", "arbitrary")),
        name="stickbreak_attn",
    )(proj, proj, proj)


def _online_softmax_step(s_tile, v_tile, m_ref, l_ref, acc_ref, hh, first):
    row_max = jnp.max(s_tile, axis=-1, keepdims=True)
    if first:
        m_new = row_max
        p = jnp.exp(s_tile - m_new)
        l_ref[hh] = jnp.sum(p, axis=-1, keepdims=True)
        acc_ref[hh] = _dot(p.astype(BF16), v_tile)
    else:
        m_old = m_ref[hh]
        m_new = jnp.maximum(m_old, row_max)
        alpha = jnp.exp(m_old - m_new)
        p = jnp.exp(s_tile - m_new)
        l_ref[hh] = alpha * l_ref[hh] + jnp.sum(p, axis=-1, keepdims=True)
        acc_ref[hh] = alpha * acc_ref[hh] + _dot(p.astype(BF16), v_tile)
    m_ref[hh] = m_new


def _fox_kernel(q_ref, k_ref, v_ref, c_ref, o_ref, acc_ref, m_ref, l_ref):
    t = q_ref.shape[0]
    hp = pl.program_id(1)
    qi = pl.program_id(2)
    q2 = q_ref[...]
    rel = _tile_rel(t)
    q_start = pl.multiple_of(qi * t, t)

    for hh in range(HEADS_PER_BLOCK):
        h = hp * HEADS_PER_BLOCK + hh
        qh = jnp.where(_head_lane_mask(hh), q2, jnp.zeros_like(q2))
        c_q0 = c_ref[0, pl.ds(h, 1), pl.ds(q_start, t)][:, 0:1]

        def tile(j, diag, hh=hh, h=h, qh=qh, c_q0=c_q0):
            start = pl.multiple_of(j * t, t)
            z = _dot_nt(qh, k_ref[pl.ds(start, t), :])
            s_tile = z + (c_q0 - c_ref[0, pl.ds(h, 1), pl.ds(start, t)])
            if diag:
                s_tile = jnp.where(rel >= 0, s_tile, NEG_INF)
            _online_softmax_step(s_tile, v_ref[pl.ds(start, t), :], m_ref, l_ref, acc_ref, hh, diag)

        tile(qi, True)

        def body(j, carry, tile=tile):
            tile(j, False)
            return carry

        lax.fori_loop(0, qi, body, 0)

    out = jnp.where(_head_lane_mask(0), acc_ref[0] / l_ref[0], acc_ref[1] / l_ref[1])
    o_ref[...] = out.astype(o_ref.dtype)


def _softmax_scratch(t):
    return [pltpu.VMEM((HEADS_PER_BLOCK, t, LANES), F32),
            pltpu.VMEM((HEADS_PER_BLOCK, t, 1), F32),
            pltpu.VMEM((HEADS_PER_BLOCK, t, 1), F32)]


def _fox_attention(proj, c, b, s, d):
    t = ATT_TILE
    in_specs, out_spec = _attn_specs(1, t, s, d)
    in_specs.append(pl.BlockSpec((1, N_HEADS, s), lambda bi, hp, qi: (bi, 0, 0)))
    return pl.pallas_call(
        _fox_kernel,
        grid=(b, N_HEAD_BLOCKS, s // t),
        in_specs=in_specs,
        out_specs=out_spec,
        out_shape=jax.ShapeDtypeStruct((b * s, BRANCH_WIDTH), BF16),
        scratch_shapes=_softmax_scratch(t),
        compiler_params=_cparams(("parallel", "parallel", "arbitrary")),
        name="forgetting_attn",
    )(proj, proj, proj, c)


def _moba_kernel(q_ref, k_ref, v_ref, o_ref, acc_ref, m_ref, l_ref, kmean_ref):
    t = q_ref.shape[0]
    n_blocks = k_ref.shape[0] // t
    hp = pl.program_id(1)
    qi = pl.program_id(2)
    q2 = q_ref[...]
    rel = _tile_rel(t)

    @pl.when(qi == 0)
    def _():
        kb = k_ref[...].astype(F32).reshape(n_blocks, t, LANES)
        kmean_ref[...] = jnp.sum(kb, axis=1) * (1.0 / t)

    km_hi, km_mid, km_lo = _split3(kmean_ref[...])
    blk = lax.broadcasted_iota(jnp.int32, (1, n_blocks), 1)
    key_off = lax.broadcasted_iota(jnp.int32, (1, t), 1).astype(F32)

    for hh in range(HEADS_PER_BLOCK):
        h = hp * HEADS_PER_BLOCK + hh
        qh = jnp.where(_head_lane_mask(hh), q2, jnp.zeros_like(q2))
        slope = pltpu.bitcast(jnp.full((1, t), (126 - h) << 23, jnp.int32), F32)

        route = _dot_nt(qh, km_hi) + _dot_nt(qh, km_mid) + _dot_nt(qh, km_lo)
        rank = jnp.zeros((t, n_blocks), F32)
        for n in range(n_blocks):
            col = route[:, n:n + 1]
            beats = (col > route) | ((col == route) & (blk > n))
            rank = rank + jnp.where(beats, jnp.where(qi > n, 1.0, 0.0), 0.0)
        sel = jnp.where((rank < MOBA_TOPK) & (blk < qi), 1.0, 0.0)

        def tile(j, diag, hh=hh, qh=qh, slope=slope, sel=sel):
            start = pl.multiple_of(j * t, t)
            z = _dot_nt(qh, k_ref[pl.ds(start, t), :])
            s_tile = z + slope * (key_off + ((j - qi) * t).astype(F32))
            v_tile = v_ref[pl.ds(start, t), :]
            if diag:
                s_tile = jnp.where(rel >= 0, s_tile, NEG_INF)
                _online_softmax_step(s_tile, v_tile, m_ref, l_ref, acc_ref, hh, True)
            else:
                picked = jnp.sum(jnp.where(blk == j, sel, 0.0), axis=-1, keepdims=True) > 0.0
                m_old = m_ref[hh]
                row_max = jnp.max(s_tile, axis=-1, keepdims=True)
                m_new = jnp.where(picked, jnp.maximum(m_old, row_max), m_old)
                alpha = jnp.exp(m_old - m_new)
                p = jnp.exp(s_tile - jnp.where(picked, m_new, -NEG_INF))
                l_ref[hh] = alpha * l_ref[hh] + jnp.sum(p, axis=-1, keepdims=True)
                acc_ref[hh] = alpha * acc_ref[hh] + _dot(p.astype(BF16), v_tile)
                m_ref[hh] = m_new

        tile(qi, True)

        def body(j, carry, tile=tile):
            tile(j, False)
            return carry

        lax.fori_loop(0, qi, body, 0)

    out = jnp.where(_head_lane_mask(0), acc_ref[0] / l_ref[0], acc_ref[1] / l_ref[1])
    o_ref[...] = out.astype(o_ref.dtype)


def _moba_attention(proj, b, s, d):
    t = MOBA_BLOCK
    in_specs, out_spec = _attn_specs(2, t, s, d)
    return pl.pallas_call(
        _moba_kernel,
        grid=(b, N_HEAD_BLOCKS, s // t),
        in_specs=in_specs,
        out_specs=out_spec,
        out_shape=jax.ShapeDtypeStruct((b * s, BRANCH_WIDTH), BF16),
        scratch_shapes=_softmax_scratch(t) + [pltpu.VMEM((s // t, LANES), F32)],
        compiler_params=_cparams(("parallel", "parallel", "arbitrary")),
        name="moba_attn",
    )(proj, proj, proj)


def _mix_kernel(x_ref, oa_ref, ob_ref, oc_ref, g_ref, wb_ref, wo_ref, y_ref):
    d = x_ref.shape[1]
    mixed = None
    for n, o_ref in enumerate((oa_ref, ob_ref, oc_ref)):
        gate = jax.nn.sigmoid(g_ref[:, n * d:(n + 1) * d].astype(F32))
        term = gate * _dot(o_ref[...], wb_ref[n])
        mixed = term if mixed is None else mixed + term
    y_ref[...] = x_ref[...] + _dot(mixed.astype(BF16), wo_ref[...])


def _mix(x, o_a, o_b, o_c, proj, wb, wo, tm):
    n, d = x.shape
    o_spec = pl.BlockSpec((tm, BRANCH_WIDTH), lambda i: (i, 0))
    return pl.pallas_call(
        _mix_kernel,
        grid=(n // tm,),
        in_specs=[
            pl.BlockSpec((tm, d), lambda i: (i, 0)),
            o_spec, o_spec, o_spec,
            pl.BlockSpec((tm, N_BRANCH * d), lambda i: (i, 0)),
            pl.BlockSpec((N_BRANCH, BRANCH_WIDTH, d), lambda i: (0, 0, 0)),
            pl.BlockSpec((d, d), lambda i: (0, 0)),
        ],
        out_specs=pl.BlockSpec((tm, d), lambda i: (i, 0)),
        out_shape=jax.ShapeDtypeStruct((n, d), F32),
        compiler_params=_cparams(("parallel",)),
        name="gated_mix_out_proj",
    )(x, o_a, o_b, o_c, proj, wb, wo)


def _mlp_kernel(x_ref, g_ref, wu_ref, wd_ref, gf_ref, y_ref, h_ref, acc_ref, *, final_norm):
    f = pl.program_id(1)

    @pl.when(f == 0)
    def _():
        h_ref[...] = _rms(x_ref[...], g_ref[...]).astype(BF16)
        acc_ref[...] = jnp.zeros_like(acc_ref)

    hid = jnp.square(jnp.maximum(_dot(h_ref[...], wu_ref[...]), 0.0))
    acc_ref[...] += _dot(hid.astype(BF16), wd_ref[...])

    @pl.when(f == pl.num_programs(1) - 1)
    def _():
        y = x_ref[...] + acc_ref[...]
        if final_norm:
            y = _rms(y, gf_ref[...])
        y_ref[...] = y


def _mlp(x, g, wu, wd, g_final, final_norm, tm, tf):
    n, d = x.shape
    d_ff = wu.shape[1]
    return pl.pallas_call(
        functools.partial(_mlp_kernel, final_norm=final_norm),
        grid=(n // tm, d_ff // tf),
        in_specs=[
            pl.BlockSpec((tm, d), lambda i, f: (i, 0)),
            pl.BlockSpec((1, d), lambda i, f: (0, 0)),
            pl.BlockSpec((d, tf), lambda i, f: (0, f)),
            pl.BlockSpec((tf, d), lambda i, f: (f, 0)),
            pl.BlockSpec((1, d), lambda i, f: (0, 0)),
        ],
        out_specs=pl.BlockSpec((tm, d), lambda i, f: (i, 0)),
        out_shape=jax.ShapeDtypeStruct((n, d), F32),
        scratch_shapes=[pltpu.VMEM((tm, d), BF16), pltpu.VMEM((tm, d), F32)],
        compiler_params=_cparams(("parallel", "arbitrary")),
        name="relu2_mlp",
    )(x, g, wu, wd, g_final)


def _prep_in_proj(w_in):
    bw = BRANCH_WIDTH
    scale = HEAD_DIM ** -0.5
    f_lo = 6 * bw
    f_hi = f_lo + N_HEADS
    g_lo = 9 * bw + N_HEADS
    pieces = [w_in[:, :, g_lo:]]
    for m in range(N_BRANCH):
        base = 3 * m * bw + (N_HEADS if m == 2 else 0)
        pieces += [w_in[:, :, base:base + bw] * scale,
                   w_in[:, :, base + bw:base + 3 * bw]]
    w_main = jnp.concatenate(pieces, axis=-1).astype(BF16)
    w_f = jnp.swapaxes(w_in[:, :, f_lo:f_hi], 1, 2)
    w_f = jnp.pad(w_f, ((0, 0), (0, 16 - N_HEADS), (0, 0)))
    wf_hi = w_f.astype(BF16)
    wf_lo = (w_f - wf_hi.astype(F32)).astype(BF16)
    return w_main, wf_hi, wf_lo


def kernel(x, norm_mix, w_in, b_forget, w_branch, w_out, norm_mlp, w_up, w_down, norm_final):
    b, s, d = x.shape
    depth = w_in.shape[0]
    n = b * s
    assert s % ATT_TILE == 0 and d % LANES == 0

    w_main, wf_hi, wf_lo = _prep_in_proj(w_in)
    wb = w_branch.astype(BF16)
    wo = w_out.astype(BF16)
    wu = w_up.astype(BF16)
    wd = w_down.astype(BF16)
    b_f = jnp.pad(b_forget, ((0, 0), (0, 16 - N_HEADS)))[:, :, None]
    tc = min(512, s)
    tri = jnp.asarray(np.triu(np.ones((tc, tc), np.float32)), BF16)

    tm = min(1024, n)
    xf = x.reshape(n, d)
    for l in range(depth):
        g_mix = norm_mix[l][None, :]
        proj = _norm_matmul(xf, g_mix, w_main[l], tm, 1536)
        c = _decay(xf.reshape(b, s, d), g_mix, wf_hi[l], wf_lo[l], b_f[l], tri, tc)
        o_a = _sb_attention(proj, b, s, d)
        o_b = _fox_attention(proj, c, b, s, d)
        o_c = _moba_attention(proj, b, s, d)
        xf = _mix(xf, o_a, o_b, o_c, proj, wb[l], wo[l], min(512, n))
        xf = _mlp(xf, norm_mlp[l][None, :], wu[l], wd[l], norm_final[None, :],
                  l == depth - 1, min(512, n), 1024)
    return xf.reshape(b, s, d)
```

```python
import functools
import math

import jax
import jax.numpy as jnp
import numpy as np
from jax import lax
from jax.experimental import pallas as pl
from jax.experimental.pallas import tpu as pltpu

F32 = jnp.float32
BF16 = jnp.bfloat16

HEAD_DIM = 64
N_HEADS = 8
BRANCH_WIDTH = N_HEADS * HEAD_DIM
N_BRANCH = 3
MOBA_BLOCK = 256
MOBA_TOPK = 3
RMS_EPS = 1e-6
NEG_INF = -1e30
LOG2E = math.log2(math.e)
LANES = 128
HEADS_PER_BLOCK = LANES // HEAD_DIM
N_HEAD_BLOCKS = BRANCH_WIDTH // LANES
N_BIAS_PIECES = 3
ATT_TILE = 256
SB_EXIT = 110.0
VMEM_LIMIT = 56 * 1024 * 1024


def _cparams(sem):
    return pltpu.CompilerParams(dimension_semantics=sem, vmem_limit_bytes=VMEM_LIMIT)


def _rms(x, g):
    ms = jnp.mean(x * x, axis=-1, keepdims=True)
    return x * lax.rsqrt(ms + RMS_EPS) * g


def _dot(a, b):
    return jnp.dot(a, b, preferred_element_type=F32)


def _dot_nt(a, b):
    return lax.dot_general(a, b, (((1,), (1,)), ((), ())), preferred_element_type=F32)


def _split2(x):
    hi = x.astype(BF16)
    lo = (x - hi.astype(F32)).astype(BF16)
    return hi, lo


def _split3(x):
    hi = x.astype(BF16)
    r = x - hi.astype(F32)
    mid = r.astype(BF16)
    lo = (r - mid.astype(F32)).astype(BF16)
    return hi, mid, lo


def _bias_lane(hh):
    return (1 - hh) * HEAD_DIM


def _norm_matmul_kernel(x_ref, g_ref, w_ref, o_ref, h_ref):
    @pl.when(pl.program_id(1) == 0)
    def _():
        h_ref[...] = _rms(x_ref[...], g_ref[...]).astype(BF16)

    o_ref[...] = _dot(h_ref[...], w_ref[...]).astype(o_ref.dtype)


def _norm_matmul(x, g, w, tm, tn):
    n, d = x.shape
    cols = w.shape[1]
    return pl.pallas_call(
        _norm_matmul_kernel,
        grid=(n // tm, cols // tn),
        in_specs=[
            pl.BlockSpec((tm, d), lambda i, j: (i, 0)),
            pl.BlockSpec((1, d), lambda i, j: (0, 0)),
            pl.BlockSpec((d, tn), lambda i, j: (0, j)),
        ],
        out_specs=pl.BlockSpec((tm, tn), lambda i, j: (i, j)),
        out_shape=jax.ShapeDtypeStruct((n, cols), BF16),
        scratch_shapes=[pltpu.VMEM((tm, d), BF16)],
        compiler_params=_cparams(("parallel", "arbitrary")),
        name="norm_in_proj",
    )(x, g, w)


def _decay_kernel(x_ref, g_ref, wf_hi_ref, wf_lo_ref, b_ref, tri_ref, place_ref, e_ref, carry_ref):
    @pl.when(pl.program_id(1) == 0)
    def _():
        carry_ref[...] = jnp.zeros_like(carry_ref)

    h_hi, h_lo = _split2(_rms(x_ref[0], g_ref[...]))
    f = _dot(h_hi, wf_hi_ref[...]) + _dot(h_lo, wf_hi_ref[...]) + _dot(h_hi, wf_lo_ref[...])
    y = f + b_ref[...]
    logf = jnp.minimum(y, 0.0) - jnp.log(1.0 + jnp.exp(-jnp.abs(y)))
    tri = tri_ref[...]
    cum = carry_ref[...]
    for piece in _split3(logf):
        cum = cum + _dot(tri, piece)
    carry_ref[...] = cum[-1:, :]
    e = None
    for i, piece in enumerate(_split3(cum * (-LOG2E))):
        term = _dot(piece, place_ref[i])
        e = term if e is None else e + term
    e_ref[0] = e.astype(BF16)


def _bias_placement():
    place = np.zeros((N_BIAS_PIECES, LANES, N_HEAD_BLOCKS * LANES), np.float32)
    for h in range(N_HEADS):
        hp, hh = divmod(h, HEADS_PER_BLOCK)
        for i in range(N_BIAS_PIECES):
            place[i, h, hp * LANES + _bias_lane(hh) + i] = 1.0
    return jnp.asarray(place, BF16)


def _decay(x3, g, wf_hi, wf_lo, b_f, tc):
    b, s, d = x3.shape
    tri = jnp.asarray(np.tril(np.ones((tc, tc), np.float32)), BF16)
    place = _bias_placement()
    cols = place.shape[2]
    const2 = lambda i, j: (0, 0)
    return pl.pallas_call(
        _decay_kernel,
        grid=(b, s // tc),
        in_specs=[
            pl.BlockSpec((1, tc, d), lambda i, j: (i, j, 0)),
            pl.BlockSpec((1, d), const2),
            pl.BlockSpec((d, LANES), const2),
            pl.BlockSpec((d, LANES), const2),
            pl.BlockSpec((1, LANES), const2),
            pl.BlockSpec((tc, tc), const2),
            pl.BlockSpec((N_BIAS_PIECES, LANES, cols), lambda i, j: (0, 0, 0)),
        ],
        out_specs=pl.BlockSpec((1, tc, cols), lambda i, j: (i, j, 0)),
        out_shape=jax.ShapeDtypeStruct((b, s, cols), BF16),
        scratch_shapes=[pltpu.VMEM((1, LANES), F32)],
        compiler_params=_cparams(("parallel", "arbitrary")),
        name="fox_decay",
    )(x3, g, wf_hi, wf_lo, b_f, tri, place)


def _head_lane_mask(hh):
    lane = lax.broadcasted_iota(jnp.int32, (1, LANES), 1)
    return (lane >= hh * HEAD_DIM) & (lane < (hh + 1) * HEAD_DIM)


def _tile_rel(t):
    return (lax.broadcasted_iota(jnp.int32, (t, t), 0)
            - lax.broadcasted_iota(jnp.int32, (t, t), 1))


def _attn_specs(mixer, t, s, d):
    nq = s // t
    qoff = N_BRANCH * d // LANES + (3 * mixer) * N_HEAD_BLOCKS
    koff = qoff + N_HEAD_BLOCKS
    voff = koff + N_HEAD_BLOCKS
    return [
        pl.BlockSpec((t, LANES), lambda b, hp, qi: (b * nq + qi, qoff + hp)),
        pl.BlockSpec((s, LANES), lambda b, hp, qi: (b, koff + hp)),
        pl.BlockSpec((s, LANES), lambda b, hp, qi: (b, voff + hp)),
    ], pl.BlockSpec((t, LANES), lambda b, hp, qi: (b * nq + qi, hp))


def _sb_kernel(q_ref, k_ref, v_ref, o_ref, acc_ref, cs_ref):
    t = q_ref.shape[0]
    qi = pl.program_id(2)
    q2 = q_ref[...]
    rel = _tile_rel(t)
    tril = jnp.where(rel >= 0, 1.0, 0.0).astype(BF16)

    def tile(hh, qh, j, diag):
        start = pl.multiple_of(j * t, t)
        z = _dot_nt(qh, k_ref[pl.ds(start, t), :])
        sp = jnp.maximum(z, 0.0) + jnp.log(1.0 + jnp.exp(-jnp.abs(z)))
        if diag:
            sp = jnp.where(rel > 0, sp, 0.0)
        hi, lo = _split2(sp)
        r = _dot(hi, tril) + _dot(lo, tril) + cs_ref[hh]
        w = jnp.exp(z - r)
        if diag:
            w = jnp.where(rel > 0, w, 0.0)
        acc_ref[hh] += _dot(w.astype(BF16), v_ref[pl.ds(start, t), :])
        cs_ref[hh] = r[:, 0:1]

    for hh in range(HEADS_PER_BLOCK):
        qh = jnp.where(_head_lane_mask(hh), q2, jnp.zeros_like(q2))
        acc_ref[hh] = jnp.zeros((t, LANES), F32)
        cs_ref[hh] = jnp.zeros((t, 1), F32)
        tile(hh, qh, qi, True)

        def cond(j):
            return j >= 0

        def body(j, hh=hh, qh=qh):
            tile(hh, qh, j, False)
            done = jnp.min(cs_ref[hh]) > SB_EXIT
            return jnp.where(done, -1, j - 1)

        lax.while_loop(cond, body, qi - 1)

    o_ref[...] = jnp.where(_head_lane_mask(0), acc_ref[0], acc_ref[1]).astype(o_ref.dtype)


def _sb_attention(proj, b, s, d):
    t = ATT_TILE
    in_specs, out_spec = _attn_specs(0, t, s, d)
    return pl.pallas_call(
        _sb_kernel,
        grid=(b, N_HEAD_BLOCKS, s // t),
        in_specs=in_specs,
        out_specs=out_spec,
        out_shape=jax.ShapeDtypeStruct((b * s, BRANCH_WIDTH), BF16),
        scratch_shapes=[pltpu.VMEM((HEADS_PER_BLOCK, t, LANES), F32),
                        pltpu.VMEM((HEADS_PER_BLOCK, t, 1), F32)],
        compiler_params=_cparams(("parallel", "parallel", "arbitrary")),
        name="stickbreak_attn",
    )(proj, proj, proj)


def _flash_kernel(*refs, moba):
    if moba:
        (q_ref, k_ref, v_ref, o_ref, kaug_ref, vt_ref, acc_ref, m_ref, s_even_ref, s_odd_ref,
         kmean_ref, sel_ref) = refs
    else:
        (q_ref, k_ref, v_ref, e_ref, o_ref, kaug_ref, vt_ref, acc_ref, m_ref, s_even_ref,
         s_odd_ref) = refs
    t = q_ref.shape[0]
    n_blocks = k_ref.shape[0] // t
    hp = pl.program_id(1)
    qi = pl.program_id(2)
    lane = lax.broadcasted_iota(jnp.int32, (1, LANES), 1)
    sub = lax.broadcasted_iota(jnp.int32, (LANES, 1), 0)

    @pl.when(qi == 0)
    def _build():
        def chunk(c, carry):
            start = pl.multiple_of(c * t, t)
            rows = pl.ds(start, t)
            k2 = k_ref[rows, :]
            v2t = v_ref[rows, :].astype(F32).T
            if moba:
                kmean_ref[pl.ds(c, 1), :] = jnp.sum(k2.astype(F32), axis=0, keepdims=True) * (1.0 / t)
                pos = (lax.broadcasted_iota(jnp.int32, (t, LANES), 0) + start).astype(F32)
            for hh in range(HEADS_PER_BLOCK):
                e0 = _bias_lane(hh)
                if moba:
                    h = hp * HEADS_PER_BLOCK + hh
                    slope = pltpu.bitcast(jnp.full((1, LANES), (126 - h) << 23, jnp.int32), F32)
                    extra = jnp.zeros((t, LANES), F32)
                    for i, piece in enumerate(_split3(pos * slope * LOG2E)):
                        extra = jnp.where(lane == e0 + i, piece.astype(F32), extra)
                    extra = extra.astype(BF16)
                else:
                    extra = e_ref[rows, :]
                kaug_ref[hh, rows, :] = jnp.where(_head_lane_mask(hh), k2, extra)
                v_rows = (sub >= hh * HEAD_DIM) & (sub < (hh + 1) * HEAD_DIM)
                ones_row = jnp.where(sub == e0, 1.0, 0.0)
                vt_ref[hh, :, rows] = jnp.where(v_rows, v2t, ones_row).astype(BF16)
            return carry

        lax.fori_loop(0, n_blocks, chunk, 0)

    q2 = q_ref[...]
    q_aug = []
    for hh in range(HEADS_PER_BLOCK):
        e0 = _bias_lane(hh)
        ones = jnp.where((lane >= e0) & (lane < e0 + N_BIAS_PIECES), 1.0, 0.0).astype(BF16)
        q_aug.append(jnp.where(_head_lane_mask(hh), q2, jnp.broadcast_to(ones, q2.shape)))

    if moba:
        km = _split3(kmean_ref[...])
        blk = lax.broadcasted_iota(jnp.int32, (n_blocks, 1), 0)
        for hh in range(HEADS_PER_BLOCK):
            qh = jnp.where(_head_lane_mask(hh), q2, jnp.zeros_like(q2))
            route = _dot_nt(km[0], qh) + _dot_nt(km[1], qh) + _dot_nt(km[2], qh)
            rank = jnp.zeros((n_blocks, t), F32)
            for n in range(n_blocks):
                row = route[n:n + 1, :]
                beats = (row > route) | ((row == route) & (blk > n))
                rank = rank + jnp.where(beats, jnp.where(qi > n, 1.0, 0.0), 0.0)
            sel_ref[hh] = jnp.where((rank < MOBA_TOPK) & (blk < qi), 1.0, 0.0)

    valid = _tile_rel(t) <= 0

    def scores(s_ref, j):
        rows = pl.ds(pl.multiple_of(j * t, t), t)
        for hh in range(HEADS_PER_BLOCK):
            s_ref[hh] = _dot_nt(kaug_ref[hh, rows, :], q_aug[hh])

    def accumulate(s_ref, j, diag):
        rows = pl.ds(pl.multiple_of(j * t, t), t)
        for hh in range(HEADS_PER_BLOCK):
            s_t = s_ref[hh]
            if diag:
                s_t = jnp.where(valid, s_t, NEG_INF)
            m_old = m_ref[hh]
            m_new = jnp.maximum(m_old, jnp.max(s_t, axis=0, keepdims=True))
            m_sub = m_new
            if moba and not diag:
                picked = sel_ref[hh, pl.ds(j, 1), :] > 0.0
                m_new = jnp.where(picked, m_new, m_old)
                m_sub = jnp.where(picked, m_new, -NEG_INF)
            alpha = jnp.exp2(m_old - m_new)
            p = jnp.exp2(s_t - m_sub)
            acc_ref[hh] = alpha * acc_ref[hh] + _dot(vt_ref[hh, :, rows], p.astype(BF16))
            m_ref[hh] = m_new

    m_ref[...] = jnp.full(m_ref.shape, NEG_INF, F32)
    acc_ref[...] = jnp.zeros(acc_ref.shape, F32)
    scores(s_even_ref, 0)

    def body(j, carry):
        @pl.when(j % 2 == 0)
        def _():
            scores(s_odd_ref, j + 1)
            accumulate(s_even_ref, j, False)

        @pl.when(j % 2 == 1)
        def _():
            scores(s_even_ref, j + 1)
            accumulate(s_odd_ref, j, False)

        return carry

    lax.fori_loop(0, qi, body, 0)

    @pl.when(qi % 2 == 0)
    def _():
        accumulate(s_even_ref, qi, True)

    @pl.when(qi % 2 == 1)
    def _():
        accumulate(s_odd_ref, qi, True)

    acc0 = acc_ref[0]
    acc1 = acc_ref[1]
    inv0 = 1.0 / acc0[_bias_lane(0):_bias_lane(0) + 1, :]
    inv1 = 1.0 / acc1[_bias_lane(1):_bias_lane(1) + 1, :]
    out_t = jnp.where(sub < HEAD_DIM, acc0 * inv0, acc1 * inv1)
    o_ref[...] = out_t.T.astype(o_ref.dtype)


def _flash_attention(proj, key_bias, b, s, d, moba):
    t = ATT_TILE
    n_blocks = s // t
    in_specs, out_spec = _attn_specs(2 if moba else 1, t, s, d)
    operands = [proj, proj, proj]
    scratch = [pltpu.VMEM((HEADS_PER_BLOCK, s, LANES), BF16),
               pltpu.VMEM((HEADS_PER_BLOCK, LANES, s), BF16),
               pltpu.VMEM((HEADS_PER_BLOCK, LANES, t), F32),
               pltpu.VMEM((HEADS_PER_BLOCK, 1, t), F32),
               pltpu.VMEM((HEADS_PER_BLOCK, t, t), F32),
               pltpu.VMEM((HEADS_PER_BLOCK, t, t), F32)]
    if moba:
        scratch += [pltpu.VMEM((n_blocks, LANES), F32),
                    pltpu.VMEM((HEADS_PER_BLOCK, n_blocks, t), F32)]
    else:
        in_specs.append(pl.BlockSpec((s, LANES), lambda bi, hp, qi: (bi, hp)))
        operands.append(key_bias)
    return pl.pallas_call(
        functools.partial(_flash_kernel, moba=moba),
        grid=(b, N_HEAD_BLOCKS, n_blocks),
        in_specs=in_specs,
        out_specs=out_spec,
        out_shape=jax.ShapeDtypeStruct((b * s, BRANCH_WIDTH), BF16),
        scratch_shapes=scratch,
        compiler_params=_cparams(("parallel", "parallel", "arbitrary")),
        name="moba_attn" if moba else "forgetting_attn",
    )(*operands)


def _mix_kernel(x_ref, oa_ref, ob_ref, oc_ref, g_ref, wb_ref, wo_ref, y_ref):
    d = x_ref.shape[1]
    mixed = None
    for n, o_ref in enumerate((oa_ref, ob_ref, oc_ref)):
        gate = jax.nn.sigmoid(g_ref[:, n * d:(n + 1) * d].astype(F32))
        term = gate * _dot(o_ref[...], wb_ref[n])
        mixed = term if mixed is None else mixed + term
    y_ref[...] = x_ref[...] + _dot(mixed.astype(BF16), wo_ref[...])


def _mix(x, o_a, o_b, o_c, proj, wb, wo, tm):
    n, d = x.shape
    o_spec = pl.BlockSpec((tm, BRANCH_WIDTH), lambda i: (i, 0))
    return pl.pallas_call(
        _mix_kernel,
        grid=(n // tm,),
        in_specs=[
            pl.BlockSpec((tm, d), lambda i: (i, 0)),
            o_spec, o_spec, o_spec,
            pl.BlockSpec((tm, N_BRANCH * d), lambda i: (i, 0)),
            pl.BlockSpec((N_BRANCH, BRANCH_WIDTH, d), lambda i: (0, 0, 0)),
            pl.BlockSpec((d, d), lambda i: (0, 0)),
        ],
        out_specs=pl.BlockSpec((tm, d), lambda i: (i, 0)),
        out_shape=jax.ShapeDtypeStruct((n, d), F32),
        compiler_params=_cparams(("parallel",)),
        name="gated_mix_out_proj",
    )(x, o_a, o_b, o_c, proj, wb, wo)


def _mlp_kernel(x_ref, g_ref, wu_ref, wd_ref, gf_ref, y_ref, h_ref, acc_ref, *, final_norm):
    f = pl.program_id(1)

    @pl.when(f == 0)
    def _():
        h_ref[...] = _rms(x_ref[...], g_ref[...]).astype(BF16)
        acc_ref[...] = jnp.zeros_like(acc_ref)

    hid = jnp.square(jnp.maximum(_dot(h_ref[...], wu_ref[...]), 0.0))
    acc_ref[...] += _dot(hid.astype(BF16), wd_ref[...])

    @pl.when(f == pl.num_programs(1) - 1)
    def _():
        y = x_ref[...] + acc_ref[...]
        if final_norm:
            y = _rms(y, gf_ref[...])
        y_ref[...] = y


def _mlp(x, g, wu, wd, g_final, final_norm, tm, tf):
    n, d = x.shape
    d_ff = wu.shape[1]
    return pl.pallas_call(
        functools.partial(_mlp_kernel, final_norm=final_norm),
        grid=(n // tm, d_ff // tf),
        in_specs=[
            pl.BlockSpec((tm, d), lambda i, f: (i, 0)),
            pl.BlockSpec((1, d), lambda i, f: (0, 0)),
            pl.BlockSpec((d, tf), lambda i, f: (0, f)),
            pl.BlockSpec((tf, d), lambda i, f: (f, 0)),
            pl.BlockSpec((1, d), lambda i, f: (0, 0)),
        ],
        out_specs=pl.BlockSpec((tm, d), lambda i, f: (i, 0)),
        out_shape=jax.ShapeDtypeStruct((n, d), F32),
        scratch_shapes=[pltpu.VMEM((tm, d), BF16), pltpu.VMEM((tm, d), F32)],
        compiler_params=_cparams(("parallel", "arbitrary")),
        name="relu2_mlp",
    )(x, g, wu, wd, g_final)


def _prep_in_proj(w_in):
    bw = BRANCH_WIDTH
    scale = HEAD_DIM ** -0.5
    q_scales = (scale, scale * LOG2E, scale * LOG2E)
    f_lo = 6 * bw
    f_hi = f_lo + N_HEADS
    g_lo = 9 * bw + N_HEADS
    pieces = [w_in[:, :, g_lo:]]
    for m in range(N_BRANCH):
        base = 3 * m * bw + (N_HEADS if m == 2 else 0)
        pieces += [w_in[:, :, base:base + bw] * q_scales[m],
                   w_in[:, :, base + bw:base + 3 * bw]]
    w_main = jnp.concatenate(pieces, axis=-1).astype(BF16)
    w_f = jnp.pad(w_in[:, :, f_lo:f_hi], ((0, 0), (0, 0), (0, LANES - N_HEADS)))
    wf_hi = w_f.astype(BF16)
    wf_lo = (w_f - wf_hi.astype(F32)).astype(BF16)
    return w_main, wf_hi, wf_lo


def kernel(x, norm_mix, w_in, b_forget, w_branch, w_out, norm_mlp, w_up, w_down, norm_final):
    b, s, d = x.shape
    depth = w_in.shape[0]
    n = b * s
    assert s % ATT_TILE == 0 and d % LANES == 0

    w_main, wf_hi, wf_lo = _prep_in_proj(w_in)
    wb = w_branch.astype(BF16)
    wo = w_out.astype(BF16)
    wu = w_up.astype(BF16)
    wd = w_down.astype(BF16)
    b_f = jnp.pad(b_forget, ((0, 0), (0, LANES - N_HEADS)))[:, None, :]

    tm = min(1024, n)
    xf = x.reshape(n, d)
    for l in range(depth):
        g_mix = norm_mix[l][None, :]
        proj = _norm_matmul(xf, g_mix, w_main[l], tm, 1536)
        key_bias = _decay(xf.reshape(b, s, d), g_mix, wf_hi[l], wf_lo[l], b_f[l], min(512, s))
        o_a = _sb_attention(proj, b, s, d)
        o_b = _flash_attention(proj, key_bias.reshape(n, -1), b, s, d, False)
        o_c = _flash_attention(proj, None, b, s, d, True)
        xf = _mix(xf, o_a, o_b, o_c, proj, wb[l], wo[l], min(512, n))
        xf = _mlp(xf, norm_mlp[l][None, :], wu[l], wd[l], norm_final[None, :],
                  l == depth - 1, min(512, n), 1024)
    return xf.reshape(b, s, d)
```

```python
import functools
import math

import jax
import jax.numpy as jnp
import numpy as np
from jax import lax
from jax.experimental import pallas as pl
from jax.experimental.pallas import tpu as pltpu

F32 = jnp.float32
BF16 = jnp.bfloat16

HEAD_DIM = 64
N_HEADS = 8
BRANCH_WIDTH = N_HEADS * HEAD_DIM
N_BRANCH = 3
MOBA_BLOCK = 256
MOBA_TOPK = 3
RMS_EPS = 1e-6
NEG_INF = -1e30
LOG2E = math.log2(math.e)
LANES = 128
HEADS_PER_BLOCK = LANES // HEAD_DIM
N_HEAD_BLOCKS = BRANCH_WIDTH // LANES
N_BIAS_PIECES = 3
ATT_TILE = 256
SB_EXIT = 110.0
FLASH_SKIP = 160.0
FLASH_BATCH = 2
V_ROWS = HEAD_DIM + 16
VMEM_LIMIT = 56 * 1024 * 1024


def _cparams(sem):
    return pltpu.CompilerParams(dimension_semantics=sem, vmem_limit_bytes=VMEM_LIMIT)


def _rms(x, g):
    ms = jnp.mean(x * x, axis=-1, keepdims=True)
    return x * lax.rsqrt(ms + RMS_EPS) * g


def _dot(a, b):
    return jnp.dot(a, b, preferred_element_type=F32)


def _dot_nt(a, b):
    return lax.dot_general(a, b, (((1,), (1,)), ((), ())), preferred_element_type=F32)


def _split2(x):
    hi = x.astype(BF16)
    lo = (x - hi.astype(F32)).astype(BF16)
    return hi, lo


def _split3(x):
    hi = x.astype(BF16)
    r = x - hi.astype(F32)
    mid = r.astype(BF16)
    lo = (r - mid.astype(F32)).astype(BF16)
    return hi, mid, lo


def _bias_lane(hh):
    return (1 - hh) * HEAD_DIM


def _norm_matmul_kernel(x_ref, g_ref, w_ref, o_ref, h_ref):
    @pl.when(pl.program_id(1) == 0)
    def _():
        h_ref[...] = _rms(x_ref[...], g_ref[...]).astype(BF16)

    o_ref[...] = _dot(h_ref[...], w_ref[...]).astype(o_ref.dtype)


def _norm_matmul(x, g, w, tm, tn):
    n, d = x.shape
    cols = w.shape[1]
    return pl.pallas_call(
        _norm_matmul_kernel,
        grid=(n // tm, cols // tn),
        in_specs=[
            pl.BlockSpec((tm, d), lambda i, j: (i, 0)),
            pl.BlockSpec((1, d), lambda i, j: (0, 0)),
            pl.BlockSpec((d, tn), lambda i, j: (0, j)),
        ],
        out_specs=pl.BlockSpec((tm, tn), lambda i, j: (i, j)),
        out_shape=jax.ShapeDtypeStruct((n, cols), BF16),
        scratch_shapes=[pltpu.VMEM((tm, d), BF16)],
        compiler_params=_cparams(("parallel", "arbitrary")),
        name="norm_in_proj",
    )(x, g, w)


def _decay_kernel(x_ref, g_ref, wf_hi_ref, wf_lo_ref, b_ref, tri_ref, place_ref, e_ref, carry_ref):
    @pl.when(pl.program_id(1) == 0)
    def _():
        carry_ref[...] = jnp.zeros_like(carry_ref)

    h_hi, h_lo = _split2(_rms(x_ref[0], g_ref[...]))
    f = _dot(h_hi, wf_hi_ref[...]) + _dot(h_lo, wf_hi_ref[...]) + _dot(h_hi, wf_lo_ref[...])
    y = f + b_ref[...]
    logf = jnp.minimum(y, 0.0) - jnp.log(1.0 + jnp.exp(-jnp.abs(y)))
    tri = tri_ref[...]
    cum = carry_ref[...]
    for piece in _split3(logf):
        cum = cum + _dot(tri, piece)
    carry_ref[...] = cum[-1:, :]
    e = None
    for i, piece in enumerate(_split3(cum * (-LOG2E))):
        term = _dot(piece, place_ref[i])
        e = term if e is None else e + term
    e_ref[0] = e.astype(BF16)


def _bias_placement():
    place = np.zeros((N_BIAS_PIECES, LANES, N_HEAD_BLOCKS * LANES), np.float32)
    for h in range(N_HEADS):
        hp, hh = divmod(h, HEADS_PER_BLOCK)
        for i in range(N_BIAS_PIECES):
            place[i, h, hp * LANES + _bias_lane(hh) + i] = 1.0
    return jnp.asarray(place, BF16)


def _decay(x3, g, wf_hi, wf_lo, b_f, tc):
    b, s, d = x3.shape
    tri = jnp.asarray(np.tril(np.ones((tc, tc), np.float32)), BF16)
    place = _bias_placement()
    cols = place.shape[2]
    const2 = lambda i, j: (0, 0)
    return pl.pallas_call(
        _decay_kernel,
        grid=(b, s // tc),
        in_specs=[
            pl.BlockSpec((1, tc, d), lambda i, j: (i, j, 0)),
            pl.BlockSpec((1, d), const2),
            pl.BlockSpec((d, LANES), const2),
            pl.BlockSpec((d, LANES), const2),
            pl.BlockSpec((1, LANES), const2),
            pl.BlockSpec((tc, tc), const2),
            pl.BlockSpec((N_BIAS_PIECES, LANES, cols), lambda i, j: (0, 0, 0)),
        ],
        out_specs=pl.BlockSpec((1, tc, cols), lambda i, j: (i, j, 0)),
        out_shape=jax.ShapeDtypeStruct((b, s, cols), BF16),
        scratch_shapes=[pltpu.VMEM((1, LANES), F32)],
        compiler_params=_cparams(("parallel", "arbitrary")),
        name="fox_decay",
    )(x3, g, wf_hi, wf_lo, b_f, tri, place)


def _head_lane_mask(hh):
    lane = lax.broadcasted_iota(jnp.int32, (1, LANES), 1)
    return (lane >= hh * HEAD_DIM) & (lane < (hh + 1) * HEAD_DIM)


def _tile_rel(t):
    return (lax.broadcasted_iota(jnp.int32, (t, t), 0)
            - lax.broadcasted_iota(jnp.int32, (t, t), 1))


def _attn_specs(mixer, t, s, d):
    nq = s // t
    qoff = N_BRANCH * d // LANES + (3 * mixer) * N_HEAD_BLOCKS
    koff = qoff + N_HEAD_BLOCKS
    voff = koff + N_HEAD_BLOCKS
    return [
        pl.BlockSpec((t, LANES), lambda b, hp, qi: (b * nq + qi, qoff + hp)),
        pl.BlockSpec((s, LANES), lambda b, hp, qi: (b, koff + hp)),
        pl.BlockSpec((s, LANES), lambda b, hp, qi: (b, voff + hp)),
    ], pl.BlockSpec((t, LANES), lambda b, hp, qi: (b * nq + qi, hp))


def _sb_kernel(q_ref, k_ref, v_ref, o_ref, acc_ref, cs_ref):
    t = q_ref.shape[0]
    qi = pl.program_id(2)
    q2 = q_ref[...]
    rel = _tile_rel(t)
    tril = jnp.where(rel >= 0, 1.0, 0.0).astype(BF16)

    def tile(hh, qh, j, diag):
        start = pl.multiple_of(j * t, t)
        z = _dot_nt(qh, k_ref[pl.ds(start, t), :])
        sp = jnp.maximum(z, 0.0) + jnp.log(1.0 + jnp.exp(-jnp.abs(z)))
        if diag:
            sp = jnp.where(rel > 0, sp, 0.0)
        hi, lo = _split2(sp)
        r = _dot(hi, tril) + _dot(lo, tril) + cs_ref[hh]
        w = jnp.exp(z - r)
        if diag:
            w = jnp.where(rel > 0, w, 0.0)
        acc_ref[hh] += _dot(w.astype(BF16), v_ref[pl.ds(start, t), :])
        cs_ref[hh] = r[:, 0:1]

    for hh in range(HEADS_PER_BLOCK):
        qh = jnp.where(_head_lane_mask(hh), q2, jnp.zeros_like(q2))
        acc_ref[hh] = jnp.zeros((t, LANES), F32)
        cs_ref[hh] = jnp.zeros((t, 1), F32)
        tile(hh, qh, qi, True)

        def cond(j):
            return j >= 0

        def body(j, hh=hh, qh=qh):
            tile(hh, qh, j, False)
            done = jnp.min(cs_ref[hh]) > SB_EXIT
            return jnp.where(done, -1, j - 1)

        lax.while_loop(cond, body, qi - 1)

    o_ref[...] = jnp.where(_head_lane_mask(0), acc_ref[0], acc_ref[1]).astype(o_ref.dtype)


def _sb_attention(proj, b, s, d):
    t = ATT_TILE
    in_specs, out_spec = _attn_specs(0, t, s, d)
    return pl.pallas_call(
        _sb_kernel,
        grid=(b, N_HEAD_BLOCKS, s // t),
        in_specs=in_specs,
        out_specs=out_spec,
        out_shape=jax.ShapeDtypeStruct((b * s, BRANCH_WIDTH), BF16),
        scratch_shapes=[pltpu.VMEM((HEADS_PER_BLOCK, t, LANES), F32),
                        pltpu.VMEM((HEADS_PER_BLOCK, t, 1), F32)],
        compiler_params=_cparams(("parallel", "parallel", "arbitrary")),
        name="stickbreak_attn",
    )(proj, proj, proj)


def _flash_kernel(*refs, moba, nb):
    if moba:
        (q_ref, k_ref, v_ref, o_ref, kaug_ref, vt_ref, acc_ref, m_ref, s_even_ref, s_odd_ref,
         bmax_ref, knorm_ref, kmean_ref, sel_ref) = refs
    else:
        (q_ref, k_ref, v_ref, e_ref, o_ref, kaug_ref, vt_ref, acc_ref, m_ref, s_even_ref,
         s_odd_ref, bmax_ref, knorm_ref) = refs
    t = q_ref.shape[1]
    n_blocks = k_ref.shape[1] // t
    hp = pl.program_id(1)
    qi = pl.program_id(2)
    lane = lax.broadcasted_iota(jnp.int32, (1, LANES), 1)
    streams = [(bi, hh) for bi in range(nb) for hh in range(HEADS_PER_BLOCK)]

    def piece_lanes(hh):
        e0 = _bias_lane(hh)
        return (lane >= e0) & (lane < e0 + N_BIAS_PIECES)

    @pl.when(qi == 0)
    def _build():
        bmax_ref[...] = jnp.full(bmax_ref.shape, NEG_INF, F32)
        knorm_ref[...] = jnp.zeros(knorm_ref.shape, F32)
        tail = jnp.where(lax.broadcasted_iota(jnp.int32, (V_ROWS - HEAD_DIM, t), 0) == 0, 1.0, 0.0)

        def chunk(c, carry):
            start = pl.multiple_of(c * t, t)
            rows = pl.ds(start, t)
            extras = []
            if moba:
                pos = (lax.broadcasted_iota(jnp.int32, (t, LANES), 0) + start).astype(F32)
                for hh in range(HEADS_PER_BLOCK):
                    h = hp * HEADS_PER_BLOCK + hh
                    slope = pltpu.bitcast(jnp.full((1, LANES), (126 - h) << 23, jnp.int32), F32)
                    extra = jnp.zeros((t, LANES), F32)
                    for i, piece in enumerate(_split3(pos * slope * LOG2E)):
                        extra = jnp.where(lane == _bias_lane(hh) + i, piece.astype(F32), extra)
                    extras.append(extra.astype(BF16))
            for bi in range(nb):
                k2 = k_ref[bi, rows, :]
                k32 = k2.astype(F32)
                v2t = v_ref[bi, rows, :].astype(F32).T
                if moba:
                    kmean_ref[bi, pl.ds(c, 1), :] = jnp.sum(k32, axis=0, keepdims=True) * (1.0 / t)
                for hh in range(HEADS_PER_BLOCK):
                    st = bi * HEADS_PER_BLOCK + hh
                    extra = extras[hh] if moba else e_ref[bi, rows, :]
                    kaug_ref[st, rows, :] = jnp.where(_head_lane_mask(hh), k2, extra)
                    v_h = v2t[hh * HEAD_DIM:(hh + 1) * HEAD_DIM, :]
                    vt_ref[st, :, rows] = jnp.concatenate([v_h, tail], axis=0).astype(BF16)
                    bias = jnp.sum(jnp.where(piece_lanes(hh), extra.astype(F32), 0.0), axis=1, keepdims=True)
                    run = jnp.maximum(bmax_ref[st, pl.ds(jnp.maximum(c - 1, 0), 1), :],
                                      jnp.max(bias, axis=0, keepdims=True))
                    bmax_ref[st, pl.ds(c, 1), :] = run
                    k_sq = jnp.sum(jnp.where(_head_lane_mask(hh), k32 * k32, 0.0), axis=1, keepdims=True)
                    knorm_ref[st] = jnp.maximum(knorm_ref[st], jnp.max(k_sq, axis=0, keepdims=True))
            return carry

        lax.fori_loop(0, n_blocks, chunk, 0)

    q_aug = []
    q_norm = []
    for bi, hh in streams:
        q2 = q_ref[bi]
        ones = jnp.where(piece_lanes(hh), 1.0, 0.0).astype(BF16)
        q_aug.append(jnp.where(_head_lane_mask(hh), q2, jnp.broadcast_to(ones, q2.shape)))
        q32 = q2.astype(F32)
        q_sq = jnp.sum(jnp.where(_head_lane_mask(hh), q32 * q32, 0.0), axis=1, keepdims=True)
        q_norm.append(jnp.sqrt(jnp.max(q_sq, axis=0, keepdims=True)))

    if moba:
        blk = lax.broadcasted_iota(jnp.int32, (n_blocks, 1), 0)
        for st, (bi, hh) in enumerate(streams):
            km = _split3(kmean_ref[bi])
            q2 = q_ref[bi]
            qh = jnp.where(_head_lane_mask(hh), q2, jnp.zeros_like(q2))
            route = _dot_nt(km[0], qh) + _dot_nt(km[1], qh) + _dot_nt(km[2], qh)
            rank = jnp.zeros((n_blocks, t), F32)
            for n in range(n_blocks):
                row = route[n:n + 1, :]
                beats = (row > route) | ((row == route) & (blk > n))
                rank = rank + jnp.where(beats, jnp.where(qi > n, 1.0, 0.0), 0.0)
            sel_ref[st] = jnp.where((rank < MOBA_TOPK) & (blk < qi), 1.0, 0.0)

    valid = _tile_rel(t) <= 0

    def scores(s_ref, j):
        rows = pl.ds(pl.multiple_of(j * t, t), t)
        for st in range(len(streams)):
            s_ref[st] = _dot_nt(kaug_ref[st, rows, :], q_aug[st])

    def accumulate(s_ref, j, diag):
        rows = pl.ds(pl.multiple_of(j * t, t), t)
        for st in range(len(streams)):
            s_t = s_ref[st]
            if diag:
                s_t = jnp.where(valid, s_t, NEG_INF)
            m_old = m_ref[st]
            m_new = jnp.maximum(m_old, jnp.max(s_t, axis=0, keepdims=True))
            m_sub = m_new
            if moba and not diag:
                picked = sel_ref[st, pl.ds(j, 1), :] > 0.0
                m_new = jnp.where(picked, m_new, m_old)
                m_sub = jnp.where(picked, m_new, -NEG_INF)
            alpha = jnp.exp2(m_old - m_new)
            p = jnp.exp2(s_t - m_sub)
            acc_ref[st] = alpha * acc_ref[st] + _dot(vt_ref[st, :, rows], p.astype(BF16))
            m_ref[st] = m_new

    m_ref[...] = jnp.full(m_ref.shape, NEG_INF, F32)
    acc_ref[...] = jnp.zeros(acc_ref.shape, F32)
    scores(s_even_ref, qi)
    scores(s_odd_ref, jnp.maximum(qi - 1, 0))
    accumulate(s_even_ref, qi, True)

    blk_col = lax.broadcasted_iota(jnp.int32, (n_blocks, 1), 0)
    n_past = jnp.int32(0)
    for st in range(len(streams)):
        reach = q_norm[st] * jnp.sqrt(knorm_ref[st])
        thr = jnp.min(m_ref[st], axis=1, keepdims=True) - FLASH_SKIP - reach
        dead = (bmax_ref[st] < thr) & (blk_col < qi)
        n_dead = jnp.sum(jnp.where(dead, 1.0, 0.0)).astype(jnp.int32)
        n_past = jnp.maximum(n_past, qi - n_dead)

    def body(i, carry):
        j = qi - i
        j_next = jnp.maximum(j - 1, 0)

        @pl.when(i % 2 == 1)
        def _():
            scores(s_even_ref, j_next)
            accumulate(s_odd_ref, j, False)

        @pl.when(i % 2 == 0)
        def _():
            scores(s_odd_ref, j_next)
            accumulate(s_even_ref, j, False)

        return carry

    lax.fori_loop(1, n_past + 1, body, 0)

    for bi in range(nb):
        halves = []
        for hh in range(HEADS_PER_BLOCK):
            acc = acc_ref[bi * HEADS_PER_BLOCK + hh]
            halves.append(acc[:HEAD_DIM] * (1.0 / acc[HEAD_DIM:HEAD_DIM + 1]))
        o_ref[bi] = jnp.concatenate(halves, axis=0).T.astype(o_ref.dtype)


def _flash_attention(proj3, key_bias3, d, moba):
    b, s, _ = proj3.shape
    t = ATT_TILE
    nb = FLASH_BATCH if b % FLASH_BATCH == 0 else 1
    n_blocks = s // t
    n_streams = nb * HEADS_PER_BLOCK
    qoff = N_BRANCH * d // LANES + (3 * (2 if moba else 1)) * N_HEAD_BLOCKS
    koff = qoff + N_HEAD_BLOCKS
    voff = koff + N_HEAD_BLOCKS
    in_specs = [
        pl.BlockSpec((nb, t, LANES), lambda bg, hp, qi: (bg, qi, qoff + hp)),
        pl.BlockSpec((nb, s, LANES), lambda bg, hp, qi: (bg, 0, koff + hp)),
        pl.BlockSpec((nb, s, LANES), lambda bg, hp, qi: (bg, 0, voff + hp)),
    ]
    operands = [proj3, proj3, proj3]
    scratch = [pltpu.VMEM((n_streams, s, LANES), BF16),
               pltpu.VMEM((n_streams, V_ROWS, s), BF16),
               pltpu.VMEM((n_streams, V_ROWS, t), F32),
               pltpu.VMEM((n_streams, 1, t), F32),
               pltpu.VMEM((n_streams, t, t), F32),
               pltpu.VMEM((n_streams, t, t), F32),
               pltpu.VMEM((n_streams, n_blocks, 1), F32),
               pltpu.VMEM((n_streams, 1, 1), F32)]
    if moba:
        scratch += [pltpu.VMEM((nb, n_blocks, LANES), F32),
                    pltpu.VMEM((n_streams, n_blocks, t), F32)]
    else:
        in_specs.append(pl.BlockSpec((nb, s, LANES), lambda bg, hp, qi: (bg, 0, hp)))
        operands.append(key_bias3)
    return pl.pallas_call(
        functools.partial(_flash_kernel, moba=moba, nb=nb),
        grid=(b // nb, N_HEAD_BLOCKS, n_blocks),
        in_specs=in_specs,
        out_specs=pl.BlockSpec((nb, t, LANES), lambda bg, hp, qi: (bg, qi, hp)),
        out_shape=jax.ShapeDtypeStruct((b, s, BRANCH_WIDTH), BF16),
        scratch_shapes=scratch,
        compiler_params=_cparams(("parallel", "parallel", "arbitrary")),
        name="moba_attn" if moba else "forgetting_attn",
    )(*operands)


def _mix_kernel(x_ref, oa_ref, ob_ref, oc_ref, g_ref, wb_ref, wo_ref, y_ref):
    d = x_ref.shape[1]
    mixed = None
    for n, o_ref in enumerate((oa_ref, ob_ref, oc_ref)):
        gate = jax.nn.sigmoid(g_ref[:, n * d:(n + 1) * d].astype(F32))
        term = gate * _dot(o_ref[...], wb_ref[n])
        mixed = term if mixed is None else mixed + term
    y_ref[...] = x_ref[...] + _dot(mixed.astype(BF16), wo_ref[...])


def _mix(x, o_a, o_b, o_c, proj, wb, wo, tm):
    n, d = x.shape
    o_spec = pl.BlockSpec((tm, BRANCH_WIDTH), lambda i: (i, 0))
    return pl.pallas_call(
        _mix_kernel,
        grid=(n // tm,),
        in_specs=[
            pl.BlockSpec((tm, d), lambda i: (i, 0)),
            o_spec, o_spec, o_spec,
            pl.BlockSpec((tm, N_BRANCH * d), lambda i: (i, 0)),
            pl.BlockSpec((N_BRANCH, BRANCH_WIDTH, d), lambda i: (0, 0, 0)),
            pl.BlockSpec((d, d), lambda i: (0, 0)),
        ],
        out_specs=pl.BlockSpec((tm, d), lambda i: (i, 0)),
        out_shape=jax.ShapeDtypeStruct((n, d), F32),
        compiler_params=_cparams(("parallel",)),
        name="gated_mix_out_proj",
    )(x, o_a, o_b, o_c, proj, wb, wo)


def _mlp_kernel(x_ref, g_ref, wu_ref, wd_ref, gf_ref, y_ref, h_ref, acc_ref, *, final_norm):
    f = pl.program_id(1)

    @pl.when(f == 0)
    def _():
        h_ref[...] = _rms(x_ref[...], g_ref[...]).astype(BF16)
        acc_ref[...] = jnp.zeros_like(acc_ref)

    hid = jnp.square(jnp.maximum(_dot(h_ref[...], wu_ref[...]), 0.0))
    acc_ref[...] += _dot(hid.astype(BF16), wd_ref[...])

    @pl.when(f == pl.num_programs(1) - 1)
    def _():
        y = x_ref[...] + acc_ref[...]
        if final_norm:
            y = _rms(y, gf_ref[...])
        y_ref[...] = y


def _mlp(x, g, wu, wd, g_final, final_norm, tm, tf):
    n, d = x.shape
    d_ff = wu.shape[1]
    return pl.pallas_call(
        functools.partial(_mlp_kernel, final_norm=final_norm),
        grid=(n // tm, d_ff // tf),
        in_specs=[
            pl.BlockSpec((tm, d), lambda i, f: (i, 0)),
            pl.BlockSpec((1, d), lambda i, f: (0, 0)),
            pl.BlockSpec((d, tf), lambda i, f: (0, f)),
            pl.BlockSpec((tf, d), lambda i, f: (f, 0)),
            pl.BlockSpec((1, d), lambda i, f: (0, 0)),
        ],
        out_specs=pl.BlockSpec((tm, d), lambda i, f: (i, 0)),
        out_shape=jax.ShapeDtypeStruct((n, d), F32),
        scratch_shapes=[pltpu.VMEM((tm, d), BF16), pltpu.VMEM((tm, d), F32)],
        compiler_params=_cparams(("parallel", "arbitrary")),
        name="relu2_mlp",
    )(x, g, wu, wd, g_final)


def _prep_in_proj(w_in):
    bw = BRANCH_WIDTH
    scale = HEAD_DIM ** -0.5
    q_scales = (scale, scale * LOG2E, scale * LOG2E)
    f_lo = 6 * bw
    f_hi = f_lo + N_HEADS
    g_lo = 9 * bw + N_HEADS
    pieces = [w_in[:, :, g_lo:]]
    for m in range(N_BRANCH):
        base = 3 * m * bw + (N_HEADS if m == 2 else 0)
        pieces += [w_in[:, :, base:base + bw] * q_scales[m],
                   w_in[:, :, base + bw:base + 3 * bw]]
    w_main = jnp.concatenate(pieces, axis=-1).astype(BF16)
    w_f = jnp.pad(w_in[:, :, f_lo:f_hi], ((0, 0), (0, 0), (0, LANES - N_HEADS)))
    wf_hi = w_f.astype(BF16)
    wf_lo = (w_f - wf_hi.astype(F32)).astype(BF16)
    return w_main, wf_hi, wf_lo


def kernel(x, norm_mix, w_in, b_forget, w_branch, w_out, norm_mlp, w_up, w_down, norm_final):
    b, s, d = x.shape
    depth = w_in.shape[0]
    n = b * s
    assert s % ATT_TILE == 0 and d % LANES == 0

    w_main, wf_hi, wf_lo = _prep_in_proj(w_in)
    wb = w_branch.astype(BF16)
    wo = w_out.astype(BF16)
    wu = w_up.astype(BF16)
    wd = w_down.astype(BF16)
    b_f = jnp.pad(b_forget, ((0, 0), (0, LANES - N_HEADS)))[:, None, :]

    tm = min(1024, n)
    xf = x.reshape(n, d)
    for l in range(depth):
        g_mix = norm_mix[l][None, :]
        proj = _norm_matmul(xf, g_mix, w_main[l], tm, 1536)
        key_bias = _decay(xf.reshape(b, s, d), g_mix, wf_hi[l], wf_lo[l], b_f[l], min(512, s))
        o_a = _sb_attention(proj, b, s, d)
        proj3 = proj.reshape(b, s, -1)
        o_b = _flash_attention(proj3, key_bias, d, False).reshape(n, -1)
        o_c = _flash_attention(proj3, None, d, True).reshape(n, -1)
        xf = _mix(xf, o_a, o_b, o_c, proj, wb[l], wo[l], min(512, n))
        xf = _mlp(xf, norm_mlp[l][None, :], wu[l], wd[l], norm_final[None, :],
                  l == depth - 1, min(512, n), 1024)
    return xf.reshape(b, s, d)
```

```python
import functools
import math

import jax
import jax.numpy as jnp
import numpy as np
from jax import lax
from jax.experimental import pallas as pl
from jax.experimental.pallas import tpu as pltpu

F32 = jnp.float32
BF16 = jnp.bfloat16

HEAD_DIM = 64
N_HEADS = 8
BRANCH_WIDTH = N_HEADS * HEAD_DIM
N_BRANCH = 3
MOBA_BLOCK = 256
MOBA_TOPK = 3
RMS_EPS = 1e-6
NEG_INF = -1e30
LOG2E = math.log2(math.e)
LANES = 128
HEADS_PER_BLOCK = LANES // HEAD_DIM
N_HEAD_BLOCKS = BRANCH_WIDTH // LANES
N_BIAS_PIECES = 3
ATT_TILE = 256
SB_EXIT = 110.0
FLASH_SKIP = 160.0
FLASH_BATCH = 2
V_ROWS = HEAD_DIM + 16
VMEM_LIMIT = 56 * 1024 * 1024


def _cparams(sem):
    return pltpu.CompilerParams(dimension_semantics=sem, vmem_limit_bytes=VMEM_LIMIT)


def _rms(x, g):
    ms = jnp.mean(x * x, axis=-1, keepdims=True)
    return x * lax.rsqrt(ms + RMS_EPS) * g


def _dot(a, b):
    return jnp.dot(a, b, preferred_element_type=F32)


def _dot_nt(a, b):
    return lax.dot_general(a, b, (((1,), (1,)), ((), ())), preferred_element_type=F32)


def _split2(x):
    hi = x.astype(BF16)
    lo = (x - hi.astype(F32)).astype(BF16)
    return hi, lo


def _split3(x):
    hi = x.astype(BF16)
    r = x - hi.astype(F32)
    mid = r.astype(BF16)
    lo = (r - mid.astype(F32)).astype(BF16)
    return hi, mid, lo


def _bias_lane(hh):
    return (1 - hh) * HEAD_DIM


def _norm_matmul_kernel(x_ref, g_ref, w_ref, o_ref, h_ref):
    @pl.when(pl.program_id(1) == 0)
    def _():
        h_ref[...] = _rms(x_ref[...], g_ref[...]).astype(BF16)

    o_ref[...] = _dot(h_ref[...], w_ref[...]).astype(o_ref.dtype)


def _norm_matmul(x, g, w, tm, tn):
    n, d = x.shape
    cols = w.shape[1]
    return pl.pallas_call(
        _norm_matmul_kernel,
        grid=(n // tm, cols // tn),
        in_specs=[
            pl.BlockSpec((tm, d), lambda i, j: (i, 0)),
            pl.BlockSpec((1, d), lambda i, j: (0, 0)),
            pl.BlockSpec((d, tn), lambda i, j: (0, j)),
        ],
        out_specs=pl.BlockSpec((tm, tn), lambda i, j: (i, j)),
        out_shape=jax.ShapeDtypeStruct((n, cols), BF16),
        scratch_shapes=[pltpu.VMEM((tm, d), BF16)],
        compiler_params=_cparams(("parallel", "arbitrary")),
        name="norm_in_proj",
    )(x, g, w)


def _decay_kernel(x_ref, g_ref, wf_hi_ref, wf_lo_ref, b_ref, tri_ref, place_ref, e_ref, carry_ref):
    @pl.when(pl.program_id(1) == 0)
    def _():
        carry_ref[...] = jnp.zeros_like(carry_ref)

    h_hi, h_lo = _split2(_rms(x_ref[0], g_ref[...]))
    f = _dot(h_hi, wf_hi_ref[...]) + _dot(h_lo, wf_hi_ref[...]) + _dot(h_hi, wf_lo_ref[...])
    y = f + b_ref[...]
    logf = jnp.minimum(y, 0.0) - jnp.log(1.0 + jnp.exp(-jnp.abs(y)))
    tri = tri_ref[...]
    cum = carry_ref[...]
    for piece in _split3(logf):
        cum = cum + _dot(tri, piece)
    carry_ref[...] = cum[-1:, :]
    e = None
    for i, piece in enumerate(_split3(cum * (-LOG2E))):
        term = _dot(piece, place_ref[i])
        e = term if e is None else e + term
    e_ref[0] = e.astype(BF16)


def _bias_placement():
    place = np.zeros((N_BIAS_PIECES, LANES, N_HEAD_BLOCKS * LANES), np.float32)
    for h in range(N_HEADS):
        hp, hh = divmod(h, HEADS_PER_BLOCK)
        for i in range(N_BIAS_PIECES):
            place[i, h, hp * LANES + _bias_lane(hh) + i] = 1.0
    return jnp.asarray(place, BF16)


def _decay(x3, g, wf_hi, wf_lo, b_f, tc):
    b, s, d = x3.shape
    tri = jnp.asarray(np.tril(np.ones((tc, tc), np.float32)), BF16)
    place = _bias_placement()
    cols = place.shape[2]
    const2 = lambda i, j: (0, 0)
    return pl.pallas_call(
        _decay_kernel,
        grid=(b, s // tc),
        in_specs=[
            pl.BlockSpec((1, tc, d), lambda i, j: (i, j, 0)),
            pl.BlockSpec((1, d), const2),
            pl.BlockSpec((d, LANES), const2),
            pl.BlockSpec((d, LANES), const2),
            pl.BlockSpec((1, LANES), const2),
            pl.BlockSpec((tc, tc), const2),
            pl.BlockSpec((N_BIAS_PIECES, LANES, cols), lambda i, j: (0, 0, 0)),
        ],
        out_specs=pl.BlockSpec((1, tc, cols), lambda i, j: (i, j, 0)),
        out_shape=jax.ShapeDtypeStruct((b, s, cols), BF16),
        scratch_shapes=[pltpu.VMEM((1, LANES), F32)],
        compiler_params=_cparams(("parallel", "arbitrary")),
        name="fox_decay",
    )(x3, g, wf_hi, wf_lo, b_f, tri, place)


def _head_lane_mask(hh):
    lane = lax.broadcasted_iota(jnp.int32, (1, LANES), 1)
    return (lane >= hh * HEAD_DIM) & (lane < (hh + 1) * HEAD_DIM)


def _tile_rel(t):
    return (lax.broadcasted_iota(jnp.int32, (t, t), 0)
            - lax.broadcasted_iota(jnp.int32, (t, t), 1))


def _mixer_specs(mixer, nb, t, s, d):
    qoff = N_BRANCH * d // LANES + (3 * mixer) * N_HEAD_BLOCKS
    koff = qoff + N_HEAD_BLOCKS
    voff = koff + N_HEAD_BLOCKS
    return [
        pl.BlockSpec((nb, t, LANES), lambda bg, hp, qi: (bg, qi, qoff + hp)),
        pl.BlockSpec((nb, s, LANES), lambda bg, hp, qi: (bg, 0, koff + hp)),
        pl.BlockSpec((nb, s, LANES), lambda bg, hp, qi: (bg, 0, voff + hp)),
    ], pl.BlockSpec((nb, t, LANES), lambda bg, hp, qi: (bg, qi, hp))


def _sb_kernel(q_ref, k_ref, v_ref, o_ref, vt_ref, acc_ref, cs_ref, r_ref, s_even_ref, s_odd_ref, *, nb):
    t = q_ref.shape[1]
    n_blocks = k_ref.shape[1] // t
    qi = pl.program_id(2)
    streams = [(bi, hh) for bi in range(nb) for hh in range(HEADS_PER_BLOCK)]

    @pl.when(qi == 0)
    def _build():
        def chunk(c, carry):
            rows = pl.ds(pl.multiple_of(c * t, t), t)
            for st, (bi, hh) in enumerate(streams):
                v2t = v_ref[bi, rows, :].astype(F32).T
                vt_ref[st, :, rows] = v2t[hh * HEAD_DIM:(hh + 1) * HEAD_DIM, :].astype(BF16)
            return carry

        lax.fori_loop(0, n_blocks, chunk, 0)

    q_m = []
    for bi, hh in streams:
        q2 = q_ref[bi]
        q_m.append(jnp.where(_head_lane_mask(hh), q2, jnp.zeros_like(q2)))

    rel = _tile_rel(t)
    strictly_past = rel < 0
    upper = jnp.where(rel <= 0, 1.0, 0.0).astype(BF16)

    def scores(s_ref, j):
        rows = pl.ds(pl.multiple_of(j * t, t), t)
        for st, (bi, hh) in enumerate(streams):
            s_ref[st] = _dot_nt(k_ref[bi, rows, :], q_m[st])

    def accumulate(s_ref, j, diag):
        rows = pl.ds(pl.multiple_of(j * t, t), t)
        for st in range(len(streams)):
            z = s_ref[st]
            sp = jnp.maximum(z, 0.0) + jnp.log(1.0 + jnp.exp(-jnp.abs(z)))
            if diag:
                sp = jnp.where(strictly_past, sp, 0.0)
            hi, lo = _split2(sp)
            r = _dot(upper, hi) + _dot(upper, lo) + cs_ref[st]
            r_ref[st] = r
            cs_ref[st] = r[0:1, :]
        for st in range(len(streams)):
            w = jnp.exp(s_ref[st] - r_ref[st])
            if diag:
                w = jnp.where(strictly_past, w, 0.0)
            acc_ref[st] += _dot(vt_ref[st, :, rows], w.astype(BF16))

    def next_distance(i):
        lowest = cs_ref[0]
        for st in range(1, len(streams)):
            lowest = jnp.minimum(lowest, cs_ref[st])
        return jnp.where(jnp.min(lowest) > SB_EXIT, qi + 1, i + 1)

    acc_ref[...] = jnp.zeros(acc_ref.shape, F32)
    cs_ref[...] = jnp.zeros(cs_ref.shape, F32)
    scores(s_even_ref, qi)
    scores(s_odd_ref, jnp.maximum(qi - 1, 0))
    accumulate(s_even_ref, qi, True)

    def body(i):
        j = qi - i
        j_next = jnp.maximum(j - 1, 0)

        @pl.when(i % 2 == 1)
        def _():
            scores(s_even_ref, j_next)
            accumulate(s_odd_ref, j, False)

        @pl.when(i % 2 == 0)
        def _():
            scores(s_odd_ref, j_next)
            accumulate(s_even_ref, j, False)

        return next_distance(i)

    lax.while_loop(lambda i: i <= qi, body, next_distance(0))

    for bi in range(nb):
        halves = [acc_ref[bi * HEADS_PER_BLOCK + hh] for hh in range(HEADS_PER_BLOCK)]
        o_ref[bi] = jnp.concatenate(halves, axis=0).T.astype(o_ref.dtype)


def _sb_attention(proj3, d):
    b, s, _ = proj3.shape
    t = ATT_TILE
    nb = FLASH_BATCH if b % FLASH_BATCH == 0 else 1
    n_streams = nb * HEADS_PER_BLOCK
    in_specs, out_spec = _mixer_specs(0, nb, t, s, d)
    return pl.pallas_call(
        functools.partial(_sb_kernel, nb=nb),
        grid=(b // nb, N_HEAD_BLOCKS, s // t),
        in_specs=in_specs,
        out_specs=out_spec,
        out_shape=jax.ShapeDtypeStruct((b, s, BRANCH_WIDTH), BF16),
        scratch_shapes=[pltpu.VMEM((n_streams, HEAD_DIM, s), BF16),
                        pltpu.VMEM((n_streams, HEAD_DIM, t), F32),
                        pltpu.VMEM((n_streams, 1, t), F32),
                        pltpu.VMEM((n_streams, t, t), F32),
                        pltpu.VMEM((n_streams, t, t), F32),
                        pltpu.VMEM((n_streams, t, t), F32)],
        compiler_params=_cparams(("parallel", "parallel", "arbitrary")),
        name="stickbreak_attn",
    )(proj3, proj3, proj3)


def _flash_kernel(*refs, moba, nb):
    if moba:
        (q_ref, k_ref, v_ref, o_ref, kaug_ref, vt_ref, acc_ref, m_ref, s_even_ref, s_odd_ref,
         bmax_ref, knorm_ref, kmean_ref, sel_ref) = refs
    else:
        (q_ref, k_ref, v_ref, e_ref, o_ref, kaug_ref, vt_ref, acc_ref, m_ref, s_even_ref,
         s_odd_ref, bmax_ref, knorm_ref) = refs
    t = q_ref.shape[1]
    n_blocks = k_ref.shape[1] // t
    hp = pl.program_id(1)
    qi = pl.program_id(2)
    lane = lax.broadcasted_iota(jnp.int32, (1, LANES), 1)
    streams = [(bi, hh) for bi in range(nb) for hh in range(HEADS_PER_BLOCK)]

    def piece_lanes(hh):
        e0 = _bias_lane(hh)
        return (lane >= e0) & (lane < e0 + N_BIAS_PIECES)

    @pl.when(qi == 0)
    def _build():
        bmax_ref[...] = jnp.full(bmax_ref.shape, NEG_INF, F32)
        knorm_ref[...] = jnp.zeros(knorm_ref.shape, F32)
        tail = jnp.where(lax.broadcasted_iota(jnp.int32, (V_ROWS - HEAD_DIM, t), 0) == 0, 1.0, 0.0)

        def chunk(c, carry):
            start = pl.multiple_of(c * t, t)
            rows = pl.ds(start, t)
            extras = []
            if moba:
                pos = (lax.broadcasted_iota(jnp.int32, (t, LANES), 0) + start).astype(F32)
                for hh in range(HEADS_PER_BLOCK):
                    h = hp * HEADS_PER_BLOCK + hh
                    slope = pltpu.bitcast(jnp.full((1, LANES), (126 - h) << 23, jnp.int32), F32)
                    extra = jnp.zeros((t, LANES), F32)
                    for i, piece in enumerate(_split3(pos * slope * LOG2E)):
                        extra = jnp.where(lane == _bias_lane(hh) + i, piece.astype(F32), extra)
                    extras.append(extra.astype(BF16))
            for bi in range(nb):
                k2 = k_ref[bi, rows, :]
                k32 = k2.astype(F32)
                v2t = v_ref[bi, rows, :].astype(F32).T
                if moba:
                    kmean_ref[bi, pl.ds(c, 1), :] = jnp.sum(k32, axis=0, keepdims=True) * (1.0 / t)
                for hh in range(HEADS_PER_BLOCK):
                    st = bi * HEADS_PER_BLOCK + hh
                    extra = extras[hh] if moba else e_ref[bi, rows, :]
                    kaug_ref[st, rows, :] = jnp.where(_head_lane_mask(hh), k2, extra)
                    v_h = v2t[hh * HEAD_DIM:(hh + 1) * HEAD_DIM, :]
                    vt_ref[st, :, rows] = jnp.concatenate([v_h, tail], axis=0).astype(BF16)
                    bias = jnp.sum(jnp.where(piece_lanes(hh), extra.astype(F32), 0.0), axis=1, keepdims=True)
                    run = jnp.maximum(bmax_ref[st, pl.ds(jnp.maximum(c - 1, 0), 1), :],
                                      jnp.max(bias, axis=0, keepdims=True))
                    bmax_ref[st, pl.ds(c, 1), :] = run
                    k_sq = jnp.sum(jnp.where(_head_lane_mask(hh), k32 * k32, 0.0), axis=1, keepdims=True)
                    knorm_ref[st] = jnp.maximum(knorm_ref[st], jnp.max(k_sq, axis=0, keepdims=True))
            return carry

        lax.fori_loop(0, n_blocks, chunk, 0)

    q_aug = []
    q_norm = []
    for bi, hh in streams:
        q2 = q_ref[bi]
        ones = jnp.where(piece_lanes(hh), 1.0, 0.0).astype(BF16)
        q_aug.append(jnp.where(_head_lane_mask(hh), q2, jnp.broadcast_to(ones, q2.shape)))
        q32 = q2.astype(F32)
        q_sq = jnp.sum(jnp.where(_head_lane_mask(hh), q32 * q32, 0.0), axis=1, keepdims=True)
        q_norm.append(jnp.sqrt(jnp.max(q_sq, axis=0, keepdims=True)))

    if moba:
        blk = lax.broadcasted_iota(jnp.int32, (n_blocks, 1), 0)
        for st, (bi, hh) in enumerate(streams):
            km = _split3(kmean_ref[bi])
            q2 = q_ref[bi]
            qh = jnp.where(_head_lane_mask(hh), q2, jnp.zeros_like(q2))
            route = _dot_nt(km[0], qh) + _dot_nt(km[1], qh) + _dot_nt(km[2], qh)
            rank = jnp.zeros((n_blocks, t), F32)
            for n in range(n_blocks):
                row = route[n:n + 1, :]
                beats = (row > route) | ((row == route) & (blk > n))
                rank = rank + jnp.where(beats, jnp.where(qi > n, 1.0, 0.0), 0.0)
            sel_ref[st] = jnp.where((rank < MOBA_TOPK) & (blk < qi), 1.0, 0.0)

    valid = _tile_rel(t) <= 0

    def scores(s_ref, j):
        rows = pl.ds(pl.multiple_of(j * t, t), t)
        for st in range(len(streams)):
            s_ref[st] = _dot_nt(kaug_ref[st, rows, :], q_aug[st])

    def accumulate(s_ref, j, diag):
        rows = pl.ds(pl.multiple_of(j * t, t), t)
        for st in range(len(streams)):
            s_t = s_ref[st]
            if diag:
                s_t = jnp.where(valid, s_t, NEG_INF)
            m_old = m_ref[st]
            m_new = jnp.maximum(m_old, jnp.max(s_t, axis=0, keepdims=True))
            m_sub = m_new
            if moba and not diag:
                picked = sel_ref[st, pl.ds(j, 1), :] > 0.0
                m_new = jnp.where(picked, m_new, m_old)
                m_sub = jnp.where(picked, m_new, -NEG_INF)
            alpha = jnp.exp2(m_old - m_new)
            p = jnp.exp2(s_t - m_sub)
            acc_ref[st] = alpha * acc_ref[st] + _dot(vt_ref[st, :, rows], p.astype(BF16))
            m_ref[st] = m_new

    m_ref[...] = jnp.full(m_ref.shape, NEG_INF, F32)
    acc_ref[...] = jnp.zeros(acc_ref.shape, F32)
    scores(s_even_ref, qi)
    scores(s_odd_ref, jnp.maximum(qi - 1, 0))
    accumulate(s_even_ref, qi, True)

    blk_col = lax.broadcasted_iota(jnp.int32, (n_blocks, 1), 0)
    n_past = jnp.int32(0)
    for st in range(len(streams)):
        reach = q_norm[st] * jnp.sqrt(knorm_ref[st])
        thr = jnp.min(m_ref[st], axis=1, keepdims=True) - FLASH_SKIP - reach
        dead = (bmax_ref[st] < thr) & (blk_col < qi)
        n_dead = jnp.sum(jnp.where(dead, 1.0, 0.0)).astype(jnp.int32)
        n_past = jnp.maximum(n_past, qi - n_dead)

    def body(i, carry):
        j = qi - i
        j_next = jnp.maximum(j - 1, 0)

        @pl.when(i % 2 == 1)
        def _():
            scores(s_even_ref, j_next)
            accumulate(s_odd_ref, j, False)

        @pl.when(i % 2 == 0)
        def _():
            scores(s_odd_ref, j_next)
            accumulate(s_even_ref, j, False)

        return carry

    lax.fori_loop(1, n_past + 1, body, 0)

    for bi in range(nb):
        halves = []
        for hh in range(HEADS_PER_BLOCK):
            acc = acc_ref[bi * HEADS_PER_BLOCK + hh]
            halves.append(acc[:HEAD_DIM] * (1.0 / acc[HEAD_DIM:HEAD_DIM + 1]))
        o_ref[bi] = jnp.concatenate(halves, axis=0).T.astype(o_ref.dtype)


def _flash_attention(proj3, key_bias3, d, moba):
    b, s, _ = proj3.shape
    t = ATT_TILE
    nb = FLASH_BATCH if b % FLASH_BATCH == 0 else 1
    n_blocks = s // t
    n_streams = nb * HEADS_PER_BLOCK
    in_specs, out_spec = _mixer_specs(2 if moba else 1, nb, t, s, d)
    operands = [proj3, proj3, proj3]
    scratch = [pltpu.VMEM((n_streams, s, LANES), BF16),
               pltpu.VMEM((n_streams, V_ROWS, s), BF16),
               pltpu.VMEM((n_streams, V_ROWS, t), F32),
               pltpu.VMEM((n_streams, 1, t), F32),
               pltpu.VMEM((n_streams, t, t), F32),
               pltpu.VMEM((n_streams, t, t), F32),
               pltpu.VMEM((n_streams, n_blocks, 1), F32),
               pltpu.VMEM((n_streams, 1, 1), F32)]
    if moba:
        scratch += [pltpu.VMEM((nb, n_blocks, LANES), F32),
                    pltpu.VMEM((n_streams, n_blocks, t), F32)]
    else:
        in_specs.append(pl.BlockSpec((nb, s, LANES), lambda bg, hp, qi: (bg, 0, hp)))
        operands.append(key_bias3)
    return pl.pallas_call(
        functools.partial(_flash_kernel, moba=moba, nb=nb),
        grid=(b // nb, N_HEAD_BLOCKS, n_blocks),
        in_specs=in_specs,
        out_specs=out_spec,
        out_shape=jax.ShapeDtypeStruct((b, s, BRANCH_WIDTH), BF16),
        scratch_shapes=scratch,
        compiler_params=_cparams(("parallel", "parallel", "arbitrary")),
        name="moba_attn" if moba else "forgetting_attn",
    )(*operands)


def _mix_kernel(x_ref, oa_ref, ob_ref, oc_ref, g_ref, wb_ref, wo_ref, y_ref):
    d = x_ref.shape[1]
    mixed = None
    for n, o_ref in enumerate((oa_ref, ob_ref, oc_ref)):
        gate = jax.nn.sigmoid(g_ref[:, n * d:(n + 1) * d].astype(F32))
        term = gate * _dot(o_ref[...], wb_ref[n])
        mixed = term if mixed is None else mixed + term
    y_ref[...] = x_ref[...] + _dot(mixed.astype(BF16), wo_ref[...])


def _mix(x, o_a, o_b, o_c, proj, wb, wo, tm):
    n, d = x.shape
    o_spec = pl.BlockSpec((tm, BRANCH_WIDTH), lambda i: (i, 0))
    return pl.pallas_call(
        _mix_kernel,
        grid=(n // tm,),
        in_specs=[
            pl.BlockSpec((tm, d), lambda i: (i, 0)),
            o_spec, o_spec, o_spec,
            pl.BlockSpec((tm, N_BRANCH * d), lambda i: (i, 0)),
            pl.BlockSpec((N_BRANCH, BRANCH_WIDTH, d), lambda i: (0, 0, 0)),
            pl.BlockSpec((d, d), lambda i: (0, 0)),
        ],
        out_specs=pl.BlockSpec((tm, d), lambda i: (i, 0)),
        out_shape=jax.ShapeDtypeStruct((n, d), F32),
        compiler_params=_cparams(("parallel",)),
        name="gated_mix_out_proj",
    )(x, o_a, o_b, o_c, proj, wb, wo)


def _mlp_kernel(x_ref, g_ref, wu_ref, wd_ref, gf_ref, y_ref, h_ref, acc_ref, *, final_norm):
    f = pl.program_id(1)

    @pl.when(f == 0)
    def _():
        h_ref[...] = _rms(x_ref[...], g_ref[...]).astype(BF16)
        acc_ref[...] = jnp.zeros_like(acc_ref)

    hid = jnp.square(jnp.maximum(_dot(h_ref[...], wu_ref[...]), 0.0))
    acc_ref[...] += _dot(hid.astype(BF16), wd_ref[...])

    @pl.when(f == pl.num_programs(1) - 1)
    def _():
        y = x_ref[...] + acc_ref[...]
        if final_norm:
            y = _rms(y, gf_ref[...])
        y_ref[...] = y


def _mlp(x, g, wu, wd, g_final, final_norm, tm, tf):
    n, d = x.shape
    d_ff = wu.shape[1]
    return pl.pallas_call(
        functools.partial(_mlp_kernel, final_norm=final_norm),
        grid=(n // tm, d_ff // tf),
        in_specs=[
            pl.BlockSpec((tm, d), lambda i, f: (i, 0)),
            pl.BlockSpec((1, d), lambda i, f: (0, 0)),
            pl.BlockSpec((d, tf), lambda i, f: (0, f)),
            pl.BlockSpec((tf, d), lambda i, f: (f, 0)),
            pl.BlockSpec((1, d), lambda i, f: (0, 0)),
        ],
        out_specs=pl.BlockSpec((tm, d), lambda i, f: (i, 0)),
        out_shape=jax.ShapeDtypeStruct((n, d), F32),
        scratch_shapes=[pltpu.VMEM((tm, d), BF16), pltpu.VMEM((tm, d), F32)],
        compiler_params=_cparams(("parallel", "arbitrary")),
        name="relu2_mlp",
    )(x, g, wu, wd, g_final)


def _prep_in_proj(w_in):
    bw = BRANCH_WIDTH
    scale = HEAD_DIM ** -0.5
    q_scales = (scale, scale * LOG2E, scale * LOG2E)
    f_lo = 6 * bw
    f_hi = f_lo + N_HEADS
    g_lo = 9 * bw + N_HEADS
    pieces = [w_in[:, :, g_lo:]]
    for m in range(N_BRANCH):
        base = 3 * m * bw + (N_HEADS if m == 2 else 0)
        pieces += [w_in[:, :, base:base + bw] * q_scales[m],
                   w_in[:, :, base + bw:base + 3 * bw]]
    w_main = jnp.concatenate(pieces, axis=-1).astype(BF16)
    w_f = jnp.pad(w_in[:, :, f_lo:f_hi], ((0, 0), (0, 0), (0, LANES - N_HEADS)))
    wf_hi = w_f.astype(BF16)
    wf_lo = (w_f - wf_hi.astype(F32)).astype(BF16)
    return w_main, wf_hi, wf_lo


def kernel(x, norm_mix, w_in, b_forget, w_branch, w_out, norm_mlp, w_up, w_down, norm_final):
    b, s, d = x.shape
    depth = w_in.shape[0]
    n = b * s
    assert s % ATT_TILE == 0 and d % LANES == 0

    w_main, wf_hi, wf_lo = _prep_in_proj(w_in)
    wb = w_branch.astype(BF16)
    wo = w_out.astype(BF16)
    wu = w_up.astype(BF16)
    wd = w_down.astype(BF16)
    b_f = jnp.pad(b_forget, ((0, 0), (0, LANES - N_HEADS)))[:, None, :]

    tm = min(1024, n)
    xf = x.reshape(n, d)
    for l in range(depth):
        g_mix = norm_mix[l][None, :]
        proj = _norm_matmul(xf, g_mix, w_main[l], tm, 1536)
        key_bias = _decay(xf.reshape(b, s, d), g_mix, wf_hi[l], wf_lo[l], b_f[l], min(512, s))
        proj3 = proj.reshape(b, s, -1)
        o_a = _sb_attention(proj3, d).reshape(n, -1)
        o_b = _flash_attention(proj3, key_bias, d, False).reshape(n, -1)
        o_c = _flash_attention(proj3, None, d, True).reshape(n, -1)
        xf = _mix(xf, o_a, o_b, o_c, proj, wb[l], wo[l], min(512, n))
        xf = _mlp(xf, norm_mlp[l][None, :], wu[l], wd[l], norm_final[None, :],
                  l == depth - 1, min(512, n), 1024)
    return xf.reshape(b, s, d)
```

```python
import functools
import math

import jax
import jax.numpy as jnp
import numpy as np
from jax import lax
from jax.experimental import pallas as pl
from jax.experimental.pallas import tpu as pltpu

F32 = jnp.float32
BF16 = jnp.bfloat16

HEAD_DIM = 64
N_HEADS = 8
BRANCH_WIDTH = N_HEADS * HEAD_DIM
N_BRANCH = 3
MOBA_BLOCK = 256
MOBA_TOPK = 3
RMS_EPS = 1e-6
NEG_INF = -1e30
LOG2E = math.log2(math.e)
LANES = 128
HEADS_PER_BLOCK = LANES // HEAD_DIM
N_HEAD_BLOCKS = BRANCH_WIDTH // LANES
N_BIAS_PIECES = 3
ATT_TILE = 256
SB_EXIT = 110.0
FLASH_SKIP = 160.0
NORM_SLACK = 1.001
FLASH_BATCH = 2
V_ROWS = HEAD_DIM + 16
VMEM_LIMIT = 56 * 1024 * 1024


def _cparams(sem):
    return pltpu.CompilerParams(dimension_semantics=sem, vmem_limit_bytes=VMEM_LIMIT)


def _rms(x, g):
    ms = jnp.mean(x * x, axis=-1, keepdims=True)
    return x * lax.rsqrt(ms + RMS_EPS) * g


def _dot(a, b):
    return jnp.dot(a, b, preferred_element_type=F32)


def _dot_nt(a, b):
    return lax.dot_general(a, b, (((1,), (1,)), ((), ())), preferred_element_type=F32)


def _split2(x):
    hi = x.astype(BF16)
    lo = (x - hi.astype(F32)).astype(BF16)
    return hi, lo


def _split3(x):
    hi = x.astype(BF16)
    r = x - hi.astype(F32)
    mid = r.astype(BF16)
    lo = (r - mid.astype(F32)).astype(BF16)
    return hi, mid, lo


def _bias_lane(hh):
    return (1 - hh) * HEAD_DIM


def _in_proj_kernel(x_ref, g_ref, w_ref, wf_ref, b_ref, tri_ref, place_ref,
                    o_ref, e_ref, h_ref, carry_ref, *, tiles_per_seq):
    @pl.when(pl.program_id(1) == 0)
    def _():
        h = _rms(x_ref[...], g_ref[...]).astype(BF16)
        h_ref[...] = h

        @pl.when(pl.program_id(0) % tiles_per_seq == 0)
        def _():
            carry_ref[...] = jnp.zeros_like(carry_ref)

        y = _dot(h, wf_ref[...]) + b_ref[...]
        logf = jnp.minimum(y, 0.0) - jnp.log(1.0 + jnp.exp(-jnp.abs(y)))
        pieces = _split3(logf)
        tri = tri_ref[...]
        tc = tri.shape[0]
        cum = carry_ref[...]
        for c in range(h.shape[0] // tc):
            rows = slice(c * tc, (c + 1) * tc)
            cum = cum[-1:, :]
            for piece in pieces:
                cum = cum + _dot(tri, piece[rows])
            bias_pieces = jnp.concatenate(_split3(cum * (-LOG2E)), axis=1)
            e_ref[rows, :] = _dot(bias_pieces, place_ref[...]).astype(BF16)
        carry_ref[...] = cum[-1:, :]

    o_ref[...] = _dot(h_ref[...], w_ref[...]).astype(o_ref.dtype)


def _bias_placement():
    place = np.zeros((N_BIAS_PIECES, LANES, N_HEAD_BLOCKS * LANES), np.float32)
    for h in range(N_HEADS):
        hp, hh = divmod(h, HEADS_PER_BLOCK)
        for i in range(N_BIAS_PIECES):
            place[i, h, hp * LANES + _bias_lane(hh) + i] = 1.0
    return jnp.asarray(place.reshape(N_BIAS_PIECES * LANES, -1), BF16)


def _in_proj(x, g, w, w_f, b_f, s, tm, tn, tc):
    n, d = x.shape
    cols = w.shape[1]
    tri = jnp.asarray(np.tril(np.ones((tc, tc), np.float32)), BF16)
    place = _bias_placement()
    e_cols = place.shape[1]
    const2 = lambda i, j: (0, 0)
    return pl.pallas_call(
        functools.partial(_in_proj_kernel, tiles_per_seq=s // tm),
        grid=(n // tm, cols // tn),
        in_specs=[
            pl.BlockSpec((tm, d), lambda i, j: (i, 0)),
            pl.BlockSpec((1, d), const2),
            pl.BlockSpec((d, tn), lambda i, j: (0, j)),
            pl.BlockSpec((d, LANES), const2),
            pl.BlockSpec((1, LANES), const2),
            pl.BlockSpec((tc, tc), const2),
            pl.BlockSpec(place.shape, const2),
        ],
        out_specs=[pl.BlockSpec((tm, tn), lambda i, j: (i, j)),
                   pl.BlockSpec((tm, e_cols), lambda i, j: (i, 0))],
        out_shape=[jax.ShapeDtypeStruct((n, cols), BF16),
                   jax.ShapeDtypeStruct((n, e_cols), BF16)],
        scratch_shapes=[pltpu.VMEM((tm, d), BF16), pltpu.VMEM((1, LANES), F32)],
        compiler_params=_cparams(("arbitrary", "arbitrary")),
        name="norm_in_proj",
    )(x, g, w, w_f, b_f, tri, place)


def _head_lane_mask(hh):
    lane = lax.broadcasted_iota(jnp.int32, (1, LANES), 1)
    return (lane >= hh * HEAD_DIM) & (lane < (hh + 1) * HEAD_DIM)


def _tile_rel(t):
    return (lax.broadcasted_iota(jnp.int32, (t, t), 0)
            - lax.broadcasted_iota(jnp.int32, (t, t), 1))


def _mixer_specs(mixer, nb, t, s, d):
    qoff = N_BRANCH * d // LANES + (3 * mixer) * N_HEAD_BLOCKS
    koff = qoff + N_HEAD_BLOCKS
    voff = koff + N_HEAD_BLOCKS
    return [
        pl.BlockSpec((nb, t, LANES), lambda bg, hp, qi: (bg, qi, qoff + hp)),
        pl.BlockSpec((nb, s, LANES), lambda bg, hp, qi: (bg, 0, koff + hp)),
        pl.BlockSpec((nb, s, LANES), lambda bg, hp, qi: (bg, 0, voff + hp)),
    ], pl.BlockSpec((nb, t, LANES), lambda bg, hp, qi: (bg, qi, hp))


def _sb_kernel(q_ref, k_ref, v_ref, o_ref, vt_ref, acc_ref, cs_ref, r_ref, s_even_ref, s_odd_ref, *, nb):
    t = q_ref.shape[1]
    n_blocks = k_ref.shape[1] // t
    qi = pl.program_id(2)
    streams = [(bi, hh) for bi in range(nb) for hh in range(HEADS_PER_BLOCK)]

    @pl.when(qi == 0)
    def _build():
        def chunk(c, carry):
            rows = pl.ds(pl.multiple_of(c * t, t), t)
            for st, (bi, hh) in enumerate(streams):
                v2t = v_ref[bi, rows, :].astype(F32).T
                vt_ref[st, :, rows] = v2t[hh * HEAD_DIM:(hh + 1) * HEAD_DIM, :].astype(BF16)
            return carry

        lax.fori_loop(0, n_blocks, chunk, 0)

    q_m = []
    for bi, hh in streams:
        q2 = q_ref[bi]
        q_m.append(jnp.where(_head_lane_mask(hh), q2, jnp.zeros_like(q2)))

    rel = _tile_rel(t)
    strictly_past = rel < 0
    upper = jnp.where(rel <= 0, 1.0, 0.0).astype(BF16)

    def scores(s_ref, j):
        rows = pl.ds(pl.multiple_of(j * t, t), t)
        for st, (bi, hh) in enumerate(streams):
            s_ref[st] = _dot_nt(k_ref[bi, rows, :], q_m[st])

    def accumulate(s_ref, j, diag):
        rows = pl.ds(pl.multiple_of(j * t, t), t)
        for st in range(len(streams)):
            z = s_ref[st]
            sp = jnp.maximum(z, 0.0) + jnp.log(1.0 + jnp.exp(-jnp.abs(z)))
            if diag:
                sp = jnp.where(strictly_past, sp, 0.0)
            hi, lo = _split2(sp)
            r = _dot(upper, hi) + _dot(upper, lo) + cs_ref[st]
            r_ref[st] = r
            cs_ref[st] = r[0:1, :]
        for st in range(len(streams)):
            w = jnp.exp(s_ref[st] - r_ref[st])
            if diag:
                w = jnp.where(strictly_past, w, 0.0)
            acc_ref[st] += _dot(vt_ref[st, :, rows], w.astype(BF16))

    def next_distance(i):
        lowest = cs_ref[0]
        for st in range(1, len(streams)):
            lowest = jnp.minimum(lowest, cs_ref[st])
        return jnp.where(jnp.min(lowest) > SB_EXIT, qi + 1, i + 1)

    acc_ref[...] = jnp.zeros(acc_ref.shape, F32)
    cs_ref[...] = jnp.zeros(cs_ref.shape, F32)
    scores(s_even_ref, qi)
    scores(s_odd_ref, jnp.maximum(qi - 1, 0))
    accumulate(s_even_ref, qi, True)

    def body(i):
        j = qi - i
        j_next = jnp.maximum(j - 1, 0)

        @pl.when(i % 2 == 1)
        def _():
            scores(s_even_ref, j_next)
            accumulate(s_odd_ref, j, False)

        @pl.when(i % 2 == 0)
        def _():
            scores(s_odd_ref, j_next)
            accumulate(s_even_ref, j, False)

        return next_distance(i)

    lax.while_loop(lambda i: i <= qi, body, next_distance(0))

    for bi in range(nb):
        halves = [acc_ref[bi * HEADS_PER_BLOCK + hh] for hh in range(HEADS_PER_BLOCK)]
        o_ref[bi] = jnp.concatenate(halves, axis=0).T.astype(o_ref.dtype)


def _sb_attention(proj3, d):
    b, s, _ = proj3.shape
    t = ATT_TILE
    nb = FLASH_BATCH if b % FLASH_BATCH == 0 else 1
    n_streams = nb * HEADS_PER_BLOCK
    in_specs, out_spec = _mixer_specs(0, nb, t, s, d)
    return pl.pallas_call(
        functools.partial(_sb_kernel, nb=nb),
        grid=(b // nb, N_HEAD_BLOCKS, s // t),
        in_specs=in_specs,
        out_specs=out_spec,
        out_shape=jax.ShapeDtypeStruct((b, s, BRANCH_WIDTH), BF16),
        scratch_shapes=[pltpu.VMEM((n_streams, HEAD_DIM, s), BF16),
                        pltpu.VMEM((n_streams, HEAD_DIM, t), F32),
                        pltpu.VMEM((n_streams, 1, t), F32),
                        pltpu.VMEM((n_streams, t, t), F32),
                        pltpu.VMEM((n_streams, t, t), F32),
                        pltpu.VMEM((n_streams, t, t), F32)],
        compiler_params=_cparams(("parallel", "parallel", "arbitrary")),
        name="stickbreak_attn",
    )(proj3, proj3, proj3)


def _flash_kernel(*refs, moba, nb):
    if moba:
        (q_ref, k_ref, v_ref, o_ref, kaug_ref, vt_ref, acc_ref, m_ref, s_even_ref, s_odd_ref,
         bmax_ref, knorm_ref, kmean_ref, sel_ref) = refs
    else:
        (q_ref, k_ref, v_ref, e_ref, o_ref, kaug_ref, vt_ref, acc_ref, m_ref, s_even_ref,
         s_odd_ref, bmax_ref, knorm_ref) = refs
    t = q_ref.shape[1]
    n_blocks = k_ref.shape[1] // t
    hp = pl.program_id(1)
    qi = pl.program_id(2)
    lane = lax.broadcasted_iota(jnp.int32, (1, LANES), 1)
    streams = [(bi, hh) for bi in range(nb) for hh in range(HEADS_PER_BLOCK)]

    def piece_lanes(hh):
        e0 = _bias_lane(hh)
        return (lane >= e0) & (lane < e0 + N_BIAS_PIECES)

    head_of_lane = jnp.where(lax.broadcasted_iota(jnp.int32, (LANES, LANES), 0) >= HEAD_DIM, 1, 0)
    head_sum = jnp.where(head_of_lane == lax.broadcasted_iota(jnp.int32, (LANES, LANES), 1), 1.0, 0.0).astype(BF16)

    def head_sq_norm_max(x32):
        hi, lo = _split2(x32 * x32)
        return jnp.max(_dot(hi, head_sum) + _dot(lo, head_sum), axis=0, keepdims=True)

    def head_col(row, hh):
        return jnp.sum(jnp.where(lane == hh, row, 0.0), axis=1, keepdims=True)

    @pl.when(qi == 0)
    def _build():
        bmax_ref[...] = jnp.full(bmax_ref.shape, NEG_INF, F32)
        knorm_ref[...] = jnp.zeros(knorm_ref.shape, F32)
        tail = jnp.where(lax.broadcasted_iota(jnp.int32, (V_ROWS - HEAD_DIM, t), 0) == 0, 1.0, 0.0)

        def chunk(c, carry):
            start = pl.multiple_of(c * t, t)
            rows = pl.ds(start, t)
            extras = []
            if moba:
                pos = (lax.broadcasted_iota(jnp.int32, (t, LANES), 0) + start).astype(F32)
                for hh in range(HEADS_PER_BLOCK):
                    h = hp * HEADS_PER_BLOCK + hh
                    slope = pltpu.bitcast(jnp.full((1, LANES), (126 - h) << 23, jnp.int32), F32)
                    extra = jnp.zeros((t, LANES), F32)
                    for i, piece in enumerate(_split3(pos * slope * LOG2E)):
                        extra = jnp.where(lane == _bias_lane(hh) + i, piece.astype(F32), extra)
                    extras.append(extra.astype(BF16))
            for bi in range(nb):
                k2 = k_ref[bi, rows, :]
                k32 = k2.astype(F32)
                v2t = v_ref[bi, rows, :].astype(F32).T
                if moba:
                    kmean_ref[bi, pl.ds(c, 1), :] = jnp.sum(k32, axis=0, keepdims=True) * (1.0 / t)
                for hh in range(HEADS_PER_BLOCK):
                    st = bi * HEADS_PER_BLOCK + hh
                    extra = extras[hh] if moba else e_ref[bi, rows, :]
                    kaug_ref[st, rows, :] = jnp.where(_head_lane_mask(hh), k2, extra)
                    v_h = v2t[hh * HEAD_DIM:(hh + 1) * HEAD_DIM, :]
                    vt_ref[st, :, rows] = jnp.concatenate([v_h, tail], axis=0).astype(BF16)
                    piece_max = jnp.max(extra.astype(F32), axis=0, keepdims=True)
                    bias_max = jnp.sum(jnp.where(piece_lanes(hh), piece_max, 0.0), axis=1, keepdims=True)
                    run = jnp.maximum(bmax_ref[st, pl.ds(jnp.maximum(c - 1, 0), 1), :], bias_max)
                    bmax_ref[st, pl.ds(c, 1), :] = run
                knorm_ref[bi] = jnp.maximum(knorm_ref[bi], head_sq_norm_max(k32))
            return carry

        lax.fori_loop(0, n_blocks, chunk, 0)

    q_aug = []
    reach = []
    for bi in range(nb):
        q2 = q_ref[bi]
        qk_sq = head_sq_norm_max(q2.astype(F32)) * knorm_ref[bi]
        for hh in range(HEADS_PER_BLOCK):
            ones = jnp.where(piece_lanes(hh), 1.0, 0.0).astype(BF16)
            q_aug.append(jnp.where(_head_lane_mask(hh), q2, jnp.broadcast_to(ones, q2.shape)))
            reach.append(jnp.sqrt(head_col(qk_sq, hh)) * NORM_SLACK)

    if moba:
        blk = lax.broadcasted_iota(jnp.int32, (n_blocks, 1), 0)
        for st, (bi, hh) in enumerate(streams):
            km = _split3(kmean_ref[bi])
            q2 = q_ref[bi]
            qh = jnp.where(_head_lane_mask(hh), q2, jnp.zeros_like(q2))
            route = _dot_nt(km[0], qh) + _dot_nt(km[1], qh) + _dot_nt(km[2], qh)
            rank = jnp.zeros((n_blocks, t), F32)
            for n in range(n_blocks):
                row = route[n:n + 1, :]
                beats = (row > route) | ((row == route) & (blk > n))
                rank = rank + jnp.where(beats, jnp.where(qi > n, 1.0, 0.0), 0.0)
            sel_ref[st] = jnp.where((rank < MOBA_TOPK) & (blk < qi), 1.0, 0.0)

    valid = _tile_rel(t) <= 0

    def scores(s_ref, j):
        rows = pl.ds(pl.multiple_of(j * t, t), t)
        for st in range(len(streams)):
            s_ref[st] = _dot_nt(kaug_ref[st, rows, :], q_aug[st])

    def accumulate(s_ref, j, diag):
        rows = pl.ds(pl.multiple_of(j * t, t), t)
        for st in range(len(streams)):
            s_t = s_ref[st]
            if diag:
                s_t = jnp.where(valid, s_t, NEG_INF)
            m_old = m_ref[st]
            m_new = jnp.maximum(m_old, jnp.max(s_t, axis=0, keepdims=True))
            m_sub = m_new
            if moba and not diag:
                picked = sel_ref[st, pl.ds(j, 1), :] > 0.0
                m_new = jnp.where(picked, m_new, m_old)
                m_sub = jnp.where(picked, m_new, -NEG_INF)
            alpha = jnp.exp2(m_old - m_new)
            p = jnp.exp2(s_t - m_sub)
            acc_ref[st] = alpha * acc_ref[st] + _dot(vt_ref[st, :, rows], p.astype(BF16))
            m_ref[st] = m_new

    m_ref[...] = jnp.full(m_ref.shape, NEG_INF, F32)
    acc_ref[...] = jnp.zeros(acc_ref.shape, F32)
    scores(s_even_ref, qi)
    scores(s_odd_ref, jnp.maximum(qi - 1, 0))
    accumulate(s_even_ref, qi, True)

    blk_col = lax.broadcasted_iota(jnp.int32, (n_blocks, 1), 0)
    fewest_dead = None
    for st in range(len(streams)):
        thr = jnp.min(m_ref[st], axis=1, keepdims=True) - FLASH_SKIP - reach[st]
        dead = (bmax_ref[st] < thr) & (blk_col < qi)
        n_dead = jnp.sum(jnp.where(dead, 1.0, 0.0), axis=0, keepdims=True)
        fewest_dead = n_dead if fewest_dead is None else jnp.minimum(fewest_dead, n_dead)
    n_past = qi - jnp.max(fewest_dead).astype(jnp.int32)

    def body(i, carry):
        j = qi - i
        j_next = jnp.maximum(j - 1, 0)

        @pl.when(i % 2 == 1)
        def _():
            scores(s_even_ref, j_next)
            accumulate(s_odd_ref, j, False)

        @pl.when(i % 2 == 0)
        def _():
            scores(s_odd_ref, j_next)
            accumulate(s_even_ref, j, False)

        return carry

    lax.fori_loop(1, n_past + 1, body, 0)

    for bi in range(nb):
        halves = []
        for hh in range(HEADS_PER_BLOCK):
            acc = acc_ref[bi * HEADS_PER_BLOCK + hh]
            halves.append(acc[:HEAD_DIM] * (1.0 / acc[HEAD_DIM:HEAD_DIM + 1]))
        o_ref[bi] = jnp.concatenate(halves, axis=0).T.astype(o_ref.dtype)


def _flash_attention(proj3, key_bias3, d, moba):
    b, s, _ = proj3.shape
    t = ATT_TILE
    nb = FLASH_BATCH if b % FLASH_BATCH == 0 else 1
    n_blocks = s // t
    n_streams = nb * HEADS_PER_BLOCK
    in_specs, out_spec = _mixer_specs(2 if moba else 1, nb, t, s, d)
    operands = [proj3, proj3, proj3]
    scratch = [pltpu.VMEM((n_streams, s, LANES), BF16),
               pltpu.VMEM((n_streams, V_ROWS, s), BF16),
               pltpu.VMEM((n_streams, V_ROWS, t), F32),
               pltpu.VMEM((n_streams, 1, t), F32),
               pltpu.VMEM((n_streams, t, t), F32),
               pltpu.VMEM((n_streams, t, t), F32),
               pltpu.VMEM((n_streams, n_blocks, 1), F32),
               pltpu.VMEM((nb, 1, LANES), F32)]
    if moba:
        scratch += [pltpu.VMEM((nb, n_blocks, LANES), F32),
                    pltpu.VMEM((n_streams, n_blocks, t), F32)]
    else:
        in_specs.append(pl.BlockSpec((nb, s, LANES), lambda bg, hp, qi: (bg, 0, hp)))
        operands.append(key_bias3)
    return pl.pallas_call(
        functools.partial(_flash_kernel, moba=moba, nb=nb),
        grid=(b // nb, N_HEAD_BLOCKS, n_blocks),
        in_specs=in_specs,
        out_specs=out_spec,
        out_shape=jax.ShapeDtypeStruct((b, s, BRANCH_WIDTH), BF16),
        scratch_shapes=scratch,
        compiler_params=_cparams(("parallel", "parallel", "arbitrary")),
        name="moba_attn" if moba else "forgetting_attn",
    )(*operands)


def _mix_kernel(x_ref, oa_ref, ob_ref, oc_ref, g_ref, wb_ref, wo_ref, y_ref):
    d = x_ref.shape[1]
    mixed = None
    for n, o_ref in enumerate((oa_ref, ob_ref, oc_ref)):
        gate = jax.nn.sigmoid(g_ref[:, n * d:(n + 1) * d].astype(F32))
        term = gate * _dot(o_ref[...], wb_ref[n])
        mixed = term if mixed is None else mixed + term
    y_ref[...] = x_ref[...] + _dot(mixed.astype(BF16), wo_ref[...])


def _mix(x, o_a, o_b, o_c, proj, wb, wo, tm):
    n, d = x.shape
    o_spec = pl.BlockSpec((tm, BRANCH_WIDTH), lambda i: (i, 0))
    return pl.pallas_call(
        _mix_kernel,
        grid=(n // tm,),
        in_specs=[
            pl.BlockSpec((tm, d), lambda i: (i, 0)),
            o_spec, o_spec, o_spec,
            pl.BlockSpec((tm, N_BRANCH * d), lambda i: (i, 0)),
            pl.BlockSpec((N_BRANCH, BRANCH_WIDTH, d), lambda i: (0, 0, 0)),
            pl.BlockSpec((d, d), lambda i: (0, 0)),
        ],
        out_specs=pl.BlockSpec((tm, d), lambda i: (i, 0)),
        out_shape=jax.ShapeDtypeStruct((n, d), F32),
        compiler_params=_cparams(("parallel",)),
        name="gated_mix_out_proj",
    )(x, o_a, o_b, o_c, proj, wb, wo)


def _mlp_kernel(x_ref, g_ref, wu_ref, wd_ref, gf_ref, y_ref, h_ref, acc_ref, *, final_norm):
    f = pl.program_id(1)

    @pl.when(f == 0)
    def _():
        h_ref[...] = _rms(x_ref[...], g_ref[...]).astype(BF16)
        acc_ref[...] = jnp.zeros_like(acc_ref)

    hid = jnp.square(jnp.maximum(_dot(h_ref[...], wu_ref[...]), 0.0))
    acc_ref[...] += _dot(hid.astype(BF16), wd_ref[...])

    @pl.when(f == pl.num_programs(1) - 1)
    def _():
        y = x_ref[...] + acc_ref[...]
        if final_norm:
            y = _rms(y, gf_ref[...])
        y_ref[...] = y


def _mlp(x, g, wu, wd, g_final, final_norm, tm, tf):
    n, d = x.shape
    d_ff = wu.shape[1]
    return pl.pallas_call(
        functools.partial(_mlp_kernel, final_norm=final_norm),
        grid=(n // tm, d_ff // tf),
        in_specs=[
            pl.BlockSpec((tm, d), lambda i, f: (i, 0)),
            pl.BlockSpec((1, d), lambda i, f: (0, 0)),
            pl.BlockSpec((d, tf), lambda i, f: (0, f)),
            pl.BlockSpec((tf, d), lambda i, f: (f, 0)),
            pl.BlockSpec((1, d), lambda i, f: (0, 0)),
        ],
        out_specs=pl.BlockSpec((tm, d), lambda i, f: (i, 0)),
        out_shape=jax.ShapeDtypeStruct((n, d), F32),
        scratch_shapes=[pltpu.VMEM((tm, d), BF16), pltpu.VMEM((tm, d), F32)],
        compiler_params=_cparams(("parallel", "arbitrary")),
        name="relu2_mlp",
    )(x, g, wu, wd, g_final)


def _prep_in_proj(w_in):
    bw = BRANCH_WIDTH
    scale = HEAD_DIM ** -0.5
    q_scales = (scale, scale * LOG2E, scale * LOG2E)
    f_lo = 6 * bw
    f_hi = f_lo + N_HEADS
    g_lo = 9 * bw + N_HEADS
    pieces = [w_in[:, :, g_lo:]]
    for m in range(N_BRANCH):
        base = 3 * m * bw + (N_HEADS if m == 2 else 0)
        pieces += [w_in[:, :, base:base + bw] * q_scales[m],
                   w_in[:, :, base + bw:base + 3 * bw]]
    w_main = jnp.concatenate([p.astype(BF16) for p in pieces], axis=-1)
    w_f = jnp.pad(w_in[:, :, f_lo:f_hi], ((0, 0), (0, 0), (0, LANES - N_HEADS)))
    return w_main, w_f.astype(BF16)


def kernel(x, norm_mix, w_in, b_forget, w_branch, w_out, norm_mlp, w_up, w_down, norm_final):
    b, s, d = x.shape
    depth = w_in.shape[0]
    n = b * s
    assert s % ATT_TILE == 0 and d % LANES == 0

    w_main, w_f = _prep_in_proj(w_in)
    wb = w_branch.astype(BF16)
    wo = w_out.astype(BF16)
    wu = w_up.astype(BF16)
    wd = w_down.astype(BF16)
    b_f = jnp.pad(b_forget, ((0, 0), (0, LANES - N_HEADS)))[:, None, :]

    tm = min(1024, s)
    xf = x.reshape(n, d)
    for l in range(depth):
        g_mix = norm_mix[l][None, :]
        proj, key_bias = _in_proj(xf, g_mix, w_main[l], w_f[l], b_f[l], s, tm, 1536, 256)
        proj3 = proj.reshape(b, s, -1)
        o_a = _sb_attention(proj3, d).reshape(n, -1)
        o_b = _flash_attention(proj3, key_bias.reshape(b, s, -1), d, False).reshape(n, -1)
        o_c = _flash_attention(proj3, None, d, True).reshape(n, -1)
        xf = _mix(xf, o_a, o_b, o_c, proj, wb[l], wo[l], min(512, n))
        xf = _mlp(xf, norm_mlp[l][None, :], wu[l], wd[l], norm_final[None, :],
                  l == depth - 1, min(512, n), 1024)
    return xf.reshape(b, s, d)
```

```python
import functools
import math

import jax
import jax.numpy as jnp
import numpy as np
from jax import lax
from jax.experimental import pallas as pl
from jax.experimental.pallas import tpu as pltpu

F32 = jnp.float32
BF16 = jnp.bfloat16

HEAD_DIM = 64
N_HEADS = 8
BRANCH_WIDTH = N_HEADS * HEAD_DIM
N_BRANCH = 3
MOBA_BLOCK = 256
MOBA_TOPK = 3
RMS_EPS = 1e-6
NEG_INF = -1e30
LOG2E = math.log2(math.e)
LANES = 128
HEADS_PER_BLOCK = LANES // HEAD_DIM
N_HEAD_BLOCKS = BRANCH_WIDTH // LANES
N_BIAS_PIECES = 3
ATT_TILE = 256
SB_EXIT = 110.0
FLASH_SKIP = 160.0
NORM_SLACK = 1.001
FLASH_BATCH = 4
V_ROWS = HEAD_DIM + 16
VMEM_LIMIT = 56 * 1024 * 1024


def _cparams(sem):
    return pltpu.CompilerParams(dimension_semantics=sem, vmem_limit_bytes=VMEM_LIMIT)


def _rms(x, g):
    ms = jnp.mean(x * x, axis=-1, keepdims=True)
    return x * lax.rsqrt(ms + RMS_EPS) * g


def _dot(a, b):
    return jnp.dot(a, b, preferred_element_type=F32)


def _dot_nt(a, b):
    return lax.dot_general(a, b, (((1,), (1,)), ((), ())), preferred_element_type=F32)


def _split2(x):
    hi = x.astype(BF16)
    lo = (x - hi.astype(F32)).astype(BF16)
    return hi, lo


def _split3(x):
    hi = x.astype(BF16)
    r = x - hi.astype(F32)
    mid = r.astype(BF16)
    lo = (r - mid.astype(F32)).astype(BF16)
    return hi, mid, lo


def _bias_lane(hh):
    return (1 - hh) * HEAD_DIM


def _in_proj_kernel(x_ref, g_ref, w_ref, wf_ref, b_ref, tri_ref, place_ref,
                    o_ref, e_ref, h_ref, carry_ref, *, tiles_per_seq):
    @pl.when(pl.program_id(1) == 0)
    def _():
        h = _rms(x_ref[...], g_ref[...]).astype(BF16)
        h_ref[...] = h

        @pl.when(pl.program_id(0) % tiles_per_seq == 0)
        def _():
            carry_ref[...] = jnp.zeros_like(carry_ref)

        y = _dot(h, wf_ref[...]) + b_ref[...]
        logf = jnp.minimum(y, 0.0) - jnp.log(1.0 + jnp.exp(-jnp.abs(y)))
        pieces = _split3(logf)
        tri = tri_ref[...]
        tc = tri.shape[0]
        cum = carry_ref[...]
        for c in range(h.shape[0] // tc):
            rows = slice(c * tc, (c + 1) * tc)
            cum = cum[-1:, :]
            for piece in pieces:
                cum = cum + _dot(tri, piece[rows])
            bias_pieces = jnp.concatenate(_split3(cum * (-LOG2E)), axis=1)
            e_ref[rows, :] = _dot(bias_pieces, place_ref[...]).astype(BF16)
        carry_ref[...] = cum[-1:, :]

    o_ref[...] = _dot(h_ref[...], w_ref[...]).astype(o_ref.dtype)


def _bias_placement():
    place = np.zeros((N_BIAS_PIECES, LANES, N_HEAD_BLOCKS * LANES), np.float32)
    for h in range(N_HEADS):
        hp, hh = divmod(h, HEADS_PER_BLOCK)
        for i in range(N_BIAS_PIECES):
            place[i, h, hp * LANES + _bias_lane(hh) + i] = 1.0
    return jnp.asarray(place.reshape(N_BIAS_PIECES * LANES, -1), BF16)


def _in_proj(x, g, w, w_f, b_f, s, tm, tn, tc):
    n, d = x.shape
    cols = w.shape[1]
    tri = jnp.asarray(np.tril(np.ones((tc, tc), np.float32)), BF16)
    place = _bias_placement()
    e_cols = place.shape[1]
    const2 = lambda i, j: (0, 0)
    return pl.pallas_call(
        functools.partial(_in_proj_kernel, tiles_per_seq=s // tm),
        grid=(n // tm, cols // tn),
        in_specs=[
            pl.BlockSpec((tm, d), lambda i, j: (i, 0)),
            pl.BlockSpec((1, d), const2),
            pl.BlockSpec((d, tn), lambda i, j: (0, j)),
            pl.BlockSpec((d, LANES), const2),
            pl.BlockSpec((1, LANES), const2),
            pl.BlockSpec((tc, tc), const2),
            pl.BlockSpec(place.shape, const2),
        ],
        out_specs=[pl.BlockSpec((tm, tn), lambda i, j: (i, j)),
                   pl.BlockSpec((tm, e_cols), lambda i, j: (i, 0))],
        out_shape=[jax.ShapeDtypeStruct((n, cols), BF16),
                   jax.ShapeDtypeStruct((n, e_cols), BF16)],
        scratch_shapes=[pltpu.VMEM((tm, d), BF16), pltpu.VMEM((1, LANES), F32)],
        compiler_params=_cparams(("arbitrary", "arbitrary")),
        name="norm_in_proj",
    )(x, g, w, w_f, b_f, tri, place)


def _head_lane_mask(hh):
    lane = lax.broadcasted_iota(jnp.int32, (1, LANES), 1)
    return (lane >= hh * HEAD_DIM) & (lane < (hh + 1) * HEAD_DIM)


def _tile_rel(t):
    return (lax.broadcasted_iota(jnp.int32, (t, t), 0)
            - lax.broadcasted_iota(jnp.int32, (t, t), 1))


def _mixer_specs(mixer, nb, t, s, d):
    qoff = N_BRANCH * d // LANES + (3 * mixer) * N_HEAD_BLOCKS
    koff = qoff + N_HEAD_BLOCKS
    voff = koff + N_HEAD_BLOCKS
    return [
        pl.BlockSpec((nb, t, LANES), lambda bg, hp, qi: (bg, qi, qoff + hp)),
        pl.BlockSpec((nb, s, LANES), lambda bg, hp, qi: (bg, 0, koff + hp)),
        pl.BlockSpec((nb, s, LANES), lambda bg, hp, qi: (bg, 0, voff + hp)),
    ], pl.BlockSpec((nb, t, LANES), lambda bg, hp, qi: (bg, qi, hp))


def _sb_kernel(q_ref, k_ref, v_ref, o_ref, vt_ref, acc_ref, cs_ref, r_ref, s_even_ref, s_odd_ref, *, nb):
    t = q_ref.shape[1]
    n_blocks = k_ref.shape[1] // t
    qi = pl.program_id(2)
    streams = [(bi, hh) for bi in range(nb) for hh in range(HEADS_PER_BLOCK)]

    @pl.when(qi == 0)
    def _build():
        def chunk(c, carry):
            rows = pl.ds(pl.multiple_of(c * t, t), t)
            for st, (bi, hh) in enumerate(streams):
                v2t = v_ref[bi, rows, :].astype(F32).T
                vt_ref[st, :, rows] = v2t[hh * HEAD_DIM:(hh + 1) * HEAD_DIM, :].astype(BF16)
            return carry

        lax.fori_loop(0, n_blocks, chunk, 0)

    q_m = []
    for bi, hh in streams:
        q2 = q_ref[bi]
        q_m.append(jnp.where(_head_lane_mask(hh), q2, jnp.zeros_like(q2)))

    rel = _tile_rel(t)
    strictly_past = rel < 0
    upper = jnp.where(rel <= 0, 1.0, 0.0).astype(BF16)

    def scores(s_ref, j):
        rows = pl.ds(pl.multiple_of(j * t, t), t)
        for st, (bi, hh) in enumerate(streams):
            s_ref[st] = _dot_nt(k_ref[bi, rows, :], q_m[st])

    def accumulate(s_ref, j, diag):
        rows = pl.ds(pl.multiple_of(j * t, t), t)
        for st in range(len(streams)):
            z = s_ref[st]
            sp = jnp.maximum(z, 0.0) + jnp.log(1.0 + jnp.exp(-jnp.abs(z)))
            if diag:
                sp = jnp.where(strictly_past, sp, 0.0)
            hi, lo = _split2(sp)
            r = _dot(upper, hi) + _dot(upper, lo) + cs_ref[st]
            r_ref[st] = r
            cs_ref[st] = r[0:1, :]
        for st in range(len(streams)):
            w = jnp.exp(s_ref[st] - r_ref[st])
            if diag:
                w = jnp.where(strictly_past, w, 0.0)
            acc_ref[st] += _dot(vt_ref[st, :, rows], w.astype(BF16))

    def next_distance(i):
        lowest = cs_ref[0]
        for st in range(1, len(streams)):
            lowest = jnp.minimum(lowest, cs_ref[st])
        return jnp.where(jnp.min(lowest) > SB_EXIT, qi + 1, i + 1)

    acc_ref[...] = jnp.zeros(acc_ref.shape, F32)
    cs_ref[...] = jnp.zeros(cs_ref.shape, F32)
    scores(s_even_ref, qi)
    scores(s_odd_ref, jnp.maximum(qi - 1, 0))
    accumulate(s_even_ref, qi, True)

    def body(i):
        j = qi - i
        j_next = jnp.maximum(j - 1, 0)

        @pl.when(i % 2 == 1)
        def _():
            scores(s_even_ref, j_next)
            accumulate(s_odd_ref, j, False)

        @pl.when(i % 2 == 0)
        def _():
            scores(s_odd_ref, j_next)
            accumulate(s_even_ref, j, False)

        return next_distance(i)

    lax.while_loop(lambda i: i <= qi, body, next_distance(0))

    for bi in range(nb):
        halves = [acc_ref[bi * HEADS_PER_BLOCK + hh] for hh in range(HEADS_PER_BLOCK)]
        o_ref[bi] = jnp.concatenate(halves, axis=0).T.astype(o_ref.dtype)


def _sb_attention(proj3, d):
    b, s, _ = proj3.shape
    t = ATT_TILE
    nb = FLASH_BATCH if b % FLASH_BATCH == 0 else 1
    n_streams = nb * HEADS_PER_BLOCK
    in_specs, out_spec = _mixer_specs(0, nb, t, s, d)
    return pl.pallas_call(
        functools.partial(_sb_kernel, nb=nb),
        grid=(b // nb, N_HEAD_BLOCKS, s // t),
        in_specs=in_specs,
        out_specs=out_spec,
        out_shape=jax.ShapeDtypeStruct((b, s, BRANCH_WIDTH), BF16),
        scratch_shapes=[pltpu.VMEM((n_streams, HEAD_DIM, s), BF16),
                        pltpu.VMEM((n_streams, HEAD_DIM, t), F32),
                        pltpu.VMEM((n_streams, 1, t), F32),
                        pltpu.VMEM((n_streams, t, t), F32),
                        pltpu.VMEM((n_streams, t, t), F32),
                        pltpu.VMEM((n_streams, t, t), F32)],
        compiler_params=_cparams(("parallel", "parallel", "arbitrary")),
        name="stickbreak_attn",
    )(proj3, proj3, proj3)


def _flash_kernel(*refs, moba, nb):
    if moba:
        (q_ref, k_ref, v_ref, o_ref, kaug_ref, vt_ref, acc_ref, m_ref, s_even_ref, s_odd_ref,
         bmax_ref, knorm_ref, kmean_ref, sel_ref) = refs
    else:
        (q_ref, k_ref, v_ref, e_ref, o_ref, kaug_ref, vt_ref, acc_ref, m_ref, s_even_ref,
         s_odd_ref, bmax_ref, knorm_ref) = refs
    t = q_ref.shape[1]
    n_blocks = k_ref.shape[1] // t
    hp = pl.program_id(1)
    qi = pl.program_id(2)
    lane = lax.broadcasted_iota(jnp.int32, (1, LANES), 1)
    streams = [(bi, hh) for bi in range(nb) for hh in range(HEADS_PER_BLOCK)]

    def piece_lanes(hh):
        e0 = _bias_lane(hh)
        return (lane >= e0) & (lane < e0 + N_BIAS_PIECES)

    head_of_lane = jnp.where(lax.broadcasted_iota(jnp.int32, (LANES, LANES), 0) >= HEAD_DIM, 1, 0)
    head_sum = jnp.where(head_of_lane == lax.broadcasted_iota(jnp.int32, (LANES, LANES), 1), 1.0, 0.0).astype(BF16)

    def head_sq_norm_max(x32):
        hi, lo = _split2(x32 * x32)
        return jnp.max(_dot(hi, head_sum) + _dot(lo, head_sum), axis=0, keepdims=True)

    def head_col(row, hh):
        return jnp.sum(jnp.where(lane == hh, row, 0.0), axis=1, keepdims=True)

    @pl.when(qi == 0)
    def _build():
        bmax_ref[...] = jnp.full(bmax_ref.shape, NEG_INF, F32)
        knorm_ref[...] = jnp.zeros(knorm_ref.shape, F32)
        tail = jnp.where(lax.broadcasted_iota(jnp.int32, (V_ROWS - HEAD_DIM, t), 0) == 0, 1.0, 0.0)

        def chunk(c, carry):
            start = pl.multiple_of(c * t, t)
            rows = pl.ds(start, t)
            extras = []
            if moba:
                pos = (lax.broadcasted_iota(jnp.int32, (t, LANES), 0) + start).astype(F32)
                for hh in range(HEADS_PER_BLOCK):
                    h = hp * HEADS_PER_BLOCK + hh
                    slope = pltpu.bitcast(jnp.full((1, LANES), (126 - h) << 23, jnp.int32), F32)
                    extra = jnp.zeros((t, LANES), F32)
                    for i, piece in enumerate(_split3(pos * slope * LOG2E)):
                        extra = jnp.where(lane == _bias_lane(hh) + i, piece.astype(F32), extra)
                    extras.append(extra.astype(BF16))
            for bi in range(nb):
                k2 = k_ref[bi, rows, :]
                k32 = k2.astype(F32)
                v2t = v_ref[bi, rows, :].astype(F32).T
                if moba:
                    kmean_ref[bi, pl.ds(c, 1), :] = jnp.sum(k32, axis=0, keepdims=True) * (1.0 / t)
                for hh in range(HEADS_PER_BLOCK):
                    st = bi * HEADS_PER_BLOCK + hh
                    extra = extras[hh] if moba else e_ref[bi, rows, :]
                    kaug_ref[st, rows, :] = jnp.where(_head_lane_mask(hh), k2, extra)
                    v_h = v2t[hh * HEAD_DIM:(hh + 1) * HEAD_DIM, :]
                    vt_ref[st, :, rows] = jnp.concatenate([v_h, tail], axis=0).astype(BF16)
                    piece_max = jnp.max(extra.astype(F32), axis=0, keepdims=True)
                    bias_max = jnp.sum(jnp.where(piece_lanes(hh), piece_max, 0.0), axis=1, keepdims=True)
                    run = jnp.maximum(bmax_ref[st, pl.ds(jnp.maximum(c - 1, 0), 1), :], bias_max)
                    bmax_ref[st, pl.ds(c, 1), :] = run
                knorm_ref[bi] = jnp.maximum(knorm_ref[bi], head_sq_norm_max(k32))
            return carry

        lax.fori_loop(0, n_blocks, chunk, 0)

    q_aug = []
    reach = []
    for bi in range(nb):
        q2 = q_ref[bi]
        qk_sq = head_sq_norm_max(q2.astype(F32)) * knorm_ref[bi]
        for hh in range(HEADS_PER_BLOCK):
            ones = jnp.where(piece_lanes(hh), 1.0, 0.0).astype(BF16)
            q_aug.append(jnp.where(_head_lane_mask(hh), q2, jnp.broadcast_to(ones, q2.shape)))
            reach.append(jnp.sqrt(head_col(qk_sq, hh)) * NORM_SLACK)

    if moba:
        blk = lax.broadcasted_iota(jnp.int32, (n_blocks, 1), 0)
        for st, (bi, hh) in enumerate(streams):
            km = _split3(kmean_ref[bi])
            q2 = q_ref[bi]
            qh = jnp.where(_head_lane_mask(hh), q2, jnp.zeros_like(q2))
            route = _dot_nt(km[0], qh) + _dot_nt(km[1], qh) + _dot_nt(km[2], qh)
            rank = jnp.zeros((n_blocks, t), F32)
            for n in range(n_blocks):
                row = route[n:n + 1, :]
                beats = (row > route) | ((row == route) & (blk > n))
                rank = rank + jnp.where(beats, jnp.where(qi > n, 1.0, 0.0), 0.0)
            sel_ref[st] = jnp.where((rank < MOBA_TOPK) & (blk < qi), 1.0, 0.0)

    valid = _tile_rel(t) <= 0

    def scores(s_ref, j):
        rows = pl.ds(pl.multiple_of(j * t, t), t)
        for st in range(len(streams)):
            s_ref[st] = _dot_nt(kaug_ref[st, rows, :], q_aug[st])

    def accumulate(s_ref, j, diag):
        rows = pl.ds(pl.multiple_of(j * t, t), t)
        for st in range(len(streams)):
            s_t = s_ref[st]
            if diag:
                s_t = jnp.where(valid, s_t, NEG_INF)
            m_old = m_ref[st]
            m_new = jnp.maximum(m_old, jnp.max(s_t, axis=0, keepdims=True))
            m_sub = m_new
            if moba and not diag:
                picked = sel_ref[st, pl.ds(j, 1), :] > 0.0
                m_new = jnp.where(picked, m_new, m_old)
                m_sub = jnp.where(picked, m_new, -NEG_INF)
            alpha = jnp.exp2(m_old - m_new)
            p = jnp.exp2(s_t - m_sub)
            acc_ref[st] = alpha * acc_ref[st] + _dot(vt_ref[st, :, rows], p.astype(BF16))
            m_ref[st] = m_new

    m_ref[...] = jnp.full(m_ref.shape, NEG_INF, F32)
    acc_ref[...] = jnp.zeros(acc_ref.shape, F32)
    scores(s_even_ref, qi)
    scores(s_odd_ref, jnp.maximum(qi - 1, 0))
    accumulate(s_even_ref, qi, True)

    blk_col = lax.broadcasted_iota(jnp.int32, (n_blocks, 1), 0)
    fewest_dead = None
    for st in range(len(streams)):
        thr = jnp.min(m_ref[st], axis=1, keepdims=True) - FLASH_SKIP - reach[st]
        dead = (bmax_ref[st] < thr) & (blk_col < qi)
        n_dead = jnp.sum(jnp.where(dead, 1.0, 0.0), axis=0, keepdims=True)
        fewest_dead = n_dead if fewest_dead is None else jnp.minimum(fewest_dead, n_dead)
    n_past = qi - jnp.max(fewest_dead).astype(jnp.int32)

    def body(i, carry):
        j = qi - i
        j_next = jnp.maximum(j - 1, 0)

        @pl.when(i % 2 == 1)
        def _():
            scores(s_even_ref, j_next)
            accumulate(s_odd_ref, j, False)

        @pl.when(i % 2 == 0)
        def _():
            scores(s_odd_ref, j_next)
            accumulate(s_even_ref, j, False)

        return carry

    lax.fori_loop(1, n_past + 1, body, 0)

    for bi in range(nb):
        halves = []
        for hh in range(HEADS_PER_BLOCK):
            acc = acc_ref[bi * HEADS_PER_BLOCK + hh]
            halves.append(acc[:HEAD_DIM] * (1.0 / acc[HEAD_DIM:HEAD_DIM + 1]))
        o_ref[bi] = jnp.concatenate(halves, axis=0).T.astype(o_ref.dtype)


def _flash_attention(proj3, key_bias3, d, moba):
    b, s, _ = proj3.shape
    t = ATT_TILE
    nb = FLASH_BATCH if b % FLASH_BATCH == 0 else 1
    n_blocks = s // t
    n_streams = nb * HEADS_PER_BLOCK
    in_specs, out_spec = _mixer_specs(2 if moba else 1, nb, t, s, d)
    operands = [proj3, proj3, proj3]
    scratch = [pltpu.VMEM((n_streams, s, LANES), BF16),
               pltpu.VMEM((n_streams, V_ROWS, s), BF16),
               pltpu.VMEM((n_streams, V_ROWS, t), F32),
               pltpu.VMEM((n_streams, 1, t), F32),
               pltpu.VMEM((n_streams, t, t), F32),
               pltpu.VMEM((n_streams, t, t), F32),
               pltpu.VMEM((n_streams, n_blocks, 1), F32),
               pltpu.VMEM((nb, 1, LANES), F32)]
    if moba:
        scratch += [pltpu.VMEM((nb, n_blocks, LANES), F32),
                    pltpu.VMEM((n_streams, n_blocks, t), F32)]
    else:
        in_specs.append(pl.BlockSpec((nb, s, LANES), lambda bg, hp, qi: (bg, 0, hp)))
        operands.append(key_bias3)
    return pl.pallas_call(
        functools.partial(_flash_kernel, moba=moba, nb=nb),
        grid=(b // nb, N_HEAD_BLOCKS, n_blocks),
        in_specs=in_specs,
        out_specs=out_spec,
        out_shape=jax.ShapeDtypeStruct((b, s, BRANCH_WIDTH), BF16),
        scratch_shapes=scratch,
        compiler_params=_cparams(("parallel", "parallel", "arbitrary")),
        name="moba_attn" if moba else "forgetting_attn",
    )(*operands)


def _mix_kernel(x_ref, oa_ref, ob_ref, oc_ref, g_ref, wb_ref, wo_ref, y_ref):
    d = x_ref.shape[1]
    mixed = None
    for n, o_ref in enumerate((oa_ref, ob_ref, oc_ref)):
        gate = jax.nn.sigmoid(g_ref[:, n * d:(n + 1) * d].astype(F32))
        term = gate * _dot(o_ref[...], wb_ref[n])
        mixed = term if mixed is None else mixed + term
    y_ref[...] = x_ref[...] + _dot(mixed.astype(BF16), wo_ref[...])


def _mix(x, o_a, o_b, o_c, proj, wb, wo, tm):
    n, d = x.shape
    o_spec = pl.BlockSpec((tm, BRANCH_WIDTH), lambda i: (i, 0))
    return pl.pallas_call(
        _mix_kernel,
        grid=(n // tm,),
        in_specs=[
            pl.BlockSpec((tm, d), lambda i: (i, 0)),
            o_spec, o_spec, o_spec,
            pl.BlockSpec((tm, N_BRANCH * d), lambda i: (i, 0)),
            pl.BlockSpec((N_BRANCH, BRANCH_WIDTH, d), lambda i: (0, 0, 0)),
            pl.BlockSpec((d, d), lambda i: (0, 0)),
        ],
        out_specs=pl.BlockSpec((tm, d), lambda i: (i, 0)),
        out_shape=jax.ShapeDtypeStruct((n, d), F32),
        compiler_params=_cparams(("parallel",)),
        name="gated_mix_out_proj",
    )(x, o_a, o_b, o_c, proj, wb, wo)


def _mlp_kernel(x_ref, g_ref, wu_ref, wd_ref, gf_ref, y_ref, h_ref, acc_ref, *, final_norm):
    f = pl.program_id(1)

    @pl.when(f == 0)
    def _():
        h_ref[...] = _rms(x_ref[...], g_ref[...]).astype(BF16)
        acc_ref[...] = jnp.zeros_like(acc_ref)

    hid = jnp.square(jnp.maximum(_dot(h_ref[...], wu_ref[...]), 0.0))
    acc_ref[...] += _dot(hid.astype(BF16), wd_ref[...])

    @pl.when(f == pl.num_programs(1) - 1)
    def _():
        y = x_ref[...] + acc_ref[...]
        if final_norm:
            y = _rms(y, gf_ref[...])
        y_ref[...] = y


def _mlp(x, g, wu, wd, g_final, final_norm, tm, tf):
    n, d = x.shape
    d_ff = wu.shape[1]
    return pl.pallas_call(
        functools.partial(_mlp_kernel, final_norm=final_norm),
        grid=(n // tm, d_ff // tf),
        in_specs=[
            pl.BlockSpec((tm, d), lambda i, f: (i, 0)),
            pl.BlockSpec((1, d), lambda i, f: (0, 0)),
            pl.BlockSpec((d, tf), lambda i, f: (0, f)),
            pl.BlockSpec((tf, d), lambda i, f: (f, 0)),
            pl.BlockSpec((1, d), lambda i, f: (0, 0)),
        ],
        out_specs=pl.BlockSpec((tm, d), lambda i, f: (i, 0)),
        out_shape=jax.ShapeDtypeStruct((n, d), F32),
        scratch_shapes=[pltpu.VMEM((tm, d), BF16), pltpu.VMEM((tm, d), F32)],
        compiler_params=_cparams(("parallel", "arbitrary")),
        name="relu2_mlp",
    )(x, g, wu, wd, g_final)


def _prep_in_proj(w_in):
    bw = BRANCH_WIDTH
    scale = HEAD_DIM ** -0.5
    q_scales = (scale, scale * LOG2E, scale * LOG2E)
    f_lo = 6 * bw
    f_hi = f_lo + N_HEADS
    g_lo = 9 * bw + N_HEADS
    pieces = [w_in[:, :, g_lo:]]
    for m in range(N_BRANCH):
        base = 3 * m * bw + (N_HEADS if m == 2 else 0)
        pieces += [w_in[:, :, base:base + bw] * q_scales[m],
                   w_in[:, :, base + bw:base + 3 * bw]]
    w_main = jnp.concatenate([p.astype(BF16) for p in pieces], axis=-1)
    w_f = jnp.pad(w_in[:, :, f_lo:f_hi], ((0, 0), (0, 0), (0, LANES - N_HEADS)))
    return w_main, w_f.astype(BF16)


def kernel(x, norm_mix, w_in, b_forget, w_branch, w_out, norm_mlp, w_up, w_down, norm_final):
    b, s, d = x.shape
    depth = w_in.shape[0]
    n = b * s
    assert s % ATT_TILE == 0 and d % LANES == 0

    w_main, w_f = _prep_in_proj(w_in)
    wb = w_branch.astype(BF16)
    wo = w_out.astype(BF16)
    wu = w_up.astype(BF16)
    wd = w_down.astype(BF16)
    b_f = jnp.pad(b_forget, ((0, 0), (0, LANES - N_HEADS)))[:, None, :]

    tm = min(1024, s)
    xf = x.reshape(n, d)
    for l in range(depth):
        g_mix = norm_mix[l][None, :]
        proj, key_bias = _in_proj(xf, g_mix, w_main[l], w_f[l], b_f[l], s, tm, 1536, 256)
        proj3 = proj.reshape(b, s, -1)
        o_a = _sb_attention(proj3, d).reshape(n, -1)
        o_b = _flash_attention(proj3, key_bias.reshape(b, s, -1), d, False).reshape(n, -1)
        o_c = _flash_attention(proj3, None, d, True).reshape(n, -1)
        xf = _mix(xf, o_a, o_b, o_c, proj, wb[l], wo[l], min(512, n))
        xf = _mlp(xf, norm_mlp[l][None, :], wu[l], wd[l], norm_final[None, :],
                  l == depth - 1, min(512, n), 1024)
    return xf.reshape(b, s, d)
```

```python
import functools
import math

import jax
import jax.numpy as jnp
import numpy as np
from jax import lax
from jax.experimental import pallas as pl
from jax.experimental.pallas import tpu as pltpu

F32 = jnp.float32
BF16 = jnp.bfloat16

HEAD_DIM = 64
N_HEADS = 8
BRANCH_WIDTH = N_HEADS * HEAD_DIM
N_BRANCH = 3
MOBA_BLOCK = 256
MOBA_TOPK = 3
RMS_EPS = 1e-6
NEG_INF = -1e30
LOG2E = math.log2(math.e)
LANES = 128
HEADS_PER_BLOCK = LANES // HEAD_DIM
N_HEAD_BLOCKS = BRANCH_WIDTH // LANES
N_BIAS_PIECES = 3
ATT_TILE = 256
SB_EXIT = 160.0
FLASH_SKIP = 160.0
NORM_SLACK = 1.001
FLASH_BATCH = 4
V_ROWS = HEAD_DIM + 16
VMEM_LIMIT = 56 * 1024 * 1024


def _cparams(sem):
    return pltpu.CompilerParams(dimension_semantics=sem, vmem_limit_bytes=VMEM_LIMIT)


def _rms(x, g):
    ms = jnp.mean(x * x, axis=-1, keepdims=True)
    return x * lax.rsqrt(ms + RMS_EPS) * g


def _dot(a, b):
    return jnp.dot(a, b, preferred_element_type=F32)


def _dot_nt(a, b):
    return lax.dot_general(a, b, (((1,), (1,)), ((), ())), preferred_element_type=F32)


def _split2(x):
    hi = x.astype(BF16)
    lo = (x - hi.astype(F32)).astype(BF16)
    return hi, lo


def _split3(x):
    hi = x.astype(BF16)
    r = x - hi.astype(F32)
    mid = r.astype(BF16)
    lo = (r - mid.astype(F32)).astype(BF16)
    return hi, mid, lo


def _bias_lane(hh):
    return (1 - hh) * HEAD_DIM


def _in_proj_kernel(x_ref, g_ref, w_ref, wf_ref, b_ref, tri_ref, place_ref,
                    o_ref, e_ref, h_ref, carry_ref, *, tiles_per_seq):
    @pl.when(pl.program_id(1) == 0)
    def _():
        h = _rms(x_ref[...], g_ref[...]).astype(BF16)
        h_ref[...] = h

        @pl.when(pl.program_id(0) % tiles_per_seq == 0)
        def _():
            carry_ref[...] = jnp.zeros_like(carry_ref)

        y = _dot(h, wf_ref[...]) + b_ref[...]
        logf = jnp.minimum(y, 0.0) - jnp.log(1.0 + jnp.exp(-jnp.abs(y)))
        pieces = _split3(logf)
        tri = tri_ref[...]
        tc = tri.shape[0]
        cum = carry_ref[...]
        for c in range(h.shape[0] // tc):
            rows = slice(c * tc, (c + 1) * tc)
            cum = cum[-1:, :]
            for piece in pieces:
                cum = cum + _dot(tri, piece[rows])
            bias_pieces = jnp.concatenate(_split3(cum * (-LOG2E)), axis=1)
            e_ref[rows, :] = _dot(bias_pieces, place_ref[...]).astype(BF16)
        carry_ref[...] = cum[-1:, :]

    o_ref[...] = _dot(h_ref[...], w_ref[...]).astype(o_ref.dtype)


def _bias_placement():
    place = np.zeros((N_BIAS_PIECES, LANES, N_HEAD_BLOCKS * LANES), np.float32)
    for h in range(N_HEADS):
        hp, hh = divmod(h, HEADS_PER_BLOCK)
        for i in range(N_BIAS_PIECES):
            place[i, h, hp * LANES + _bias_lane(hh) + i] = 1.0
    return jnp.asarray(place.reshape(N_BIAS_PIECES * LANES, -1), BF16)


def _in_proj(x, g, w, w_f, b_f, s, tm, tn, tc):
    n, d = x.shape
    cols = w.shape[1]
    tri = jnp.asarray(np.tril(np.ones((tc, tc), np.float32)), BF16)
    place = _bias_placement()
    e_cols = place.shape[1]
    const2 = lambda i, j: (0, 0)
    return pl.pallas_call(
        functools.partial(_in_proj_kernel, tiles_per_seq=s // tm),
        grid=(n // tm, cols // tn),
        in_specs=[
            pl.BlockSpec((tm, d), lambda i, j: (i, 0)),
            pl.BlockSpec((1, d), const2),
            pl.BlockSpec((d, tn), lambda i, j: (0, j)),
            pl.BlockSpec((d, LANES), const2),
            pl.BlockSpec((1, LANES), const2),
            pl.BlockSpec((tc, tc), const2),
            pl.BlockSpec(place.shape, const2),
        ],
        out_specs=[pl.BlockSpec((tm, tn), lambda i, j: (i, j)),
                   pl.BlockSpec((tm, e_cols), lambda i, j: (i, 0))],
        out_shape=[jax.ShapeDtypeStruct((n, cols), BF16),
                   jax.ShapeDtypeStruct((n, e_cols), BF16)],
        scratch_shapes=[pltpu.VMEM((tm, d), BF16), pltpu.VMEM((1, LANES), F32)],
        compiler_params=_cparams(("arbitrary", "arbitrary")),
        name="norm_in_proj",
    )(x, g, w, w_f, b_f, tri, place)


def _head_lane_mask(hh):
    lane = lax.broadcasted_iota(jnp.int32, (1, LANES), 1)
    return (lane >= hh * HEAD_DIM) & (lane < (hh + 1) * HEAD_DIM)


def _tile_rel(t):
    return (lax.broadcasted_iota(jnp.int32, (t, t), 0)
            - lax.broadcasted_iota(jnp.int32, (t, t), 1))


def _mixer_specs(mixer, nb, t, s, d):
    qoff = N_BRANCH * d // LANES + (3 * mixer) * N_HEAD_BLOCKS
    koff = qoff + N_HEAD_BLOCKS
    voff = koff + N_HEAD_BLOCKS
    return [
        pl.BlockSpec((nb, t, LANES), lambda bg, hp, qi: (bg, qi, qoff + hp)),
        pl.BlockSpec((nb, s, LANES), lambda bg, hp, qi: (bg, 0, koff + hp)),
        pl.BlockSpec((nb, s, LANES), lambda bg, hp, qi: (bg, 0, voff + hp)),
    ], pl.BlockSpec((nb, t, LANES), lambda bg, hp, qi: (bg, qi, hp))


def _sb_kernel(q_ref, k_ref, v_ref, o_ref, vt_ref, acc_ref, cs_ref, r_ref, s_even_ref, s_odd_ref, *, nb):
    t = q_ref.shape[1]
    n_blocks = k_ref.shape[1] // t
    qi = pl.program_id(2)
    streams = [(bi, hh) for bi in range(nb) for hh in range(HEADS_PER_BLOCK)]

    @pl.when(qi == 0)
    def _build():
        def chunk(c, carry):
            rows = pl.ds(pl.multiple_of(c * t, t), t)
            for st, (bi, hh) in enumerate(streams):
                v2t = v_ref[bi, rows, :].astype(F32).T
                vt_ref[st, :, rows] = v2t[hh * HEAD_DIM:(hh + 1) * HEAD_DIM, :].astype(BF16)
            return carry

        lax.fori_loop(0, n_blocks, chunk, 0)

    q_m = []
    for bi, hh in streams:
        q2 = q_ref[bi]
        q_m.append(jnp.where(_head_lane_mask(hh), q2, jnp.zeros_like(q2)))

    rel = _tile_rel(t)
    strictly_past = rel < 0
    upper = jnp.where(rel <= 0, 1.0, 0.0).astype(BF16)

    def scores(s_ref, j):
        rows = pl.ds(pl.multiple_of(j * t, t), t)
        for st, (bi, hh) in enumerate(streams):
            s_ref[st] = _dot_nt(k_ref[bi, rows, :], q_m[st])

    def accumulate(s_ref, j, diag):
        rows = pl.ds(pl.multiple_of(j * t, t), t)
        for st in range(len(streams)):
            z = s_ref[st]
            sp = jnp.maximum(z, 0.0) + jnp.log2(1.0 + jnp.exp2(-jnp.abs(z)))
            if diag:
                sp = jnp.where(strictly_past, sp, 0.0)
            r = _dot(upper, sp.astype(BF16)) + cs_ref[st]
            r_ref[st] = r
            cs_ref[st] = r[0:1, :]
        for st in range(len(streams)):
            w = jnp.exp2(s_ref[st] - r_ref[st])
            if diag:
                w = jnp.where(strictly_past, w, 0.0)
            acc_ref[st] += _dot(vt_ref[st, :, rows], w.astype(BF16))

    def next_distance(i):
        lowest = cs_ref[0]
        for st in range(1, len(streams)):
            lowest = jnp.minimum(lowest, cs_ref[st])
        return jnp.where(jnp.min(lowest) > SB_EXIT, qi + 1, i + 1)

    acc_ref[...] = jnp.zeros(acc_ref.shape, F32)
    cs_ref[...] = jnp.zeros(cs_ref.shape, F32)
    scores(s_even_ref, qi)
    scores(s_odd_ref, jnp.maximum(qi - 1, 0))
    accumulate(s_even_ref, qi, True)

    def body(i):
        j = qi - i
        j_next = jnp.maximum(j - 1, 0)

        @pl.when(i % 2 == 1)
        def _():
            scores(s_even_ref, j_next)
            accumulate(s_odd_ref, j, False)

        @pl.when(i % 2 == 0)
        def _():
            scores(s_odd_ref, j_next)
            accumulate(s_even_ref, j, False)

        return next_distance(i)

    lax.while_loop(lambda i: i <= qi, body, next_distance(0))

    for bi in range(nb):
        halves = [acc_ref[bi * HEADS_PER_BLOCK + hh] for hh in range(HEADS_PER_BLOCK)]
        o_ref[bi] = jnp.concatenate(halves, axis=0).T.astype(o_ref.dtype)


def _sb_attention(proj3, d):
    b, s, _ = proj3.shape
    t = ATT_TILE
    nb = FLASH_BATCH if b % FLASH_BATCH == 0 else 1
    n_streams = nb * HEADS_PER_BLOCK
    in_specs, out_spec = _mixer_specs(0, nb, t, s, d)
    return pl.pallas_call(
        functools.partial(_sb_kernel, nb=nb),
        grid=(b // nb, N_HEAD_BLOCKS, s // t),
        in_specs=in_specs,
        out_specs=out_spec,
        out_shape=jax.ShapeDtypeStruct((b, s, BRANCH_WIDTH), BF16),
        scratch_shapes=[pltpu.VMEM((n_streams, HEAD_DIM, s), BF16),
                        pltpu.VMEM((n_streams, HEAD_DIM, t), F32),
                        pltpu.VMEM((n_streams, 1, t), F32),
                        pltpu.VMEM((n_streams, t, t), F32),
                        pltpu.VMEM((n_streams, t, t), F32),
                        pltpu.VMEM((n_streams, t, t), F32)],
        compiler_params=_cparams(("parallel", "parallel", "arbitrary")),
        name="stickbreak_attn",
    )(proj3, proj3, proj3)


def _flash_kernel(*refs, moba, nb):
    if moba:
        (q_ref, k_ref, v_ref, o_ref, kaug_ref, vt_ref, acc_ref, m_ref, s_even_ref, s_odd_ref,
         bmax_ref, knorm_ref, kmean_ref, sel_ref) = refs
    else:
        (q_ref, k_ref, v_ref, e_ref, o_ref, kaug_ref, vt_ref, acc_ref, m_ref, s_even_ref,
         s_odd_ref, bmax_ref, knorm_ref) = refs
    t = q_ref.shape[1]
    n_blocks = k_ref.shape[1] // t
    hp = pl.program_id(1)
    qi = pl.program_id(2)
    lane = lax.broadcasted_iota(jnp.int32, (1, LANES), 1)
    streams = [(bi, hh) for bi in range(nb) for hh in range(HEADS_PER_BLOCK)]

    def piece_lanes(hh):
        e0 = _bias_lane(hh)
        return (lane >= e0) & (lane < e0 + N_BIAS_PIECES)

    head_of_lane = jnp.where(lax.broadcasted_iota(jnp.int32, (LANES, LANES), 0) >= HEAD_DIM, 1, 0)
    head_sum = jnp.where(head_of_lane == lax.broadcasted_iota(jnp.int32, (LANES, LANES), 1), 1.0, 0.0).astype(BF16)

    def head_sq_norm_max(x32):
        hi, lo = _split2(x32 * x32)
        return jnp.max(_dot(hi, head_sum) + _dot(lo, head_sum), axis=0, keepdims=True)

    def head_col(row, hh):
        return jnp.sum(jnp.where(lane == hh, row, 0.0), axis=1, keepdims=True)

    @pl.when(qi == 0)
    def _build():
        bmax_ref[...] = jnp.full(bmax_ref.shape, NEG_INF, F32)
        knorm_ref[...] = jnp.zeros(knorm_ref.shape, F32)
        tail = jnp.where(lax.broadcasted_iota(jnp.int32, (V_ROWS - HEAD_DIM, t), 0) == 0, 1.0, 0.0)

        def chunk(c, carry):
            start = pl.multiple_of(c * t, t)
            rows = pl.ds(start, t)
            extras = []
            if moba:
                pos = (lax.broadcasted_iota(jnp.int32, (t, LANES), 0) + start).astype(F32)
                for hh in range(HEADS_PER_BLOCK):
                    h = hp * HEADS_PER_BLOCK + hh
                    slope = pltpu.bitcast(jnp.full((1, LANES), (126 - h) << 23, jnp.int32), F32)
                    extra = jnp.zeros((t, LANES), F32)
                    for i, piece in enumerate(_split3(pos * slope * LOG2E)):
                        extra = jnp.where(lane == _bias_lane(hh) + i, piece.astype(F32), extra)
                    extras.append(extra.astype(BF16))
            for bi in range(nb):
                k2 = k_ref[bi, rows, :]
                k32 = k2.astype(F32)
                v2t = v_ref[bi, rows, :].astype(F32).T
                if moba:
                    kmean_ref[bi, pl.ds(c, 1), :] = jnp.sum(k32, axis=0, keepdims=True) * (1.0 / t)
                for hh in range(HEADS_PER_BLOCK):
                    st = bi * HEADS_PER_BLOCK + hh
                    extra = extras[hh] if moba else e_ref[bi, rows, :]
                    kaug_ref[st, rows, :] = jnp.where(_head_lane_mask(hh), k2, extra)
                    v_h = v2t[hh * HEAD_DIM:(hh + 1) * HEAD_DIM, :]
                    vt_ref[st, :, rows] = jnp.concatenate([v_h, tail], axis=0).astype(BF16)
                    piece_max = jnp.max(extra.astype(F32), axis=0, keepdims=True)
                    bias_max = jnp.sum(jnp.where(piece_lanes(hh), piece_max, 0.0), axis=1, keepdims=True)
                    run = jnp.maximum(bmax_ref[st, pl.ds(jnp.maximum(c - 1, 0), 1), :], bias_max)
                    bmax_ref[st, pl.ds(c, 1), :] = run
                knorm_ref[bi] = jnp.maximum(knorm_ref[bi], head_sq_norm_max(k32))
            return carry

        lax.fori_loop(0, n_blocks, chunk, 0)

    q_aug = []
    reach = []
    for bi in range(nb):
        q2 = q_ref[bi]
        qk_sq = head_sq_norm_max(q2.astype(F32)) * knorm_ref[bi]
        for hh in range(HEADS_PER_BLOCK):
            ones = jnp.where(piece_lanes(hh), 1.0, 0.0).astype(BF16)
            q_aug.append(jnp.where(_head_lane_mask(hh), q2, jnp.broadcast_to(ones, q2.shape)))
            reach.append(jnp.sqrt(head_col(qk_sq, hh)) * NORM_SLACK)

    if moba:
        blk = lax.broadcasted_iota(jnp.int32, (n_blocks, 1), 0)
        blk_f = blk.astype(F32)
        for st, (bi, hh) in enumerate(streams):
            km = _split3(kmean_ref[bi])
            q2 = q_ref[bi]
            qh = jnp.where(_head_lane_mask(hh), q2, jnp.zeros_like(q2))
            route = _dot_nt(km[0], qh) + _dot_nt(km[1], qh) + _dot_nt(km[2], qh)
            route = jnp.where(blk < qi, route, NEG_INF)
            sel = jnp.zeros((n_blocks, t), F32)
            for _ in range(MOBA_TOPK):
                top = jnp.max(route, axis=0, keepdims=True)
                first = jnp.min(jnp.where(route == top, blk_f, float(n_blocks)), axis=0, keepdims=True)
                hit = blk_f == first
                sel = jnp.where(hit, 1.0, sel)
                route = jnp.where(hit, -jnp.inf, route)
            sel_ref[st] = jnp.where(blk < qi, sel, 0.0)

    valid = _tile_rel(t) <= 0

    def scores(s_ref, j):
        rows = pl.ds(pl.multiple_of(j * t, t), t)
        for st in range(len(streams)):
            s_ref[st] = _dot_nt(kaug_ref[st, rows, :], q_aug[st])

    def accumulate(s_ref, j, diag):
        rows = pl.ds(pl.multiple_of(j * t, t), t)
        for st in range(len(streams)):
            s_t = s_ref[st]
            if diag:
                s_t = jnp.where(valid, s_t, NEG_INF)
            m_old = m_ref[st]
            m_new = jnp.maximum(m_old, jnp.max(s_t, axis=0, keepdims=True))
            m_sub = m_new
            if moba and not diag:
                picked = sel_ref[st, pl.ds(j, 1), :] > 0.0
                m_new = jnp.where(picked, m_new, m_old)
                m_sub = jnp.where(picked, m_new, -NEG_INF)
            alpha = jnp.exp2(m_old - m_new)
            p = jnp.exp2(s_t - m_sub)
            acc_ref[st] = alpha * acc_ref[st] + _dot(vt_ref[st, :, rows], p.astype(BF16))
            m_ref[st] = m_new

    m_ref[...] = jnp.full(m_ref.shape, NEG_INF, F32)
    acc_ref[...] = jnp.zeros(acc_ref.shape, F32)
    scores(s_even_ref, qi)
    scores(s_odd_ref, jnp.maximum(qi - 1, 0))
    accumulate(s_even_ref, qi, True)

    blk_col = lax.broadcasted_iota(jnp.int32, (n_blocks, 1), 0)
    fewest_dead = None
    for st in range(len(streams)):
        thr = jnp.min(m_ref[st], axis=1, keepdims=True) - FLASH_SKIP - reach[st]
        dead = (bmax_ref[st] < thr) & (blk_col < qi)
        n_dead = jnp.sum(jnp.where(dead, 1.0, 0.0), axis=0, keepdims=True)
        fewest_dead = n_dead if fewest_dead is None else jnp.minimum(fewest_dead, n_dead)
    n_past = qi - jnp.max(fewest_dead).astype(jnp.int32)

    def body(i, carry):
        j = qi - i
        j_next = jnp.maximum(j - 1, 0)

        @pl.when(i % 2 == 1)
        def _():
            scores(s_even_ref, j_next)
            accumulate(s_odd_ref, j, False)

        @pl.when(i % 2 == 0)
        def _():
            scores(s_odd_ref, j_next)
            accumulate(s_even_ref, j, False)

        return carry

    lax.fori_loop(1, n_past + 1, body, 0)

    for bi in range(nb):
        halves = []
        for hh in range(HEADS_PER_BLOCK):
            acc = acc_ref[bi * HEADS_PER_BLOCK + hh]
            halves.append(acc[:HEAD_DIM] * (1.0 / acc[HEAD_DIM:HEAD_DIM + 1]))
        o_ref[bi] = jnp.concatenate(halves, axis=0).T.astype(o_ref.dtype)


def _flash_attention(proj3, key_bias3, d, moba):
    b, s, _ = proj3.shape
    t = ATT_TILE
    nb = FLASH_BATCH if b % FLASH_BATCH == 0 else 1
    n_blocks = s // t
    n_streams = nb * HEADS_PER_BLOCK
    in_specs, out_spec = _mixer_specs(2 if moba else 1, nb, t, s, d)
    operands = [proj3, proj3, proj3]
    scratch = [pltpu.VMEM((n_streams, s, LANES), BF16),
               pltpu.VMEM((n_streams, V_ROWS, s), BF16),
               pltpu.VMEM((n_streams, V_ROWS, t), F32),
               pltpu.VMEM((n_streams, 1, t), F32),
               pltpu.VMEM((n_streams, t, t), F32),
               pltpu.VMEM((n_streams, t, t), F32),
               pltpu.VMEM((n_streams, n_blocks, 1), F32),
               pltpu.VMEM((nb, 1, LANES), F32)]
    if moba:
        scratch += [pltpu.VMEM((nb, n_blocks, LANES), F32),
                    pltpu.VMEM((n_streams, n_blocks, t), F32)]
    else:
        in_specs.append(pl.BlockSpec((nb, s, LANES), lambda bg, hp, qi: (bg, 0, hp)))
        operands.append(key_bias3)
    return pl.pallas_call(
        functools.partial(_flash_kernel, moba=moba, nb=nb),
        grid=(b // nb, N_HEAD_BLOCKS, n_blocks),
        in_specs=in_specs,
        out_specs=out_spec,
        out_shape=jax.ShapeDtypeStruct((b, s, BRANCH_WIDTH), BF16),
        scratch_shapes=scratch,
        compiler_params=_cparams(("parallel", "parallel", "arbitrary")),
        name="moba_attn" if moba else "forgetting_attn",
    )(*operands)


def _mix_kernel(x_ref, oa_ref, ob_ref, oc_ref, g_ref, wb_ref, wo_ref, y_ref):
    d = x_ref.shape[1]
    mixed = None
    for n, o_ref in enumerate((oa_ref, ob_ref, oc_ref)):
        gate = jax.nn.sigmoid(g_ref[:, n * d:(n + 1) * d].astype(F32))
        term = gate * _dot(o_ref[...], wb_ref[n])
        mixed = term if mixed is None else mixed + term
    y_ref[...] = x_ref[...] + _dot(mixed.astype(BF16), wo_ref[...])


def _mix(x, o_a, o_b, o_c, proj, wb, wo, tm):
    n, d = x.shape
    o_spec = pl.BlockSpec((tm, BRANCH_WIDTH), lambda i: (i, 0))
    return pl.pallas_call(
        _mix_kernel,
        grid=(n // tm,),
        in_specs=[
            pl.BlockSpec((tm, d), lambda i: (i, 0)),
            o_spec, o_spec, o_spec,
            pl.BlockSpec((tm, N_BRANCH * d), lambda i: (i, 0)),
            pl.BlockSpec((N_BRANCH, BRANCH_WIDTH, d), lambda i: (0, 0, 0)),
            pl.BlockSpec((d, d), lambda i: (0, 0)),
        ],
        out_specs=pl.BlockSpec((tm, d), lambda i: (i, 0)),
        out_shape=jax.ShapeDtypeStruct((n, d), F32),
        compiler_params=_cparams(("parallel",)),
        name="gated_mix_out_proj",
    )(x, o_a, o_b, o_c, proj, wb, wo)


def _mlp_kernel(x_ref, g_ref, wu_ref, wd_ref, gf_ref, y_ref, h_ref, acc_ref, *, final_norm):
    f = pl.program_id(1)

    @pl.when(f == 0)
    def _():
        h_ref[...] = _rms(x_ref[...], g_ref[...]).astype(BF16)
        acc_ref[...] = jnp.zeros_like(acc_ref)

    hid = jnp.square(jnp.maximum(_dot(h_ref[...], wu_ref[...]), 0.0))
    acc_ref[...] += _dot(hid.astype(BF16), wd_ref[...])

    @pl.when(f == pl.num_programs(1) - 1)
    def _():
        y = x_ref[...] + acc_ref[...]
        if final_norm:
            y = _rms(y, gf_ref[...])
        y_ref[...] = y


def _mlp(x, g, wu, wd, g_final, final_norm, tm, tf):
    n, d = x.shape
    d_ff = wu.shape[1]
    return pl.pallas_call(
        functools.partial(_mlp_kernel, final_norm=final_norm),
        grid=(n // tm, d_ff // tf),
        in_specs=[
            pl.BlockSpec((tm, d), lambda i, f: (i, 0)),
            pl.BlockSpec((1, d), lambda i, f: (0, 0)),
            pl.BlockSpec((d, tf), lambda i, f: (0, f)),
            pl.BlockSpec((tf, d), lambda i, f: (f, 0)),
            pl.BlockSpec((1, d), lambda i, f: (0, 0)),
        ],
        out_specs=pl.BlockSpec((tm, d), lambda i, f: (i, 0)),
        out_shape=jax.ShapeDtypeStruct((n, d), F32),
        scratch_shapes=[pltpu.VMEM((tm, d), BF16), pltpu.VMEM((tm, d), F32)],
        compiler_params=_cparams(("parallel", "arbitrary")),
        name="relu2_mlp",
    )(x, g, wu, wd, g_final)


def _prep_in_proj(w_in):
    bw = BRANCH_WIDTH
    q_scale = HEAD_DIM ** -0.5 * LOG2E
    f_lo = 6 * bw
    f_hi = f_lo + N_HEADS
    g_lo = 9 * bw + N_HEADS
    pieces = [w_in[:, :, g_lo:]]
    for m in range(N_BRANCH):
        base = 3 * m * bw + (N_HEADS if m == 2 else 0)
        pieces += [w_in[:, :, base:base + bw] * q_scale,
                   w_in[:, :, base + bw:base + 3 * bw]]
    w_main = jnp.concatenate([p.astype(BF16) for p in pieces], axis=-1)
    w_f = jnp.pad(w_in[:, :, f_lo:f_hi], ((0, 0), (0, 0), (0, LANES - N_HEADS)))
    return w_main, w_f.astype(BF16)


def kernel(x, norm_mix, w_in, b_forget, w_branch, w_out, norm_mlp, w_up, w_down, norm_final):
    b, s, d = x.shape
    depth = w_in.shape[0]
    n = b * s
    assert s % ATT_TILE == 0 and d % LANES == 0

    w_main, w_f = _prep_in_proj(w_in)
    wb = w_branch.astype(BF16)
    wo = w_out.astype(BF16)
    wu = w_up.astype(BF16)
    wd = w_down.astype(BF16)
    b_f = jnp.pad(b_forget, ((0, 0), (0, LANES - N_HEADS)))[:, None, :]

    tm = min(1024, s)
    xf = x.reshape(n, d)
    for l in range(depth):
        g_mix = norm_mix[l][None, :]
        proj, key_bias = _in_proj(xf, g_mix, w_main[l], w_f[l], b_f[l], s, tm, 1536, 256)
        proj3 = proj.reshape(b, s, -1)
        o_a = _sb_attention(proj3, d).reshape(n, -1)
        o_b = _flash_attention(proj3, key_bias.reshape(b, s, -1), d, False).reshape(n, -1)
        o_c = _flash_attention(proj3, None, d, True).reshape(n, -1)
        xf = _mix(xf, o_a, o_b, o_c, proj, wb[l], wo[l], min(512, n))
        xf = _mlp(xf, norm_mlp[l][None, :], wu[l], wd[l], norm_final[None, :],
                  l == depth - 1, min(1024, n), 1024)
    return xf.reshape(b, s, d)
```

```python
import functools
import math

import jax
import jax.numpy as jnp
import numpy as np
from jax import lax
from jax.experimental import pallas as pl
from jax.experimental.pallas import tpu as pltpu

F32 = jnp.float32
BF16 = jnp.bfloat16

HEAD_DIM = 64
N_HEADS = 8
BRANCH_WIDTH = N_HEADS * HEAD_DIM
N_BRANCH = 3
MOBA_BLOCK = 256
MOBA_TOPK = 3
RMS_EPS = 1e-6
NEG_INF = -1e30
LOG2E = math.log2(math.e)
LANES = 128
HEADS_PER_BLOCK = LANES // HEAD_DIM
N_HEAD_BLOCKS = BRANCH_WIDTH // LANES
N_BIAS_PIECES = 3
ATT_TILE = 256
SB_EXIT = 160.0
FLASH_SKIP = 160.0
NORM_SLACK = 1.001
FLASH_BATCH = 4
V_ROWS = HEAD_DIM + 16
VMEM_LIMIT = 56 * 1024 * 1024


def _cparams(sem):
    return pltpu.CompilerParams(dimension_semantics=sem, vmem_limit_bytes=VMEM_LIMIT)


def _rms(x, g):
    ms = jnp.mean(x * x, axis=-1, keepdims=True)
    return x * lax.rsqrt(ms + RMS_EPS) * g


def _dot(a, b):
    return jnp.dot(a, b, preferred_element_type=F32)


def _dot_nt(a, b):
    return lax.dot_general(a, b, (((1,), (1,)), ((), ())), preferred_element_type=F32)


def _split2(x):
    hi = x.astype(BF16)
    lo = (x - hi.astype(F32)).astype(BF16)
    return hi, lo


def _split3(x):
    hi = x.astype(BF16)
    r = x - hi.astype(F32)
    mid = r.astype(BF16)
    lo = (r - mid.astype(F32)).astype(BF16)
    return hi, mid, lo


def _bias_lane(hh):
    return (1 - hh) * HEAD_DIM


def _in_proj_kernel(x_ref, g_ref, w_ref, wf_ref, b_ref, tri_ref, place_ref,
                    o_ref, e_ref, h_ref, carry_ref, *, tiles_per_seq):
    @pl.when(pl.program_id(1) == 0)
    def _():
        h = _rms(x_ref[...], g_ref[...]).astype(BF16)
        h_ref[...] = h

        @pl.when(pl.program_id(0) % tiles_per_seq == 0)
        def _():
            carry_ref[...] = jnp.zeros_like(carry_ref)

        y = _dot_nt(h, wf_ref[...]) + b_ref[...]
        logf = jnp.minimum(y, 0.0) - jnp.log(1.0 + jnp.exp(-jnp.abs(y)))
        pieces = _split3(logf)
        tri = tri_ref[...]
        tc = tri.shape[0]
        cum = carry_ref[...]
        for c in range(h.shape[0] // tc):
            rows = slice(c * tc, (c + 1) * tc)
            cum = cum[-1:, :]
            for piece in pieces:
                cum = cum + _dot(tri, piece[rows])
            bias_pieces = jnp.concatenate(_split3(cum * (-LOG2E)), axis=1)
            e_ref[rows, :] = _dot(bias_pieces, place_ref[...]).astype(BF16)
        carry_ref[...] = cum[-1:, :]

    o_ref[...] = _dot_nt(h_ref[...], w_ref[...]).astype(o_ref.dtype)


def _bias_placement():
    place = np.zeros((N_BIAS_PIECES, LANES, N_HEAD_BLOCKS * LANES), np.float32)
    for h in range(N_HEADS):
        hp, hh = divmod(h, HEADS_PER_BLOCK)
        for i in range(N_BIAS_PIECES):
            place[i, h, hp * LANES + _bias_lane(hh) + i] = 1.0
    return jnp.asarray(place.reshape(N_BIAS_PIECES * LANES, -1), BF16)


def _in_proj(x, g, w, w_f, b_f, layer, s, tm, tn, tc):
    n, d = x.shape
    cols = w.shape[1]
    tri = jnp.asarray(np.tril(np.ones((tc, tc), np.float32)), BF16)
    place = _bias_placement()
    e_cols = place.shape[1]
    const2 = lambda i, j: (0, 0)
    return pl.pallas_call(
        functools.partial(_in_proj_kernel, tiles_per_seq=s // tm),
        grid=(n // tm, cols // tn),
        in_specs=[
            pl.BlockSpec((tm, d), lambda i, j: (i, 0)),
            pl.BlockSpec((1, d), const2),
            pl.BlockSpec((None, tn, d), lambda i, j: (layer, j, 0)),
            pl.BlockSpec((None, LANES, d), lambda i, j: (layer, 0, 0)),
            pl.BlockSpec((1, LANES), const2),
            pl.BlockSpec((tc, tc), const2),
            pl.BlockSpec(place.shape, const2),
        ],
        out_specs=[pl.BlockSpec((tm, tn), lambda i, j: (i, j)),
                   pl.BlockSpec((tm, e_cols), lambda i, j: (i, 0))],
        out_shape=[jax.ShapeDtypeStruct((n, cols), BF16),
                   jax.ShapeDtypeStruct((n, e_cols), BF16)],
        scratch_shapes=[pltpu.VMEM((tm, d), BF16), pltpu.VMEM((1, LANES), F32)],
        compiler_params=_cparams(("arbitrary", "arbitrary")),
        name="norm_in_proj",
    )(x, g, w, w_f, b_f, tri, place)


def _head_lane_mask(hh):
    lane = lax.broadcasted_iota(jnp.int32, (1, LANES), 1)
    return (lane >= hh * HEAD_DIM) & (lane < (hh + 1) * HEAD_DIM)


def _tile_rel(t):
    return (lax.broadcasted_iota(jnp.int32, (t, t), 0)
            - lax.broadcasted_iota(jnp.int32, (t, t), 1))


def _mixer_specs(mixer, nb, t, s, d):
    qoff = N_BRANCH * d // LANES + (3 * mixer) * N_HEAD_BLOCKS
    koff = qoff + N_HEAD_BLOCKS
    voff = koff + N_HEAD_BLOCKS
    return [
        pl.BlockSpec((nb, t, LANES), lambda bg, hp, qi: (bg, qi, qoff + hp)),
        pl.BlockSpec((nb, s, LANES), lambda bg, hp, qi: (bg, 0, koff + hp)),
        pl.BlockSpec((nb, s, LANES), lambda bg, hp, qi: (bg, 0, voff + hp)),
    ], pl.BlockSpec((nb, t, LANES), lambda bg, hp, qi: (bg, qi, hp))


def _sb_kernel(q_ref, k_ref, v_ref, o_ref, vt_ref, acc_ref, cs_ref, r_ref, s_even_ref, s_odd_ref, *, nb):
    t = q_ref.shape[1]
    n_blocks = k_ref.shape[1] // t
    qi = pl.program_id(2)
    streams = [(bi, hh) for bi in range(nb) for hh in range(HEADS_PER_BLOCK)]

    @pl.when(qi == 0)
    def _build():
        def chunk(c, carry):
            rows = pl.ds(pl.multiple_of(c * t, t), t)
            for st, (bi, hh) in enumerate(streams):
                v2t = v_ref[bi, rows, :].astype(F32).T
                vt_ref[st, :, rows] = v2t[hh * HEAD_DIM:(hh + 1) * HEAD_DIM, :].astype(BF16)
            return carry

        lax.fori_loop(0, n_blocks, chunk, 0)

    q_m = []
    for bi, hh in streams:
        q2 = q_ref[bi]
        q_m.append(jnp.where(_head_lane_mask(hh), q2, jnp.zeros_like(q2)))

    rel = _tile_rel(t)
    strictly_past = rel < 0
    upper = jnp.where(rel <= 0, 1.0, 0.0).astype(BF16)

    def scores(s_ref, j):
        rows = pl.ds(pl.multiple_of(j * t, t), t)
        for st, (bi, hh) in enumerate(streams):
            s_ref[st] = _dot_nt(k_ref[bi, rows, :], q_m[st])

    def accumulate(s_ref, j, diag):
        rows = pl.ds(pl.multiple_of(j * t, t), t)
        for st in range(len(streams)):
            z = s_ref[st]
            sp = jnp.maximum(z, 0.0) + jnp.log2(1.0 + jnp.exp2(-jnp.abs(z)))
            if diag:
                sp = jnp.where(strictly_past, sp, 0.0)
            r = _dot(upper, sp.astype(BF16)) + cs_ref[st]
            r_ref[st] = r
            cs_ref[st] = r[0:1, :]
        for st in range(len(streams)):
            w = jnp.exp2(s_ref[st] - r_ref[st])
            if diag:
                w = jnp.where(strictly_past, w, 0.0)
            acc_ref[st] += _dot(vt_ref[st, :, rows], w.astype(BF16))

    def next_distance(i):
        lowest = cs_ref[0]
        for st in range(1, len(streams)):
            lowest = jnp.minimum(lowest, cs_ref[st])
        return jnp.where(jnp.min(lowest) > SB_EXIT, qi + 1, i + 1)

    acc_ref[...] = jnp.zeros(acc_ref.shape, F32)
    cs_ref[...] = jnp.zeros(cs_ref.shape, F32)
    scores(s_even_ref, qi)
    scores(s_odd_ref, jnp.maximum(qi - 1, 0))
    accumulate(s_even_ref, qi, True)

    def body(i):
        j = qi - i
        j_next = jnp.maximum(j - 1, 0)

        @pl.when(i % 2 == 1)
        def _():
            scores(s_even_ref, j_next)
            accumulate(s_odd_ref, j, False)

        @pl.when(i % 2 == 0)
        def _():
            scores(s_odd_ref, j_next)
            accumulate(s_even_ref, j, False)

        return next_distance(i)

    lax.while_loop(lambda i: i <= qi, body, next_distance(0))

    for bi in range(nb):
        halves = [acc_ref[bi * HEADS_PER_BLOCK + hh] for hh in range(HEADS_PER_BLOCK)]
        o_ref[bi] = jnp.concatenate(halves, axis=0).T.astype(o_ref.dtype)


def _sb_attention(proj3, d):
    b, s, _ = proj3.shape
    t = ATT_TILE
    nb = FLASH_BATCH if b % FLASH_BATCH == 0 else 1
    n_streams = nb * HEADS_PER_BLOCK
    in_specs, out_spec = _mixer_specs(0, nb, t, s, d)
    return pl.pallas_call(
        functools.partial(_sb_kernel, nb=nb),
        grid=(b // nb, N_HEAD_BLOCKS, s // t),
        in_specs=in_specs,
        out_specs=out_spec,
        out_shape=jax.ShapeDtypeStruct((b, s, BRANCH_WIDTH), BF16),
        scratch_shapes=[pltpu.VMEM((n_streams, HEAD_DIM, s), BF16),
                        pltpu.VMEM((n_streams, HEAD_DIM, t), F32),
                        pltpu.VMEM((n_streams, 1, t), F32),
                        pltpu.VMEM((n_streams, t, t), F32),
                        pltpu.VMEM((n_streams, t, t), F32),
                        pltpu.VMEM((n_streams, t, t), F32)],
        compiler_params=_cparams(("parallel", "parallel", "arbitrary")),
        name="stickbreak_attn",
    )(proj3, proj3, proj3)


def _flash_kernel(*refs, moba, nb):
    if moba:
        (q_ref, k_ref, v_ref, o_ref, kaug_ref, vt_ref, acc_ref, m_ref, s_even_ref, s_odd_ref,
         bmax_ref, knorm_ref, kmean_ref, sel_ref) = refs
    else:
        (q_ref, k_ref, v_ref, e_ref, o_ref, kaug_ref, vt_ref, acc_ref, m_ref, s_even_ref,
         s_odd_ref, bmax_ref, knorm_ref) = refs
    t = q_ref.shape[1]
    n_blocks = k_ref.shape[1] // t
    hp = pl.program_id(1)
    qi = pl.program_id(2)
    lane = lax.broadcasted_iota(jnp.int32, (1, LANES), 1)
    streams = [(bi, hh) for bi in range(nb) for hh in range(HEADS_PER_BLOCK)]

    def piece_lanes(hh):
        e0 = _bias_lane(hh)
        return (lane >= e0) & (lane < e0 + N_BIAS_PIECES)

    head_of_lane = jnp.where(lax.broadcasted_iota(jnp.int32, (LANES, LANES), 0) >= HEAD_DIM, 1, 0)
    head_sum = jnp.where(head_of_lane == lax.broadcasted_iota(jnp.int32, (LANES, LANES), 1), 1.0, 0.0).astype(BF16)

    def head_sq_norm_max(x32):
        hi, lo = _split2(x32 * x32)
        return jnp.max(_dot(hi, head_sum) + _dot(lo, head_sum), axis=0, keepdims=True)

    def head_col(row, hh):
        return jnp.sum(jnp.where(lane == hh, row, 0.0), axis=1, keepdims=True)

    @pl.when(qi == 0)
    def _build():
        bmax_ref[...] = jnp.full(bmax_ref.shape, NEG_INF, F32)
        knorm_ref[...] = jnp.zeros(knorm_ref.shape, F32)
        tail = jnp.where(lax.broadcasted_iota(jnp.int32, (V_ROWS - HEAD_DIM, t), 0) == 0, 1.0, 0.0)

        def chunk(c, carry):
            start = pl.multiple_of(c * t, t)
            rows = pl.ds(start, t)
            extras = []
            if moba:
                pos = (lax.broadcasted_iota(jnp.int32, (t, LANES), 0) + start).astype(F32)
                for hh in range(HEADS_PER_BLOCK):
                    h = hp * HEADS_PER_BLOCK + hh
                    slope = pltpu.bitcast(jnp.full((1, LANES), (126 - h) << 23, jnp.int32), F32)
                    extra = jnp.zeros((t, LANES), F32)
                    for i, piece in enumerate(_split3(pos * slope * LOG2E)):
                        extra = jnp.where(lane == _bias_lane(hh) + i, piece.astype(F32), extra)
                    extras.append(extra.astype(BF16))
            for bi in range(nb):
                k2 = k_ref[bi, rows, :]
                k32 = k2.astype(F32)
                v2t = v_ref[bi, rows, :].astype(F32).T
                if moba:
                    kmean_ref[bi, pl.ds(c, 1), :] = jnp.sum(k32, axis=0, keepdims=True) * (1.0 / t)
                for hh in range(HEADS_PER_BLOCK):
                    st = bi * HEADS_PER_BLOCK + hh
                    extra = extras[hh] if moba else e_ref[bi, rows, :]
                    kaug_ref[st, rows, :] = jnp.where(_head_lane_mask(hh), k2, extra)
                    v_h = v2t[hh * HEAD_DIM:(hh + 1) * HEAD_DIM, :]
                    vt_ref[st, :, rows] = jnp.concatenate([v_h, tail], axis=0).astype(BF16)
                    piece_max = jnp.max(extra.astype(F32), axis=0, keepdims=True)
                    bias_max = jnp.sum(jnp.where(piece_lanes(hh), piece_max, 0.0), axis=1, keepdims=True)
                    run = jnp.maximum(bmax_ref[st, pl.ds(jnp.maximum(c - 1, 0), 1), :], bias_max)
                    bmax_ref[st, pl.ds(c, 1), :] = run
                knorm_ref[bi] = jnp.maximum(knorm_ref[bi], head_sq_norm_max(k32))
            return carry

        lax.fori_loop(0, n_blocks, chunk, 0)

    q_aug = []
    reach = []
    for bi in range(nb):
        q2 = q_ref[bi]
        qk_sq = head_sq_norm_max(q2.astype(F32)) * knorm_ref[bi]
        for hh in range(HEADS_PER_BLOCK):
            ones = jnp.where(piece_lanes(hh), 1.0, 0.0).astype(BF16)
            q_aug.append(jnp.where(_head_lane_mask(hh), q2, jnp.broadcast_to(ones, q2.shape)))
            reach.append(jnp.sqrt(head_col(qk_sq, hh)) * NORM_SLACK)

    if moba:
        blk = lax.broadcasted_iota(jnp.int32, (n_blocks, 1), 0)
        blk_f = blk.astype(F32)
        for st, (bi, hh) in enumerate(streams):
            km = _split3(kmean_ref[bi])
            q2 = q_ref[bi]
            qh = jnp.where(_head_lane_mask(hh), q2, jnp.zeros_like(q2))
            route = _dot_nt(km[0], qh) + _dot_nt(km[1], qh) + _dot_nt(km[2], qh)
            route = jnp.where(blk < qi, route, NEG_INF)
            sel = jnp.zeros((n_blocks, t), F32)
            for _ in range(MOBA_TOPK):
                top = jnp.max(route, axis=0, keepdims=True)
                first = jnp.min(jnp.where(route == top, blk_f, float(n_blocks)), axis=0, keepdims=True)
                hit = blk_f == first
                sel = jnp.where(hit, 1.0, sel)
                route = jnp.where(hit, -jnp.inf, route)
            sel_ref[st] = jnp.where(blk < qi, sel, 0.0)

    valid = _tile_rel(t) <= 0

    def scores(s_ref, j):
        rows = pl.ds(pl.multiple_of(j * t, t), t)
        for st in range(len(streams)):
            s_ref[st] = _dot_nt(kaug_ref[st, rows, :], q_aug[st])

    def accumulate(s_ref, j, diag):
        rows = pl.ds(pl.multiple_of(j * t, t), t)
        for st in range(len(streams)):
            s_t = s_ref[st]
            if diag:
                s_t = jnp.where(valid, s_t, NEG_INF)
            m_old = m_ref[st]
            m_new = jnp.maximum(m_old, jnp.max(s_t, axis=0, keepdims=True))
            m_sub = m_new
            if moba and not diag:
                picked = sel_ref[st, pl.ds(j, 1), :] > 0.0
                m_new = jnp.where(picked, m_new, m_old)
                m_sub = jnp.where(picked, m_new, -NEG_INF)
            alpha = jnp.exp2(m_old - m_new)
            p = jnp.exp2(s_t - m_sub)
            acc_ref[st] = alpha * acc_ref[st] + _dot(vt_ref[st, :, rows], p.astype(BF16))
            m_ref[st] = m_new

    m_ref[...] = jnp.full(m_ref.shape, NEG_INF, F32)
    acc_ref[...] = jnp.zeros(acc_ref.shape, F32)
    scores(s_even_ref, qi)
    scores(s_odd_ref, jnp.maximum(qi - 1, 0))
    accumulate(s_even_ref, qi, True)

    blk_col = lax.broadcasted_iota(jnp.int32, (n_blocks, 1), 0)
    fewest_dead = None
    for st in range(len(streams)):
        thr = jnp.min(m_ref[st], axis=1, keepdims=True) - FLASH_SKIP - reach[st]
        dead = (bmax_ref[st] < thr) & (blk_col < qi)
        n_dead = jnp.sum(jnp.where(dead, 1.0, 0.0), axis=0, keepdims=True)
        fewest_dead = n_dead if fewest_dead is None else jnp.minimum(fewest_dead, n_dead)
    n_past = qi - jnp.max(fewest_dead).astype(jnp.int32)

    def body(i, carry):
        j = qi - i
        j_next = jnp.maximum(j - 1, 0)

        @pl.when(i % 2 == 1)
        def _():
            scores(s_even_ref, j_next)
            accumulate(s_odd_ref, j, False)

        @pl.when(i % 2 == 0)
        def _():
            scores(s_odd_ref, j_next)
            accumulate(s_even_ref, j, False)

        return carry

    lax.fori_loop(1, n_past + 1, body, 0)

    for bi in range(nb):
        halves = []
        for hh in range(HEADS_PER_BLOCK):
            acc = acc_ref[bi * HEADS_PER_BLOCK + hh]
            halves.append(acc[:HEAD_DIM] * (1.0 / acc[HEAD_DIM:HEAD_DIM + 1]))
        o_ref[bi] = jnp.concatenate(halves, axis=0).T.astype(o_ref.dtype)


def _flash_attention(proj3, key_bias3, d, moba):
    b, s, _ = proj3.shape
    t = ATT_TILE
    nb = FLASH_BATCH if b % FLASH_BATCH == 0 else 1
    n_blocks = s // t
    n_streams = nb * HEADS_PER_BLOCK
    in_specs, out_spec = _mixer_specs(2 if moba else 1, nb, t, s, d)
    operands = [proj3, proj3, proj3]
    scratch = [pltpu.VMEM((n_streams, s, LANES), BF16),
               pltpu.VMEM((n_streams, V_ROWS, s), BF16),
               pltpu.VMEM((n_streams, V_ROWS, t), F32),
               pltpu.VMEM((n_streams, 1, t), F32),
               pltpu.VMEM((n_streams, t, t), F32),
               pltpu.VMEM((n_streams, t, t), F32),
               pltpu.VMEM((n_streams, n_blocks, 1), F32),
               pltpu.VMEM((nb, 1, LANES), F32)]
    if moba:
        scratch += [pltpu.VMEM((nb, n_blocks, LANES), F32),
                    pltpu.VMEM((n_streams, n_blocks, t), F32)]
    else:
        in_specs.append(pl.BlockSpec((nb, s, LANES), lambda bg, hp, qi: (bg, 0, hp)))
        operands.append(key_bias3)
    return pl.pallas_call(
        functools.partial(_flash_kernel, moba=moba, nb=nb),
        grid=(b // nb, N_HEAD_BLOCKS, n_blocks),
        in_specs=in_specs,
        out_specs=out_spec,
        out_shape=jax.ShapeDtypeStruct((b, s, BRANCH_WIDTH), BF16),
        scratch_shapes=scratch,
        compiler_params=_cparams(("parallel", "parallel", "arbitrary")),
        name="moba_attn" if moba else "forgetting_attn",
    )(*operands)


def _mix_kernel(x_ref, oa_ref, ob_ref, oc_ref, g_ref, wb_ref, wo_ref, y_ref):
    d = x_ref.shape[1]
    mixed = None
    for n, o_ref in enumerate((oa_ref, ob_ref, oc_ref)):
        gate = jax.nn.sigmoid(g_ref[:, n * d:(n + 1) * d].astype(F32))
        term = gate * _dot(o_ref[...], wb_ref[n])
        mixed = term if mixed is None else mixed + term
    y_ref[...] = x_ref[...] + _dot(mixed.astype(BF16), wo_ref[...])


def _mix(x, o_a, o_b, o_c, proj, wb, wo, layer, tm):
    n, d = x.shape
    o_spec = pl.BlockSpec((tm, BRANCH_WIDTH), lambda i: (i, 0))
    return pl.pallas_call(
        _mix_kernel,
        grid=(n // tm,),
        in_specs=[
            pl.BlockSpec((tm, d), lambda i: (i, 0)),
            o_spec, o_spec, o_spec,
            pl.BlockSpec((tm, N_BRANCH * d), lambda i: (i, 0)),
            pl.BlockSpec((None, N_BRANCH, BRANCH_WIDTH, d), lambda i: (layer, 0, 0, 0)),
            pl.BlockSpec((None, d, d), lambda i: (layer, 0, 0)),
        ],
        out_specs=pl.BlockSpec((tm, d), lambda i: (i, 0)),
        out_shape=jax.ShapeDtypeStruct((n, d), F32),
        compiler_params=_cparams(("parallel",)),
        name="gated_mix_out_proj",
    )(x, o_a, o_b, o_c, proj, wb, wo)


def _mlp_kernel(x_ref, g_ref, wu_ref, wd_ref, gf_ref, y_ref, h_ref, acc_ref, *, final_norm):
    f = pl.program_id(1)

    @pl.when(f == 0)
    def _():
        h_ref[...] = _rms(x_ref[...], g_ref[...]).astype(BF16)
        acc_ref[...] = jnp.zeros_like(acc_ref)

    hid = jnp.square(jnp.maximum(_dot(h_ref[...], wu_ref[...]), 0.0))
    acc_ref[...] += _dot(hid.astype(BF16), wd_ref[...])

    @pl.when(f == pl.num_programs(1) - 1)
    def _():
        y = x_ref[...] + acc_ref[...]
        if final_norm:
            y = _rms(y, gf_ref[...])
        y_ref[...] = y


def _mlp(x, g, wu, wd, g_final, layer, final_norm, tm, tf):
    n, d = x.shape
    d_ff = wu.shape[2]
    return pl.pallas_call(
        functools.partial(_mlp_kernel, final_norm=final_norm),
        grid=(n // tm, d_ff // tf),
        in_specs=[
            pl.BlockSpec((tm, d), lambda i, f: (i, 0)),
            pl.BlockSpec((1, d), lambda i, f: (0, 0)),
            pl.BlockSpec((None, d, tf), lambda i, f: (layer, 0, f)),
            pl.BlockSpec((None, tf, d), lambda i, f: (layer, f, 0)),
            pl.BlockSpec((1, d), lambda i, f: (0, 0)),
        ],
        out_specs=pl.BlockSpec((tm, d), lambda i, f: (i, 0)),
        out_shape=jax.ShapeDtypeStruct((n, d), F32),
        scratch_shapes=[pltpu.VMEM((tm, d), BF16), pltpu.VMEM((tm, d), F32)],
        compiler_params=_cparams(("parallel", "arbitrary")),
        name="relu2_mlp",
    )(x, g, wu, wd, g_final)


REPACK_ROWS = 256


def _repack_kernel(w_ref, wf_ref, main_ref, f_ref):
    i = pl.program_id(1)
    q_chunks = BRANCH_WIDTH // REPACK_ROWS
    n_gate_chunks = N_BRANCH * w_ref.shape[2] // REPACK_ROWS
    mixer_chunk = jnp.maximum(i - n_gate_chunks, 0) % (3 * q_chunks)
    is_q = (i >= n_gate_chunks) & (mixer_chunk < q_chunks)
    scale = jnp.where(is_q, HEAD_DIM ** -0.5 * LOG2E, 1.0)
    main_ref[...] = (w_ref[0] * scale).astype(BF16)

    @pl.when(i == 0)
    def _():
        pad = jnp.zeros((f_ref.shape[0] - N_HEADS, f_ref.shape[1]), F32)
        f_ref[...] = jnp.concatenate([wf_ref[0], pad], axis=0).astype(BF16)


def _prep_in_proj(w_in):
    depth, d, cols = w_in.shape
    w_t = jnp.swapaxes(w_in, 1, 2)
    bw = BRANCH_WIDTH
    r = REPACK_ROWS
    f_lo = 6 * bw
    c_lo = f_lo + N_HEADS
    g_lo = c_lo + 3 * bw
    main_rows = cols - N_HEADS
    n_gate = (cols - g_lo) // r
    n_ab = f_lo // r

    def src_row(l, i):
        row = jnp.where(i < n_gate, g_lo + r * i,
                        jnp.where(i < n_gate + n_ab, r * (i - n_gate), c_lo + r * (i - n_gate - n_ab)))
        return l, pl.multiple_of(row, N_HEADS), 0

    return pl.pallas_call(
        _repack_kernel,
        grid=(depth, main_rows // r),
        in_specs=[pl.BlockSpec((pl.Element(1), pl.Element(r), pl.Element(d)), src_row),
                  pl.BlockSpec((pl.Element(1), pl.Element(N_HEADS), pl.Element(d)), lambda l, i: (l, f_lo, 0))],
        out_specs=[pl.BlockSpec((None, r, d), lambda l, i: (l, i, 0)),
                   pl.BlockSpec((None, LANES, d), lambda l, i: (l, 0, 0))],
        out_shape=[jax.ShapeDtypeStruct((depth, main_rows, d), BF16),
                   jax.ShapeDtypeStruct((depth, LANES, d), BF16)],
        compiler_params=_cparams(("parallel", "arbitrary")),
        name="repack_w_in",
    )(w_t, w_t)


def kernel(x, norm_mix, w_in, b_forget, w_branch, w_out, norm_mlp, w_up, w_down, norm_final):
    b, s, d = x.shape
    depth = w_in.shape[0]
    n = b * s
    assert s % ATT_TILE == 0 and d % LANES == 0

    w_main, w_f = _prep_in_proj(w_in)
    wb = w_branch.astype(BF16)
    wo = w_out.astype(BF16)
    wu = w_up.astype(BF16)
    wd = w_down.astype(BF16)
    b_f = jnp.pad(b_forget, ((0, 0), (0, LANES - N_HEADS)))[:, None, :]

    tm = min(1024, s)
    xf = x.reshape(n, d)
    for l in range(depth):
        g_mix = norm_mix[l][None, :]
        proj, key_bias = _in_proj(xf, g_mix, w_main, w_f, b_f[l], l, s, tm, 1536, 256)
        proj3 = proj.reshape(b, s, -1)
        o_a = _sb_attention(proj3, d).reshape(n, -1)
        o_b = _flash_attention(proj3, key_bias.reshape(b, s, -1), d, False).reshape(n, -1)
        o_c = _flash_attention(proj3, None, d, True).reshape(n, -1)
        xf = _mix(xf, o_a, o_b, o_c, proj, wb, wo, l, min(512, n))
        xf = _mlp(xf, norm_mlp[l][None, :], wu, wd, norm_final[None, :],
                  l, l == depth - 1, min(1024, n), 1024)
    return xf.reshape(b, s, d)
```

```python
import functools
import math

import jax
import jax.numpy as jnp
import numpy as np
from jax import lax
from jax.experimental import pallas as pl
from jax.experimental.pallas import tpu as pltpu

F32 = jnp.float32
BF16 = jnp.bfloat16

HEAD_DIM = 64
N_HEADS = 8
BRANCH_WIDTH = N_HEADS * HEAD_DIM
N_BRANCH = 3
MOBA_BLOCK = 256
MOBA_TOPK = 3
RMS_EPS = 1e-6
NEG_INF = -1e30
LOG2E = math.log2(math.e)
LANES = 128
HEADS_PER_BLOCK = LANES // HEAD_DIM
N_HEAD_BLOCKS = BRANCH_WIDTH // LANES
N_BIAS_PIECES = 3
ATT_TILE = 256
SB_EXIT = 160.0
FLASH_SKIP = 160.0
NORM_SLACK = 1.001
FLASH_BATCH = 4
V_ROWS = HEAD_DIM + 16
VMEM_LIMIT = 56 * 1024 * 1024


def _cparams(sem):
    return pltpu.CompilerParams(dimension_semantics=sem, vmem_limit_bytes=VMEM_LIMIT)


def _rms(x, g):
    ms = jnp.mean(x * x, axis=-1, keepdims=True)
    return x * lax.rsqrt(ms + RMS_EPS) * g


def _dot(a, b):
    return jnp.dot(a, b, preferred_element_type=F32)


def _dot_nt(a, b):
    return lax.dot_general(a, b, (((1,), (1,)), ((), ())), preferred_element_type=F32)


def _split2(x):
    hi = x.astype(BF16)
    lo = (x - hi.astype(F32)).astype(BF16)
    return hi, lo


def _split3(x):
    hi = x.astype(BF16)
    r = x - hi.astype(F32)
    mid = r.astype(BF16)
    lo = (r - mid.astype(F32)).astype(BF16)
    return hi, mid, lo


def _bias_lane(hh):
    return (1 - hh) * HEAD_DIM


def _in_proj_kernel(x_ref, g_ref, w_ref, wf_ref, b_ref, tri_ref, place_ref,
                    o_ref, e_ref, h_ref, carry_ref, *, tiles_per_seq):
    @pl.when(pl.program_id(1) == 0)
    def _():
        h = _rms(x_ref[...], g_ref[...]).astype(BF16)
        h_ref[...] = h

        @pl.when(pl.program_id(0) % tiles_per_seq == 0)
        def _():
            carry_ref[...] = jnp.zeros_like(carry_ref)

        y = _dot_nt(h, wf_ref[...]) + b_ref[...]
        logf = jnp.minimum(y, 0.0) - jnp.log(1.0 + jnp.exp(-jnp.abs(y)))
        pieces = _split3(logf)
        tri = tri_ref[...]
        tc = tri.shape[0]
        cum = carry_ref[...]
        for c in range(h.shape[0] // tc):
            rows = slice(c * tc, (c + 1) * tc)
            cum = cum[-1:, :]
            for piece in pieces:
                cum = cum + _dot(tri, piece[rows])
            bias_pieces = jnp.concatenate(_split3(cum * (-LOG2E)), axis=1)
            e_ref[rows, :] = _dot(bias_pieces, place_ref[...]).astype(BF16)
        carry_ref[...] = cum[-1:, :]

    o_ref[...] = _dot_nt(h_ref[...], w_ref[...]).astype(o_ref.dtype)


def _bias_placement():
    place = np.zeros((N_BIAS_PIECES, LANES, N_HEAD_BLOCKS * LANES), np.float32)
    for h in range(N_HEADS):
        hp, hh = divmod(h, HEADS_PER_BLOCK)
        for i in range(N_BIAS_PIECES):
            place[i, h, hp * LANES + _bias_lane(hh) + i] = 1.0
    return jnp.asarray(place.reshape(N_BIAS_PIECES * LANES, -1), BF16)


def _in_proj(x, g, w, w_f, b_f, layer, s, tm, tn, tc):
    n, d = x.shape
    cols = w.shape[1]
    tri = jnp.asarray(np.tril(np.ones((tc, tc), np.float32)), BF16)
    place = _bias_placement()
    e_cols = place.shape[1]
    const2 = lambda i, j: (0, 0)
    return pl.pallas_call(
        functools.partial(_in_proj_kernel, tiles_per_seq=s // tm),
        grid=(n // tm, cols // tn),
        in_specs=[
            pl.BlockSpec((tm, d), lambda i, j: (i, 0)),
            pl.BlockSpec((1, d), const2),
            pl.BlockSpec((None, tn, d), lambda i, j: (layer, j, 0)),
            pl.BlockSpec((None, LANES, d), lambda i, j: (layer, 0, 0)),
            pl.BlockSpec((1, LANES), const2),
            pl.BlockSpec((tc, tc), const2),
            pl.BlockSpec(place.shape, const2),
        ],
        out_specs=[pl.BlockSpec((tm, tn), lambda i, j: (i, j)),
                   pl.BlockSpec((tm, e_cols), lambda i, j: (i, 0))],
        out_shape=[jax.ShapeDtypeStruct((n, cols), BF16),
                   jax.ShapeDtypeStruct((n, e_cols), BF16)],
        scratch_shapes=[pltpu.VMEM((tm, d), BF16), pltpu.VMEM((1, LANES), F32)],
        compiler_params=_cparams(("arbitrary", "arbitrary")),
        name="norm_in_proj",
    )(x, g, w, w_f, b_f, tri, place)


def _head_lane_mask(hh):
    lane = lax.broadcasted_iota(jnp.int32, (1, LANES), 1)
    return (lane >= hh * HEAD_DIM) & (lane < (hh + 1) * HEAD_DIM)


def _tile_rel(t):
    return (lax.broadcasted_iota(jnp.int32, (t, t), 0)
            - lax.broadcasted_iota(jnp.int32, (t, t), 1))


def _mixer_specs(mixer, nb, t, s, d):
    qoff = N_BRANCH * d // LANES + (3 * mixer) * N_HEAD_BLOCKS
    koff = qoff + N_HEAD_BLOCKS
    voff = koff + N_HEAD_BLOCKS
    return [
        pl.BlockSpec((nb, t, LANES), lambda bg, hp, qi: (bg, qi, qoff + hp)),
        pl.BlockSpec((nb, s, LANES), lambda bg, hp, qi: (bg, 0, koff + hp)),
        pl.BlockSpec((nb, s, LANES), lambda bg, hp, qi: (bg, 0, voff + hp)),
    ], pl.BlockSpec((nb, t, LANES), lambda bg, hp, qi: (bg, qi, hp))


def _sb_kernel(q_ref, k_ref, v_ref, o_ref, vt_ref, acc_ref, cs_ref, r_ref, s_even_ref, s_odd_ref, *, nb):
    t = q_ref.shape[1]
    n_blocks = k_ref.shape[1] // t
    qi = pl.program_id(2)
    streams = [(bi, hh) for bi in range(nb) for hh in range(HEADS_PER_BLOCK)]

    @pl.when(qi == 0)
    def _build():
        def chunk(c, carry):
            rows = pl.ds(pl.multiple_of(c * t, t), t)
            for st, (bi, hh) in enumerate(streams):
                v2t = v_ref[bi, rows, :].astype(F32).T
                vt_ref[st, :, rows] = v2t[hh * HEAD_DIM:(hh + 1) * HEAD_DIM, :].astype(BF16)
            return carry

        lax.fori_loop(0, n_blocks, chunk, 0)

    q_m = []
    for bi, hh in streams:
        q2 = q_ref[bi]
        q_m.append(jnp.where(_head_lane_mask(hh), q2, jnp.zeros_like(q2)))

    rel = _tile_rel(t)
    strictly_past = rel < 0
    upper = jnp.where(rel <= 0, 1.0, 0.0).astype(BF16)

    def scores(s_ref, j):
        rows = pl.ds(pl.multiple_of(j * t, t), t)
        for st, (bi, hh) in enumerate(streams):
            s_ref[st] = _dot_nt(k_ref[bi, rows, :], q_m[st])

    def accumulate(s_ref, j, diag):
        rows = pl.ds(pl.multiple_of(j * t, t), t)
        for st in range(len(streams)):
            z = s_ref[st]
            sp = jnp.maximum(z, 0.0) + jnp.log2(1.0 + jnp.exp2(-jnp.abs(z)))
            if diag:
                sp = jnp.where(strictly_past, sp, 0.0)
            r = _dot(upper, sp.astype(BF16)) + cs_ref[st]
            r_ref[st] = r
            cs_ref[st] = r[0:1, :]
        for st in range(len(streams)):
            w = jnp.exp2(s_ref[st] - r_ref[st])
            if diag:
                w = jnp.where(strictly_past, w, 0.0)
            acc_ref[st] += _dot(vt_ref[st, :, rows], w.astype(BF16))

    def next_distance(i):
        lowest = cs_ref[0]
        for st in range(1, len(streams)):
            lowest = jnp.minimum(lowest, cs_ref[st])
        return jnp.where(jnp.min(lowest) > SB_EXIT, qi + 1, i + 1)

    acc_ref[...] = jnp.zeros(acc_ref.shape, F32)
    cs_ref[...] = jnp.zeros(cs_ref.shape, F32)
    scores(s_even_ref, qi)
    scores(s_odd_ref, jnp.maximum(qi - 1, 0))
    accumulate(s_even_ref, qi, True)

    def body(i):
        j = qi - i
        j_next = jnp.maximum(j - 1, 0)

        @pl.when(i % 2 == 1)
        def _():
            scores(s_even_ref, j_next)
            accumulate(s_odd_ref, j, False)

        @pl.when(i % 2 == 0)
        def _():
            scores(s_odd_ref, j_next)
            accumulate(s_even_ref, j, False)

        return next_distance(i)

    lax.while_loop(lambda i: i <= qi, body, next_distance(0))

    for bi in range(nb):
        halves = [acc_ref[bi * HEADS_PER_BLOCK + hh] for hh in range(HEADS_PER_BLOCK)]
        o_ref[bi] = jnp.concatenate(halves, axis=0).T.astype(o_ref.dtype)


def _sb_attention(proj3, d):
    b, s, _ = proj3.shape
    t = ATT_TILE
    nb = FLASH_BATCH if b % FLASH_BATCH == 0 else 1
    n_streams = nb * HEADS_PER_BLOCK
    in_specs, out_spec = _mixer_specs(0, nb, t, s, d)
    return pl.pallas_call(
        functools.partial(_sb_kernel, nb=nb),
        grid=(b // nb, N_HEAD_BLOCKS, s // t),
        in_specs=in_specs,
        out_specs=out_spec,
        out_shape=jax.ShapeDtypeStruct((b, s, BRANCH_WIDTH), BF16),
        scratch_shapes=[pltpu.VMEM((n_streams, HEAD_DIM, s), BF16),
                        pltpu.VMEM((n_streams, HEAD_DIM, t), F32),
                        pltpu.VMEM((n_streams, 1, t), F32),
                        pltpu.VMEM((n_streams, t, t), F32),
                        pltpu.VMEM((n_streams, t, t), F32),
                        pltpu.VMEM((n_streams, t, t), F32)],
        compiler_params=_cparams(("parallel", "parallel", "arbitrary")),
        name="stickbreak_attn",
    )(proj3, proj3, proj3)


def _flash_kernel(*refs, moba, nb):
    if moba:
        (q_ref, k_ref, v_ref, o_ref, kaug_ref, vt_ref, acc_ref, m_ref, s_even_ref, s_odd_ref,
         bmax_ref, knorm_ref, kmean_ref, sel_ref) = refs
    else:
        (q_ref, k_ref, v_ref, e_ref, o_ref, kaug_ref, vt_ref, acc_ref, m_ref, s_even_ref,
         s_odd_ref, bmax_ref, knorm_ref) = refs
    t = q_ref.shape[1]
    n_blocks = k_ref.shape[1] // t
    hp = pl.program_id(1)
    qi = pl.program_id(2)
    lane = lax.broadcasted_iota(jnp.int32, (1, LANES), 1)
    streams = [(bi, hh) for bi in range(nb) for hh in range(HEADS_PER_BLOCK)]

    def piece_lanes(hh):
        e0 = _bias_lane(hh)
        return (lane >= e0) & (lane < e0 + N_BIAS_PIECES)

    head_of_lane = jnp.where(lax.broadcasted_iota(jnp.int32, (LANES, LANES), 0) >= HEAD_DIM, 1, 0)
    head_sum = jnp.where(head_of_lane == lax.broadcasted_iota(jnp.int32, (LANES, LANES), 1), 1.0, 0.0).astype(BF16)

    def head_sq_norm_max(x32):
        hi, lo = _split2(x32 * x32)
        return jnp.max(_dot(hi, head_sum) + _dot(lo, head_sum), axis=0, keepdims=True)

    def head_col(row, hh):
        return jnp.sum(jnp.where(lane == hh, row, 0.0), axis=1, keepdims=True)

    @pl.when(qi == 0)
    def _build():
        bmax_ref[...] = jnp.full(bmax_ref.shape, NEG_INF, F32)
        knorm_ref[...] = jnp.zeros(knorm_ref.shape, F32)
        tail = jnp.where(lax.broadcasted_iota(jnp.int32, (V_ROWS - HEAD_DIM, t), 0) == 0, 1.0, 0.0)

        def chunk(c, carry):
            start = pl.multiple_of(c * t, t)
            rows = pl.ds(start, t)
            extras = []
            if moba:
                pos = (lax.broadcasted_iota(jnp.int32, (t, LANES), 0) + start).astype(F32)
                for hh in range(HEADS_PER_BLOCK):
                    h = hp * HEADS_PER_BLOCK + hh
                    slope = pltpu.bitcast(jnp.full((1, LANES), (126 - h) << 23, jnp.int32), F32)
                    extra = jnp.zeros((t, LANES), F32)
                    for i, piece in enumerate(_split3(pos * slope * LOG2E)):
                        extra = jnp.where(lane == _bias_lane(hh) + i, piece.astype(F32), extra)
                    extras.append(extra.astype(BF16))
            for bi in range(nb):
                k2 = k_ref[bi, rows, :]
                k32 = k2.astype(F32)
                v2t = v_ref[bi, rows, :].astype(F32).T
                if moba:
                    kmean_ref[bi, pl.ds(c, 1), :] = jnp.sum(k32, axis=0, keepdims=True) * (1.0 / t)
                for hh in range(HEADS_PER_BLOCK):
                    st = bi * HEADS_PER_BLOCK + hh
                    extra = extras[hh] if moba else e_ref[bi, rows, :]
                    kaug_ref[st, rows, :] = jnp.where(_head_lane_mask(hh), k2, extra)
                    v_h = v2t[hh * HEAD_DIM:(hh + 1) * HEAD_DIM, :]
                    vt_ref[st, :, rows] = jnp.concatenate([v_h, tail], axis=0).astype(BF16)
                    piece_max = jnp.max(extra.astype(F32), axis=0, keepdims=True)
                    bias_max = jnp.sum(jnp.where(piece_lanes(hh), piece_max, 0.0), axis=1, keepdims=True)
                    run = jnp.maximum(bmax_ref[st, pl.ds(jnp.maximum(c - 1, 0), 1), :], bias_max)
                    bmax_ref[st, pl.ds(c, 1), :] = run
                knorm_ref[bi] = jnp.maximum(knorm_ref[bi], head_sq_norm_max(k32))
            return carry

        lax.fori_loop(0, n_blocks, chunk, 0)

    q_aug = []
    reach = []
    for bi in range(nb):
        q2 = q_ref[bi]
        qk_sq = head_sq_norm_max(q2.astype(F32)) * knorm_ref[bi]
        for hh in range(HEADS_PER_BLOCK):
            ones = jnp.where(piece_lanes(hh), 1.0, 0.0).astype(BF16)
            q_aug.append(jnp.where(_head_lane_mask(hh), q2, jnp.broadcast_to(ones, q2.shape)))
            reach.append(jnp.sqrt(head_col(qk_sq, hh)) * NORM_SLACK)

    if moba:
        blk = lax.broadcasted_iota(jnp.int32, (n_blocks, 1), 0)
        blk_f = blk.astype(F32)
        for st, (bi, hh) in enumerate(streams):
            km = _split3(kmean_ref[bi])
            q2 = q_ref[bi]
            qh = jnp.where(_head_lane_mask(hh), q2, jnp.zeros_like(q2))
            route = _dot_nt(km[0], qh) + _dot_nt(km[1], qh) + _dot_nt(km[2], qh)
            route = jnp.where(blk < qi, route, NEG_INF)
            sel = jnp.zeros((n_blocks, t), F32)
            for _ in range(MOBA_TOPK):
                top = jnp.max(route, axis=0, keepdims=True)
                first = jnp.min(jnp.where(route == top, blk_f, float(n_blocks)), axis=0, keepdims=True)
                hit = blk_f == first
                sel = jnp.where(hit, 1.0, sel)
                route = jnp.where(hit, -jnp.inf, route)
            sel_ref[st] = jnp.where(blk < qi, sel, 0.0)

    valid = _tile_rel(t) <= 0

    def scores(s_ref, j):
        rows = pl.ds(pl.multiple_of(j * t, t), t)
        for st in range(len(streams)):
            s_ref[st] = _dot_nt(kaug_ref[st, rows, :], q_aug[st])

    def accumulate(s_ref, j, diag):
        rows = pl.ds(pl.multiple_of(j * t, t), t)
        for st in range(len(streams)):
            s_t = s_ref[st]
            if diag:
                s_t = jnp.where(valid, s_t, NEG_INF)
            m_old = m_ref[st]
            m_new = jnp.maximum(m_old, jnp.max(s_t, axis=0, keepdims=True))
            m_sub = m_new
            if moba and not diag:
                picked = sel_ref[st, pl.ds(j, 1), :] > 0.0
                m_new = jnp.where(picked, m_new, m_old)
                m_sub = jnp.where(picked, m_new, -NEG_INF)
            alpha = jnp.exp2(m_old - m_new)
            p = jnp.exp2(s_t - m_sub)
            acc_ref[st] = alpha * acc_ref[st] + _dot(vt_ref[st, :, rows], p.astype(BF16))
            m_ref[st] = m_new

    m_ref[...] = jnp.full(m_ref.shape, NEG_INF, F32)
    acc_ref[...] = jnp.zeros(acc_ref.shape, F32)
    scores(s_even_ref, qi)
    scores(s_odd_ref, jnp.maximum(qi - 1, 0))
    accumulate(s_even_ref, qi, True)

    blk_col = lax.broadcasted_iota(jnp.int32, (n_blocks, 1), 0)
    fewest_dead = None
    for st in range(len(streams)):
        thr = jnp.min(m_ref[st], axis=1, keepdims=True) - FLASH_SKIP - reach[st]
        dead = (bmax_ref[st] < thr) & (blk_col < qi)
        n_dead = jnp.sum(jnp.where(dead, 1.0, 0.0), axis=0, keepdims=True)
        fewest_dead = n_dead if fewest_dead is None else jnp.minimum(fewest_dead, n_dead)
    n_past = qi - jnp.max(fewest_dead).astype(jnp.int32)

    def body(i, carry):
        j = qi - i
        j_next = jnp.maximum(j - 1, 0)

        @pl.when(i % 2 == 1)
        def _():
            scores(s_even_ref, j_next)
            accumulate(s_odd_ref, j, False)

        @pl.when(i % 2 == 0)
        def _():
            scores(s_odd_ref, j_next)
            accumulate(s_even_ref, j, False)

        return carry

    lax.fori_loop(1, n_past + 1, body, 0)

    for bi in range(nb):
        halves = []
        for hh in range(HEADS_PER_BLOCK):
            acc = acc_ref[bi * HEADS_PER_BLOCK + hh]
            halves.append(acc[:HEAD_DIM] * (1.0 / acc[HEAD_DIM:HEAD_DIM + 1]))
        o_ref[bi] = jnp.concatenate(halves, axis=0).T.astype(o_ref.dtype)


def _flash_attention(proj3, key_bias3, d, moba):
    b, s, _ = proj3.shape
    t = ATT_TILE
    nb = FLASH_BATCH if b % FLASH_BATCH == 0 else 1
    n_blocks = s // t
    n_streams = nb * HEADS_PER_BLOCK
    in_specs, out_spec = _mixer_specs(2 if moba else 1, nb, t, s, d)
    operands = [proj3, proj3, proj3]
    scratch = [pltpu.VMEM((n_streams, s, LANES), BF16),
               pltpu.VMEM((n_streams, V_ROWS, s), BF16),
               pltpu.VMEM((n_streams, V_ROWS, t), F32),
               pltpu.VMEM((n_streams, 1, t), F32),
               pltpu.VMEM((n_streams, t, t), F32),
               pltpu.VMEM((n_streams, t, t), F32),
               pltpu.VMEM((n_streams, n_blocks, 1), F32),
               pltpu.VMEM((nb, 1, LANES), F32)]
    if moba:
        scratch += [pltpu.VMEM((nb, n_blocks, LANES), F32),
                    pltpu.VMEM((n_streams, n_blocks, t), F32)]
    else:
        in_specs.append(pl.BlockSpec((nb, s, LANES), lambda bg, hp, qi: (bg, 0, hp)))
        operands.append(key_bias3)
    return pl.pallas_call(
        functools.partial(_flash_kernel, moba=moba, nb=nb),
        grid=(b // nb, N_HEAD_BLOCKS, n_blocks),
        in_specs=in_specs,
        out_specs=out_spec,
        out_shape=jax.ShapeDtypeStruct((b, s, BRANCH_WIDTH), BF16),
        scratch_shapes=scratch,
        compiler_params=_cparams(("parallel", "parallel", "arbitrary")),
        name="moba_attn" if moba else "forgetting_attn",
    )(*operands)


def _mix_kernel(x_ref, oa_ref, ob_ref, oc_ref, g_ref, wb_ref, wo_ref, y_ref):
    d = x_ref.shape[1]
    mixed = None
    for n, o_ref in enumerate((oa_ref, ob_ref, oc_ref)):
        gate = jax.nn.sigmoid(g_ref[:, n * d:(n + 1) * d].astype(F32))
        term = gate * _dot(o_ref[...], wb_ref[n])
        mixed = term if mixed is None else mixed + term
    y_ref[...] = x_ref[...] + _dot(mixed.astype(BF16), wo_ref[...])


def _mix(x, o_a, o_b, o_c, proj, wb, wo, layer, tm):
    n, d = x.shape
    o_spec = pl.BlockSpec((tm, BRANCH_WIDTH), lambda i: (i, 0))
    return pl.pallas_call(
        _mix_kernel,
        grid=(n // tm,),
        in_specs=[
            pl.BlockSpec((tm, d), lambda i: (i, 0)),
            o_spec, o_spec, o_spec,
            pl.BlockSpec((tm, N_BRANCH * d), lambda i: (i, 0)),
            pl.BlockSpec((None, N_BRANCH, BRANCH_WIDTH, d), lambda i: (layer, 0, 0, 0)),
            pl.BlockSpec((None, d, d), lambda i: (layer, 0, 0)),
        ],
        out_specs=pl.BlockSpec((tm, d), lambda i: (i, 0)),
        out_shape=jax.ShapeDtypeStruct((n, d), F32),
        compiler_params=_cparams(("parallel",)),
        name="gated_mix_out_proj",
    )(x, o_a, o_b, o_c, proj, wb, wo)


def _mlp_kernel(x_ref, g_ref, wu_ref, wd_ref, gf_ref, y_ref, h_ref, acc_ref, *, final_norm):
    f = pl.program_id(1)

    @pl.when(f == 0)
    def _():
        h_ref[...] = _rms(x_ref[...], g_ref[...]).astype(BF16)
        acc_ref[...] = jnp.zeros_like(acc_ref)

    hid = jnp.square(jnp.maximum(_dot(h_ref[...], wu_ref[...]), 0.0))
    acc_ref[...] += _dot(hid.astype(BF16), wd_ref[...])

    @pl.when(f == pl.num_programs(1) - 1)
    def _():
        y = x_ref[...] + acc_ref[...]
        if final_norm:
            y = _rms(y, gf_ref[...])
        y_ref[...] = y


def _mlp(x, g, wu, wd, g_final, layer, final_norm, tm, tf):
    n, d = x.shape
    d_ff = wu.shape[2]
    return pl.pallas_call(
        functools.partial(_mlp_kernel, final_norm=final_norm),
        grid=(n // tm, d_ff // tf),
        in_specs=[
            pl.BlockSpec((tm, d), lambda i, f: (i, 0)),
            pl.BlockSpec((1, d), lambda i, f: (0, 0)),
            pl.BlockSpec((None, d, tf), lambda i, f: (layer, 0, f)),
            pl.BlockSpec((None, tf, d), lambda i, f: (layer, f, 0)),
            pl.BlockSpec((1, d), lambda i, f: (0, 0)),
        ],
        out_specs=pl.BlockSpec((tm, d), lambda i, f: (i, 0)),
        out_shape=jax.ShapeDtypeStruct((n, d), F32),
        scratch_shapes=[pltpu.VMEM((tm, d), BF16), pltpu.VMEM((tm, d), F32)],
        compiler_params=_cparams(("parallel", "arbitrary")),
        name="relu2_mlp",
    )(x, g, wu, wd, g_final)


REPACK_ROWS = 512


def _repack_kernel(w_ref, wf_ref, main_ref, f_ref):
    i = pl.program_id(1)
    q_chunks = BRANCH_WIDTH // REPACK_ROWS
    n_gate_chunks = N_BRANCH * w_ref.shape[2] // REPACK_ROWS
    mixer_chunk = jnp.maximum(i - n_gate_chunks, 0) % (3 * q_chunks)
    is_q = (i >= n_gate_chunks) & (mixer_chunk < q_chunks)
    scale = jnp.where(is_q, HEAD_DIM ** -0.5 * LOG2E, 1.0)
    main_ref[...] = (w_ref[0] * scale).astype(BF16)

    @pl.when(i == 0)
    def _():
        pad = jnp.zeros((f_ref.shape[0] - N_HEADS, f_ref.shape[1]), F32)
        f_ref[...] = jnp.concatenate([wf_ref[0], pad], axis=0).astype(BF16)


def _prep_in_proj(w_in):
    depth, d, cols = w_in.shape
    w_t = jnp.swapaxes(w_in, 1, 2)
    bw = BRANCH_WIDTH
    r = REPACK_ROWS
    f_lo = 6 * bw
    c_lo = f_lo + N_HEADS
    g_lo = c_lo + 3 * bw
    main_rows = cols - N_HEADS
    n_gate = (cols - g_lo) // r
    n_ab = f_lo // r

    def src_row(l, i):
        row = jnp.where(i < n_gate, g_lo + r * i,
                        jnp.where(i < n_gate + n_ab, r * (i - n_gate), c_lo + r * (i - n_gate - n_ab)))
        return l, pl.multiple_of(row, N_HEADS), 0

    return pl.pallas_call(
        _repack_kernel,
        grid=(depth, main_rows // r),
        in_specs=[pl.BlockSpec((pl.Element(1), pl.Element(r), pl.Element(d)), src_row),
                  pl.BlockSpec((pl.Element(1), pl.Element(N_HEADS), pl.Element(d)), lambda l, i: (l, f_lo, 0))],
        out_specs=[pl.BlockSpec((None, r, d), lambda l, i: (l, i, 0)),
                   pl.BlockSpec((None, LANES, d), lambda l, i: (l, 0, 0))],
        out_shape=[jax.ShapeDtypeStruct((depth, main_rows, d), BF16),
                   jax.ShapeDtypeStruct((depth, LANES, d), BF16)],
        compiler_params=_cparams(("parallel", "arbitrary")),
        name="repack_w_in",
    )(w_t, w_t)


def kernel(x, norm_mix, w_in, b_forget, w_branch, w_out, norm_mlp, w_up, w_down, norm_final):
    b, s, d = x.shape
    depth = w_in.shape[0]
    n = b * s
    assert s % ATT_TILE == 0 and d % LANES == 0

    w_main, w_f = _prep_in_proj(w_in)
    wb = w_branch.astype(BF16)
    wo = w_out.astype(BF16)
    wu = w_up.astype(BF16)
    wd = w_down.astype(BF16)
    b_f = jnp.pad(b_forget, ((0, 0), (0, LANES - N_HEADS)))[:, None, :]

    tm = min(2048, s)
    xf = x.reshape(n, d)
    for l in range(depth):
        g_mix = norm_mix[l][None, :]
        proj, key_bias = _in_proj(xf, g_mix, w_main, w_f, b_f[l], l, s, tm, 768, 256)
        proj3 = proj.reshape(b, s, -1)
        o_a = _sb_attention(proj3, d).reshape(n, -1)
        o_b = _flash_attention(proj3, key_bias.reshape(b, s, -1), d, False).reshape(n, -1)
        o_c = _flash_attention(proj3, None, d, True).reshape(n, -1)
        xf = _mix(xf, o_a, o_b, o_c, proj, wb, wo, l, min(512, n))
        xf = _mlp(xf, norm_mlp[l][None, :], wu, wd, norm_final[None, :],
                  l, l == depth - 1, min(1024, n), 1024)
    return xf.reshape(b, s, d)
```

```python
import functools
import math

import jax
import jax.numpy as jnp
import numpy as np
from jax import lax
from jax.experimental import pallas as pl
from jax.experimental.pallas import tpu as pltpu

F32 = jnp.float32
BF16 = jnp.bfloat16

HEAD_DIM = 64
N_HEADS = 8
BRANCH_WIDTH = N_HEADS * HEAD_DIM
N_BRANCH = 3
MOBA_BLOCK = 256
MOBA_TOPK = 3
RMS_EPS = 1e-6
NEG_INF = -1e30
LOG2E = math.log2(math.e)
LANES = 128
HEADS_PER_BLOCK = LANES // HEAD_DIM
N_HEAD_BLOCKS = BRANCH_WIDTH // LANES
N_BIAS_PIECES = 3
ATT_TILE = 256
SB_EXIT = 160.0
FLASH_SKIP = 160.0
NORM_SLACK = 1.001
FLASH_BATCH = 4
V_ROWS = HEAD_DIM + 16
VMEM_LIMIT = 60000 * 1024


def _cparams(sem):
    return pltpu.CompilerParams(dimension_semantics=sem, vmem_limit_bytes=VMEM_LIMIT)


def _rms(x, g):
    ms = jnp.mean(x * x, axis=-1, keepdims=True)
    return x * lax.rsqrt(ms + RMS_EPS) * g


def _dot(a, b):
    return jnp.dot(a, b, preferred_element_type=F32)


def _dot_nt(a, b):
    return lax.dot_general(a, b, (((1,), (1,)), ((), ())), preferred_element_type=F32)


def _split2(x):
    hi = x.astype(BF16)
    lo = (x - hi.astype(F32)).astype(BF16)
    return hi, lo


def _split3(x):
    hi = x.astype(BF16)
    r = x - hi.astype(F32)
    mid = r.astype(BF16)
    lo = (r - mid.astype(F32)).astype(BF16)
    return hi, mid, lo


def _bias_lane(hh):
    return (1 - hh) * HEAD_DIM


def _in_proj_kernel(x_ref, g_ref, w_ref, wf_ref, b_ref, tri_ref, place_ref,
                    o_ref, e_ref, h_ref, carry_ref, *, tiles_per_seq):
    @pl.when(pl.program_id(1) == 0)
    def _():
        h = _rms(x_ref[...], g_ref[...]).astype(BF16)
        h_ref[...] = h

        @pl.when(pl.program_id(0) % tiles_per_seq == 0)
        def _():
            carry_ref[...] = jnp.zeros_like(carry_ref)

        y = _dot_nt(h, wf_ref[...]) + b_ref[...]
        logf = jnp.minimum(y, 0.0) - jnp.log(1.0 + jnp.exp(-jnp.abs(y)))
        pieces = _split3(logf)
        tri = tri_ref[...]
        tc = tri.shape[0]
        cum = carry_ref[...]
        for c in range(h.shape[0] // tc):
            rows = slice(c * tc, (c + 1) * tc)
            cum = cum[-1:, :]
            for piece in pieces:
                cum = cum + _dot(tri, piece[rows])
            bias_pieces = jnp.concatenate(_split3(cum * (-LOG2E)), axis=1)
            e_ref[rows, :] = _dot(bias_pieces, place_ref[...]).astype(BF16)
        carry_ref[...] = cum[-1:, :]

    o_ref[...] = _dot_nt(h_ref[...], w_ref[...]).astype(o_ref.dtype)


def _bias_placement():
    place = np.zeros((N_BIAS_PIECES, LANES, N_HEAD_BLOCKS * LANES), np.float32)
    for h in range(N_HEADS):
        hp, hh = divmod(h, HEADS_PER_BLOCK)
        for i in range(N_BIAS_PIECES):
            place[i, h, hp * LANES + _bias_lane(hh) + i] = 1.0
    return jnp.asarray(place.reshape(N_BIAS_PIECES * LANES, -1), BF16)


def _in_proj(x, g, w, w_f, b_f, layer, s, tm, tn, tc):
    n, d = x.shape
    cols = w.shape[1]
    tri = jnp.asarray(np.tril(np.ones((tc, tc), np.float32)), BF16)
    place = _bias_placement()
    e_cols = place.shape[1]
    const2 = lambda i, j: (0, 0)
    return pl.pallas_call(
        functools.partial(_in_proj_kernel, tiles_per_seq=s // tm),
        grid=(n // tm, cols // tn),
        in_specs=[
            pl.BlockSpec((tm, d), lambda i, j: (i, 0)),
            pl.BlockSpec((1, d), const2),
            pl.BlockSpec((None, tn, d), lambda i, j: (layer, j, 0)),
            pl.BlockSpec((None, LANES, d), lambda i, j: (layer, 0, 0)),
            pl.BlockSpec((1, LANES), const2),
            pl.BlockSpec((tc, tc), const2),
            pl.BlockSpec(place.shape, const2),
        ],
        out_specs=[pl.BlockSpec((tm, tn), lambda i, j: (i, j)),
                   pl.BlockSpec((tm, e_cols), lambda i, j: (i, 0))],
        out_shape=[jax.ShapeDtypeStruct((n, cols), BF16),
                   jax.ShapeDtypeStruct((n, e_cols), BF16)],
        scratch_shapes=[pltpu.VMEM((tm, d), BF16), pltpu.VMEM((1, LANES), F32)],
        compiler_params=_cparams(("arbitrary", "arbitrary")),
        name="norm_in_proj",
    )(x, g, w, w_f, b_f, tri, place)


def _head_lane_mask(hh):
    lane = lax.broadcasted_iota(jnp.int32, (1, LANES), 1)
    return (lane >= hh * HEAD_DIM) & (lane < (hh + 1) * HEAD_DIM)


def _tile_rel(t):
    return (lax.broadcasted_iota(jnp.int32, (t, t), 0)
            - lax.broadcasted_iota(jnp.int32, (t, t), 1))


def _mixer_specs(mixer, nb, t, s, d):
    qoff = N_BRANCH * d // LANES + (3 * mixer) * N_HEAD_BLOCKS
    koff = qoff + N_HEAD_BLOCKS
    voff = koff + N_HEAD_BLOCKS
    return [
        pl.BlockSpec((nb, t, LANES), lambda bg, hp, qi: (bg, qi, qoff + hp)),
        pl.BlockSpec((nb, s, LANES), lambda bg, hp, qi: (bg, 0, koff + hp)),
        pl.BlockSpec((nb, s, LANES), lambda bg, hp, qi: (bg, 0, voff + hp)),
    ], pl.BlockSpec((nb, t, LANES), lambda bg, hp, qi: (bg, qi, hp))


def _sb_kernel(q_ref, k_ref, v_ref, o_ref, vt_ref, acc_ref, cs_ref, r_ref, s_even_ref, s_odd_ref, *, nb):
    t = q_ref.shape[1]
    n_blocks = k_ref.shape[1] // t
    qi = pl.program_id(2)
    streams = [(bi, hh) for bi in range(nb) for hh in range(HEADS_PER_BLOCK)]

    @pl.when(qi == 0)
    def _build():
        def chunk(c, carry):
            rows = pl.ds(pl.multiple_of(c * t, t), t)
            for st, (bi, hh) in enumerate(streams):
                v2t = v_ref[bi, rows, :].astype(F32).T
                vt_ref[st, :, rows] = v2t[hh * HEAD_DIM:(hh + 1) * HEAD_DIM, :].astype(BF16)
            return carry

        lax.fori_loop(0, n_blocks, chunk, 0)

    q_m = []
    for bi, hh in streams:
        q2 = q_ref[bi]
        q_m.append(jnp.where(_head_lane_mask(hh), q2, jnp.zeros_like(q2)))

    rel = _tile_rel(t)
    strictly_past = rel < 0
    upper = jnp.where(rel <= 0, 1.0, 0.0).astype(BF16)

    stream_groups = [range(g, g + HEADS_PER_BLOCK) for g in range(0, len(streams), HEADS_PER_BLOCK)]

    def scores(s_ref, j, group):
        rows = pl.ds(pl.multiple_of(j * t, t), t)
        for st in group:
            s_ref[st] = _dot_nt(k_ref[streams[st][0], rows, :], q_m[st])

    def softplus_sums(s_ref, diag, group):
        for st in group:
            z = s_ref[st]
            sp = jnp.maximum(z, 0.0) + jnp.log2(1.0 + jnp.exp2(-jnp.abs(z)))
            if diag:
                sp = jnp.where(strictly_past, sp, 0.0)
            r = _dot(upper, sp.astype(BF16)) + cs_ref[st]
            r_ref[st] = r
            cs_ref[st] = r[0:1, :]

    def weigh_values(s_ref, j, diag, group):
        rows = pl.ds(pl.multiple_of(j * t, t), t)
        for st in group:
            w = jnp.exp2(s_ref[st] - r_ref[st])
            if diag:
                w = jnp.where(strictly_past, w, 0.0)
            acc_ref[st] += _dot(vt_ref[st, :, rows], w.astype(BF16))

    def step(cur_ref, next_ref, j, j_next, diag):
        for group in stream_groups:
            scores(next_ref, j_next, group)
            softplus_sums(cur_ref, diag, group)
        for group in stream_groups:
            weigh_values(cur_ref, j, diag, group)

    def next_distance(i):
        lowest = cs_ref[0]
        for st in range(1, len(streams)):
            lowest = jnp.minimum(lowest, cs_ref[st])
        return jnp.where(jnp.min(lowest) > SB_EXIT, qi + 1, i + 1)

    acc_ref[...] = jnp.zeros(acc_ref.shape, F32)
    cs_ref[...] = jnp.zeros(cs_ref.shape, F32)
    for group in stream_groups:
        scores(s_even_ref, qi, group)
    step(s_even_ref, s_odd_ref, qi, jnp.maximum(qi - 1, 0), True)

    def body(i):
        j = qi - i
        j_next = jnp.maximum(j - 1, 0)

        @pl.when(i % 2 == 1)
        def _():
            step(s_odd_ref, s_even_ref, j, j_next, False)

        @pl.when(i % 2 == 0)
        def _():
            step(s_even_ref, s_odd_ref, j, j_next, False)

        return next_distance(i)

    lax.while_loop(lambda i: i <= qi, body, next_distance(0))

    for bi in range(nb):
        halves = [acc_ref[bi * HEADS_PER_BLOCK + hh] for hh in range(HEADS_PER_BLOCK)]
        o_ref[bi] = jnp.concatenate(halves, axis=0).T.astype(o_ref.dtype)


def _sb_attention(proj3, d):
    b, s, _ = proj3.shape
    t = ATT_TILE
    nb = FLASH_BATCH if b % FLASH_BATCH == 0 else 1
    n_streams = nb * HEADS_PER_BLOCK
    in_specs, out_spec = _mixer_specs(0, nb, t, s, d)
    return pl.pallas_call(
        functools.partial(_sb_kernel, nb=nb),
        grid=(b // nb, N_HEAD_BLOCKS, s // t),
        in_specs=in_specs,
        out_specs=out_spec,
        out_shape=jax.ShapeDtypeStruct((b, s, BRANCH_WIDTH), BF16),
        scratch_shapes=[pltpu.VMEM((n_streams, HEAD_DIM, s), BF16),
                        pltpu.VMEM((n_streams, HEAD_DIM, t), F32),
                        pltpu.VMEM((n_streams, 1, t), F32),
                        pltpu.VMEM((n_streams, t, t), F32),
                        pltpu.VMEM((n_streams, t, t), F32),
                        pltpu.VMEM((n_streams, t, t), F32)],
        compiler_params=_cparams(("parallel", "parallel", "arbitrary")),
        name="stickbreak_attn",
    )(proj3, proj3, proj3)


def _flash_kernel(*refs, moba, nb):
    if moba:
        (q_ref, k_ref, v_ref, o_ref, kaug_ref, vt_ref, acc_ref, m_ref, s0_ref, s1_ref, s2_ref, s3_ref,
         bmax_ref, knorm_ref, kmean_ref, sel_ref) = refs
    else:
        (q_ref, k_ref, v_ref, e_ref, o_ref, kaug_ref, vt_ref, acc_ref, m_ref, s0_ref, s1_ref, s2_ref,
         s3_ref, bmax_ref, knorm_ref) = refs
    s_bufs = (s0_ref, s1_ref, s2_ref, s3_ref)
    t = q_ref.shape[1]
    n_blocks = k_ref.shape[1] // t
    hp = pl.program_id(1)
    qi = pl.program_id(2)
    lane = lax.broadcasted_iota(jnp.int32, (1, LANES), 1)
    streams = [(bi, hh) for bi in range(nb) for hh in range(HEADS_PER_BLOCK)]

    def piece_lanes(hh):
        e0 = _bias_lane(hh)
        return (lane >= e0) & (lane < e0 + N_BIAS_PIECES)

    head_of_lane = jnp.where(lax.broadcasted_iota(jnp.int32, (LANES, LANES), 0) >= HEAD_DIM, 1, 0)
    head_sum = jnp.where(head_of_lane == lax.broadcasted_iota(jnp.int32, (LANES, LANES), 1), 1.0, 0.0).astype(BF16)

    def head_sq_norm_max(x32):
        hi, lo = _split2(x32 * x32)
        return jnp.max(_dot(hi, head_sum) + _dot(lo, head_sum), axis=0, keepdims=True)

    def head_col(row, hh):
        return jnp.sum(jnp.where(lane == hh, row, 0.0), axis=1, keepdims=True)

    @pl.when(qi == 0)
    def _build():
        bmax_ref[...] = jnp.full(bmax_ref.shape, NEG_INF, F32)
        knorm_ref[...] = jnp.zeros(knorm_ref.shape, F32)
        tail = jnp.where(lax.broadcasted_iota(jnp.int32, (V_ROWS - HEAD_DIM, t), 0) == 0, 1.0, 0.0)

        def chunk(c, carry):
            start = pl.multiple_of(c * t, t)
            rows = pl.ds(start, t)
            extras = []
            if moba:
                pos = (lax.broadcasted_iota(jnp.int32, (t, LANES), 0) + start).astype(F32)
                for hh in range(HEADS_PER_BLOCK):
                    h = hp * HEADS_PER_BLOCK + hh
                    slope = pltpu.bitcast(jnp.full((1, LANES), (126 - h) << 23, jnp.int32), F32)
                    extra = jnp.zeros((t, LANES), F32)
                    for i, piece in enumerate(_split3(pos * slope * LOG2E)):
                        extra = jnp.where(lane == _bias_lane(hh) + i, piece.astype(F32), extra)
                    extras.append(extra.astype(BF16))
            for bi in range(nb):
                k2 = k_ref[bi, rows, :]
                k32 = k2.astype(F32)
                v2t = v_ref[bi, rows, :].astype(F32).T
                if moba:
                    kmean_ref[bi, pl.ds(c, 1), :] = jnp.sum(k32, axis=0, keepdims=True) * (1.0 / t)
                for hh in range(HEADS_PER_BLOCK):
                    st = bi * HEADS_PER_BLOCK + hh
                    extra = extras[hh] if moba else e_ref[bi, rows, :]
                    kaug_ref[st, rows, :] = jnp.where(_head_lane_mask(hh), k2, extra)
                    v_h = v2t[hh * HEAD_DIM:(hh + 1) * HEAD_DIM, :]
                    vt_ref[st, :, rows] = jnp.concatenate([v_h, tail], axis=0).astype(BF16)
                    piece_max = jnp.max(extra.astype(F32), axis=0, keepdims=True)
                    bias_max = jnp.sum(jnp.where(piece_lanes(hh), piece_max, 0.0), axis=1, keepdims=True)
                    run = jnp.maximum(bmax_ref[st, pl.ds(jnp.maximum(c - 1, 0), 1), :], bias_max)
                    bmax_ref[st, pl.ds(c, 1), :] = run
                knorm_ref[bi] = jnp.maximum(knorm_ref[bi], head_sq_norm_max(k32))
            return carry

        lax.fori_loop(0, n_blocks, chunk, 0)

    q_aug = []
    reach = []
    for bi in range(nb):
        q2 = q_ref[bi]
        qk_sq = head_sq_norm_max(q2.astype(F32)) * knorm_ref[bi]
        for hh in range(HEADS_PER_BLOCK):
            ones = jnp.where(piece_lanes(hh), 1.0, 0.0).astype(BF16)
            q_aug.append(jnp.where(_head_lane_mask(hh), q2, jnp.broadcast_to(ones, q2.shape)))
            reach.append(jnp.sqrt(head_col(qk_sq, hh)) * NORM_SLACK)

    if moba:
        blk = lax.broadcasted_iota(jnp.int32, (n_blocks, 1), 0)
        blk_f = blk.astype(F32)
        for st, (bi, hh) in enumerate(streams):
            km = _split3(kmean_ref[bi])
            q2 = q_ref[bi]
            qh = jnp.where(_head_lane_mask(hh), q2, jnp.zeros_like(q2))
            route = _dot_nt(km[0], qh) + _dot_nt(km[1], qh) + _dot_nt(km[2], qh)
            route = jnp.where(blk < qi, route, NEG_INF)
            sel = jnp.zeros((n_blocks, t), F32)
            for _ in range(MOBA_TOPK):
                top = jnp.max(route, axis=0, keepdims=True)
                first = jnp.min(jnp.where(route == top, blk_f, float(n_blocks)), axis=0, keepdims=True)
                hit = blk_f == first
                sel = jnp.where(hit, 1.0, sel)
                route = jnp.where(hit, -jnp.inf, route)
            sel_ref[st] = jnp.where(blk < qi, sel, 0.0)

    valid = _tile_rel(t) <= 0

    all_streams = range(len(streams))
    stream_groups = [range(g, g + HEADS_PER_BLOCK) for g in range(0, len(streams), HEADS_PER_BLOCK)]

    def scores(s_ref, j, group=all_streams):
        rows = pl.ds(pl.multiple_of(j * t, t), t)
        for st in group:
            s_ref[st] = _dot_nt(kaug_ref[st, rows, :], q_aug[st])

    def accumulate(s_ref, j, diag, group=all_streams):
        rows = pl.ds(pl.multiple_of(j * t, t), t)
        for st in group:
            s_t = s_ref[st]
            if diag:
                s_t = jnp.where(valid, s_t, NEG_INF)
            m_old = m_ref[st]
            m_new = jnp.maximum(m_old, jnp.max(s_t, axis=0, keepdims=True))
            m_sub = m_new
            if moba and not diag:
                picked = sel_ref[st, pl.ds(j, 1), :] > 0.0
                m_new = jnp.where(picked, m_new, m_old)
                m_sub = jnp.where(picked, m_new, -NEG_INF)
            alpha = jnp.exp2(m_old - m_new)
            p = jnp.exp2(s_t - m_sub)
            acc_ref[st] = alpha * acc_ref[st] + _dot(vt_ref[st, :, rows], p.astype(BF16))
            m_ref[st] = m_new

    def tile_at(dist):
        return jnp.maximum(qi - dist, 0)

    m_ref[...] = jnp.full(m_ref.shape, NEG_INF, F32)
    acc_ref[...] = jnp.zeros(acc_ref.shape, F32)
    scores(s_bufs[0], qi)
    for group in stream_groups:
        scores(s_bufs[1], tile_at(1), group)
        accumulate(s_bufs[0], qi, True, group)
        scores(s_bufs[2], tile_at(2), group)

    blk_col = lax.broadcasted_iota(jnp.int32, (n_blocks, 1), 0)
    fewest_dead = None
    for st in range(len(streams)):
        thr = jnp.min(m_ref[st], axis=1, keepdims=True) - FLASH_SKIP - reach[st]
        dead = (bmax_ref[st] < thr) & (blk_col < qi)
        n_dead = jnp.sum(jnp.where(dead, 1.0, 0.0), axis=0, keepdims=True)
        fewest_dead = n_dead if fewest_dead is None else jnp.minimum(fewest_dead, n_dead)
    n_past = qi - jnp.max(fewest_dead).astype(jnp.int32)

    def pair(first, cur, nxt):
        for k in range(2):
            for group in stream_groups:
                scores(s_bufs[nxt[k]], tile_at(first + 2 + k), group)
                accumulate(s_bufs[cur[k]], qi - first - k, False, group)

    def body(p, carry):
        first = 2 * p + 1

        @pl.when(p % 2 == 0)
        def _():
            pair(first, (1, 2), (3, 0))

        @pl.when(p % 2 == 1)
        def _():
            pair(first, (3, 0), (1, 2))

        return carry

    lax.fori_loop(0, n_past // 2, body, 0)

    for slot in (1, 3):
        @pl.when(n_past % 4 == slot)
        def _(slot=slot):
            accumulate(s_bufs[slot], qi - n_past, False)

    for bi in range(nb):
        halves = []
        for hh in range(HEADS_PER_BLOCK):
            acc = acc_ref[bi * HEADS_PER_BLOCK + hh]
            halves.append(acc[:HEAD_DIM] * (1.0 / acc[HEAD_DIM:HEAD_DIM + 1]))
        o_ref[bi] = jnp.concatenate(halves, axis=0).T.astype(o_ref.dtype)


def _flash_attention(proj3, key_bias3, d, moba):
    b, s, _ = proj3.shape
    t = ATT_TILE
    nb = FLASH_BATCH if b % FLASH_BATCH == 0 else 1
    n_blocks = s // t
    n_streams = nb * HEADS_PER_BLOCK
    in_specs, out_spec = _mixer_specs(2 if moba else 1, nb, t, s, d)
    operands = [proj3, proj3, proj3]
    scratch = [pltpu.VMEM((n_streams, s, LANES), BF16),
               pltpu.VMEM((n_streams, V_ROWS, s), BF16),
               pltpu.VMEM((n_streams, V_ROWS, t), F32),
               pltpu.VMEM((n_streams, 1, t), F32),
               pltpu.VMEM((n_streams, t, t), F32),
               pltpu.VMEM((n_streams, t, t), F32),
               pltpu.VMEM((n_streams, t, t), F32),
               pltpu.VMEM((n_streams, t, t), F32),
               pltpu.VMEM((n_streams, n_blocks, 1), F32),
               pltpu.VMEM((nb, 1, LANES), F32)]
    if moba:
        scratch += [pltpu.VMEM((nb, n_blocks, LANES), F32),
                    pltpu.VMEM((n_streams, n_blocks, t), F32)]
    else:
        in_specs.append(pl.BlockSpec((nb, s, LANES), lambda bg, hp, qi: (bg, 0, hp)))
        operands.append(key_bias3)
    return pl.pallas_call(
        functools.partial(_flash_kernel, moba=moba, nb=nb),
        grid=(b // nb, N_HEAD_BLOCKS, n_blocks),
        in_specs=in_specs,
        out_specs=out_spec,
        out_shape=jax.ShapeDtypeStruct((b, s, BRANCH_WIDTH), BF16),
        scratch_shapes=scratch,
        compiler_params=_cparams(("parallel", "parallel", "arbitrary")),
        name="moba_attn" if moba else "forgetting_attn",
    )(*operands)


def _mix_kernel(x_ref, oa_ref, ob_ref, oc_ref, g_ref, wb_ref, wo_ref, y_ref):
    d = x_ref.shape[1]
    mixed = None
    for n, o_ref in enumerate((oa_ref, ob_ref, oc_ref)):
        gate = jax.nn.sigmoid(g_ref[:, n * d:(n + 1) * d].astype(F32))
        term = gate * _dot(o_ref[...], wb_ref[n])
        mixed = term if mixed is None else mixed + term
    y_ref[...] = x_ref[...] + _dot(mixed.astype(BF16), wo_ref[...])


def _mix(x, o_a, o_b, o_c, proj, wb, wo, layer, tm):
    n, d = x.shape
    o_spec = pl.BlockSpec((tm, BRANCH_WIDTH), lambda i: (i, 0))
    return pl.pallas_call(
        _mix_kernel,
        grid=(n // tm,),
        in_specs=[
            pl.BlockSpec((tm, d), lambda i: (i, 0)),
            o_spec, o_spec, o_spec,
            pl.BlockSpec((tm, N_BRANCH * d), lambda i: (i, 0)),
            pl.BlockSpec((None, N_BRANCH, BRANCH_WIDTH, d), lambda i: (layer, 0, 0, 0)),
            pl.BlockSpec((None, d, d), lambda i: (layer, 0, 0)),
        ],
        out_specs=pl.BlockSpec((tm, d), lambda i: (i, 0)),
        out_shape=jax.ShapeDtypeStruct((n, d), F32),
        compiler_params=_cparams(("parallel",)),
        name="gated_mix_out_proj",
    )(x, o_a, o_b, o_c, proj, wb, wo)


def _mlp_kernel(x_ref, g_ref, wu_ref, wd_ref, gf_ref, y_ref, h_ref, acc_ref, *, final_norm):
    f = pl.program_id(1)

    @pl.when(f == 0)
    def _():
        h_ref[...] = _rms(x_ref[...], g_ref[...]).astype(BF16)
        acc_ref[...] = jnp.zeros_like(acc_ref)

    hid = jnp.square(jnp.maximum(_dot(h_ref[...], wu_ref[...]), 0.0))
    acc_ref[...] += _dot(hid.astype(BF16), wd_ref[...])

    @pl.when(f == pl.num_programs(1) - 1)
    def _():
        y = x_ref[...] + acc_ref[...]
        if final_norm:
            y = _rms(y, gf_ref[...])
        y_ref[...] = y


def _mlp(x, g, wu, wd, g_final, layer, final_norm, tm, tf):
    n, d = x.shape
    d_ff = wu.shape[2]
    return pl.pallas_call(
        functools.partial(_mlp_kernel, final_norm=final_norm),
        grid=(n // tm, d_ff // tf),
        in_specs=[
            pl.BlockSpec((tm, d), lambda i, f: (i, 0)),
            pl.BlockSpec((1, d), lambda i, f: (0, 0)),
            pl.BlockSpec((None, d, tf), lambda i, f: (layer, 0, f)),
            pl.BlockSpec((None, tf, d), lambda i, f: (layer, f, 0)),
            pl.BlockSpec((1, d), lambda i, f: (0, 0)),
        ],
        out_specs=pl.BlockSpec((tm, d), lambda i, f: (i, 0)),
        out_shape=jax.ShapeDtypeStruct((n, d), F32),
        scratch_shapes=[pltpu.VMEM((tm, d), BF16), pltpu.VMEM((tm, d), F32)],
        compiler_params=_cparams(("parallel", "arbitrary")),
        name="relu2_mlp",
    )(x, g, wu, wd, g_final)


REPACK_ROWS = 512


def _repack_kernel(w_ref, wf_ref, main_ref, f_ref):
    i = pl.program_id(1)
    q_chunks = BRANCH_WIDTH // REPACK_ROWS
    n_gate_chunks = N_BRANCH * w_ref.shape[2] // REPACK_ROWS
    mixer_chunk = jnp.maximum(i - n_gate_chunks, 0) % (3 * q_chunks)
    is_q = (i >= n_gate_chunks) & (mixer_chunk < q_chunks)
    scale = jnp.where(is_q, HEAD_DIM ** -0.5 * LOG2E, 1.0)
    main_ref[...] = (w_ref[0] * scale).astype(BF16)

    @pl.when(i == 0)
    def _():
        pad = jnp.zeros((f_ref.shape[0] - N_HEADS, f_ref.shape[1]), F32)
        f_ref[...] = jnp.concatenate([wf_ref[0], pad], axis=0).astype(BF16)


def _prep_in_proj(w_in):
    depth, d, cols = w_in.shape
    w_t = jnp.swapaxes(w_in, 1, 2)
    bw = BRANCH_WIDTH
    r = REPACK_ROWS
    f_lo = 6 * bw
    c_lo = f_lo + N_HEADS
    g_lo = c_lo + 3 * bw
    main_rows = cols - N_HEADS
    n_gate = (cols - g_lo) // r
    n_ab = f_lo // r

    def src_row(l, i):
        row = jnp.where(i < n_gate, g_lo + r * i,
                        jnp.where(i < n_gate + n_ab, r * (i - n_gate), c_lo + r * (i - n_gate - n_ab)))
        return l, pl.multiple_of(row, N_HEADS), 0

    return pl.pallas_call(
        _repack_kernel,
        grid=(depth, main_rows // r),
        in_specs=[pl.BlockSpec((pl.Element(1), pl.Element(r), pl.Element(d)), src_row),
                  pl.BlockSpec((pl.Element(1), pl.Element(N_HEADS), pl.Element(d)), lambda l, i: (l, f_lo, 0))],
        out_specs=[pl.BlockSpec((None, r, d), lambda l, i: (l, i, 0)),
                   pl.BlockSpec((None, LANES, d), lambda l, i: (l, 0, 0))],
        out_shape=[jax.ShapeDtypeStruct((depth, main_rows, d), BF16),
                   jax.ShapeDtypeStruct((depth, LANES, d), BF16)],
        compiler_params=_cparams(("parallel", "arbitrary")),
        name="repack_w_in",
    )(w_t, w_t)


def kernel(x, norm_mix, w_in, b_forget, w_branch, w_out, norm_mlp, w_up, w_down, norm_final):
    b, s, d = x.shape
    depth = w_in.shape[0]
    n = b * s
    assert s % ATT_TILE == 0 and d % LANES == 0

    w_main, w_f = _prep_in_proj(w_in)
    wb = w_branch.astype(BF16)
    wo = w_out.astype(BF16)
    wu = w_up.astype(BF16)
    wd = w_down.astype(BF16)
    b_f = jnp.pad(b_forget, ((0, 0), (0, LANES - N_HEADS)))[:, None, :]

    tm = min(1024, s)
    xf = x.reshape(n, d)
    for l in range(depth):
        g_mix = norm_mix[l][None, :]
        proj, key_bias = _in_proj(xf, g_mix, w_main, w_f, b_f[l], l, s, tm, 1536, 256)
        proj3 = proj.reshape(b, s, -1)
        o_a = _sb_attention(proj3, d).reshape(n, -1)
        o_b = _flash_attention(proj3, key_bias.reshape(b, s, -1), d, False).reshape(n, -1)
        o_c = _flash_attention(proj3, None, d, True).reshape(n, -1)
        xf = _mix(xf, o_a, o_b, o_c, proj, wb, wo, l, min(512, n))
        xf = _mlp(xf, norm_mlp[l][None, :], wu, wd, norm_final[None, :],
                  l, l == depth - 1, min(1024, n), 1024)
    return xf.reshape(b, s, d)
```

```python
import functools
import math

import jax
import jax.numpy as jnp
import numpy as np
from jax import lax
from jax.experimental import pallas as pl
from jax.experimental.pallas import tpu as pltpu

F32 = jnp.float32
BF16 = jnp.bfloat16

HEAD_DIM = 64
N_HEADS = 8
BRANCH_WIDTH = N_HEADS * HEAD_DIM
N_BRANCH = 3
MOBA_BLOCK = 256
MOBA_TOPK = 3
RMS_EPS = 1e-6
NEG_INF = -1e30
LOG2E = math.log2(math.e)
LANES = 128
HEADS_PER_BLOCK = LANES // HEAD_DIM
N_HEAD_BLOCKS = BRANCH_WIDTH // LANES
N_BIAS_PIECES = 3
ATT_TILE = 256
SB_EXIT = 160.0
FLASH_SKIP = 160.0
NORM_SLACK = 1.001
FLASH_BATCH = 4
V_ROWS = HEAD_DIM + 16
VMEM_LIMIT = 60000 * 1024


def _cparams(sem):
    return pltpu.CompilerParams(dimension_semantics=sem, vmem_limit_bytes=VMEM_LIMIT)


def _rms(x, g):
    ms = jnp.mean(x * x, axis=-1, keepdims=True)
    return x * lax.rsqrt(ms + RMS_EPS) * g


def _dot(a, b):
    return jnp.dot(a, b, preferred_element_type=F32)


def _dot_nt(a, b):
    return lax.dot_general(a, b, (((1,), (1,)), ((), ())), preferred_element_type=F32)


def _split2(x):
    hi = x.astype(BF16)
    lo = (x - hi.astype(F32)).astype(BF16)
    return hi, lo


def _split3(x):
    hi = x.astype(BF16)
    r = x - hi.astype(F32)
    mid = r.astype(BF16)
    lo = (r - mid.astype(F32)).astype(BF16)
    return hi, mid, lo


def _bias_lane(hh):
    return (1 - hh) * HEAD_DIM


def _in_proj_kernel(x_ref, g_ref, w_ref, wf_ref, b_ref, tri_ref, place_ref,
                    o_ref, e_ref, h_ref, carry_ref, *, tiles_per_seq):
    @pl.when(pl.program_id(1) == 0)
    def _():
        h = _rms(x_ref[...], g_ref[...]).astype(BF16)
        h_ref[...] = h

        @pl.when(pl.program_id(0) % tiles_per_seq == 0)
        def _():
            carry_ref[...] = jnp.zeros_like(carry_ref)

        y = _dot_nt(h, wf_ref[...]) + b_ref[...]
        logf = jnp.minimum(y, 0.0) - jnp.log(1.0 + jnp.exp(-jnp.abs(y)))
        pieces = _split3(logf)
        tri = tri_ref[...]
        tc = tri.shape[0]
        cum = carry_ref[...]
        for c in range(h.shape[0] // tc):
            rows = slice(c * tc, (c + 1) * tc)
            cum = cum[-1:, :]
            for piece in pieces:
                cum = cum + _dot(tri, piece[rows])
            bias_pieces = jnp.concatenate(_split3(cum * (-LOG2E)), axis=1)
            e_ref[rows, :] = _dot(bias_pieces, place_ref[...]).astype(BF16)
        carry_ref[...] = cum[-1:, :]

    o_ref[...] = _dot_nt(h_ref[...], w_ref[...]).astype(o_ref.dtype)


def _bias_placement():
    place = np.zeros((N_BIAS_PIECES, LANES, N_HEAD_BLOCKS * LANES), np.float32)
    for h in range(N_HEADS):
        hp, hh = divmod(h, HEADS_PER_BLOCK)
        for i in range(N_BIAS_PIECES):
            place[i, h, hp * LANES + _bias_lane(hh) + i] = 1.0
    return jnp.asarray(place.reshape(N_BIAS_PIECES * LANES, -1), BF16)


def _in_proj(x, g, w, w_f, b_f, layer, s, tm, tn, tc):
    n, d = x.shape
    cols = w.shape[1]
    tri = jnp.asarray(np.tril(np.ones((tc, tc), np.float32)), BF16)
    place = _bias_placement()
    e_cols = place.shape[1]
    const2 = lambda i, j: (0, 0)
    return pl.pallas_call(
        functools.partial(_in_proj_kernel, tiles_per_seq=s // tm),
        grid=(n // tm, cols // tn),
        in_specs=[
            pl.BlockSpec((tm, d), lambda i, j: (i, 0)),
            pl.BlockSpec((1, d), const2),
            pl.BlockSpec((None, tn, d), lambda i, j: (layer, j, 0)),
            pl.BlockSpec((None, LANES, d), lambda i, j: (layer, 0, 0)),
            pl.BlockSpec((1, LANES), const2),
            pl.BlockSpec((tc, tc), const2),
            pl.BlockSpec(place.shape, const2),
        ],
        out_specs=[pl.BlockSpec((tm, tn), lambda i, j: (i, j)),
                   pl.BlockSpec((tm, e_cols), lambda i, j: (i, 0))],
        out_shape=[jax.ShapeDtypeStruct((n, cols), BF16),
                   jax.ShapeDtypeStruct((n, e_cols), BF16)],
        scratch_shapes=[pltpu.VMEM((tm, d), BF16), pltpu.VMEM((1, LANES), F32)],
        compiler_params=_cparams(("arbitrary", "arbitrary")),
        name="norm_in_proj",
    )(x, g, w, w_f, b_f, tri, place)


def _head_lane_mask(hh):
    lane = lax.broadcasted_iota(jnp.int32, (1, LANES), 1)
    return (lane >= hh * HEAD_DIM) & (lane < (hh + 1) * HEAD_DIM)


def _tile_rel(t):
    return (lax.broadcasted_iota(jnp.int32, (t, t), 0)
            - lax.broadcasted_iota(jnp.int32, (t, t), 1))


def _mixer_specs(mixer, nb, t, s, d):
    qoff = N_BRANCH * d // LANES + (3 * mixer) * N_HEAD_BLOCKS
    koff = qoff + N_HEAD_BLOCKS
    voff = koff + N_HEAD_BLOCKS
    return [
        pl.BlockSpec((nb, t, LANES), lambda bg, hp, qi: (bg, qi, qoff + hp)),
        pl.BlockSpec((nb, s, LANES), lambda bg, hp, qi: (bg, 0, koff + hp)),
        pl.BlockSpec((nb, s, LANES), lambda bg, hp, qi: (bg, 0, voff + hp)),
    ], pl.BlockSpec((nb, t, LANES), lambda bg, hp, qi: (bg, qi, hp))


def _sb_kernel(q_ref, k_ref, v_ref, o_ref, vt_ref, acc_ref, cs_ref, r_ref, s_even_ref, s_odd_ref, *, nb):
    t = q_ref.shape[1]
    n_blocks = k_ref.shape[1] // t
    qi = pl.program_id(2)
    streams = [(bi, hh) for bi in range(nb) for hh in range(HEADS_PER_BLOCK)]

    @pl.when(qi == 0)
    def _build():
        def chunk(c, carry):
            rows = pl.ds(pl.multiple_of(c * t, t), t)
            for st, (bi, hh) in enumerate(streams):
                v2t = v_ref[bi, rows, :].astype(F32).T
                vt_ref[st, :, rows] = v2t[hh * HEAD_DIM:(hh + 1) * HEAD_DIM, :].astype(BF16)
            return carry

        lax.fori_loop(0, n_blocks, chunk, 0)

    q_m = []
    for bi, hh in streams:
        q2 = q_ref[bi]
        q_m.append(jnp.where(_head_lane_mask(hh), q2, jnp.zeros_like(q2)))

    rel = _tile_rel(t)
    strictly_past = rel < 0
    upper = jnp.where(rel <= 0, 1.0, 0.0).astype(BF16)

    stream_groups = [range(g, g + HEADS_PER_BLOCK) for g in range(0, len(streams), HEADS_PER_BLOCK)]

    def scores(s_ref, j, group):
        rows = pl.ds(pl.multiple_of(j * t, t), t)
        for st in group:
            s_ref[st] = _dot_nt(k_ref[streams[st][0], rows, :], q_m[st])

    def softplus_sums(s_ref, diag, group):
        for st in group:
            z = s_ref[st]
            sp = jnp.maximum(z, 0.0) + jnp.log2(1.0 + jnp.exp2(-jnp.abs(z)))
            if diag:
                sp = jnp.where(strictly_past, sp, 0.0)
            r = _dot(upper, sp.astype(BF16)) + cs_ref[st]
            r_ref[st] = r
            cs_ref[st] = r[0:1, :]

    def weigh_values(s_ref, j, diag, group):
        rows = pl.ds(pl.multiple_of(j * t, t), t)
        for st in group:
            w = jnp.exp2(s_ref[st] - r_ref[st])
            if diag:
                w = jnp.where(strictly_past, w, 0.0)
            acc_ref[st] += _dot(vt_ref[st, :, rows], w.astype(BF16))

    def step(cur_ref, next_ref, j, j_next, diag):
        for group in stream_groups:
            scores(next_ref, j_next, group)
            softplus_sums(cur_ref, diag, group)
        for group in stream_groups:
            weigh_values(cur_ref, j, diag, group)

    def next_distance(i):
        lowest = cs_ref[0]
        for st in range(1, len(streams)):
            lowest = jnp.minimum(lowest, cs_ref[st])
        return jnp.where(jnp.min(lowest) > SB_EXIT, qi + 1, i + 1)

    acc_ref[...] = jnp.zeros(acc_ref.shape, F32)
    cs_ref[...] = jnp.zeros(cs_ref.shape, F32)
    for group in stream_groups:
        scores(s_even_ref, qi, group)
    step(s_even_ref, s_odd_ref, qi, jnp.maximum(qi - 1, 0), True)

    def body(i):
        j = qi - i
        j_next = jnp.maximum(j - 1, 0)

        @pl.when(i % 2 == 1)
        def _():
            step(s_odd_ref, s_even_ref, j, j_next, False)

        @pl.when(i % 2 == 0)
        def _():
            step(s_even_ref, s_odd_ref, j, j_next, False)

        return next_distance(i)

    lax.while_loop(lambda i: i <= qi, body, next_distance(0))

    for bi in range(nb):
        halves = [acc_ref[bi * HEADS_PER_BLOCK + hh] for hh in range(HEADS_PER_BLOCK)]
        o_ref[bi] = jnp.concatenate(halves, axis=0).T.astype(o_ref.dtype)


def _sb_attention(proj3, d):
    b, s, _ = proj3.shape
    t = ATT_TILE
    nb = FLASH_BATCH if b % FLASH_BATCH == 0 else 1
    n_streams = nb * HEADS_PER_BLOCK
    in_specs, out_spec = _mixer_specs(0, nb, t, s, d)
    return pl.pallas_call(
        functools.partial(_sb_kernel, nb=nb),
        grid=(b // nb, N_HEAD_BLOCKS, s // t),
        in_specs=in_specs,
        out_specs=out_spec,
        out_shape=jax.ShapeDtypeStruct((b, s, BRANCH_WIDTH), BF16),
        scratch_shapes=[pltpu.VMEM((n_streams, HEAD_DIM, s), BF16),
                        pltpu.VMEM((n_streams, HEAD_DIM, t), F32),
                        pltpu.VMEM((n_streams, 1, t), F32),
                        pltpu.VMEM((n_streams, t, t), F32),
                        pltpu.VMEM((n_streams, t, t), F32),
                        pltpu.VMEM((n_streams, t, t), F32)],
        compiler_params=_cparams(("parallel", "parallel", "arbitrary")),
        name="stickbreak_attn",
    )(proj3, proj3, proj3)


def _flash_kernel(*refs, moba, nb):
    if moba:
        (q_ref, k_ref, v_ref, o_ref, kaug_ref, vt_ref, acc_ref, m_ref, s0_ref, s1_ref, s2_ref, s3_ref,
         bmax_ref, knorm_ref, kmean_ref, sel_ref) = refs
    else:
        (q_ref, k_ref, v_ref, e_ref, o_ref, kaug_ref, vt_ref, acc_ref, m_ref, s0_ref, s1_ref, s2_ref,
         s3_ref, bmax_ref, knorm_ref) = refs
    s_bufs = (s0_ref, s1_ref, s2_ref, s3_ref)
    t = q_ref.shape[1]
    n_blocks = k_ref.shape[1] // t
    hp = pl.program_id(1)
    qi = pl.program_id(2)
    lane = lax.broadcasted_iota(jnp.int32, (1, LANES), 1)
    streams = [(bi, hh) for bi in range(nb) for hh in range(HEADS_PER_BLOCK)]

    def piece_lanes(hh):
        e0 = _bias_lane(hh)
        return (lane >= e0) & (lane < e0 + N_BIAS_PIECES)

    head_of_lane = jnp.where(lax.broadcasted_iota(jnp.int32, (LANES, LANES), 0) >= HEAD_DIM, 1, 0)
    head_sum = jnp.where(head_of_lane == lax.broadcasted_iota(jnp.int32, (LANES, LANES), 1), 1.0, 0.0).astype(BF16)

    def head_sq_norm_max(x32):
        hi, lo = _split2(x32 * x32)
        return jnp.max(_dot(hi, head_sum) + _dot(lo, head_sum), axis=0, keepdims=True)

    def head_col(row, hh):
        return jnp.sum(jnp.where(lane == hh, row, 0.0), axis=1, keepdims=True)

    @pl.when(qi == 0)
    def _build():
        bmax_ref[...] = jnp.full(bmax_ref.shape, NEG_INF, F32)
        knorm_ref[...] = jnp.zeros(knorm_ref.shape, F32)
        tail = jnp.where(lax.broadcasted_iota(jnp.int32, (V_ROWS - HEAD_DIM, t), 0) == 0, 1.0, 0.0)

        def chunk(c, carry):
            start = pl.multiple_of(c * t, t)
            rows = pl.ds(start, t)
            extras = []
            if moba:
                pos = (lax.broadcasted_iota(jnp.int32, (t, LANES), 0) + start).astype(F32)
                for hh in range(HEADS_PER_BLOCK):
                    h = hp * HEADS_PER_BLOCK + hh
                    slope = pltpu.bitcast(jnp.full((1, LANES), (126 - h) << 23, jnp.int32), F32)
                    extra = jnp.zeros((t, LANES), F32)
                    for i, piece in enumerate(_split3(pos * slope * LOG2E)):
                        extra = jnp.where(lane == _bias_lane(hh) + i, piece.astype(F32), extra)
                    extras.append(extra.astype(BF16))
            for bi in range(nb):
                k2 = k_ref[bi, rows, :]
                k32 = k2.astype(F32)
                v2t = v_ref[bi, rows, :].astype(F32).T
                if moba:
                    kmean_ref[bi, pl.ds(c, 1), :] = jnp.sum(k32, axis=0, keepdims=True) * (1.0 / t)
                for hh in range(HEADS_PER_BLOCK):
                    st = bi * HEADS_PER_BLOCK + hh
                    extra = extras[hh] if moba else e_ref[bi, rows, :]
                    kaug_ref[st, rows, :] = jnp.where(_head_lane_mask(hh), k2, extra)
                    v_h = v2t[hh * HEAD_DIM:(hh + 1) * HEAD_DIM, :]
                    vt_ref[st, :, rows] = jnp.concatenate([v_h, tail], axis=0).astype(BF16)
                    piece_max = jnp.max(extra.astype(F32), axis=0, keepdims=True)
                    bias_max = jnp.sum(jnp.where(piece_lanes(hh), piece_max, 0.0), axis=1, keepdims=True)
                    run = jnp.maximum(bmax_ref[st, pl.ds(jnp.maximum(c - 1, 0), 1), :], bias_max)
                    bmax_ref[st, pl.ds(c, 1), :] = run
                knorm_ref[bi] = jnp.maximum(knorm_ref[bi], head_sq_norm_max(k32))
            return carry

        lax.fori_loop(0, n_blocks, chunk, 0)

    q_aug = []
    reach = []
    for bi in range(nb):
        q2 = q_ref[bi]
        qk_sq = head_sq_norm_max(q2.astype(F32)) * knorm_ref[bi]
        for hh in range(HEADS_PER_BLOCK):
            ones = jnp.where(piece_lanes(hh), 1.0, 0.0).astype(BF16)
            q_aug.append(jnp.where(_head_lane_mask(hh), q2, jnp.broadcast_to(ones, q2.shape)))
            reach.append(jnp.sqrt(head_col(qk_sq, hh)) * NORM_SLACK)

    if moba:
        blk = lax.broadcasted_iota(jnp.int32, (n_blocks, 1), 0)
        blk_f = blk.astype(F32)
        for st, (bi, hh) in enumerate(streams):
            km = _split3(kmean_ref[bi])
            q2 = q_ref[bi]
            qh = jnp.where(_head_lane_mask(hh), q2, jnp.zeros_like(q2))
            route = _dot_nt(km[0], qh) + _dot_nt(km[1], qh) + _dot_nt(km[2], qh)
            route = jnp.where(blk < qi, route, NEG_INF)
            sel = jnp.zeros((n_blocks, t), F32)
            for _ in range(MOBA_TOPK):
                top = jnp.max(route, axis=0, keepdims=True)
                first = jnp.min(jnp.where(route == top, blk_f, float(n_blocks)), axis=0, keepdims=True)
                hit = blk_f == first
                sel = jnp.where(hit, 1.0, sel)
                route = jnp.where(hit, -jnp.inf, route)
            sel_ref[st] = jnp.where(blk < qi, sel, 0.0)

    valid = _tile_rel(t) <= 0

    all_streams = range(len(streams))
    stream_groups = [range(g, g + HEADS_PER_BLOCK) for g in range(0, len(streams), HEADS_PER_BLOCK)]

    def scores(s_ref, j, group=all_streams):
        rows = pl.ds(pl.multiple_of(j * t, t), t)
        for st in group:
            s_ref[st] = _dot_nt(kaug_ref[st, rows, :], q_aug[st])

    def accumulate(s_ref, j, diag, group=all_streams):
        rows = pl.ds(pl.multiple_of(j * t, t), t)
        for st in group:
            s_t = s_ref[st]
            if diag:
                s_t = jnp.where(valid, s_t, NEG_INF)
            m_old = m_ref[st]
            m_new = jnp.maximum(m_old, jnp.max(s_t, axis=0, keepdims=True))
            m_sub = m_new
            if moba and not diag:
                picked = sel_ref[st, pl.ds(j, 1), :] > 0.0
                m_new = jnp.where(picked, m_new, m_old)
                m_sub = jnp.where(picked, m_new, -NEG_INF)
            alpha = jnp.exp2(m_old - m_new)
            p = jnp.exp2(s_t - m_sub)
            acc_ref[st] = alpha * acc_ref[st] + _dot(vt_ref[st, :, rows], p.astype(BF16))
            m_ref[st] = m_new

    def tile_at(dist):
        return jnp.maximum(qi - dist, 0)

    m_ref[...] = jnp.full(m_ref.shape, NEG_INF, F32)
    acc_ref[...] = jnp.zeros(acc_ref.shape, F32)
    scores(s_bufs[0], qi)
    for group in stream_groups:
        scores(s_bufs[1], tile_at(1), group)
        accumulate(s_bufs[0], qi, True, group)
        scores(s_bufs[2], tile_at(2), group)

    blk_col = lax.broadcasted_iota(jnp.int32, (n_blocks, 1), 0)
    fewest_dead = None
    for st in range(len(streams)):
        thr = jnp.min(m_ref[st], axis=1, keepdims=True) - FLASH_SKIP - reach[st]
        dead = (bmax_ref[st] < thr) & (blk_col < qi)
        n_dead = jnp.sum(jnp.where(dead, 1.0, 0.0), axis=0, keepdims=True)
        fewest_dead = n_dead if fewest_dead is None else jnp.minimum(fewest_dead, n_dead)
    n_past = qi - jnp.max(fewest_dead).astype(jnp.int32)

    def pair(first, cur, nxt):
        for k in range(2):
            for group in stream_groups:
                scores(s_bufs[nxt[k]], tile_at(first + 2 + k), group)
                accumulate(s_bufs[cur[k]], qi - first - k, False, group)

    def body(p, carry):
        first = 2 * p + 1

        @pl.when(p % 2 == 0)
        def _():
            pair(first, (1, 2), (3, 0))

        @pl.when(p % 2 == 1)
        def _():
            pair(first, (3, 0), (1, 2))

        return carry

    lax.fori_loop(0, n_past // 2, body, 0)

    for slot in (1, 3):
        @pl.when(n_past % 4 == slot)
        def _(slot=slot):
            accumulate(s_bufs[slot], qi - n_past, False)

    for bi in range(nb):
        halves = []
        for hh in range(HEADS_PER_BLOCK):
            acc = acc_ref[bi * HEADS_PER_BLOCK + hh]
            halves.append(acc[:HEAD_DIM] * (1.0 / acc[HEAD_DIM:HEAD_DIM + 1]))
        o_ref[bi] = jnp.concatenate(halves, axis=0).T.astype(o_ref.dtype)


def _flash_attention(proj3, key_bias3, d, moba):
    b, s, _ = proj3.shape
    t = ATT_TILE
    nb = FLASH_BATCH if b % FLASH_BATCH == 0 else 1
    n_blocks = s // t
    n_streams = nb * HEADS_PER_BLOCK
    in_specs, out_spec = _mixer_specs(2 if moba else 1, nb, t, s, d)
    operands = [proj3, proj3, proj3]
    scratch = [pltpu.VMEM((n_streams, s, LANES), BF16),
               pltpu.VMEM((n_streams, V_ROWS, s), BF16),
               pltpu.VMEM((n_streams, V_ROWS, t), F32),
               pltpu.VMEM((n_streams, 1, t), F32),
               pltpu.VMEM((n_streams, t, t), F32),
               pltpu.VMEM((n_streams, t, t), F32),
               pltpu.VMEM((n_streams, t, t), F32),
               pltpu.VMEM((n_streams, t, t), F32),
               pltpu.VMEM((n_streams, n_blocks, 1), F32),
               pltpu.VMEM((nb, 1, LANES), F32)]
    if moba:
        scratch += [pltpu.VMEM((nb, n_blocks, LANES), F32),
                    pltpu.VMEM((n_streams, n_blocks, t), F32)]
    else:
        in_specs.append(pl.BlockSpec((nb, s, LANES), lambda bg, hp, qi: (bg, 0, hp)))
        operands.append(key_bias3)
    return pl.pallas_call(
        functools.partial(_flash_kernel, moba=moba, nb=nb),
        grid=(b // nb, N_HEAD_BLOCKS, n_blocks),
        in_specs=in_specs,
        out_specs=out_spec,
        out_shape=jax.ShapeDtypeStruct((b, s, BRANCH_WIDTH), BF16),
        scratch_shapes=scratch,
        compiler_params=_cparams(("parallel", "parallel", "arbitrary")),
        name="moba_attn" if moba else "forgetting_attn",
    )(*operands)


def _mix_kernel(x_ref, oa_ref, ob_ref, oc_ref, g_ref, wb_ref, wo_ref, y_ref):
    d = x_ref.shape[1]
    mixed = None
    for n, o_ref in enumerate((oa_ref, ob_ref, oc_ref)):
        gate = jax.nn.sigmoid(g_ref[:, n * d:(n + 1) * d].astype(F32))
        term = gate * _dot(o_ref[...], wb_ref[n])
        mixed = term if mixed is None else mixed + term
    y_ref[...] = x_ref[...] + _dot(mixed.astype(BF16), wo_ref[...])


def _mix(x, o_a, o_b, o_c, proj, wb, wo, layer, tm):
    n, d = x.shape
    o_spec = pl.BlockSpec((tm, BRANCH_WIDTH), lambda i: (i, 0))
    return pl.pallas_call(
        _mix_kernel,
        grid=(n // tm,),
        in_specs=[
            pl.BlockSpec((tm, d), lambda i: (i, 0)),
            o_spec, o_spec, o_spec,
            pl.BlockSpec((tm, N_BRANCH * d), lambda i: (i, 0)),
            pl.BlockSpec((None, N_BRANCH, BRANCH_WIDTH, d), lambda i: (layer, 0, 0, 0)),
            pl.BlockSpec((None, d, d), lambda i: (layer, 0, 0)),
        ],
        out_specs=pl.BlockSpec((tm, d), lambda i: (i, 0)),
        out_shape=jax.ShapeDtypeStruct((n, d), F32),
        compiler_params=_cparams(("parallel",)),
        name="gated_mix_out_proj",
    )(x, o_a, o_b, o_c, proj, wb, wo)


def _mlp_kernel(x_ref, g_ref, wu_ref, wd_ref, gf_ref, y_ref, h_ref, acc_ref, *, final_norm):
    f = pl.program_id(1)

    @pl.when(f == 0)
    def _():
        h_ref[...] = _rms(x_ref[...], g_ref[...]).astype(BF16)
        acc_ref[...] = jnp.zeros_like(acc_ref)

    hid = jnp.square(jnp.maximum(_dot(h_ref[...], wu_ref[...]), 0.0))
    acc_ref[...] += _dot(hid.astype(BF16), wd_ref[...])

    @pl.when(f == pl.num_programs(1) - 1)
    def _():
        y = x_ref[...] + acc_ref[...]
        if final_norm:
            y = _rms(y, gf_ref[...])
        y_ref[...] = y


def _mlp(x, g, wu, wd, g_final, layer, final_norm, tm, tf):
    n, d = x.shape
    d_ff = wu.shape[2]
    return pl.pallas_call(
        functools.partial(_mlp_kernel, final_norm=final_norm),
        grid=(n // tm, d_ff // tf),
        in_specs=[
            pl.BlockSpec((tm, d), lambda i, f: (i, 0)),
            pl.BlockSpec((1, d), lambda i, f: (0, 0)),
            pl.BlockSpec((None, d, tf), lambda i, f: (layer, 0, f)),
            pl.BlockSpec((None, tf, d), lambda i, f: (layer, f, 0)),
            pl.BlockSpec((1, d), lambda i, f: (0, 0)),
        ],
        out_specs=pl.BlockSpec((tm, d), lambda i, f: (i, 0)),
        out_shape=jax.ShapeDtypeStruct((n, d), F32),
        scratch_shapes=[pltpu.VMEM((tm, d), BF16), pltpu.VMEM((tm, d), F32)],
        compiler_params=_cparams(("parallel", "arbitrary")),
        name="relu2_mlp",
    )(x, g, wu, wd, g_final)


REPACK_ROWS = 512


def _repack_kernel(perm_ref, w_ref, wf_ref, main_ref, f_ref):
    layer = pl.program_id(0)
    i = pl.program_id(1)
    q_chunks = BRANCH_WIDTH // REPACK_ROWS
    n_gate_chunks = N_BRANCH * w_ref.shape[2] // REPACK_ROWS
    mixer_chunk = jnp.maximum(i - n_gate_chunks, 0) % (3 * q_chunks)
    is_q = (i >= n_gate_chunks) & (mixer_chunk < q_chunks)
    scale = jnp.where(is_q, HEAD_DIM ** -0.5 * LOG2E, 1.0)
    fox_lo = n_gate_chunks + 3 * q_chunks
    is_fox = (i >= fox_lo) & (i < fox_lo + 3 * q_chunks)

    @pl.when(jnp.logical_not(is_fox))
    def _():
        main_ref[...] = (w_ref[0] * scale).astype(BF16)

    @pl.when(is_fox)
    def _():
        for p in range(N_HEADS):
            src = pl.multiple_of(perm_ref[layer * N_HEADS + p] * HEAD_DIM, HEAD_DIM)
            rows = w_ref[0, pl.ds(src, HEAD_DIM), :]
            main_ref[p * HEAD_DIM:(p + 1) * HEAD_DIM, :] = (rows * scale).astype(BF16)

    @pl.when(i == 0)
    def _():
        rows = [wf_ref[0, pl.ds(perm_ref[layer * N_HEADS + p], 1), :] for p in range(N_HEADS)]
        rows.append(jnp.zeros((f_ref.shape[0] - N_HEADS, f_ref.shape[1]), F32))
        f_ref[...] = jnp.concatenate(rows, axis=0).astype(BF16)


def _prep_in_proj(w_in, perm):
    depth, d, cols = w_in.shape
    w_t = jnp.swapaxes(w_in, 1, 2)
    bw = BRANCH_WIDTH
    r = REPACK_ROWS
    assert r == bw
    f_lo = 6 * bw
    c_lo = f_lo + N_HEADS
    g_lo = c_lo + 3 * bw
    main_rows = cols - N_HEADS
    n_gate = (cols - g_lo) // r
    n_ab = f_lo // r

    def src_row(l, i, perm_ref):
        row = jnp.where(i < n_gate, g_lo + r * i,
                        jnp.where(i < n_gate + n_ab, r * (i - n_gate), c_lo + r * (i - n_gate - n_ab)))
        return l, pl.multiple_of(row, N_HEADS), 0

    grid_spec = pltpu.PrefetchScalarGridSpec(
        num_scalar_prefetch=1,
        grid=(depth, main_rows // r),
        in_specs=[pl.BlockSpec((pl.Element(1), pl.Element(r), pl.Element(d)), src_row),
                  pl.BlockSpec((pl.Element(1), pl.Element(N_HEADS), pl.Element(d)),
                               lambda l, i, perm_ref: (l, f_lo, 0))],
        out_specs=[pl.BlockSpec((None, r, d), lambda l, i, perm_ref: (l, i, 0)),
                   pl.BlockSpec((None, LANES, d), lambda l, i, perm_ref: (l, 0, 0))],
    )
    return pl.pallas_call(
        _repack_kernel,
        grid_spec=grid_spec,
        out_shape=[jax.ShapeDtypeStruct((depth, main_rows, d), BF16),
                   jax.ShapeDtypeStruct((depth, LANES, d), BF16)],
        compiler_params=_cparams(("parallel", "arbitrary")),
        name="repack_w_in",
    )(perm.reshape(-1), w_t, w_t)


def _permute_fox_heads(w_branch, b_forget, perm):
    depth, _, width, d = w_branch.shape
    lift = w_branch[:, 1].reshape(depth, N_HEADS, HEAD_DIM, d)
    lift = jnp.take_along_axis(lift, perm[:, :, None, None], axis=1).reshape(depth, 1, width, d)
    w_branch = jnp.concatenate([w_branch[:, :1], lift, w_branch[:, 2:]], axis=1)
    return w_branch, jnp.take_along_axis(b_forget, perm, axis=1)


def kernel(x, norm_mix, w_in, b_forget, w_branch, w_out, norm_mlp, w_up, w_down, norm_final):
    b, s, d = x.shape
    depth = w_in.shape[0]
    n = b * s
    assert s % ATT_TILE == 0 and d % LANES == 0

    perm = jnp.argsort(b_forget, axis=1).astype(jnp.int32)
    w_main, w_f = _prep_in_proj(w_in, perm)
    w_branch, b_forget = _permute_fox_heads(w_branch, b_forget, perm)
    wb = w_branch.astype(BF16)
    wo = w_out.astype(BF16)
    wu = w_up.astype(BF16)
    wd = w_down.astype(BF16)
    b_f = jnp.pad(b_forget, ((0, 0), (0, LANES - N_HEADS)))[:, None, :]

    tm = min(1024, s)
    xf = x.reshape(n, d)
    for l in range(depth):
        g_mix = norm_mix[l][None, :]
        proj, key_bias = _in_proj(xf, g_mix, w_main, w_f, b_f[l], l, s, tm, 1536, 256)
        proj3 = proj.reshape(b, s, -1)
        o_a = _sb_attention(proj3, d).reshape(n, -1)
        o_b = _flash_attention(proj3, key_bias.reshape(b, s, -1), d, False).reshape(n, -1)
        o_c = _flash_attention(proj3, None, d, True).reshape(n, -1)
        xf = _mix(xf, o_a, o_b, o_c, proj, wb, wo, l, min(512, n))
        xf = _mlp(xf, norm_mlp[l][None, :], wu, wd, norm_final[None, :],
                  l, l == depth - 1, min(1024, n), 1024)
    return xf.reshape(b, s, d)
```

```python
import functools
import math

import jax
import jax.numpy as jnp
import numpy as np
from jax import lax
from jax.experimental import pallas as pl
from jax.experimental.pallas import tpu as pltpu

F32 = jnp.float32
BF16 = jnp.bfloat16

HEAD_DIM = 64
N_HEADS = 8
BRANCH_WIDTH = N_HEADS * HEAD_DIM
N_BRANCH = 3
MOBA_BLOCK = 256
MOBA_TOPK = 3
RMS_EPS = 1e-6
NEG_INF = -1e30
LOG2E = math.log2(math.e)
LANES = 128
HEADS_PER_BLOCK = LANES // HEAD_DIM
N_HEAD_BLOCKS = BRANCH_WIDTH // LANES
N_BIAS_PIECES = 3
ATT_TILE = 256
Q_TILES_PER_STEP = 4
SB_EXIT = 160.0
FLASH_SKIP = 160.0
NORM_SLACK = 1.001
FLASH_BATCH = 4
V_ROWS = HEAD_DIM + 16
VMEM_LIMIT = 60000 * 1024


def _cparams(sem):
    return pltpu.CompilerParams(dimension_semantics=sem, vmem_limit_bytes=VMEM_LIMIT)


def _rms(x, g):
    ms = jnp.mean(x * x, axis=-1, keepdims=True)
    return x * lax.rsqrt(ms + RMS_EPS) * g


def _dot(a, b):
    return jnp.dot(a, b, preferred_element_type=F32)


def _dot_nt(a, b):
    return lax.dot_general(a, b, (((1,), (1,)), ((), ())), preferred_element_type=F32)


def _split2(x):
    hi = x.astype(BF16)
    lo = (x - hi.astype(F32)).astype(BF16)
    return hi, lo


def _split3(x):
    hi = x.astype(BF16)
    r = x - hi.astype(F32)
    mid = r.astype(BF16)
    lo = (r - mid.astype(F32)).astype(BF16)
    return hi, mid, lo


def _bias_lane(hh):
    return (1 - hh) * HEAD_DIM


def _in_proj_kernel(x_ref, g_ref, w_ref, wf_ref, b_ref, tri_ref, place_ref,
                    o_ref, e_ref, h_ref, carry_ref, *, tiles_per_seq):
    @pl.when(pl.program_id(1) == 0)
    def _():
        h = _rms(x_ref[...], g_ref[...]).astype(BF16)
        h_ref[...] = h

        @pl.when(pl.program_id(0) % tiles_per_seq == 0)
        def _():
            carry_ref[...] = jnp.zeros_like(carry_ref)

        y = _dot_nt(h, wf_ref[...]) + b_ref[...]
        logf = jnp.minimum(y, 0.0) - jnp.log(1.0 + jnp.exp(-jnp.abs(y)))
        pieces = _split3(logf)
        tri = tri_ref[...]
        tc = tri.shape[0]
        cum = carry_ref[...]
        for c in range(h.shape[0] // tc):
            rows = slice(c * tc, (c + 1) * tc)
            cum = cum[-1:, :]
            for piece in pieces:
                cum = cum + _dot(tri, piece[rows])
            bias_pieces = jnp.concatenate(_split3(cum * (-LOG2E)), axis=1)
            e_ref[rows, :] = _dot(bias_pieces, place_ref[...]).astype(BF16)
        carry_ref[...] = cum[-1:, :]

    o_ref[...] = _dot_nt(h_ref[...], w_ref[...]).astype(o_ref.dtype)


def _bias_placement():
    place = np.zeros((N_BIAS_PIECES, LANES, N_HEAD_BLOCKS * LANES), np.float32)
    for h in range(N_HEADS):
        hp, hh = divmod(h, HEADS_PER_BLOCK)
        for i in range(N_BIAS_PIECES):
            place[i, h, hp * LANES + _bias_lane(hh) + i] = 1.0
    return jnp.asarray(place.reshape(N_BIAS_PIECES * LANES, -1), BF16)


def _in_proj(x, g, w, w_f, b_f, layer, s, tm, tn, tc):
    n, d = x.shape
    cols = w.shape[1]
    tri = jnp.asarray(np.tril(np.ones((tc, tc), np.float32)), BF16)
    place = _bias_placement()
    e_cols = place.shape[1]
    const2 = lambda i, j: (0, 0)
    return pl.pallas_call(
        functools.partial(_in_proj_kernel, tiles_per_seq=s // tm),
        grid=(n // tm, cols // tn),
        in_specs=[
            pl.BlockSpec((tm, d), lambda i, j: (i, 0)),
            pl.BlockSpec((1, d), const2),
            pl.BlockSpec((None, tn, d), lambda i, j: (layer, j, 0)),
            pl.BlockSpec((None, LANES, d), lambda i, j: (layer, 0, 0)),
            pl.BlockSpec((1, LANES), const2),
            pl.BlockSpec((tc, tc), const2),
            pl.BlockSpec(place.shape, const2),
        ],
        out_specs=[pl.BlockSpec((tm, tn), lambda i, j: (i, j)),
                   pl.BlockSpec((tm, e_cols), lambda i, j: (i, 0))],
        out_shape=[jax.ShapeDtypeStruct((n, cols), BF16),
                   jax.ShapeDtypeStruct((n, e_cols), BF16)],
        scratch_shapes=[pltpu.VMEM((tm, d), BF16), pltpu.VMEM((1, LANES), F32)],
        compiler_params=_cparams(("arbitrary", "arbitrary")),
        name="norm_in_proj",
    )(x, g, w, w_f, b_f, tri, place)


def _head_lane_mask(hh):
    lane = lax.broadcasted_iota(jnp.int32, (1, LANES), 1)
    return (lane >= hh * HEAD_DIM) & (lane < (hh + 1) * HEAD_DIM)


def _tile_rel(t):
    return (lax.broadcasted_iota(jnp.int32, (t, t), 0)
            - lax.broadcasted_iota(jnp.int32, (t, t), 1))


def _mixer_specs(mixer, nb, t, s, d):
    qoff = N_BRANCH * d // LANES + (3 * mixer) * N_HEAD_BLOCKS
    koff = qoff + N_HEAD_BLOCKS
    voff = koff + N_HEAD_BLOCKS
    return [
        pl.BlockSpec((nb, t, LANES), lambda bg, hp, qi: (bg, qi, qoff + hp)),
        pl.BlockSpec((nb, s, LANES), lambda bg, hp, qi: (bg, 0, koff + hp)),
        pl.BlockSpec((nb, s, LANES), lambda bg, hp, qi: (bg, 0, voff + hp)),
    ], pl.BlockSpec((nb, t, LANES), lambda bg, hp, qi: (bg, qi, hp))


def _sb_kernel(q_ref, k_ref, v_ref, o_ref, vt_ref, *scratch, nb, t):
    n_blocks = k_ref.shape[1] // t
    q_tiles = q_ref.shape[1] // t
    streams = [(bi, hh) for bi in range(nb) for hh in range(HEADS_PER_BLOCK)]

    @pl.when(pl.program_id(2) == 0)
    def _build():
        def chunk(c, carry):
            rows = pl.ds(pl.multiple_of(c * t, t), t)
            for st, (bi, hh) in enumerate(streams):
                v2t = v_ref[bi, rows, :].astype(F32).T
                vt_ref[st, :, rows] = v2t[hh * HEAD_DIM:(hh + 1) * HEAD_DIM, :].astype(BF16)
            return carry

        lax.fori_loop(0, n_blocks, chunk, 0)

    def q_tile(sub, carry):
        q_rows = pl.ds(pl.multiple_of(sub * t, t), t)
        _sb_q_tile(pl.program_id(2) * q_tiles + sub, q_rows, q_ref, k_ref, o_ref, vt_ref, *scratch, nb=nb, t=t)
        return carry

    lax.fori_loop(0, q_tiles, q_tile, 0)


def _sb_q_tile(qi, q_rows, q_ref, k_ref, o_ref, vt_ref, acc_ref, cs_ref, r_ref, s_even_ref, s_odd_ref, *, nb, t):
    streams = [(bi, hh) for bi in range(nb) for hh in range(HEADS_PER_BLOCK)]
    q_m = []
    for bi, hh in streams:
        q2 = q_ref[bi, q_rows, :]
        q_m.append(jnp.where(_head_lane_mask(hh), q2, jnp.zeros_like(q2)))

    rel = _tile_rel(t)
    strictly_past = rel < 0
    upper = jnp.where(rel <= 0, 1.0, 0.0).astype(BF16)

    stream_groups = [range(g, g + HEADS_PER_BLOCK) for g in range(0, len(streams), HEADS_PER_BLOCK)]

    def scores(s_ref, j, group):
        rows = pl.ds(pl.multiple_of(j * t, t), t)
        for st in group:
            s_ref[st] = _dot_nt(k_ref[streams[st][0], rows, :], q_m[st])

    def softplus_sums(s_ref, diag, group):
        for st in group:
            z = s_ref[st]
            sp = jnp.maximum(z, 0.0) + jnp.log2(1.0 + jnp.exp2(-jnp.abs(z)))
            if diag:
                sp = jnp.where(strictly_past, sp, 0.0)
            r = _dot(upper, sp.astype(BF16)) + cs_ref[st]
            r_ref[st] = r
            cs_ref[st] = r[0:1, :]

    def weigh_values(s_ref, j, diag, group):
        rows = pl.ds(pl.multiple_of(j * t, t), t)
        for st in group:
            w = jnp.exp2(s_ref[st] - r_ref[st])
            if diag:
                w = jnp.where(strictly_past, w, 0.0)
            acc_ref[st] += _dot(vt_ref[st, :, rows], w.astype(BF16))

    def step(cur_ref, next_ref, j, j_next, diag):
        for group in stream_groups:
            scores(next_ref, j_next, group)
            softplus_sums(cur_ref, diag, group)
        for group in stream_groups:
            weigh_values(cur_ref, j, diag, group)

    def next_distance(i):
        lowest = cs_ref[0]
        for st in range(1, len(streams)):
            lowest = jnp.minimum(lowest, cs_ref[st])
        return jnp.where(jnp.min(lowest) > SB_EXIT, qi + 1, i + 1)

    acc_ref[...] = jnp.zeros(acc_ref.shape, F32)
    cs_ref[...] = jnp.zeros(cs_ref.shape, F32)
    for group in stream_groups:
        scores(s_even_ref, qi, group)
    step(s_even_ref, s_odd_ref, qi, jnp.maximum(qi - 1, 0), True)

    def body(i):
        j = qi - i
        j_next = jnp.maximum(j - 1, 0)

        @pl.when(i % 2 == 1)
        def _():
            step(s_odd_ref, s_even_ref, j, j_next, False)

        @pl.when(i % 2 == 0)
        def _():
            step(s_even_ref, s_odd_ref, j, j_next, False)

        return next_distance(i)

    lax.while_loop(lambda i: i <= qi, body, next_distance(0))

    for bi in range(nb):
        halves = [acc_ref[bi * HEADS_PER_BLOCK + hh] for hh in range(HEADS_PER_BLOCK)]
        o_ref[bi, q_rows, :] = jnp.concatenate(halves, axis=0).T.astype(o_ref.dtype)


def _sb_attention(proj3, d):
    b, s, _ = proj3.shape
    t = ATT_TILE
    nb = FLASH_BATCH if b % FLASH_BATCH == 0 else 1
    n_streams = nb * HEADS_PER_BLOCK
    q_rows = min(Q_TILES_PER_STEP * t, s)
    in_specs, out_spec = _mixer_specs(0, nb, q_rows, s, d)
    return pl.pallas_call(
        functools.partial(_sb_kernel, nb=nb, t=t),
        grid=(b // nb, N_HEAD_BLOCKS, s // q_rows),
        in_specs=in_specs,
        out_specs=out_spec,
        out_shape=jax.ShapeDtypeStruct((b, s, BRANCH_WIDTH), BF16),
        scratch_shapes=[pltpu.VMEM((n_streams, HEAD_DIM, s), BF16),
                        pltpu.VMEM((n_streams, HEAD_DIM, t), F32),
                        pltpu.VMEM((n_streams, 1, t), F32),
                        pltpu.VMEM((n_streams, t, t), F32),
                        pltpu.VMEM((n_streams, t, t), F32),
                        pltpu.VMEM((n_streams, t, t), F32)],
        compiler_params=_cparams(("parallel", "parallel", "arbitrary")),
        name="stickbreak_attn",
    )(proj3, proj3, proj3)


def _flash_kernel(*refs, moba, nb, t):
    q_tiles = refs[0].shape[1] // t

    def q_tile(sub, carry):
        q_rows = pl.ds(pl.multiple_of(sub * t, t), t)
        first = (pl.program_id(2) == 0) & (sub == 0)
        _flash_q_tile(pl.program_id(2) * q_tiles + sub, q_rows, first, *refs, moba=moba, nb=nb, t=t)
        return carry

    lax.fori_loop(0, q_tiles, q_tile, 0)


def _flash_q_tile(qi, q_rows, first, *refs, moba, nb, t):
    if moba:
        (q_ref, k_ref, v_ref, o_ref, kaug_ref, vt_ref, acc_ref, m_ref, s0_ref, s1_ref, s2_ref, s3_ref,
         bmax_ref, knorm_ref, kmean_ref, sel_ref) = refs
    else:
        (q_ref, k_ref, v_ref, e_ref, o_ref, kaug_ref, vt_ref, acc_ref, m_ref, s0_ref, s1_ref, s2_ref,
         s3_ref, bmax_ref, knorm_ref) = refs
    s_bufs = (s0_ref, s1_ref, s2_ref, s3_ref)
    n_blocks = k_ref.shape[1] // t
    hp = pl.program_id(1)
    lane = lax.broadcasted_iota(jnp.int32, (1, LANES), 1)
    streams = [(bi, hh) for bi in range(nb) for hh in range(HEADS_PER_BLOCK)]

    def piece_lanes(hh):
        e0 = _bias_lane(hh)
        return (lane >= e0) & (lane < e0 + N_BIAS_PIECES)

    head_of_lane = jnp.where(lax.broadcasted_iota(jnp.int32, (LANES, LANES), 0) >= HEAD_DIM, 1, 0)
    head_sum = jnp.where(head_of_lane == lax.broadcasted_iota(jnp.int32, (LANES, LANES), 1), 1.0, 0.0).astype(BF16)

    def head_sq_norm_max(x32):
        hi, lo = _split2(x32 * x32)
        return jnp.max(_dot(hi, head_sum) + _dot(lo, head_sum), axis=0, keepdims=True)

    def head_col(row, hh):
        return jnp.sum(jnp.where(lane == hh, row, 0.0), axis=1, keepdims=True)

    @pl.when(first)
    def _build():
        bmax_ref[...] = jnp.full(bmax_ref.shape, NEG_INF, F32)
        knorm_ref[...] = jnp.zeros(knorm_ref.shape, F32)
        tail = jnp.where(lax.broadcasted_iota(jnp.int32, (V_ROWS - HEAD_DIM, t), 0) == 0, 1.0, 0.0)

        def chunk(c, carry):
            start = pl.multiple_of(c * t, t)
            rows = pl.ds(start, t)
            extras = []
            if moba:
                pos = (lax.broadcasted_iota(jnp.int32, (t, LANES), 0) + start).astype(F32)
                for hh in range(HEADS_PER_BLOCK):
                    h = hp * HEADS_PER_BLOCK + hh
                    slope = pltpu.bitcast(jnp.full((1, LANES), (126 - h) << 23, jnp.int32), F32)
                    extra = jnp.zeros((t, LANES), F32)
                    for i, piece in enumerate(_split3(pos * slope * LOG2E)):
                        extra = jnp.where(lane == _bias_lane(hh) + i, piece.astype(F32), extra)
                    extras.append(extra.astype(BF16))
            for bi in range(nb):
                k2 = k_ref[bi, rows, :]
                k32 = k2.astype(F32)
                v2t = v_ref[bi, rows, :].astype(F32).T
                if moba:
                    kmean_ref[bi, pl.ds(c, 1), :] = jnp.sum(k32, axis=0, keepdims=True) * (1.0 / t)
                for hh in range(HEADS_PER_BLOCK):
                    st = bi * HEADS_PER_BLOCK + hh
                    extra = extras[hh] if moba else e_ref[bi, rows, :]
                    kaug_ref[st, rows, :] = jnp.where(_head_lane_mask(hh), k2, extra)
                    v_h = v2t[hh * HEAD_DIM:(hh + 1) * HEAD_DIM, :]
                    vt_ref[st, :, rows] = jnp.concatenate([v_h, tail], axis=0).astype(BF16)
                    piece_max = jnp.max(extra.astype(F32), axis=0, keepdims=True)
                    bias_max = jnp.sum(jnp.where(piece_lanes(hh), piece_max, 0.0), axis=1, keepdims=True)
                    run = jnp.maximum(bmax_ref[st, pl.ds(jnp.maximum(c - 1, 0), 1), :], bias_max)
                    bmax_ref[st, pl.ds(c, 1), :] = run
                knorm_ref[bi] = jnp.maximum(knorm_ref[bi], head_sq_norm_max(k32))
            return carry

        lax.fori_loop(0, n_blocks, chunk, 0)

    q_aug = []
    reach = []
    for bi in range(nb):
        q2 = q_ref[bi, q_rows, :]
        qk_sq = head_sq_norm_max(q2.astype(F32)) * knorm_ref[bi]
        for hh in range(HEADS_PER_BLOCK):
            ones = jnp.where(piece_lanes(hh), 1.0, 0.0).astype(BF16)
            q_aug.append(jnp.where(_head_lane_mask(hh), q2, jnp.broadcast_to(ones, q2.shape)))
            reach.append(jnp.sqrt(head_col(qk_sq, hh)) * NORM_SLACK)

    if moba:
        blk = lax.broadcasted_iota(jnp.int32, (n_blocks, 1), 0)
        blk_f = blk.astype(F32)
        for st, (bi, hh) in enumerate(streams):
            km = _split3(kmean_ref[bi])
            q2 = q_ref[bi, q_rows, :]
            qh = jnp.where(_head_lane_mask(hh), q2, jnp.zeros_like(q2))
            route = _dot_nt(km[0], qh) + _dot_nt(km[1], qh) + _dot_nt(km[2], qh)
            route = jnp.where(blk < qi, route, NEG_INF)
            sel = jnp.zeros((n_blocks, t), F32)
            for _ in range(MOBA_TOPK):
                top = jnp.max(route, axis=0, keepdims=True)
                first = jnp.min(jnp.where(route == top, blk_f, float(n_blocks)), axis=0, keepdims=True)
                hit = blk_f == first
                sel = jnp.where(hit, 1.0, sel)
                route = jnp.where(hit, -jnp.inf, route)
            sel_ref[st] = jnp.where(blk < qi, sel, 0.0)

    valid = _tile_rel(t) <= 0

    all_streams = range(len(streams))
    stream_groups = [range(g, g + HEADS_PER_BLOCK) for g in range(0, len(streams), HEADS_PER_BLOCK)]

    def scores(s_ref, j, group=all_streams):
        rows = pl.ds(pl.multiple_of(j * t, t), t)
        for st in group:
            s_ref[st] = _dot_nt(kaug_ref[st, rows, :], q_aug[st])

    def accumulate(s_ref, j, diag, group=all_streams):
        rows = pl.ds(pl.multiple_of(j * t, t), t)
        for st in group:
            s_t = s_ref[st]
            if diag:
                s_t = jnp.where(valid, s_t, NEG_INF)
            m_old = m_ref[st]
            m_new = jnp.maximum(m_old, jnp.max(s_t, axis=0, keepdims=True))
            m_sub = m_new
            if moba and not diag:
                picked = sel_ref[st, pl.ds(j, 1), :] > 0.0
                m_new = jnp.where(picked, m_new, m_old)
                m_sub = jnp.where(picked, m_new, -NEG_INF)
            alpha = jnp.exp2(m_old - m_new)
            p = jnp.exp2(s_t - m_sub)
            acc_ref[st] = alpha * acc_ref[st] + _dot(vt_ref[st, :, rows], p.astype(BF16))
            m_ref[st] = m_new

    def tile_at(dist):
        return jnp.maximum(qi - dist, 0)

    m_ref[...] = jnp.full(m_ref.shape, NEG_INF, F32)
    acc_ref[...] = jnp.zeros(acc_ref.shape, F32)
    scores(s_bufs[0], qi)
    for group in stream_groups:
        scores(s_bufs[1], tile_at(1), group)
        accumulate(s_bufs[0], qi, True, group)
        scores(s_bufs[2], tile_at(2), group)

    blk_col = lax.broadcasted_iota(jnp.int32, (n_blocks, 1), 0)
    fewest_dead = None
    for st in range(len(streams)):
        thr = jnp.min(m_ref[st], axis=1, keepdims=True) - FLASH_SKIP - reach[st]
        dead = (bmax_ref[st] < thr) & (blk_col < qi)
        n_dead = jnp.sum(jnp.where(dead, 1.0, 0.0), axis=0, keepdims=True)
        fewest_dead = n_dead if fewest_dead is None else jnp.minimum(fewest_dead, n_dead)
    n_past = qi - jnp.max(fewest_dead).astype(jnp.int32)

    def pair(first, cur, nxt):
        for k in range(2):
            for group in stream_groups:
                scores(s_bufs[nxt[k]], tile_at(first + 2 + k), group)
                accumulate(s_bufs[cur[k]], qi - first - k, False, group)

    def body(p, carry):
        first = 2 * p + 1

        @pl.when(p % 2 == 0)
        def _():
            pair(first, (1, 2), (3, 0))

        @pl.when(p % 2 == 1)
        def _():
            pair(first, (3, 0), (1, 2))

        return carry

    lax.fori_loop(0, n_past // 2, body, 0)

    for slot in (1, 3):
        @pl.when(n_past % 4 == slot)
        def _(slot=slot):
            accumulate(s_bufs[slot], qi - n_past, False)

    for bi in range(nb):
        halves = []
        for hh in range(HEADS_PER_BLOCK):
            acc = acc_ref[bi * HEADS_PER_BLOCK + hh]
            halves.append(acc[:HEAD_DIM] * (1.0 / acc[HEAD_DIM:HEAD_DIM + 1]))
        o_ref[bi, q_rows, :] = jnp.concatenate(halves, axis=0).T.astype(o_ref.dtype)


def _flash_attention(proj3, key_bias3, d, moba):
    b, s, _ = proj3.shape
    t = ATT_TILE
    nb = FLASH_BATCH if b % FLASH_BATCH == 0 else 1
    n_blocks = s // t
    n_streams = nb * HEADS_PER_BLOCK
    q_rows = min(Q_TILES_PER_STEP * t, s)
    in_specs, out_spec = _mixer_specs(2 if moba else 1, nb, q_rows, s, d)
    operands = [proj3, proj3, proj3]
    scratch = [pltpu.VMEM((n_streams, s, LANES), BF16),
               pltpu.VMEM((n_streams, V_ROWS, s), BF16),
               pltpu.VMEM((n_streams, V_ROWS, t), F32),
               pltpu.VMEM((n_streams, 1, t), F32),
               pltpu.VMEM((n_streams, t, t), F32),
               pltpu.VMEM((n_streams, t, t), F32),
               pltpu.VMEM((n_streams, t, t), F32),
               pltpu.VMEM((n_streams, t, t), F32),
               pltpu.VMEM((n_streams, n_blocks, 1), F32),
               pltpu.VMEM((nb, 1, LANES), F32)]
    if moba:
        scratch += [pltpu.VMEM((nb, n_blocks, LANES), F32),
                    pltpu.VMEM((n_streams, n_blocks, t), F32)]
    else:
        in_specs.append(pl.BlockSpec((nb, s, LANES), lambda bg, hp, qi: (bg, 0, hp)))
        operands.append(key_bias3)
    return pl.pallas_call(
        functools.partial(_flash_kernel, moba=moba, nb=nb, t=t),
        grid=(b // nb, N_HEAD_BLOCKS, s // q_rows),
        in_specs=in_specs,
        out_specs=out_spec,
        out_shape=jax.ShapeDtypeStruct((b, s, BRANCH_WIDTH), BF16),
        scratch_shapes=scratch,
        compiler_params=_cparams(("parallel", "parallel", "arbitrary")),
        name="moba_attn" if moba else "forgetting_attn",
    )(*operands)


def _mix_kernel(x_ref, oa_ref, ob_ref, oc_ref, g_ref, wb_ref, wo_ref, y_ref):
    d = x_ref.shape[1]
    mixed = None
    for n, o_ref in enumerate((oa_ref, ob_ref, oc_ref)):
        gate = jax.nn.sigmoid(g_ref[:, n * d:(n + 1) * d].astype(F32))
        term = gate * _dot(o_ref[...], wb_ref[n])
        mixed = term if mixed is None else mixed + term
    y_ref[...] = x_ref[...] + _dot(mixed.astype(BF16), wo_ref[...])


def _mix(x, o_a, o_b, o_c, proj, wb, wo, layer, tm):
    n, d = x.shape
    o_spec = pl.BlockSpec((tm, BRANCH_WIDTH), lambda i: (i, 0))
    return pl.pallas_call(
        _mix_kernel,
        grid=(n // tm,),
        in_specs=[
            pl.BlockSpec((tm, d), lambda i: (i, 0)),
            o_spec, o_spec, o_spec,
            pl.BlockSpec((tm, N_BRANCH * d), lambda i: (i, 0)),
            pl.BlockSpec((None, N_BRANCH, BRANCH_WIDTH, d), lambda i: (layer, 0, 0, 0)),
            pl.BlockSpec((None, d, d), lambda i: (layer, 0, 0)),
        ],
        out_specs=pl.BlockSpec((tm, d), lambda i: (i, 0)),
        out_shape=jax.ShapeDtypeStruct((n, d), F32),
        compiler_params=_cparams(("parallel",)),
        name="gated_mix_out_proj",
    )(x, o_a, o_b, o_c, proj, wb, wo)


def _mlp_kernel(x_ref, g_ref, wu_ref, wd_ref, gf_ref, y_ref, h_ref, acc_ref, *, final_norm):
    f = pl.program_id(1)

    @pl.when(f == 0)
    def _():
        h_ref[...] = _rms(x_ref[...], g_ref[...]).astype(BF16)
        acc_ref[...] = jnp.zeros_like(acc_ref)

    hid = jnp.square(jnp.maximum(_dot(h_ref[...], wu_ref[...]), 0.0))
    acc_ref[...] += _dot(hid.astype(BF16), wd_ref[...])

    @pl.when(f == pl.num_programs(1) - 1)
    def _():
        y = x_ref[...] + acc_ref[...]
        if final_norm:
            y = _rms(y, gf_ref[...])
        y_ref[...] = y


def _mlp(x, g, wu, wd, g_final, layer, final_norm, tm, tf):
    n, d = x.shape
    d_ff = wu.shape[2]
    return pl.pallas_call(
        functools.partial(_mlp_kernel, final_norm=final_norm),
        grid=(n // tm, d_ff // tf),
        in_specs=[
            pl.BlockSpec((tm, d), lambda i, f: (i, 0)),
            pl.BlockSpec((1, d), lambda i, f: (0, 0)),
            pl.BlockSpec((None, d, tf), lambda i, f: (layer, 0, f)),
            pl.BlockSpec((None, tf, d), lambda i, f: (layer, f, 0)),
            pl.BlockSpec((1, d), lambda i, f: (0, 0)),
        ],
        out_specs=pl.BlockSpec((tm, d), lambda i, f: (i, 0)),
        out_shape=jax.ShapeDtypeStruct((n, d), F32),
        scratch_shapes=[pltpu.VMEM((tm, d), BF16), pltpu.VMEM((tm, d), F32)],
        compiler_params=_cparams(("parallel", "arbitrary")),
        name="relu2_mlp",
    )(x, g, wu, wd, g_final)


REPACK_ROWS = 512


def _repack_kernel(w_ref, wf_ref, main_ref, f_ref):
    i = pl.program_id(1)
    q_chunks = BRANCH_WIDTH // REPACK_ROWS
    n_gate_chunks = N_BRANCH * w_ref.shape[2] // REPACK_ROWS
    mixer_chunk = jnp.maximum(i - n_gate_chunks, 0) % (3 * q_chunks)
    is_q = (i >= n_gate_chunks) & (mixer_chunk < q_chunks)
    scale = jnp.where(is_q, HEAD_DIM ** -0.5 * LOG2E, 1.0)
    main_ref[...] = (w_ref[0] * scale).astype(BF16)

    @pl.when(i == 0)
    def _():
        pad = jnp.zeros((f_ref.shape[0] - N_HEADS, f_ref.shape[1]), F32)
        f_ref[...] = jnp.concatenate([wf_ref[0], pad], axis=0).astype(BF16)


def _prep_in_proj(w_in):
    depth, d, cols = w_in.shape
    w_t = jnp.swapaxes(w_in, 1, 2)
    bw = BRANCH_WIDTH
    r = REPACK_ROWS
    f_lo = 6 * bw
    c_lo = f_lo + N_HEADS
    g_lo = c_lo + 3 * bw
    main_rows = cols - N_HEADS
    n_gate = (cols - g_lo) // r
    n_ab = f_lo // r

    def src_row(l, i):
        row = jnp.where(i < n_gate, g_lo + r * i,
                        jnp.where(i < n_gate + n_ab, r * (i - n_gate), c_lo + r * (i - n_gate - n_ab)))
        return l, pl.multiple_of(row, N_HEADS), 0

    return pl.pallas_call(
        _repack_kernel,
        grid=(depth, main_rows // r),
        in_specs=[pl.BlockSpec((pl.Element(1), pl.Element(r), pl.Element(d)), src_row),
                  pl.BlockSpec((pl.Element(1), pl.Element(N_HEADS), pl.Element(d)), lambda l, i: (l, f_lo, 0))],
        out_specs=[pl.BlockSpec((None, r, d), lambda l, i: (l, i, 0)),
                   pl.BlockSpec((None, LANES, d), lambda l, i: (l, 0, 0))],
        out_shape=[jax.ShapeDtypeStruct((depth, main_rows, d), BF16),
                   jax.ShapeDtypeStruct((depth, LANES, d), BF16)],
        compiler_params=_cparams(("parallel", "arbitrary")),
        name="repack_w_in",
    )(w_t, w_t)


def kernel(x, norm_mix, w_in, b_forget, w_branch, w_out, norm_mlp, w_up, w_down, norm_final):
    b, s, d = x.shape
    depth = w_in.shape[0]
    n = b * s
    assert s % ATT_TILE == 0 and d % LANES == 0

    w_main, w_f = _prep_in_proj(w_in)
    wb = w_branch.astype(BF16)
    wo = w_out.astype(BF16)
    wu = w_up.astype(BF16)
    wd = w_down.astype(BF16)
    b_f = jnp.pad(b_forget, ((0, 0), (0, LANES - N_HEADS)))[:, None, :]

    tm = min(1024, s)
    xf = x.reshape(n, d)
    for l in range(depth):
        g_mix = norm_mix[l][None, :]
        proj, key_bias = _in_proj(xf, g_mix, w_main, w_f, b_f[l], l, s, tm, 1536, 256)
        proj3 = proj.reshape(b, s, -1)
        o_a = _sb_attention(proj3, d).reshape(n, -1)
        o_b = _flash_attention(proj3, key_bias.reshape(b, s, -1), d, False).reshape(n, -1)
        o_c = _flash_attention(proj3, None, d, True).reshape(n, -1)
        xf = _mix(xf, o_a, o_b, o_c, proj, wb, wo, l, min(512, n))
        xf = _mlp(xf, norm_mlp[l][None, :], wu, wd, norm_final[None, :],
                  l, l == depth - 1, min(1024, n), 1024)
    return xf.reshape(b, s, d)
```

```python
import functools
import math

import jax
import jax.numpy as jnp
import numpy as np
from jax import lax
from jax.experimental import pallas as pl
from jax.experimental.pallas import tpu as pltpu

F32 = jnp.float32
BF16 = jnp.bfloat16

HEAD_DIM = 64
N_HEADS = 8
BRANCH_WIDTH = N_HEADS * HEAD_DIM
N_BRANCH = 3
MOBA_BLOCK = 256
MOBA_TOPK = 3
RMS_EPS = 1e-6
NEG_INF = -1e30
LOG2E = math.log2(math.e)
LANES = 128
HEADS_PER_BLOCK = LANES // HEAD_DIM
N_HEAD_BLOCKS = BRANCH_WIDTH // LANES
N_BIAS_PIECES = 3
ATT_TILE = 256
Q_TILES_PER_STEP = 4
SB_EXIT = 160.0
FLASH_SKIP = 160.0
NORM_SLACK = 1.001
FLASH_BATCH = 4
V_ROWS = HEAD_DIM + 16
VMEM_LIMIT = 60000 * 1024


def _cparams(sem):
    return pltpu.CompilerParams(dimension_semantics=sem, vmem_limit_bytes=VMEM_LIMIT)


def _rms(x, g):
    ms = jnp.mean(x * x, axis=-1, keepdims=True)
    return x * lax.rsqrt(ms + RMS_EPS) * g


def _dot(a, b):
    return jnp.dot(a, b, preferred_element_type=F32)


def _dot_nt(a, b):
    return lax.dot_general(a, b, (((1,), (1,)), ((), ())), preferred_element_type=F32)


def _transposed(x):
    return x.astype(F32).T.astype(BF16)


def _split2(x):
    hi = x.astype(BF16)
    lo = (x - hi.astype(F32)).astype(BF16)
    return hi, lo


def _split3(x):
    hi = x.astype(BF16)
    r = x - hi.astype(F32)
    mid = r.astype(BF16)
    lo = (r - mid.astype(F32)).astype(BF16)
    return hi, mid, lo


def _bias_lane(hh):
    return (1 - hh) * HEAD_DIM


def _in_proj_kernel(x_ref, g_ref, w_ref, wf_ref, b_ref, tri_ref, place_ref,
                    o_ref, e_ref, h_ref, carry_ref, *, tiles_per_seq):
    @pl.when(pl.program_id(1) == 0)
    def _():
        h = _rms(x_ref[...], g_ref[...]).astype(BF16)
        h_ref[...] = h

        @pl.when(pl.program_id(0) % tiles_per_seq == 0)
        def _():
            carry_ref[...] = jnp.zeros_like(carry_ref)

        y = _dot_nt(h, wf_ref[...]) + b_ref[...]
        logf = jnp.minimum(y, 0.0) - jnp.log(1.0 + jnp.exp(-jnp.abs(y)))
        pieces = _split3(logf)
        tri = tri_ref[...]
        tc = tri.shape[0]
        cum = carry_ref[...]
        for c in range(h.shape[0] // tc):
            rows = slice(c * tc, (c + 1) * tc)
            cum = cum[-1:, :]
            for piece in pieces:
                cum = cum + _dot(tri, piece[rows])
            bias_pieces = jnp.concatenate(_split3(cum * (-LOG2E)), axis=1)
            e_ref[rows, :] = _dot(bias_pieces, place_ref[...]).astype(BF16)
        carry_ref[...] = cum[-1:, :]

    o_ref[...] = _dot_nt(h_ref[...], w_ref[...]).astype(o_ref.dtype)


def _bias_placement():
    place = np.zeros((N_BIAS_PIECES, LANES, N_HEAD_BLOCKS * LANES), np.float32)
    for h in range(N_HEADS):
        hp, hh = divmod(h, HEADS_PER_BLOCK)
        for i in range(N_BIAS_PIECES):
            place[i, h, hp * LANES + _bias_lane(hh) + i] = 1.0
    return jnp.asarray(place.reshape(N_BIAS_PIECES * LANES, -1), BF16)


def _in_proj(x, g, w, w_f, b_f, layer, s, tm, tn, tc):
    n, d = x.shape
    cols = w.shape[1]
    tri = jnp.asarray(np.tril(np.ones((tc, tc), np.float32)), BF16)
    place = _bias_placement()
    e_cols = place.shape[1]
    const2 = lambda i, j: (0, 0)
    return pl.pallas_call(
        functools.partial(_in_proj_kernel, tiles_per_seq=s // tm),
        grid=(n // tm, cols // tn),
        in_specs=[
            pl.BlockSpec((tm, d), lambda i, j: (i, 0)),
            pl.BlockSpec((1, d), const2),
            pl.BlockSpec((None, tn, d), lambda i, j: (layer, j, 0)),
            pl.BlockSpec((None, LANES, d), lambda i, j: (layer, 0, 0)),
            pl.BlockSpec((1, LANES), const2),
            pl.BlockSpec((tc, tc), const2),
            pl.BlockSpec(place.shape, const2),
        ],
        out_specs=[pl.BlockSpec((tm, tn), lambda i, j: (i, j)),
                   pl.BlockSpec((tm, e_cols), lambda i, j: (i, 0))],
        out_shape=[jax.ShapeDtypeStruct((n, cols), BF16),
                   jax.ShapeDtypeStruct((n, e_cols), BF16)],
        scratch_shapes=[pltpu.VMEM((tm, d), BF16), pltpu.VMEM((1, LANES), F32)],
        compiler_params=_cparams(("arbitrary", "arbitrary")),
        name="norm_in_proj",
    )(x, g, w, w_f, b_f, tri, place)


def _head_lane_mask(hh):
    lane = lax.broadcasted_iota(jnp.int32, (1, LANES), 1)
    return (lane >= hh * HEAD_DIM) & (lane < (hh + 1) * HEAD_DIM)


def _tile_rel(t):
    return (lax.broadcasted_iota(jnp.int32, (t, t), 0)
            - lax.broadcasted_iota(jnp.int32, (t, t), 1))


def _mixer_specs(mixer, nb, t, s, d):
    qoff = N_BRANCH * d // LANES + (3 * mixer) * N_HEAD_BLOCKS
    koff = qoff + N_HEAD_BLOCKS
    voff = koff + N_HEAD_BLOCKS
    return [
        pl.BlockSpec((nb, t, LANES), lambda bg, hp, qi: (bg, qi, qoff + hp)),
        pl.BlockSpec((nb, s, LANES), lambda bg, hp, qi: (bg, 0, koff + hp)),
        pl.BlockSpec((nb, s, LANES), lambda bg, hp, qi: (bg, 0, voff + hp)),
    ], pl.BlockSpec((nb, t, LANES), lambda bg, hp, qi: (bg, qi, hp))


def _sb_kernel(q_ref, k_ref, v_ref, o_ref, vt_ref, *scratch, nb, t):
    n_blocks = k_ref.shape[1] // t
    q_tiles = q_ref.shape[1] // t
    streams = [(bi, hh) for bi in range(nb) for hh in range(HEADS_PER_BLOCK)]

    @pl.when(pl.program_id(2) == 0)
    def _build():
        def chunk(c, carry):
            rows = pl.ds(pl.multiple_of(c * t, t), t)
            for st, (bi, hh) in enumerate(streams):
                v2t = v_ref[bi, rows, :].astype(F32).T
                vt_ref[st, :, rows] = v2t[hh * HEAD_DIM:(hh + 1) * HEAD_DIM, :].astype(BF16)
            return carry

        lax.fori_loop(0, n_blocks, chunk, 0)

    def q_tile(sub, carry):
        q_rows = pl.ds(pl.multiple_of(sub * t, t), t)
        _sb_q_tile(pl.program_id(2) * q_tiles + sub, q_rows, q_ref, k_ref, o_ref, vt_ref, *scratch, nb=nb, t=t)
        return carry

    lax.fori_loop(0, q_tiles, q_tile, 0)


def _sb_q_tile(qi, q_rows, q_ref, k_ref, o_ref, vt_ref, acc_ref, cs_ref, r_ref, s_even_ref, s_odd_ref, *, nb, t):
    streams = [(bi, hh) for bi in range(nb) for hh in range(HEADS_PER_BLOCK)]
    q_m = []
    for bi, hh in streams:
        q2 = q_ref[bi, q_rows, :]
        q_m.append(_transposed(jnp.where(_head_lane_mask(hh), q2, jnp.zeros_like(q2))))

    rel = _tile_rel(t)
    strictly_past = rel < 0
    upper = jnp.where(rel <= 0, 1.0, 0.0).astype(BF16)

    stream_groups = [range(g, g + HEADS_PER_BLOCK) for g in range(0, len(streams), HEADS_PER_BLOCK)]

    def scores(s_ref, j, group):
        rows = pl.ds(pl.multiple_of(j * t, t), t)
        for st in group:
            s_ref[st] = _dot(k_ref[streams[st][0], rows, :], q_m[st])

    def softplus_sums(s_ref, diag, group):
        for st in group:
            z = s_ref[st]
            sp = jnp.maximum(z, 0.0) + jnp.log2(1.0 + jnp.exp2(-jnp.abs(z)))
            if diag:
                sp = jnp.where(strictly_past, sp, 0.0)
            r = _dot(upper, sp.astype(BF16)) + cs_ref[st]
            r_ref[st] = r
            cs_ref[st] = r[0:1, :]

    def weigh_values(s_ref, j, diag, group):
        rows = pl.ds(pl.multiple_of(j * t, t), t)
        for st in group:
            w = jnp.exp2(s_ref[st] - r_ref[st])
            if diag:
                w = jnp.where(strictly_past, w, 0.0)
            acc_ref[st] += _dot(vt_ref[st, :, rows], w.astype(BF16))

    def step(cur_ref, next_ref, j, j_next, diag):
        for group in stream_groups:
            scores(next_ref, j_next, group)
            softplus_sums(cur_ref, diag, group)
        for group in stream_groups:
            weigh_values(cur_ref, j, diag, group)

    def next_distance(i):
        lowest = cs_ref[0]
        for st in range(1, len(streams)):
            lowest = jnp.minimum(lowest, cs_ref[st])
        return jnp.where(jnp.min(lowest) > SB_EXIT, qi + 1, i + 1)

    acc_ref[...] = jnp.zeros(acc_ref.shape, F32)
    cs_ref[...] = jnp.zeros(cs_ref.shape, F32)
    for group in stream_groups:
        scores(s_even_ref, qi, group)
    step(s_even_ref, s_odd_ref, qi, jnp.maximum(qi - 1, 0), True)

    def body(i):
        j = qi - i
        j_next = jnp.maximum(j - 1, 0)

        @pl.when(i % 2 == 1)
        def _():
            step(s_odd_ref, s_even_ref, j, j_next, False)

        @pl.when(i % 2 == 0)
        def _():
            step(s_even_ref, s_odd_ref, j, j_next, False)

        return next_distance(i)

    lax.while_loop(lambda i: i <= qi, body, next_distance(0))

    for bi in range(nb):
        halves = [acc_ref[bi * HEADS_PER_BLOCK + hh] for hh in range(HEADS_PER_BLOCK)]
        o_ref[bi, q_rows, :] = jnp.concatenate(halves, axis=0).T.astype(o_ref.dtype)


def _sb_attention(proj3, d):
    b, s, _ = proj3.shape
    t = ATT_TILE
    nb = FLASH_BATCH if b % FLASH_BATCH == 0 else 1
    n_streams = nb * HEADS_PER_BLOCK
    q_rows = min(Q_TILES_PER_STEP * t, s)
    in_specs, out_spec = _mixer_specs(0, nb, q_rows, s, d)
    return pl.pallas_call(
        functools.partial(_sb_kernel, nb=nb, t=t),
        grid=(b // nb, N_HEAD_BLOCKS, s // q_rows),
        in_specs=in_specs,
        out_specs=out_spec,
        out_shape=jax.ShapeDtypeStruct((b, s, BRANCH_WIDTH), BF16),
        scratch_shapes=[pltpu.VMEM((n_streams, HEAD_DIM, s), BF16),
                        pltpu.VMEM((n_streams, HEAD_DIM, t), F32),
                        pltpu.VMEM((n_streams, 1, t), F32),
                        pltpu.VMEM((n_streams, t, t), F32),
                        pltpu.VMEM((n_streams, t, t), F32),
                        pltpu.VMEM((n_streams, t, t), F32)],
        compiler_params=_cparams(("parallel", "parallel", "arbitrary")),
        name="stickbreak_attn",
    )(proj3, proj3, proj3)


def _flash_kernel(*refs, moba, nb, t):
    q_tiles = refs[0].shape[1] // t

    def q_tile(sub, carry):
        q_rows = pl.ds(pl.multiple_of(sub * t, t), t)
        first = (pl.program_id(2) == 0) & (sub == 0)
        _flash_q_tile(pl.program_id(2) * q_tiles + sub, q_rows, first, *refs, moba=moba, nb=nb, t=t)
        return carry

    lax.fori_loop(0, q_tiles, q_tile, 0)


def _flash_q_tile(qi, q_rows, first, *refs, moba, nb, t):
    if moba:
        (q_ref, k_ref, v_ref, o_ref, kaug_ref, vt_ref, acc_ref, m_ref, s0_ref, s1_ref, s2_ref, s3_ref,
         bmax_ref, knorm_ref, kmean_ref, sel_ref) = refs
    else:
        (q_ref, k_ref, v_ref, e_ref, o_ref, kaug_ref, vt_ref, acc_ref, m_ref, s0_ref, s1_ref, s2_ref,
         s3_ref, bmax_ref, knorm_ref) = refs
    s_bufs = (s0_ref, s1_ref, s2_ref, s3_ref)
    n_blocks = k_ref.shape[1] // t
    hp = pl.program_id(1)
    lane = lax.broadcasted_iota(jnp.int32, (1, LANES), 1)
    streams = [(bi, hh) for bi in range(nb) for hh in range(HEADS_PER_BLOCK)]

    def piece_lanes(hh):
        e0 = _bias_lane(hh)
        return (lane >= e0) & (lane < e0 + N_BIAS_PIECES)

    head_of_lane = jnp.where(lax.broadcasted_iota(jnp.int32, (LANES, LANES), 0) >= HEAD_DIM, 1, 0)
    head_sum = jnp.where(head_of_lane == lax.broadcasted_iota(jnp.int32, (LANES, LANES), 1), 1.0, 0.0).astype(BF16)

    def head_sq_norm_max(x32):
        hi, lo = _split2(x32 * x32)
        return jnp.max(_dot(hi, head_sum) + _dot(lo, head_sum), axis=0, keepdims=True)

    def head_col(row, hh):
        return jnp.sum(jnp.where(lane == hh, row, 0.0), axis=1, keepdims=True)

    @pl.when(first)
    def _build():
        bmax_ref[...] = jnp.full(bmax_ref.shape, NEG_INF, F32)
        knorm_ref[...] = jnp.zeros(knorm_ref.shape, F32)
        tail = jnp.where(lax.broadcasted_iota(jnp.int32, (V_ROWS - HEAD_DIM, t), 0) == 0, 1.0, 0.0)

        def chunk(c, carry):
            start = pl.multiple_of(c * t, t)
            rows = pl.ds(start, t)
            extras = []
            if moba:
                pos = (lax.broadcasted_iota(jnp.int32, (t, LANES), 0) + start).astype(F32)
                for hh in range(HEADS_PER_BLOCK):
                    h = hp * HEADS_PER_BLOCK + hh
                    slope = pltpu.bitcast(jnp.full((1, LANES), (126 - h) << 23, jnp.int32), F32)
                    extra = jnp.zeros((t, LANES), F32)
                    for i, piece in enumerate(_split3(pos * slope * LOG2E)):
                        extra = jnp.where(lane == _bias_lane(hh) + i, piece.astype(F32), extra)
                    extras.append(extra.astype(BF16))
            for bi in range(nb):
                k2 = k_ref[bi, rows, :]
                k32 = k2.astype(F32)
                v2t = v_ref[bi, rows, :].astype(F32).T
                if moba:
                    kmean_ref[bi, pl.ds(c, 1), :] = jnp.sum(k32, axis=0, keepdims=True) * (1.0 / t)
                for hh in range(HEADS_PER_BLOCK):
                    st = bi * HEADS_PER_BLOCK + hh
                    extra = extras[hh] if moba else e_ref[bi, rows, :]
                    kaug_ref[st, rows, :] = jnp.where(_head_lane_mask(hh), k2, extra)
                    v_h = v2t[hh * HEAD_DIM:(hh + 1) * HEAD_DIM, :]
                    vt_ref[st, :, rows] = jnp.concatenate([v_h, tail], axis=0).astype(BF16)
                    piece_max = jnp.max(extra.astype(F32), axis=0, keepdims=True)
                    bias_max = jnp.sum(jnp.where(piece_lanes(hh), piece_max, 0.0), axis=1, keepdims=True)
                    run = jnp.maximum(bmax_ref[st, pl.ds(jnp.maximum(c - 1, 0), 1), :], bias_max)
                    bmax_ref[st, pl.ds(c, 1), :] = run
                knorm_ref[bi] = jnp.maximum(knorm_ref[bi], head_sq_norm_max(k32))
            return carry

        lax.fori_loop(0, n_blocks, chunk, 0)

    q_aug = []
    reach = []
    for bi in range(nb):
        q2 = q_ref[bi, q_rows, :]
        qk_sq = head_sq_norm_max(q2.astype(F32)) * knorm_ref[bi]
        for hh in range(HEADS_PER_BLOCK):
            ones = jnp.where(piece_lanes(hh), 1.0, 0.0).astype(BF16)
            q_aug.append(_transposed(jnp.where(_head_lane_mask(hh), q2, jnp.broadcast_to(ones, q2.shape))))
            reach.append(jnp.sqrt(head_col(qk_sq, hh)) * NORM_SLACK)

    if moba:
        blk = lax.broadcasted_iota(jnp.int32, (n_blocks, 1), 0)
        blk_f = blk.astype(F32)
        for st, (bi, hh) in enumerate(streams):
            km = _split3(kmean_ref[bi])
            q2 = q_ref[bi, q_rows, :]
            qh = jnp.where(_head_lane_mask(hh), q2, jnp.zeros_like(q2))
            route = _dot_nt(km[0], qh) + _dot_nt(km[1], qh) + _dot_nt(km[2], qh)
            route = jnp.where(blk < qi, route, NEG_INF)
            sel = jnp.zeros((n_blocks, t), F32)
            for _ in range(MOBA_TOPK):
                top = jnp.max(route, axis=0, keepdims=True)
                first = jnp.min(jnp.where(route == top, blk_f, float(n_blocks)), axis=0, keepdims=True)
                hit = blk_f == first
                sel = jnp.where(hit, 1.0, sel)
                route = jnp.where(hit, -jnp.inf, route)
            sel_ref[st] = jnp.where(blk < qi, sel, 0.0)

    valid = _tile_rel(t) <= 0

    all_streams = range(len(streams))
    stream_groups = [range(g, g + HEADS_PER_BLOCK) for g in range(0, len(streams), HEADS_PER_BLOCK)]

    def scores(s_ref, j, group=all_streams):
        rows = pl.ds(pl.multiple_of(j * t, t), t)
        for st in group:
            s_ref[st] = _dot(kaug_ref[st, rows, :], q_aug[st])

    def accumulate(s_ref, j, diag, group=all_streams):
        rows = pl.ds(pl.multiple_of(j * t, t), t)
        for st in group:
            s_t = s_ref[st]
            if diag:
                s_t = jnp.where(valid, s_t, NEG_INF)
            m_old = m_ref[st]
            m_new = jnp.maximum(m_old, jnp.max(s_t, axis=0, keepdims=True))
            m_sub = m_new
            if moba and not diag:
                picked = sel_ref[st, pl.ds(j, 1), :] > 0.0
                m_new = jnp.where(picked, m_new, m_old)
                m_sub = jnp.where(picked, m_new, -NEG_INF)
            alpha = jnp.exp2(m_old - m_new)
            p = jnp.exp2(s_t - m_sub)
            acc_ref[st] = alpha * acc_ref[st] + _dot(vt_ref[st, :, rows], p.astype(BF16))
            m_ref[st] = m_new

    def tile_at(dist):
        return jnp.maximum(qi - dist, 0)

    m_ref[...] = jnp.full(m_ref.shape, NEG_INF, F32)
    acc_ref[...] = jnp.zeros(acc_ref.shape, F32)
    scores(s_bufs[0], qi)
    for group in stream_groups:
        scores(s_bufs[1], tile_at(1), group)
        accumulate(s_bufs[0], qi, True, group)
        scores(s_bufs[2], tile_at(2), group)

    blk_col = lax.broadcasted_iota(jnp.int32, (n_blocks, 1), 0)
    fewest_dead = None
    for st in range(len(streams)):
        thr = jnp.min(m_ref[st], axis=1, keepdims=True) - FLASH_SKIP - reach[st]
        dead = (bmax_ref[st] < thr) & (blk_col < qi)
        n_dead = jnp.sum(jnp.where(dead, 1.0, 0.0), axis=0, keepdims=True)
        fewest_dead = n_dead if fewest_dead is None else jnp.minimum(fewest_dead, n_dead)
    n_past = qi - jnp.max(fewest_dead).astype(jnp.int32)

    def pair(first, cur, nxt):
        for k in range(2):
            for group in stream_groups:
                scores(s_bufs[nxt[k]], tile_at(first + 2 + k), group)
                accumulate(s_bufs[cur[k]], qi - first - k, False, group)

    def body(p, carry):
        first = 2 * p + 1

        @pl.when(p % 2 == 0)
        def _():
            pair(first, (1, 2), (3, 0))

        @pl.when(p % 2 == 1)
        def _():
            pair(first, (3, 0), (1, 2))

        return carry

    lax.fori_loop(0, n_past // 2, body, 0)

    for slot in (1, 3):
        @pl.when(n_past % 4 == slot)
        def _(slot=slot):
            accumulate(s_bufs[slot], qi - n_past, False)

    for bi in range(nb):
        halves = []
        for hh in range(HEADS_PER_BLOCK):
            acc = acc_ref[bi * HEADS_PER_BLOCK + hh]
            halves.append(acc[:HEAD_DIM] * (1.0 / acc[HEAD_DIM:HEAD_DIM + 1]))
        o_ref[bi, q_rows, :] = jnp.concatenate(halves, axis=0).T.astype(o_ref.dtype)


def _flash_attention(proj3, key_bias3, d, moba):
    b, s, _ = proj3.shape
    t = ATT_TILE
    nb = FLASH_BATCH if b % FLASH_BATCH == 0 else 1
    n_blocks = s // t
    n_streams = nb * HEADS_PER_BLOCK
    q_rows = min(Q_TILES_PER_STEP * t, s)
    in_specs, out_spec = _mixer_specs(2 if moba else 1, nb, q_rows, s, d)
    operands = [proj3, proj3, proj3]
    scratch = [pltpu.VMEM((n_streams, s, LANES), BF16),
               pltpu.VMEM((n_streams, V_ROWS, s), BF16),
               pltpu.VMEM((n_streams, V_ROWS, t), F32),
               pltpu.VMEM((n_streams, 1, t), F32),
               pltpu.VMEM((n_streams, t, t), F32),
               pltpu.VMEM((n_streams, t, t), F32),
               pltpu.VMEM((n_streams, t, t), F32),
               pltpu.VMEM((n_streams, t, t), F32),
               pltpu.VMEM((n_streams, n_blocks, 1), F32),
               pltpu.VMEM((nb, 1, LANES), F32)]
    if moba:
        scratch += [pltpu.VMEM((nb, n_blocks, LANES), F32),
                    pltpu.VMEM((n_streams, n_blocks, t), F32)]
    else:
        in_specs.append(pl.BlockSpec((nb, s, LANES), lambda bg, hp, qi: (bg, 0, hp)))
        operands.append(key_bias3)
    return pl.pallas_call(
        functools.partial(_flash_kernel, moba=moba, nb=nb, t=t),
        grid=(b // nb, N_HEAD_BLOCKS, s // q_rows),
        in_specs=in_specs,
        out_specs=out_spec,
        out_shape=jax.ShapeDtypeStruct((b, s, BRANCH_WIDTH), BF16),
        scratch_shapes=scratch,
        compiler_params=_cparams(("parallel", "parallel", "arbitrary")),
        name="moba_attn" if moba else "forgetting_attn",
    )(*operands)


def _mix_kernel(x_ref, oa_ref, ob_ref, oc_ref, g_ref, wb_ref, wo_ref, y_ref):
    d = x_ref.shape[1]
    mixed = None
    for n, o_ref in enumerate((oa_ref, ob_ref, oc_ref)):
        gate = jax.nn.sigmoid(g_ref[:, n * d:(n + 1) * d].astype(F32))
        term = gate * _dot(o_ref[...], wb_ref[n])
        mixed = term if mixed is None else mixed + term
    y_ref[...] = x_ref[...] + _dot(mixed.astype(BF16), wo_ref[...])


def _mix(x, o_a, o_b, o_c, proj, wb, wo, layer, tm):
    n, d = x.shape
    o_spec = pl.BlockSpec((tm, BRANCH_WIDTH), lambda i: (i, 0))
    return pl.pallas_call(
        _mix_kernel,
        grid=(n // tm,),
        in_specs=[
            pl.BlockSpec((tm, d), lambda i: (i, 0)),
            o_spec, o_spec, o_spec,
            pl.BlockSpec((tm, N_BRANCH * d), lambda i: (i, 0)),
            pl.BlockSpec((None, N_BRANCH, BRANCH_WIDTH, d), lambda i: (layer, 0, 0, 0)),
            pl.BlockSpec((None, d, d), lambda i: (layer, 0, 0)),
        ],
        out_specs=pl.BlockSpec((tm, d), lambda i: (i, 0)),
        out_shape=jax.ShapeDtypeStruct((n, d), F32),
        compiler_params=_cparams(("parallel",)),
        name="gated_mix_out_proj",
    )(x, o_a, o_b, o_c, proj, wb, wo)


def _mlp_kernel(x_ref, g_ref, wu_ref, wd_ref, gf_ref, y_ref, h_ref, acc_ref, *, final_norm):
    f = pl.program_id(1)

    @pl.when(f == 0)
    def _():
        h_ref[...] = _rms(x_ref[...], g_ref[...]).astype(BF16)
        acc_ref[...] = jnp.zeros_like(acc_ref)

    hid = jnp.square(jnp.maximum(_dot(h_ref[...], wu_ref[...]), 0.0))
    acc_ref[...] += _dot(hid.astype(BF16), wd_ref[...])

    @pl.when(f == pl.num_programs(1) - 1)
    def _():
        y = x_ref[...] + acc_ref[...]
        if final_norm:
            y = _rms(y, gf_ref[...])
        y_ref[...] = y


def _mlp(x, g, wu, wd, g_final, layer, final_norm, tm, tf):
    n, d = x.shape
    d_ff = wu.shape[2]
    return pl.pallas_call(
        functools.partial(_mlp_kernel, final_norm=final_norm),
        grid=(n // tm, d_ff // tf),
        in_specs=[
            pl.BlockSpec((tm, d), lambda i, f: (i, 0)),
            pl.BlockSpec((1, d), lambda i, f: (0, 0)),
            pl.BlockSpec((None, d, tf), lambda i, f: (layer, 0, f)),
            pl.BlockSpec((None, tf, d), lambda i, f: (layer, f, 0)),
            pl.BlockSpec((1, d), lambda i, f: (0, 0)),
        ],
        out_specs=pl.BlockSpec((tm, d), lambda i, f: (i, 0)),
        out_shape=jax.ShapeDtypeStruct((n, d), F32),
        scratch_shapes=[pltpu.VMEM((tm, d), BF16), pltpu.VMEM((tm, d), F32)],
        compiler_params=_cparams(("parallel", "arbitrary")),
        name="relu2_mlp",
    )(x, g, wu, wd, g_final)


REPACK_ROWS = 512


def _repack_kernel(w_ref, wf_ref, main_ref, f_ref):
    i = pl.program_id(1)
    q_chunks = BRANCH_WIDTH // REPACK_ROWS
    n_gate_chunks = N_BRANCH * w_ref.shape[2] // REPACK_ROWS
    mixer_chunk = jnp.maximum(i - n_gate_chunks, 0) % (3 * q_chunks)
    is_q = (i >= n_gate_chunks) & (mixer_chunk < q_chunks)
    scale = jnp.where(is_q, HEAD_DIM ** -0.5 * LOG2E, 1.0)
    main_ref[...] = (w_ref[0] * scale).astype(BF16)

    @pl.when(i == 0)
    def _():
        pad = jnp.zeros((f_ref.shape[0] - N_HEADS, f_ref.shape[1]), F32)
        f_ref[...] = jnp.concatenate([wf_ref[0], pad], axis=0).astype(BF16)


def _prep_in_proj(w_in):
    depth, d, cols = w_in.shape
    w_t = jnp.swapaxes(w_in, 1, 2)
    bw = BRANCH_WIDTH
    r = REPACK_ROWS
    f_lo = 6 * bw
    c_lo = f_lo + N_HEADS
    g_lo = c_lo + 3 * bw
    main_rows = cols - N_HEADS
    n_gate = (cols - g_lo) // r
    n_ab = f_lo // r

    def src_row(l, i):
        row = jnp.where(i < n_gate, g_lo + r * i,
                        jnp.where(i < n_gate + n_ab, r * (i - n_gate), c_lo + r * (i - n_gate - n_ab)))
        return l, pl.multiple_of(row, N_HEADS), 0

    return pl.pallas_call(
        _repack_kernel,
        grid=(depth, main_rows // r),
        in_specs=[pl.BlockSpec((pl.Element(1), pl.Element(r), pl.Element(d)), src_row),
                  pl.BlockSpec((pl.Element(1), pl.Element(N_HEADS), pl.Element(d)), lambda l, i: (l, f_lo, 0))],
        out_specs=[pl.BlockSpec((None, r, d), lambda l, i: (l, i, 0)),
                   pl.BlockSpec((None, LANES, d), lambda l, i: (l, 0, 0))],
        out_shape=[jax.ShapeDtypeStruct((depth, main_rows, d), BF16),
                   jax.ShapeDtypeStruct((depth, LANES, d), BF16)],
        compiler_params=_cparams(("parallel", "arbitrary")),
        name="repack_w_in",
    )(w_t, w_t)


def kernel(x, norm_mix, w_in, b_forget, w_branch, w_out, norm_mlp, w_up, w_down, norm_final):
    b, s, d = x.shape
    depth = w_in.shape[0]
    n = b * s
    assert s % ATT_TILE == 0 and d % LANES == 0

    w_main, w_f = _prep_in_proj(w_in)
    wb = w_branch.astype(BF16)
    wo = w_out.astype(BF16)
    wu = w_up.astype(BF16)
    wd = w_down.astype(BF16)
    b_f = jnp.pad(b_forget, ((0, 0), (0, LANES - N_HEADS)))[:, None, :]

    tm = min(1024, s)
    xf = x.reshape(n, d)
    for l in range(depth):
        g_mix = norm_mix[l][None, :]
        proj, key_bias = _in_proj(xf, g_mix, w_main, w_f, b_f[l], l, s, tm, 1536, 256)
        proj3 = proj.reshape(b, s, -1)
        o_a = _sb_attention(proj3, d).reshape(n, -1)
        o_b = _flash_attention(proj3, key_bias.reshape(b, s, -1), d, False).reshape(n, -1)
        o_c = _flash_attention(proj3, None, d, True).reshape(n, -1)
        xf = _mix(xf, o_a, o_b, o_c, proj, wb, wo, l, min(512, n))
        xf = _mlp(xf, norm_mlp[l][None, :], wu, wd, norm_final[None, :],
                  l, l == depth - 1, min(1024, n), 1024)
    return xf.reshape(b, s, d)
```

```python
import functools
import math

import jax
import jax.numpy as jnp
import numpy as np
from jax import lax
from jax.experimental import pallas as pl
from jax.experimental.pallas import tpu as pltpu

F32 = jnp.float32
BF16 = jnp.bfloat16

HEAD_DIM = 64
N_HEADS = 8
BRANCH_WIDTH = N_HEADS * HEAD_DIM
N_BRANCH = 3
MOBA_BLOCK = 256
MOBA_TOPK = 3
RMS_EPS = 1e-6
NEG_INF = -1e30
LOG2E = math.log2(math.e)
LANES = 128
HEADS_PER_BLOCK = LANES // HEAD_DIM
N_HEAD_BLOCKS = BRANCH_WIDTH // LANES
N_BIAS_PIECES = 3
ATT_TILE = 256
Q_TILES_PER_STEP = 4
SB_EXIT = 160.0
FLASH_SKIP = 160.0
NORM_SLACK = 1.001
FLASH_BATCH = 4
V_ROWS = HEAD_DIM + 16
VMEM_LIMIT = 60000 * 1024


def _cparams(sem):
    return pltpu.CompilerParams(dimension_semantics=sem, vmem_limit_bytes=VMEM_LIMIT)


def _rms(x, g):
    ms = jnp.mean(x * x, axis=-1, keepdims=True)
    return x * lax.rsqrt(ms + RMS_EPS) * g


def _dot(a, b):
    return jnp.dot(a, b, preferred_element_type=F32)


def _dot_nt(a, b):
    return lax.dot_general(a, b, (((1,), (1,)), ((), ())), preferred_element_type=F32)


def _transposed(x):
    return x.astype(F32).T.astype(BF16)


def _split2(x):
    hi = x.astype(BF16)
    lo = (x - hi.astype(F32)).astype(BF16)
    return hi, lo


def _split3(x):
    hi = x.astype(BF16)
    r = x - hi.astype(F32)
    mid = r.astype(BF16)
    lo = (r - mid.astype(F32)).astype(BF16)
    return hi, mid, lo


def _bias_lane(hh):
    return (1 - hh) * HEAD_DIM


def _in_proj_kernel(x_ref, g_ref, w_ref, wf_ref, b_ref, tri_ref, place_ref,
                    o_ref, e_ref, h_ref, carry_ref, *, tiles_per_seq):
    @pl.when(pl.program_id(1) == 0)
    def _():
        h = _rms(x_ref[...], g_ref[...]).astype(BF16)
        h_ref[...] = h

        @pl.when(pl.program_id(0) % tiles_per_seq == 0)
        def _():
            carry_ref[...] = jnp.zeros_like(carry_ref)

        y = _dot_nt(h, wf_ref[...]) + b_ref[...]
        logf = jnp.minimum(y, 0.0) - jnp.log(1.0 + jnp.exp(-jnp.abs(y)))
        pieces = _split3(logf)
        tri = tri_ref[...]
        tc = tri.shape[0]
        cum = carry_ref[...]
        for c in range(h.shape[0] // tc):
            rows = slice(c * tc, (c + 1) * tc)
            cum = cum[-1:, :]
            for piece in pieces:
                cum = cum + _dot(tri, piece[rows])
            bias_pieces = jnp.concatenate(_split3(cum * (-LOG2E)), axis=1)
            e_ref[rows, :] = _dot(bias_pieces, place_ref[...]).astype(BF16)
        carry_ref[...] = cum[-1:, :]

    o_ref[...] = _dot_nt(h_ref[...], w_ref[...]).astype(o_ref.dtype)


def _bias_placement():
    place = np.zeros((N_BIAS_PIECES, LANES, N_HEAD_BLOCKS * LANES), np.float32)
    for h in range(N_HEADS):
        hp, hh = divmod(h, HEADS_PER_BLOCK)
        for i in range(N_BIAS_PIECES):
            place[i, h, hp * LANES + _bias_lane(hh) + i] = 1.0
    return jnp.asarray(place.reshape(N_BIAS_PIECES * LANES, -1), BF16)


def _in_proj(x, g, w, w_f, b_f, layer, s, tm, tn, tc):
    n, d = x.shape
    cols = w.shape[1]
    tri = jnp.asarray(np.tril(np.ones((tc, tc), np.float32)), BF16)
    place = _bias_placement()
    e_cols = place.shape[1]
    const2 = lambda i, j: (0, 0)
    return pl.pallas_call(
        functools.partial(_in_proj_kernel, tiles_per_seq=s // tm),
        grid=(n // tm, cols // tn),
        in_specs=[
            pl.BlockSpec((tm, d), lambda i, j: (i, 0)),
            pl.BlockSpec((1, d), const2),
            pl.BlockSpec((None, tn, d), lambda i, j: (layer, j, 0)),
            pl.BlockSpec((None, LANES, d), lambda i, j: (layer, 0, 0)),
            pl.BlockSpec((1, LANES), const2),
            pl.BlockSpec((tc, tc), const2),
            pl.BlockSpec(place.shape, const2),
        ],
        out_specs=[pl.BlockSpec((tm, tn), lambda i, j: (i, j)),
                   pl.BlockSpec((tm, e_cols), lambda i, j: (i, 0))],
        out_shape=[jax.ShapeDtypeStruct((n, cols), BF16),
                   jax.ShapeDtypeStruct((n, e_cols), BF16)],
        scratch_shapes=[pltpu.VMEM((tm, d), BF16), pltpu.VMEM((1, LANES), F32)],
        compiler_params=_cparams(("arbitrary", "arbitrary")),
        name="norm_in_proj",
    )(x, g, w, w_f, b_f, tri, place)


def _head_lane_mask(hh):
    lane = lax.broadcasted_iota(jnp.int32, (1, LANES), 1)
    return (lane >= hh * HEAD_DIM) & (lane < (hh + 1) * HEAD_DIM)


def _tile_rel(t):
    return (lax.broadcasted_iota(jnp.int32, (t, t), 0)
            - lax.broadcasted_iota(jnp.int32, (t, t), 1))


def _mixer_specs(mixer, nb, t, s, d):
    qoff = N_BRANCH * d // LANES + (3 * mixer) * N_HEAD_BLOCKS
    koff = qoff + N_HEAD_BLOCKS
    voff = koff + N_HEAD_BLOCKS
    return [
        pl.BlockSpec((nb, t, LANES), lambda bg, hp, qi: (bg, qi, qoff + hp)),
        pl.BlockSpec((nb, s, LANES), lambda bg, hp, qi: (bg, 0, koff + hp)),
        pl.BlockSpec((nb, s, LANES), lambda bg, hp, qi: (bg, 0, voff + hp)),
    ], pl.BlockSpec((nb, t, LANES), lambda bg, hp, qi: (bg, qi, hp))


def _sb_kernel(q_ref, k_ref, v_ref, o_ref, vt_ref, *scratch, nb, t):
    n_blocks = k_ref.shape[1] // t
    q_tiles = q_ref.shape[1] // t
    streams = [(bi, hh) for bi in range(nb) for hh in range(HEADS_PER_BLOCK)]

    @pl.when(pl.program_id(2) == 0)
    def _build():
        def chunk(c, carry):
            rows = pl.ds(pl.multiple_of(c * t, t), t)
            for st, (bi, hh) in enumerate(streams):
                v2t = v_ref[bi, rows, :].astype(F32).T
                vt_ref[st, :, rows] = v2t[hh * HEAD_DIM:(hh + 1) * HEAD_DIM, :].astype(BF16)
            return carry

        lax.fori_loop(0, n_blocks, chunk, 0)

    def q_tile(sub, carry):
        q_rows = pl.ds(pl.multiple_of(sub * t, t), t)
        _sb_q_tile(pl.program_id(2) * q_tiles + sub, q_rows, q_ref, k_ref, o_ref, vt_ref, *scratch, nb=nb, t=t)
        return carry

    lax.fori_loop(0, q_tiles, q_tile, 0)


def _sb_q_tile(qi, q_rows, q_ref, k_ref, o_ref, vt_ref, acc_ref, cs_ref, r_ref, s_even_ref, s_odd_ref, *, nb, t):
    streams = [(bi, hh) for bi in range(nb) for hh in range(HEADS_PER_BLOCK)]
    q_m = []
    for bi, hh in streams:
        q2 = q_ref[bi, q_rows, :]
        q_m.append(_transposed(jnp.where(_head_lane_mask(hh), q2, jnp.zeros_like(q2))))

    rel = _tile_rel(t)
    strictly_past = rel < 0
    upper = jnp.where(rel <= 0, 1.0, 0.0).astype(BF16)

    stream_groups = [range(g, g + HEADS_PER_BLOCK) for g in range(0, len(streams), HEADS_PER_BLOCK)]

    def scores(s_ref, j, group):
        rows = pl.ds(pl.multiple_of(j * t, t), t)
        for st in group:
            s_ref[st] = _dot(k_ref[streams[st][0], rows, :], q_m[st])

    def softplus_sums(s_ref, diag, group):
        for st in group:
            z = s_ref[st]
            sp = jnp.maximum(z, 0.0) + jnp.log2(1.0 + jnp.exp2(-jnp.abs(z)))
            if diag:
                sp = jnp.where(strictly_past, sp, 0.0)
            r = _dot(upper, sp.astype(BF16)) + cs_ref[st]
            r_ref[st] = r
            cs_ref[st] = r[0:1, :]

    def weigh_values(s_ref, j, diag, group):
        rows = pl.ds(pl.multiple_of(j * t, t), t)
        for st in group:
            w = jnp.exp2(s_ref[st] - r_ref[st])
            if diag:
                w = jnp.where(strictly_past, w, 0.0)
            acc_ref[st] += _dot(vt_ref[st, :, rows], w.astype(BF16))

    def step(cur_ref, next_ref, j, j_next, diag):
        for group in stream_groups:
            scores(next_ref, j_next, group)
            softplus_sums(cur_ref, diag, group)
        for group in stream_groups:
            weigh_values(cur_ref, j, diag, group)

    def next_distance(i):
        lowest = cs_ref[0]
        for st in range(1, len(streams)):
            lowest = jnp.minimum(lowest, cs_ref[st])
        return jnp.where(jnp.min(lowest) > SB_EXIT, qi + 1, i + 1)

    acc_ref[...] = jnp.zeros(acc_ref.shape, F32)
    cs_ref[...] = jnp.zeros(cs_ref.shape, F32)
    for group in stream_groups:
        scores(s_even_ref, qi, group)
    step(s_even_ref, s_odd_ref, qi, jnp.maximum(qi - 1, 0), True)

    def body(i):
        j = qi - i
        j_next = jnp.maximum(j - 1, 0)

        @pl.when(i % 2 == 1)
        def _():
            step(s_odd_ref, s_even_ref, j, j_next, False)

        @pl.when(i % 2 == 0)
        def _():
            step(s_even_ref, s_odd_ref, j, j_next, False)

        return next_distance(i)

    lax.while_loop(lambda i: i <= qi, body, next_distance(0))

    for bi in range(nb):
        halves = [acc_ref[bi * HEADS_PER_BLOCK + hh] for hh in range(HEADS_PER_BLOCK)]
        o_ref[bi, q_rows, :] = jnp.concatenate(halves, axis=0).T.astype(o_ref.dtype)


def _sb_attention(proj3, d):
    b, s, _ = proj3.shape
    t = ATT_TILE
    nb = FLASH_BATCH if b % FLASH_BATCH == 0 else 1
    n_streams = nb * HEADS_PER_BLOCK
    q_rows = min(Q_TILES_PER_STEP * t, s)
    in_specs, out_spec = _mixer_specs(0, nb, q_rows, s, d)
    return pl.pallas_call(
        functools.partial(_sb_kernel, nb=nb, t=t),
        grid=(b // nb, N_HEAD_BLOCKS, s // q_rows),
        in_specs=in_specs,
        out_specs=out_spec,
        out_shape=jax.ShapeDtypeStruct((b, s, BRANCH_WIDTH), BF16),
        scratch_shapes=[pltpu.VMEM((n_streams, HEAD_DIM, s), BF16),
                        pltpu.VMEM((n_streams, HEAD_DIM, t), F32),
                        pltpu.VMEM((n_streams, 1, t), F32),
                        pltpu.VMEM((n_streams, t, t), F32),
                        pltpu.VMEM((n_streams, t, t), F32),
                        pltpu.VMEM((n_streams, t, t), F32)],
        compiler_params=_cparams(("parallel", "parallel", "arbitrary")),
        name="stickbreak_attn",
    )(proj3, proj3, proj3)


def _flash_kernel(*refs, moba, nb, t):
    q_tiles = refs[0].shape[1] // t

    def q_tile(sub, carry):
        q_rows = pl.ds(pl.multiple_of(sub * t, t), t)
        first = (pl.program_id(2) == 0) & (sub == 0)
        _flash_q_tile(pl.program_id(2) * q_tiles + sub, q_rows, first, *refs, moba=moba, nb=nb, t=t)
        return carry

    lax.fori_loop(0, q_tiles, q_tile, 0)


def _flash_q_tile(qi, q_rows, first, *refs, moba, nb, t):
    if moba:
        (q_ref, k_ref, v_ref, o_ref, kaug_ref, vt_ref, acc_ref, m_ref, s0_ref, s1_ref, s2_ref, s3_ref,
         bmax_ref, knorm_ref, kmean_ref, sel_ref) = refs
    else:
        (q_ref, k_ref, v_ref, e_ref, o_ref, kaug_ref, vt_ref, acc_ref, m_ref, s0_ref, s1_ref, s2_ref,
         s3_ref, bmax_ref, knorm_ref) = refs
    s_bufs = (s0_ref, s1_ref, s2_ref, s3_ref)
    n_blocks = k_ref.shape[1] // t
    hp = pl.program_id(1)
    lane = lax.broadcasted_iota(jnp.int32, (1, LANES), 1)
    streams = [(bi, hh) for bi in range(nb) for hh in range(HEADS_PER_BLOCK)]

    def piece_lanes(hh):
        e0 = _bias_lane(hh)
        return (lane >= e0) & (lane < e0 + N_BIAS_PIECES)

    head_of_lane = jnp.where(lax.broadcasted_iota(jnp.int32, (LANES, LANES), 0) >= HEAD_DIM, 1, 0)
    head_sum = jnp.where(head_of_lane == lax.broadcasted_iota(jnp.int32, (LANES, LANES), 1), 1.0, 0.0).astype(BF16)

    def head_sq_norm_max(x32):
        hi, lo = _split2(x32 * x32)
        return jnp.max(_dot(hi, head_sum) + _dot(lo, head_sum), axis=0, keepdims=True)

    def head_col(row, hh):
        return jnp.sum(jnp.where(lane == hh, row, 0.0), axis=1, keepdims=True)

    @pl.when(first)
    def _build():
        bmax_ref[...] = jnp.full(bmax_ref.shape, NEG_INF, F32)
        knorm_ref[...] = jnp.zeros(knorm_ref.shape, F32)
        tail = jnp.where(lax.broadcasted_iota(jnp.int32, (V_ROWS - HEAD_DIM, t), 0) == 0, 1.0, 0.0)

        def chunk(c, carry):
            start = pl.multiple_of(c * t, t)
            rows = pl.ds(start, t)
            extras = []
            if moba:
                pos = (lax.broadcasted_iota(jnp.int32, (t, LANES), 0) + start).astype(F32)
                for hh in range(HEADS_PER_BLOCK):
                    h = hp * HEADS_PER_BLOCK + hh
                    slope = pltpu.bitcast(jnp.full((1, LANES), (126 - h) << 23, jnp.int32), F32)
                    extra = jnp.zeros((t, LANES), F32)
                    for i, piece in enumerate(_split3(pos * slope * LOG2E)):
                        extra = jnp.where(lane == _bias_lane(hh) + i, piece.astype(F32), extra)
                    extras.append(extra.astype(BF16))
            for bi in range(nb):
                k2 = k_ref[bi, rows, :]
                k32 = k2.astype(F32)
                v2t = v_ref[bi, rows, :].astype(F32).T
                if moba:
                    kmean_ref[bi, pl.ds(c, 1), :] = jnp.sum(k32, axis=0, keepdims=True) * (1.0 / t)
                for hh in range(HEADS_PER_BLOCK):
                    st = bi * HEADS_PER_BLOCK + hh
                    extra = extras[hh] if moba else e_ref[bi, rows, :]
                    kaug_ref[st, rows, :] = jnp.where(_head_lane_mask(hh), k2, extra)
                    v_h = v2t[hh * HEAD_DIM:(hh + 1) * HEAD_DIM, :]
                    vt_ref[st, :, rows] = jnp.concatenate([v_h, tail], axis=0).astype(BF16)
                    piece_max = jnp.max(extra.astype(F32), axis=0, keepdims=True)
                    bias_max = jnp.sum(jnp.where(piece_lanes(hh), piece_max, 0.0), axis=1, keepdims=True)
                    run = jnp.maximum(bmax_ref[st, pl.ds(jnp.maximum(c - 1, 0), 1), :], bias_max)
                    bmax_ref[st, pl.ds(c, 1), :] = run
                knorm_ref[bi] = jnp.maximum(knorm_ref[bi], head_sq_norm_max(k32))
            return carry

        lax.fori_loop(0, n_blocks, chunk, 0)

    q_aug = []
    reach = []
    for bi in range(nb):
        q2 = q_ref[bi, q_rows, :]
        qk_sq = head_sq_norm_max(q2.astype(F32)) * knorm_ref[bi]
        for hh in range(HEADS_PER_BLOCK):
            ones = jnp.where(piece_lanes(hh), 1.0, 0.0).astype(BF16)
            q_aug.append(_transposed(jnp.where(_head_lane_mask(hh), q2, jnp.broadcast_to(ones, q2.shape))))
            reach.append(jnp.sqrt(head_col(qk_sq, hh)) * NORM_SLACK)

    if moba:
        blk = lax.broadcasted_iota(jnp.int32, (n_blocks, 1), 0)
        blk_f = blk.astype(F32)
        for st, (bi, hh) in enumerate(streams):
            km = jnp.where(_head_lane_mask(hh), kmean_ref[bi], 0.0)
            pieces = _dot(jnp.concatenate(_split3(km), axis=0), q_aug[st])
            route = pieces[:n_blocks] + pieces[n_blocks:2 * n_blocks] + pieces[2 * n_blocks:]
            route = jnp.where(blk < qi, route, NEG_INF)
            sel = jnp.zeros((n_blocks, t), F32)
            for _ in range(MOBA_TOPK):
                top = jnp.max(route, axis=0, keepdims=True)
                first = jnp.min(jnp.where(route == top, blk_f, float(n_blocks)), axis=0, keepdims=True)
                hit = blk_f == first
                sel = jnp.where(hit, 1.0, sel)
                route = jnp.where(hit, -jnp.inf, route)
            sel_ref[st] = jnp.where(blk < qi, sel, 0.0)

    valid = _tile_rel(t) <= 0

    all_streams = range(len(streams))
    stream_groups = [range(g, g + HEADS_PER_BLOCK) for g in range(0, len(streams), HEADS_PER_BLOCK)]

    def scores(s_ref, j, group=all_streams):
        rows = pl.ds(pl.multiple_of(j * t, t), t)
        for st in group:
            s_ref[st] = _dot(kaug_ref[st, rows, :], q_aug[st])

    def accumulate(s_ref, j, diag, group=all_streams):
        rows = pl.ds(pl.multiple_of(j * t, t), t)
        for st in group:
            s_t = s_ref[st]
            if diag:
                s_t = jnp.where(valid, s_t, NEG_INF)
            m_old = m_ref[st]
            m_new = jnp.maximum(m_old, jnp.max(s_t, axis=0, keepdims=True))
            m_sub = m_new
            if moba and not diag:
                picked = sel_ref[st, pl.ds(j, 1), :] > 0.0
                m_new = jnp.where(picked, m_new, m_old)
                m_sub = jnp.where(picked, m_new, -NEG_INF)
            alpha = jnp.exp2(m_old - m_new)
            p = jnp.exp2(s_t - m_sub)
            acc_ref[st] = alpha * acc_ref[st] + _dot(vt_ref[st, :, rows], p.astype(BF16))
            m_ref[st] = m_new

    def tile_at(dist):
        return jnp.maximum(qi - dist, 0)

    m_ref[...] = jnp.full(m_ref.shape, NEG_INF, F32)
    acc_ref[...] = jnp.zeros(acc_ref.shape, F32)
    scores(s_bufs[0], qi)
    for group in stream_groups:
        scores(s_bufs[1], tile_at(1), group)
        accumulate(s_bufs[0], qi, True, group)
        scores(s_bufs[2], tile_at(2), group)

    blk_col = lax.broadcasted_iota(jnp.int32, (n_blocks, 1), 0)
    fewest_dead = None
    for st in range(len(streams)):
        thr = jnp.min(m_ref[st], axis=1, keepdims=True) - FLASH_SKIP - reach[st]
        dead = (bmax_ref[st] < thr) & (blk_col < qi)
        n_dead = jnp.sum(jnp.where(dead, 1.0, 0.0), axis=0, keepdims=True)
        fewest_dead = n_dead if fewest_dead is None else jnp.minimum(fewest_dead, n_dead)
    n_past = qi - jnp.max(fewest_dead).astype(jnp.int32)

    def pair(first, cur, nxt):
        for k in range(2):
            for group in stream_groups:
                scores(s_bufs[nxt[k]], tile_at(first + 2 + k), group)
                accumulate(s_bufs[cur[k]], qi - first - k, False, group)

    def body(p, carry):
        first = 2 * p + 1

        @pl.when(p % 2 == 0)
        def _():
            pair(first, (1, 2), (3, 0))

        @pl.when(p % 2 == 1)
        def _():
            pair(first, (3, 0), (1, 2))

        return carry

    lax.fori_loop(0, n_past // 2, body, 0)

    for slot in (1, 3):
        @pl.when(n_past % 4 == slot)
        def _(slot=slot):
            accumulate(s_bufs[slot], qi - n_past, False)

    for bi in range(nb):
        halves = []
        for hh in range(HEADS_PER_BLOCK):
            acc = acc_ref[bi * HEADS_PER_BLOCK + hh]
            halves.append(acc[:HEAD_DIM] * (1.0 / acc[HEAD_DIM:HEAD_DIM + 1]))
        o_ref[bi, q_rows, :] = jnp.concatenate(halves, axis=0).T.astype(o_ref.dtype)


def _flash_attention(proj3, key_bias3, d, moba):
    b, s, _ = proj3.shape
    t = ATT_TILE
    nb = FLASH_BATCH if b % FLASH_BATCH == 0 else 1
    n_blocks = s // t
    n_streams = nb * HEADS_PER_BLOCK
    q_rows = min(Q_TILES_PER_STEP * t, s)
    in_specs, out_spec = _mixer_specs(2 if moba else 1, nb, q_rows, s, d)
    operands = [proj3, proj3, proj3]
    scratch = [pltpu.VMEM((n_streams, s, LANES), BF16),
               pltpu.VMEM((n_streams, V_ROWS, s), BF16),
               pltpu.VMEM((n_streams, V_ROWS, t), F32),
               pltpu.VMEM((n_streams, 1, t), F32),
               pltpu.VMEM((n_streams, t, t), F32),
               pltpu.VMEM((n_streams, t, t), F32),
               pltpu.VMEM((n_streams, t, t), F32),
               pltpu.VMEM((n_streams, t, t), F32),
               pltpu.VMEM((n_streams, n_blocks, 1), F32),
               pltpu.VMEM((nb, 1, LANES), F32)]
    if moba:
        scratch += [pltpu.VMEM((nb, n_blocks, LANES), F32),
                    pltpu.VMEM((n_streams, n_blocks, t), F32)]
    else:
        in_specs.append(pl.BlockSpec((nb, s, LANES), lambda bg, hp, qi: (bg, 0, hp)))
        operands.append(key_bias3)
    return pl.pallas_call(
        functools.partial(_flash_kernel, moba=moba, nb=nb, t=t),
        grid=(b // nb, N_HEAD_BLOCKS, s // q_rows),
        in_specs=in_specs,
        out_specs=out_spec,
        out_shape=jax.ShapeDtypeStruct((b, s, BRANCH_WIDTH), BF16),
        scratch_shapes=scratch,
        compiler_params=_cparams(("parallel", "parallel", "arbitrary")),
        name="moba_attn" if moba else "forgetting_attn",
    )(*operands)


def _mix_kernel(x_ref, oa_ref, ob_ref, oc_ref, g_ref, wb_ref, wo_ref, y_ref):
    d = x_ref.shape[1]
    mixed = None
    for n, o_ref in enumerate((oa_ref, ob_ref, oc_ref)):
        gate = jax.nn.sigmoid(g_ref[:, n * d:(n + 1) * d].astype(F32))
        term = gate * _dot(o_ref[...], wb_ref[n])
        mixed = term if mixed is None else mixed + term
    y_ref[...] = x_ref[...] + _dot(mixed.astype(BF16), wo_ref[...])


def _mix(x, o_a, o_b, o_c, proj, wb, wo, layer, tm):
    n, d = x.shape
    o_spec = pl.BlockSpec((tm, BRANCH_WIDTH), lambda i: (i, 0))
    return pl.pallas_call(
        _mix_kernel,
        grid=(n // tm,),
        in_specs=[
            pl.BlockSpec((tm, d), lambda i: (i, 0)),
            o_spec, o_spec, o_spec,
            pl.BlockSpec((tm, N_BRANCH * d), lambda i: (i, 0)),
            pl.BlockSpec((None, N_BRANCH, BRANCH_WIDTH, d), lambda i: (layer, 0, 0, 0)),
            pl.BlockSpec((None, d, d), lambda i: (layer, 0, 0)),
        ],
        out_specs=pl.BlockSpec((tm, d), lambda i: (i, 0)),
        out_shape=jax.ShapeDtypeStruct((n, d), F32),
        compiler_params=_cparams(("parallel",)),
        name="gated_mix_out_proj",
    )(x, o_a, o_b, o_c, proj, wb, wo)


def _mlp_kernel(x_ref, g_ref, wu_ref, wd_ref, gf_ref, y_ref, h_ref, acc_ref, *, final_norm):
    f = pl.program_id(1)

    @pl.when(f == 0)
    def _():
        h_ref[...] = _rms(x_ref[...], g_ref[...]).astype(BF16)
        acc_ref[...] = jnp.zeros_like(acc_ref)

    hid = jnp.square(jnp.maximum(_dot(h_ref[...], wu_ref[...]), 0.0))
    acc_ref[...] += _dot(hid.astype(BF16), wd_ref[...])

    @pl.when(f == pl.num_programs(1) - 1)
    def _():
        y = x_ref[...] + acc_ref[...]
        if final_norm:
            y = _rms(y, gf_ref[...])
        y_ref[...] = y


def _mlp(x, g, wu, wd, g_final, layer, final_norm, tm, tf):
    n, d = x.shape
    d_ff = wu.shape[2]
    return pl.pallas_call(
        functools.partial(_mlp_kernel, final_norm=final_norm),
        grid=(n // tm, d_ff // tf),
        in_specs=[
            pl.BlockSpec((tm, d), lambda i, f: (i, 0)),
            pl.BlockSpec((1, d), lambda i, f: (0, 0)),
            pl.BlockSpec((None, d, tf), lambda i, f: (layer, 0, f)),
            pl.BlockSpec((None, tf, d), lambda i, f: (layer, f, 0)),
            pl.BlockSpec((1, d), lambda i, f: (0, 0)),
        ],
        out_specs=pl.BlockSpec((tm, d), lambda i, f: (i, 0)),
        out_shape=jax.ShapeDtypeStruct((n, d), F32),
        scratch_shapes=[pltpu.VMEM((tm, d), BF16), pltpu.VMEM((tm, d), F32)],
        compiler_params=_cparams(("parallel", "arbitrary")),
        name="relu2_mlp",
    )(x, g, wu, wd, g_final)


REPACK_ROWS = 512


def _repack_kernel(w_ref, wf_ref, main_ref, f_ref):
    i = pl.program_id(1)
    q_chunks = BRANCH_WIDTH // REPACK_ROWS
    n_gate_chunks = N_BRANCH * w_ref.shape[2] // REPACK_ROWS
    mixer_chunk = jnp.maximum(i - n_gate_chunks, 0) % (3 * q_chunks)
    is_q = (i >= n_gate_chunks) & (mixer_chunk < q_chunks)
    scale = jnp.where(is_q, HEAD_DIM ** -0.5 * LOG2E, 1.0)
    main_ref[...] = (w_ref[0] * scale).astype(BF16)

    @pl.when(i == 0)
    def _():
        pad = jnp.zeros((f_ref.shape[0] - N_HEADS, f_ref.shape[1]), F32)
        f_ref[...] = jnp.concatenate([wf_ref[0], pad], axis=0).astype(BF16)


def _prep_in_proj(w_in):
    depth, d, cols = w_in.shape
    w_t = jnp.swapaxes(w_in, 1, 2)
    bw = BRANCH_WIDTH
    r = REPACK_ROWS
    f_lo = 6 * bw
    c_lo = f_lo + N_HEADS
    g_lo = c_lo + 3 * bw
    main_rows = cols - N_HEADS
    n_gate = (cols - g_lo) // r
    n_ab = f_lo // r

    def src_row(l, i):
        row = jnp.where(i < n_gate, g_lo + r * i,
                        jnp.where(i < n_gate + n_ab, r * (i - n_gate), c_lo + r * (i - n_gate - n_ab)))
        return l, pl.multiple_of(row, N_HEADS), 0

    return pl.pallas_call(
        _repack_kernel,
        grid=(depth, main_rows // r),
        in_specs=[pl.BlockSpec((pl.Element(1), pl.Element(r), pl.Element(d)), src_row),
                  pl.BlockSpec((pl.Element(1), pl.Element(N_HEADS), pl.Element(d)), lambda l, i: (l, f_lo, 0))],
        out_specs=[pl.BlockSpec((None, r, d), lambda l, i: (l, i, 0)),
                   pl.BlockSpec((None, LANES, d), lambda l, i: (l, 0, 0))],
        out_shape=[jax.ShapeDtypeStruct((depth, main_rows, d), BF16),
                   jax.ShapeDtypeStruct((depth, LANES, d), BF16)],
        compiler_params=_cparams(("parallel", "arbitrary")),
        name="repack_w_in",
    )(w_t, w_t)


def kernel(x, norm_mix, w_in, b_forget, w_branch, w_out, norm_mlp, w_up, w_down, norm_final):
    b, s, d = x.shape
    depth = w_in.shape[0]
    n = b * s
    assert s % ATT_TILE == 0 and d % LANES == 0

    w_main, w_f = _prep_in_proj(w_in)
    wb = w_branch.astype(BF16)
    wo = w_out.astype(BF16)
    wu = w_up.astype(BF16)
    wd = w_down.astype(BF16)
    b_f = jnp.pad(b_forget, ((0, 0), (0, LANES - N_HEADS)))[:, None, :]

    tm = min(1024, s)
    xf = x.reshape(n, d)
    for l in range(depth):
        g_mix = norm_mix[l][None, :]
        proj, key_bias = _in_proj(xf, g_mix, w_main, w_f, b_f[l], l, s, tm, 1536, 256)
        proj3 = proj.reshape(b, s, -1)
        o_a = _sb_attention(proj3, d).reshape(n, -1)
        o_b = _flash_attention(proj3, key_bias.reshape(b, s, -1), d, False).reshape(n, -1)
        o_c = _flash_attention(proj3, None, d, True).reshape(n, -1)
        xf = _mix(xf, o_a, o_b, o_c, proj, wb, wo, l, min(512, n))
        xf = _mlp(xf, norm_mlp[l][None, :], wu, wd, norm_final[None, :],
                  l, l == depth - 1, min(1024, n), 1024)
    return xf.reshape(b, s, d)
```

```python
import functools
import math

import jax
import jax.numpy as jnp
import numpy as np
from jax import lax
from jax.experimental import pallas as pl
from jax.experimental.pallas import tpu as pltpu

F32 = jnp.float32
BF16 = jnp.bfloat16

HEAD_DIM = 64
N_HEADS = 8
BRANCH_WIDTH = N_HEADS * HEAD_DIM
N_BRANCH = 3
MOBA_BLOCK = 256
MOBA_TOPK = 3
RMS_EPS = 1e-6
NEG_INF = -1e30
LOG2E = math.log2(math.e)
LANES = 128
HEADS_PER_BLOCK = LANES // HEAD_DIM
N_HEAD_BLOCKS = BRANCH_WIDTH // LANES
N_BIAS_PIECES = 3
ATT_TILE = 256
Q_TILES_PER_STEP = 4
SB_EXIT = 160.0
FLASH_SKIP = 160.0
NORM_SLACK = 1.001
FLASH_BATCH = 4
V_ROWS = HEAD_DIM + 16
VMEM_LIMIT = 60000 * 1024


def _cparams(sem):
    return pltpu.CompilerParams(dimension_semantics=sem, vmem_limit_bytes=VMEM_LIMIT)


def _rms(x, g):
    ms = jnp.mean(x * x, axis=-1, keepdims=True)
    return x * lax.rsqrt(ms + RMS_EPS) * g


def _dot(a, b):
    return jnp.dot(a, b, preferred_element_type=F32)


def _dot_nt(a, b):
    return lax.dot_general(a, b, (((1,), (1,)), ((), ())), preferred_element_type=F32)


def _transposed(x):
    return x.astype(F32).T.astype(BF16)


def _split2(x):
    hi = x.astype(BF16)
    lo = (x - hi.astype(F32)).astype(BF16)
    return hi, lo


def _split3(x):
    hi = x.astype(BF16)
    r = x - hi.astype(F32)
    mid = r.astype(BF16)
    lo = (r - mid.astype(F32)).astype(BF16)
    return hi, mid, lo


def _bias_lane(hh):
    return (1 - hh) * HEAD_DIM


def _in_proj_kernel(x_ref, g_ref, w_ref, wf_ref, b_ref, tri_ref, place_ref,
                    o_ref, e_ref, h_ref, carry_ref, *, tiles_per_seq):
    @pl.when(pl.program_id(1) == 0)
    def _():
        h = _rms(x_ref[...], g_ref[...]).astype(BF16)
        h_ref[...] = h

        @pl.when(pl.program_id(0) % tiles_per_seq == 0)
        def _():
            carry_ref[...] = jnp.zeros_like(carry_ref)

        y = _dot_nt(h, wf_ref[...]) + b_ref[...]
        logf = jnp.minimum(y, 0.0) - jnp.log(1.0 + jnp.exp(-jnp.abs(y)))
        pieces = _split3(logf)
        tri = tri_ref[...]
        tc = tri.shape[0]
        cum = carry_ref[...]
        for c in range(h.shape[0] // tc):
            rows = slice(c * tc, (c + 1) * tc)
            cum = cum[-1:, :]
            for piece in pieces:
                cum = cum + _dot(tri, piece[rows])
            bias_pieces = jnp.concatenate(_split3(cum * (-LOG2E)), axis=1)
            e_ref[rows, :] = _dot(bias_pieces, place_ref[...]).astype(BF16)
        carry_ref[...] = cum[-1:, :]

    o_ref[...] = _dot_nt(h_ref[...], w_ref[...]).astype(o_ref.dtype)


def _bias_placement():
    place = np.zeros((N_BIAS_PIECES, LANES, N_HEAD_BLOCKS * LANES), np.float32)
    for h in range(N_HEADS):
        hp, hh = divmod(h, HEADS_PER_BLOCK)
        for i in range(N_BIAS_PIECES):
            place[i, h, hp * LANES + _bias_lane(hh) + i] = 1.0
    return jnp.asarray(place.reshape(N_BIAS_PIECES * LANES, -1), BF16)


def _in_proj(x, g, w, w_f, b_f, layer, s, tm, tn, tc):
    n, d = x.shape
    cols = w.shape[1]
    tri = jnp.asarray(np.tril(np.ones((tc, tc), np.float32)), BF16)
    place = _bias_placement()
    e_cols = place.shape[1]
    const2 = lambda i, j: (0, 0)
    return pl.pallas_call(
        functools.partial(_in_proj_kernel, tiles_per_seq=s // tm),
        grid=(n // tm, cols // tn),
        in_specs=[
            pl.BlockSpec((tm, d), lambda i, j: (i, 0)),
            pl.BlockSpec((1, d), const2),
            pl.BlockSpec((None, tn, d), lambda i, j: (layer, j, 0)),
            pl.BlockSpec((None, LANES, d), lambda i, j: (layer, 0, 0)),
            pl.BlockSpec((1, LANES), const2),
            pl.BlockSpec((tc, tc), const2),
            pl.BlockSpec(place.shape, const2),
        ],
        out_specs=[pl.BlockSpec((tm, tn), lambda i, j: (i, j)),
                   pl.BlockSpec((tm, e_cols), lambda i, j: (i, 0))],
        out_shape=[jax.ShapeDtypeStruct((n, cols), BF16),
                   jax.ShapeDtypeStruct((n, e_cols), BF16)],
        scratch_shapes=[pltpu.VMEM((tm, d), BF16), pltpu.VMEM((1, LANES), F32)],
        compiler_params=_cparams(("arbitrary", "arbitrary")),
        name="norm_in_proj",
    )(x, g, w, w_f, b_f, tri, place)


def _head_lane_mask(hh):
    lane = lax.broadcasted_iota(jnp.int32, (1, LANES), 1)
    return (lane >= hh * HEAD_DIM) & (lane < (hh + 1) * HEAD_DIM)


def _tile_rel(t):
    return (lax.broadcasted_iota(jnp.int32, (t, t), 0)
            - lax.broadcasted_iota(jnp.int32, (t, t), 1))


def _mixer_specs(mixer, nb, t, s, d):
    qoff = N_BRANCH * d // LANES + (3 * mixer) * N_HEAD_BLOCKS
    koff = qoff + N_HEAD_BLOCKS
    voff = koff + N_HEAD_BLOCKS
    return [
        pl.BlockSpec((nb, t, LANES), lambda bg, hp, qi: (bg, qi, qoff + hp)),
        pl.BlockSpec((nb, s, LANES), lambda bg, hp, qi: (bg, 0, koff + hp)),
        pl.BlockSpec((nb, s, LANES), lambda bg, hp, qi: (bg, 0, voff + hp)),
    ], pl.BlockSpec((nb, t, LANES), lambda bg, hp, qi: (bg, qi, hp))


def _sb_kernel(q_ref, k_ref, v_ref, o_ref, vt_ref, *scratch, nb, t):
    n_blocks = k_ref.shape[1] // t
    q_tiles = q_ref.shape[1] // t
    streams = [(bi, hh) for bi in range(nb) for hh in range(HEADS_PER_BLOCK)]

    @pl.when(pl.program_id(2) == 0)
    def _build():
        def chunk(c, carry):
            rows = pl.ds(pl.multiple_of(c * t, t), t)
            for st, (bi, hh) in enumerate(streams):
                v2t = v_ref[bi, rows, :].astype(F32).T
                vt_ref[st, :, rows] = v2t[hh * HEAD_DIM:(hh + 1) * HEAD_DIM, :].astype(BF16)
            return carry

        lax.fori_loop(0, n_blocks, chunk, 0)

    def q_tile(sub, carry):
        q_rows = pl.ds(pl.multiple_of(sub * t, t), t)
        _sb_q_tile(pl.program_id(2) * q_tiles + sub, q_rows, q_ref, k_ref, o_ref, vt_ref, *scratch, nb=nb, t=t)
        return carry

    lax.fori_loop(0, q_tiles, q_tile, 0)


def _sb_q_tile(qi, q_rows, q_ref, k_ref, o_ref, vt_ref, acc_ref, cs_ref, r_ref, s_even_ref, s_odd_ref, *, nb, t):
    streams = [(bi, hh) for bi in range(nb) for hh in range(HEADS_PER_BLOCK)]
    q_m = []
    for bi, hh in streams:
        q2 = q_ref[bi, q_rows, :]
        q_m.append(_transposed(jnp.where(_head_lane_mask(hh), q2, jnp.zeros_like(q2))))

    rel = _tile_rel(t)
    strictly_past = rel < 0
    upper = jnp.where(rel <= 0, 1.0, 0.0).astype(BF16)

    stream_groups = [range(g, g + HEADS_PER_BLOCK) for g in range(0, len(streams), HEADS_PER_BLOCK)]

    def scores(s_ref, j, group):
        rows = pl.ds(pl.multiple_of(j * t, t), t)
        for st in group:
            s_ref[st] = _dot(k_ref[streams[st][0], rows, :], q_m[st])

    def softplus_sums(s_ref, diag, group):
        for st in group:
            z = s_ref[st]
            sp = jnp.maximum(z, 0.0) + jnp.log2(1.0 + jnp.exp2(-jnp.abs(z)))
            if diag:
                sp = jnp.where(strictly_past, sp, 0.0)
            r = _dot(upper, sp.astype(BF16)) + cs_ref[st]
            r_ref[st] = r
            cs_ref[st] = r[0:1, :]

    def weigh_values(s_ref, j, diag, group):
        rows = pl.ds(pl.multiple_of(j * t, t), t)
        for st in group:
            w = jnp.exp2(s_ref[st] - r_ref[st])
            if diag:
                w = jnp.where(strictly_past, w, 0.0)
            acc_ref[st] += _dot(vt_ref[st, :, rows], w.astype(BF16))

    def step(cur_ref, next_ref, j, j_next, diag):
        for group in stream_groups:
            scores(next_ref, j_next, group)
            softplus_sums(cur_ref, diag, group)
        for group in stream_groups:
            weigh_values(cur_ref, j, diag, group)

    def next_distance(i):
        lowest = cs_ref[0]
        for st in range(1, len(streams)):
            lowest = jnp.minimum(lowest, cs_ref[st])
        return jnp.where(jnp.min(lowest) > SB_EXIT, qi + 1, i + 1)

    acc_ref[...] = jnp.zeros(acc_ref.shape, F32)
    cs_ref[...] = jnp.zeros(cs_ref.shape, F32)
    for group in stream_groups:
        scores(s_even_ref, qi, group)
    step(s_even_ref, s_odd_ref, qi, jnp.maximum(qi - 1, 0), True)

    def body(i):
        j = qi - i
        j_next = jnp.maximum(j - 1, 0)

        @pl.when(i % 2 == 1)
        def _():
            step(s_odd_ref, s_even_ref, j, j_next, False)

        @pl.when(i % 2 == 0)
        def _():
            step(s_even_ref, s_odd_ref, j, j_next, False)

        return next_distance(i)

    lax.while_loop(lambda i: i <= qi, body, next_distance(0))

    for bi in range(nb):
        halves = [acc_ref[bi * HEADS_PER_BLOCK + hh] for hh in range(HEADS_PER_BLOCK)]
        o_ref[bi, q_rows, :] = jnp.concatenate(halves, axis=0).T.astype(o_ref.dtype)


def _sb_attention(proj3, d):
    b, s, _ = proj3.shape
    t = ATT_TILE
    nb = FLASH_BATCH if b % FLASH_BATCH == 0 else 1
    n_streams = nb * HEADS_PER_BLOCK
    q_rows = min(Q_TILES_PER_STEP * t, s)
    in_specs, out_spec = _mixer_specs(0, nb, q_rows, s, d)
    return pl.pallas_call(
        functools.partial(_sb_kernel, nb=nb, t=t),
        grid=(b // nb, N_HEAD_BLOCKS, s // q_rows),
        in_specs=in_specs,
        out_specs=out_spec,
        out_shape=jax.ShapeDtypeStruct((b, s, BRANCH_WIDTH), BF16),
        scratch_shapes=[pltpu.VMEM((n_streams, HEAD_DIM, s), BF16),
                        pltpu.VMEM((n_streams, HEAD_DIM, t), F32),
                        pltpu.VMEM((n_streams, 1, t), F32),
                        pltpu.VMEM((n_streams, t, t), F32),
                        pltpu.VMEM((n_streams, t, t), F32),
                        pltpu.VMEM((n_streams, t, t), F32)],
        compiler_params=_cparams(("parallel", "parallel", "arbitrary")),
        name="stickbreak_attn",
    )(proj3, proj3, proj3)


def _flash_kernel(*refs, moba, nb, t):
    q_tiles = refs[0].shape[1] // t

    def q_tile(sub, carry):
        q_rows = pl.ds(pl.multiple_of(sub * t, t), t)
        first = (pl.program_id(2) == 0) & (sub == 0)
        _flash_q_tile(pl.program_id(2) * q_tiles + sub, q_rows, first, *refs, moba=moba, nb=nb, t=t)
        return carry

    lax.fori_loop(0, q_tiles, q_tile, 0)


def _flash_q_tile(qi, q_rows, first, *refs, moba, nb, t):
    if moba:
        (q_ref, k_ref, v_ref, o_ref, kaug_ref, vt_ref, acc_ref, m_ref, s0_ref, s1_ref, s2_ref, s3_ref,
         bmax_ref, knorm_ref, kmean_ref, sel_ref) = refs
    else:
        (q_ref, k_ref, v_ref, e_ref, o_ref, kaug_ref, vt_ref, acc_ref, m_ref, s0_ref, s1_ref, s2_ref,
         s3_ref, bmax_ref, knorm_ref) = refs
    s_bufs = (s0_ref, s1_ref, s2_ref, s3_ref)
    n_blocks = k_ref.shape[1] // t
    hp = pl.program_id(1)
    lane = lax.broadcasted_iota(jnp.int32, (1, LANES), 1)
    streams = [(bi, hh) for bi in range(nb) for hh in range(HEADS_PER_BLOCK)]

    def piece_lanes(hh):
        e0 = _bias_lane(hh)
        return (lane >= e0) & (lane < e0 + N_BIAS_PIECES)

    head_of_lane = jnp.where(lax.broadcasted_iota(jnp.int32, (LANES, LANES), 0) >= HEAD_DIM, 1, 0)
    head_sum = jnp.where(head_of_lane == lax.broadcasted_iota(jnp.int32, (LANES, LANES), 1), 1.0, 0.0).astype(BF16)

    def head_sq_norm_max(x32):
        hi, lo = _split2(x32 * x32)
        return jnp.max(_dot(hi, head_sum) + _dot(lo, head_sum), axis=0, keepdims=True)

    def head_col(row, hh):
        return jnp.sum(jnp.where(lane == hh, row, 0.0), axis=1, keepdims=True)

    @pl.when(first)
    def _build():
        bmax_ref[...] = jnp.full(bmax_ref.shape, NEG_INF, F32)
        knorm_ref[...] = jnp.zeros(knorm_ref.shape, F32)
        tail = jnp.where(lax.broadcasted_iota(jnp.int32, (V_ROWS - HEAD_DIM, t), 0) == 0, 1.0, 0.0)

        def chunk(c, carry):
            start = pl.multiple_of(c * t, t)
            rows = pl.ds(start, t)
            extras = []
            if moba:
                pos = (lax.broadcasted_iota(jnp.int32, (t, LANES), 0) + start).astype(F32)
                for hh in range(HEADS_PER_BLOCK):
                    h = hp * HEADS_PER_BLOCK + hh
                    slope = pltpu.bitcast(jnp.full((1, LANES), (126 - h) << 23, jnp.int32), F32)
                    extra = jnp.zeros((t, LANES), F32)
                    for i, piece in enumerate(_split3(pos * slope * LOG2E)):
                        extra = jnp.where(lane == _bias_lane(hh) + i, piece.astype(F32), extra)
                    extras.append(extra.astype(BF16))
            for bi in range(nb):
                k2 = k_ref[bi, rows, :]
                k32 = k2.astype(F32)
                v2t = v_ref[bi, rows, :].astype(F32).T
                if moba:
                    kmean_ref[bi, pl.ds(c, 1), :] = jnp.sum(k32, axis=0, keepdims=True) * (1.0 / t)
                for hh in range(HEADS_PER_BLOCK):
                    st = bi * HEADS_PER_BLOCK + hh
                    extra = extras[hh] if moba else e_ref[bi, rows, :]
                    kaug_ref[st, rows, :] = jnp.where(_head_lane_mask(hh), k2, extra)
                    v_h = v2t[hh * HEAD_DIM:(hh + 1) * HEAD_DIM, :]
                    vt_ref[st, :, rows] = jnp.concatenate([v_h, tail], axis=0).astype(BF16)
                    piece_max = jnp.max(extra.astype(F32), axis=0, keepdims=True)
                    bias_max = jnp.sum(jnp.where(piece_lanes(hh), piece_max, 0.0), axis=1, keepdims=True)
                    run = jnp.maximum(bmax_ref[st, pl.ds(jnp.maximum(c - 1, 0), 1), :], bias_max)
                    bmax_ref[st, pl.ds(c, 1), :] = run
                knorm_ref[bi] = jnp.maximum(knorm_ref[bi], head_sq_norm_max(k32))
            return carry

        lax.fori_loop(0, n_blocks, chunk, 0)

    q_aug = []
    reach = []
    for bi in range(nb):
        q2 = q_ref[bi, q_rows, :]
        q_sq = q2.astype(F32) * q2.astype(F32)
        for hh in range(HEADS_PER_BLOCK):
            ones = jnp.where(piece_lanes(hh), 1.0, 0.0).astype(BF16)
            q_aug.append(_transposed(jnp.where(_head_lane_mask(hh), q2, jnp.broadcast_to(ones, q2.shape))))
            q_norm_sq = jnp.max(jnp.sum(jnp.where(_head_lane_mask(hh), q_sq, 0.0), axis=1, keepdims=True),
                                axis=0, keepdims=True)
            reach.append(jnp.sqrt(q_norm_sq * head_col(knorm_ref[bi], hh)) * NORM_SLACK)

    if moba:
        blk = lax.broadcasted_iota(jnp.int32, (n_blocks, 1), 0)
        blk_f = blk.astype(F32)
        for st, (bi, hh) in enumerate(streams):
            km = jnp.where(_head_lane_mask(hh), kmean_ref[bi], 0.0)
            pieces = _dot(jnp.concatenate(_split3(km), axis=0), q_aug[st])
            route = pieces[:n_blocks] + pieces[n_blocks:2 * n_blocks] + pieces[2 * n_blocks:]
            route = jnp.where(blk < qi, route, NEG_INF)
            sel = jnp.zeros((n_blocks, t), F32)
            for _ in range(MOBA_TOPK):
                top = jnp.max(route, axis=0, keepdims=True)
                first = jnp.min(jnp.where(route == top, blk_f, float(n_blocks)), axis=0, keepdims=True)
                hit = blk_f == first
                sel = jnp.where(hit, 1.0, sel)
                route = jnp.where(hit, -jnp.inf, route)
            sel_ref[st] = jnp.where(blk < qi, sel, 0.0)

    valid = _tile_rel(t) <= 0

    all_streams = range(len(streams))
    stream_groups = [range(g, g + HEADS_PER_BLOCK) for g in range(0, len(streams), HEADS_PER_BLOCK)]

    def scores(s_ref, j, group=all_streams):
        rows = pl.ds(pl.multiple_of(j * t, t), t)
        for st in group:
            s_ref[st] = _dot(kaug_ref[st, rows, :], q_aug[st])

    def accumulate(s_ref, j, diag, group=all_streams):
        rows = pl.ds(pl.multiple_of(j * t, t), t)
        for st in group:
            s_t = s_ref[st]
            if diag:
                s_t = jnp.where(valid, s_t, NEG_INF)
            m_old = m_ref[st]
            m_new = jnp.maximum(m_old, jnp.max(s_t, axis=0, keepdims=True))
            m_sub = m_new
            if moba and not diag:
                picked = sel_ref[st, pl.ds(j, 1), :] > 0.0
                m_new = jnp.where(picked, m_new, m_old)
                m_sub = jnp.where(picked, m_new, -NEG_INF)
            alpha = jnp.exp2(m_old - m_new)
            p = jnp.exp2(s_t - m_sub)
            acc_ref[st] = alpha * acc_ref[st] + _dot(vt_ref[st, :, rows], p.astype(BF16))
            m_ref[st] = m_new

    def tile_at(dist):
        return jnp.maximum(qi - dist, 0)

    m_ref[...] = jnp.full(m_ref.shape, NEG_INF, F32)
    acc_ref[...] = jnp.zeros(acc_ref.shape, F32)
    scores(s_bufs[0], qi)
    for group in stream_groups:
        scores(s_bufs[1], tile_at(1), group)
        accumulate(s_bufs[0], qi, True, group)
        scores(s_bufs[2], tile_at(2), group)

    blk_col = lax.broadcasted_iota(jnp.int32, (n_blocks, 1), 0)
    fewest_dead = None
    for st in range(len(streams)):
        thr = jnp.min(m_ref[st], axis=1, keepdims=True) - FLASH_SKIP - reach[st]
        dead = (bmax_ref[st] < thr) & (blk_col < qi)
        n_dead = jnp.sum(jnp.where(dead, 1.0, 0.0), axis=0, keepdims=True)
        fewest_dead = n_dead if fewest_dead is None else jnp.minimum(fewest_dead, n_dead)
    n_past = qi - jnp.max(fewest_dead).astype(jnp.int32)

    def pair(first, cur, nxt):
        for k in range(2):
            for group in stream_groups:
                scores(s_bufs[nxt[k]], tile_at(first + 2 + k), group)
                accumulate(s_bufs[cur[k]], qi - first - k, False, group)

    def body(p, carry):
        first = 2 * p + 1

        @pl.when(p % 2 == 0)
        def _():
            pair(first, (1, 2), (3, 0))

        @pl.when(p % 2 == 1)
        def _():
            pair(first, (3, 0), (1, 2))

        return carry

    lax.fori_loop(0, n_past // 2, body, 0)

    for slot in (1, 3):
        @pl.when(n_past % 4 == slot)
        def _(slot=slot):
            accumulate(s_bufs[slot], qi - n_past, False)

    for bi in range(nb):
        halves = []
        for hh in range(HEADS_PER_BLOCK):
            acc = acc_ref[bi * HEADS_PER_BLOCK + hh]
            halves.append(acc[:HEAD_DIM] * (1.0 / acc[HEAD_DIM:HEAD_DIM + 1]))
        o_ref[bi, q_rows, :] = jnp.concatenate(halves, axis=0).T.astype(o_ref.dtype)


def _flash_attention(proj3, key_bias3, d, moba):
    b, s, _ = proj3.shape
    t = ATT_TILE
    nb = FLASH_BATCH if b % FLASH_BATCH == 0 else 1
    n_blocks = s // t
    n_streams = nb * HEADS_PER_BLOCK
    q_rows = min(Q_TILES_PER_STEP * t, s)
    in_specs, out_spec = _mixer_specs(2 if moba else 1, nb, q_rows, s, d)
    operands = [proj3, proj3, proj3]
    scratch = [pltpu.VMEM((n_streams, s, LANES), BF16),
               pltpu.VMEM((n_streams, V_ROWS, s), BF16),
               pltpu.VMEM((n_streams, V_ROWS, t), F32),
               pltpu.VMEM((n_streams, 1, t), F32),
               pltpu.VMEM((n_streams, t, t), F32),
               pltpu.VMEM((n_streams, t, t), F32),
               pltpu.VMEM((n_streams, t, t), F32),
               pltpu.VMEM((n_streams, t, t), F32),
               pltpu.VMEM((n_streams, n_blocks, 1), F32),
               pltpu.VMEM((nb, 1, LANES), F32)]
    if moba:
        scratch += [pltpu.VMEM((nb, n_blocks, LANES), F32),
                    pltpu.VMEM((n_streams, n_blocks, t), F32)]
    else:
        in_specs.append(pl.BlockSpec((nb, s, LANES), lambda bg, hp, qi: (bg, 0, hp)))
        operands.append(key_bias3)
    return pl.pallas_call(
        functools.partial(_flash_kernel, moba=moba, nb=nb, t=t),
        grid=(b // nb, N_HEAD_BLOCKS, s // q_rows),
        in_specs=in_specs,
        out_specs=out_spec,
        out_shape=jax.ShapeDtypeStruct((b, s, BRANCH_WIDTH), BF16),
        scratch_shapes=scratch,
        compiler_params=_cparams(("parallel", "parallel", "arbitrary")),
        name="moba_attn" if moba else "forgetting_attn",
    )(*operands)


def _mix_kernel(x_ref, oa_ref, ob_ref, oc_ref, g_ref, wb_ref, wo_ref, y_ref):
    d = x_ref.shape[1]
    mixed = None
    for n, o_ref in enumerate((oa_ref, ob_ref, oc_ref)):
        gate = jax.nn.sigmoid(g_ref[:, n * d:(n + 1) * d].astype(F32))
        term = gate * _dot(o_ref[...], wb_ref[n])
        mixed = term if mixed is None else mixed + term
    y_ref[...] = x_ref[...] + _dot(mixed.astype(BF16), wo_ref[...])


def _mix(x, o_a, o_b, o_c, proj, wb, wo, layer, tm):
    n, d = x.shape
    o_spec = pl.BlockSpec((tm, BRANCH_WIDTH), lambda i: (i, 0))
    return pl.pallas_call(
        _mix_kernel,
        grid=(n // tm,),
        in_specs=[
            pl.BlockSpec((tm, d), lambda i: (i, 0)),
            o_spec, o_spec, o_spec,
            pl.BlockSpec((tm, N_BRANCH * d), lambda i: (i, 0)),
            pl.BlockSpec((None, N_BRANCH, BRANCH_WIDTH, d), lambda i: (layer, 0, 0, 0)),
            pl.BlockSpec((None, d, d), lambda i: (layer, 0, 0)),
        ],
        out_specs=pl.BlockSpec((tm, d), lambda i: (i, 0)),
        out_shape=jax.ShapeDtypeStruct((n, d), F32),
        compiler_params=_cparams(("parallel",)),
        name="gated_mix_out_proj",
    )(x, o_a, o_b, o_c, proj, wb, wo)


def _mlp_kernel(x_ref, g_ref, wu_ref, wd_ref, gf_ref, y_ref, h_ref, acc_ref, *, final_norm):
    f = pl.program_id(1)

    @pl.when(f == 0)
    def _():
        h_ref[...] = _rms(x_ref[...], g_ref[...]).astype(BF16)
        acc_ref[...] = jnp.zeros_like(acc_ref)

    hid = jnp.square(jnp.maximum(_dot(h_ref[...], wu_ref[...]), 0.0))
    acc_ref[...] += _dot(hid.astype(BF16), wd_ref[...])

    @pl.when(f == pl.num_programs(1) - 1)
    def _():
        y = x_ref[...] + acc_ref[...]
        if final_norm:
            y = _rms(y, gf_ref[...])
        y_ref[...] = y


def _mlp(x, g, wu, wd, g_final, layer, final_norm, tm, tf):
    n, d = x.shape
    d_ff = wu.shape[2]
    return pl.pallas_call(
        functools.partial(_mlp_kernel, final_norm=final_norm),
        grid=(n // tm, d_ff // tf),
        in_specs=[
            pl.BlockSpec((tm, d), lambda i, f: (i, 0)),
            pl.BlockSpec((1, d), lambda i, f: (0, 0)),
            pl.BlockSpec((None, d, tf), lambda i, f: (layer, 0, f)),
            pl.BlockSpec((None, tf, d), lambda i, f: (layer, f, 0)),
            pl.BlockSpec((1, d), lambda i, f: (0, 0)),
        ],
        out_specs=pl.BlockSpec((tm, d), lambda i, f: (i, 0)),
        out_shape=jax.ShapeDtypeStruct((n, d), F32),
        scratch_shapes=[pltpu.VMEM((tm, d), BF16), pltpu.VMEM((tm, d), F32)],
        compiler_params=_cparams(("parallel", "arbitrary")),
        name="relu2_mlp",
    )(x, g, wu, wd, g_final)


REPACK_ROWS = 512


def _repack_kernel(w_ref, wf_ref, main_ref, f_ref):
    i = pl.program_id(1)
    q_chunks = BRANCH_WIDTH // REPACK_ROWS
    n_gate_chunks = N_BRANCH * w_ref.shape[2] // REPACK_ROWS
    mixer_chunk = jnp.maximum(i - n_gate_chunks, 0) % (3 * q_chunks)
    is_q = (i >= n_gate_chunks) & (mixer_chunk < q_chunks)
    scale = jnp.where(is_q, HEAD_DIM ** -0.5 * LOG2E, 1.0)
    main_ref[...] = (w_ref[0] * scale).astype(BF16)

    @pl.when(i == 0)
    def _():
        pad = jnp.zeros((f_ref.shape[0] - N_HEADS, f_ref.shape[1]), F32)
        f_ref[...] = jnp.concatenate([wf_ref[0], pad], axis=0).astype(BF16)


def _prep_in_proj(w_in):
    depth, d, cols = w_in.shape
    w_t = jnp.swapaxes(w_in, 1, 2)
    bw = BRANCH_WIDTH
    r = REPACK_ROWS
    f_lo = 6 * bw
    c_lo = f_lo + N_HEADS
    g_lo = c_lo + 3 * bw
    main_rows = cols - N_HEADS
    n_gate = (cols - g_lo) // r
    n_ab = f_lo // r

    def src_row(l, i):
        row = jnp.where(i < n_gate, g_lo + r * i,
                        jnp.where(i < n_gate + n_ab, r * (i - n_gate), c_lo + r * (i - n_gate - n_ab)))
        return l, pl.multiple_of(row, N_HEADS), 0

    return pl.pallas_call(
        _repack_kernel,
        grid=(depth, main_rows // r),
        in_specs=[pl.BlockSpec((pl.Element(1), pl.Element(r), pl.Element(d)), src_row),
                  pl.BlockSpec((pl.Element(1), pl.Element(N_HEADS), pl.Element(d)), lambda l, i: (l, f_lo, 0))],
        out_specs=[pl.BlockSpec((None, r, d), lambda l, i: (l, i, 0)),
                   pl.BlockSpec((None, LANES, d), lambda l, i: (l, 0, 0))],
        out_shape=[jax.ShapeDtypeStruct((depth, main_rows, d), BF16),
                   jax.ShapeDtypeStruct((depth, LANES, d), BF16)],
        compiler_params=_cparams(("parallel", "arbitrary")),
        name="repack_w_in",
    )(w_t, w_t)


def kernel(x, norm_mix, w_in, b_forget, w_branch, w_out, norm_mlp, w_up, w_down, norm_final):
    b, s, d = x.shape
    depth = w_in.shape[0]
    n = b * s
    assert s % ATT_TILE == 0 and d % LANES == 0

    w_main, w_f = _prep_in_proj(w_in)
    wb = w_branch.astype(BF16)
    wo = w_out.astype(BF16)
    wu = w_up.astype(BF16)
    wd = w_down.astype(BF16)
    b_f = jnp.pad(b_forget, ((0, 0), (0, LANES - N_HEADS)))[:, None, :]

    tm = min(1024, s)
    xf = x.reshape(n, d)
    for l in range(depth):
        g_mix = norm_mix[l][None, :]
        proj, key_bias = _in_proj(xf, g_mix, w_main, w_f, b_f[l], l, s, tm, 1536, 256)
        proj3 = proj.reshape(b, s, -1)
        o_a = _sb_attention(proj3, d).reshape(n, -1)
        o_b = _flash_attention(proj3, key_bias.reshape(b, s, -1), d, False).reshape(n, -1)
        o_c = _flash_attention(proj3, None, d, True).reshape(n, -1)
        xf = _mix(xf, o_a, o_b, o_c, proj, wb, wo, l, min(512, n))
        xf = _mlp(xf, norm_mlp[l][None, :], wu, wd, norm_final[None, :],
                  l, l == depth - 1, min(1024, n), 1024)
    return xf.reshape(b, s, d)
```

```python
import functools
import math

import jax
import jax.numpy as jnp
import numpy as np
from jax import lax
from jax.experimental import pallas as pl
from jax.experimental.pallas import tpu as pltpu

F32 = jnp.float32
BF16 = jnp.bfloat16

HEAD_DIM = 64
N_HEADS = 8
BRANCH_WIDTH = N_HEADS * HEAD_DIM
N_BRANCH = 3
MOBA_BLOCK = 256
MOBA_TOPK = 3
RMS_EPS = 1e-6
NEG_INF = -1e30
LOG2E = math.log2(math.e)
LANES = 128
HEADS_PER_BLOCK = LANES // HEAD_DIM
N_HEAD_BLOCKS = BRANCH_WIDTH // LANES
N_BIAS_PIECES = 3
ATT_TILE = 256
Q_TILES_PER_STEP = 4
SB_EXIT = 160.0
SP2_CLAMP = 120.0
FLASH_SKIP = 160.0
NORM_SLACK = 1.001
FLASH_BATCH = 4
V_ROWS = HEAD_DIM + 16
VMEM_LIMIT = 60000 * 1024


def _cparams(sem):
    return pltpu.CompilerParams(dimension_semantics=sem, vmem_limit_bytes=VMEM_LIMIT)


def _rms(x, g):
    ms = jnp.mean(x * x, axis=-1, keepdims=True)
    return x * lax.rsqrt(ms + RMS_EPS) * g


def _dot(a, b):
    return jnp.dot(a, b, preferred_element_type=F32)


def _dot_nt(a, b):
    return lax.dot_general(a, b, (((1,), (1,)), ((), ())), preferred_element_type=F32)


def _transposed(x):
    return x.astype(F32).T.astype(BF16)


def _split2(x):
    hi = x.astype(BF16)
    lo = (x - hi.astype(F32)).astype(BF16)
    return hi, lo


def _split3(x):
    hi = x.astype(BF16)
    r = x - hi.astype(F32)
    mid = r.astype(BF16)
    lo = (r - mid.astype(F32)).astype(BF16)
    return hi, mid, lo


def _bias_lane(hh):
    return (1 - hh) * HEAD_DIM


def _in_proj_kernel(x_ref, g_ref, w_ref, wf_ref, b_ref, tri_ref, place_ref,
                    o_ref, e_ref, h_ref, carry_ref, *, tiles_per_seq):
    @pl.when(pl.program_id(1) == 0)
    def _():
        h = _rms(x_ref[...], g_ref[...]).astype(BF16)
        h_ref[...] = h

        @pl.when(pl.program_id(0) % tiles_per_seq == 0)
        def _():
            carry_ref[...] = jnp.zeros_like(carry_ref)

        y = _dot_nt(h, wf_ref[...]) + b_ref[...]
        logf = jnp.minimum(y, 0.0) - jnp.log(1.0 + jnp.exp(-jnp.abs(y)))
        pieces = _split3(logf)
        tri = tri_ref[...]
        tc = tri.shape[0]
        cum = carry_ref[...]
        for c in range(h.shape[0] // tc):
            rows = slice(c * tc, (c + 1) * tc)
            cum = cum[-1:, :]
            for piece in pieces:
                cum = cum + _dot(tri, piece[rows])
            bias_pieces = jnp.concatenate(_split3(cum * (-LOG2E)), axis=1)
            e_ref[rows, :] = _dot(bias_pieces, place_ref[...]).astype(BF16)
        carry_ref[...] = cum[-1:, :]

    o_ref[...] = _dot_nt(h_ref[...], w_ref[...]).astype(o_ref.dtype)


def _bias_placement():
    place = np.zeros((N_BIAS_PIECES, LANES, N_HEAD_BLOCKS * LANES), np.float32)
    for h in range(N_HEADS):
        hp, hh = divmod(h, HEADS_PER_BLOCK)
        for i in range(N_BIAS_PIECES):
            place[i, h, hp * LANES + _bias_lane(hh) + i] = 1.0
    return jnp.asarray(place.reshape(N_BIAS_PIECES * LANES, -1), BF16)


def _in_proj(x, g, w, w_f, b_f, layer, s, tm, tn, tc):
    n, d = x.shape
    cols = w.shape[1]
    tri = jnp.asarray(np.tril(np.ones((tc, tc), np.float32)), BF16)
    place = _bias_placement()
    e_cols = place.shape[1]
    const2 = lambda i, j: (0, 0)
    return pl.pallas_call(
        functools.partial(_in_proj_kernel, tiles_per_seq=s // tm),
        grid=(n // tm, cols // tn),
        in_specs=[
            pl.BlockSpec((tm, d), lambda i, j: (i, 0)),
            pl.BlockSpec((1, d), const2),
            pl.BlockSpec((None, tn, d), lambda i, j: (layer, j, 0)),
            pl.BlockSpec((None, LANES, d), lambda i, j: (layer, 0, 0)),
            pl.BlockSpec((1, LANES), const2),
            pl.BlockSpec((tc, tc), const2),
            pl.BlockSpec(place.shape, const2),
        ],
        out_specs=[pl.BlockSpec((tm, tn), lambda i, j: (i, j)),
                   pl.BlockSpec((tm, e_cols), lambda i, j: (i, 0))],
        out_shape=[jax.ShapeDtypeStruct((n, cols), BF16),
                   jax.ShapeDtypeStruct((n, e_cols), BF16)],
        scratch_shapes=[pltpu.VMEM((tm, d), BF16), pltpu.VMEM((1, LANES), F32)],
        compiler_params=_cparams(("arbitrary", "arbitrary")),
        name="norm_in_proj",
    )(x, g, w, w_f, b_f, tri, place)


def _head_lane_mask(hh):
    lane = lax.broadcasted_iota(jnp.int32, (1, LANES), 1)
    return (lane >= hh * HEAD_DIM) & (lane < (hh + 1) * HEAD_DIM)


def _tile_rel(t):
    return (lax.broadcasted_iota(jnp.int32, (t, t), 0)
            - lax.broadcasted_iota(jnp.int32, (t, t), 1))


def _mixer_specs(mixer, nb, t, s, d):
    qoff = N_BRANCH * d // LANES + (3 * mixer) * N_HEAD_BLOCKS
    koff = qoff + N_HEAD_BLOCKS
    voff = koff + N_HEAD_BLOCKS
    return [
        pl.BlockSpec((nb, t, LANES), lambda bg, hp, qi: (bg, qi, qoff + hp)),
        pl.BlockSpec((nb, s, LANES), lambda bg, hp, qi: (bg, 0, koff + hp)),
        pl.BlockSpec((nb, s, LANES), lambda bg, hp, qi: (bg, 0, voff + hp)),
    ], pl.BlockSpec((nb, t, LANES), lambda bg, hp, qi: (bg, qi, hp))


def _sb_kernel(q_ref, k_ref, v_ref, o_ref, vt_ref, *scratch, nb, t):
    n_blocks = k_ref.shape[1] // t
    q_tiles = q_ref.shape[1] // t
    streams = [(bi, hh) for bi in range(nb) for hh in range(HEADS_PER_BLOCK)]

    @pl.when(pl.program_id(2) == 0)
    def _build():
        def chunk(c, carry):
            rows = pl.ds(pl.multiple_of(c * t, t), t)
            for st, (bi, hh) in enumerate(streams):
                v2t = v_ref[bi, rows, :].astype(F32).T
                vt_ref[st, :, rows] = v2t[hh * HEAD_DIM:(hh + 1) * HEAD_DIM, :].astype(BF16)
            return carry

        lax.fori_loop(0, n_blocks, chunk, 0)

    def q_tile(sub, carry):
        q_rows = pl.ds(pl.multiple_of(sub * t, t), t)
        _sb_q_tile(pl.program_id(2) * q_tiles + sub, q_rows, q_ref, k_ref, o_ref, vt_ref, *scratch, nb=nb, t=t)
        return carry

    lax.fori_loop(0, q_tiles, q_tile, 0)


def _sb_q_tile(qi, q_rows, q_ref, k_ref, o_ref, vt_ref, acc_ref, cs_ref, r_ref, s_even_ref, s_odd_ref, *, nb, t):
    streams = [(bi, hh) for bi in range(nb) for hh in range(HEADS_PER_BLOCK)]
    q_m = []
    for bi, hh in streams:
        q2 = q_ref[bi, q_rows, :]
        q_m.append(_transposed(jnp.where(_head_lane_mask(hh), q2, jnp.zeros_like(q2))))

    rel = _tile_rel(t)
    strictly_past = rel < 0
    upper = jnp.where(rel <= 0, 1.0, 0.0).astype(BF16)

    stream_groups = [range(g, g + HEADS_PER_BLOCK) for g in range(0, len(streams), HEADS_PER_BLOCK)]

    def scores(s_ref, j, group):
        rows = pl.ds(pl.multiple_of(j * t, t), t)
        for st in group:
            s_ref[st] = _dot(k_ref[streams[st][0], rows, :], q_m[st])

    def softplus_sums(s_ref, diag, group):
        for st in group:
            z = s_ref[st]
            sp = jnp.maximum(z, jnp.log2(1.0 + jnp.exp2(jnp.minimum(z, SP2_CLAMP))))
            if diag:
                sp = jnp.where(strictly_past, sp, 0.0)
            r = _dot(upper, sp.astype(BF16)) + cs_ref[st]
            r_ref[st] = r
            cs_ref[st] = r[0:1, :]

    def weigh_values(s_ref, j, diag, group):
        rows = pl.ds(pl.multiple_of(j * t, t), t)
        for st in group:
            w = jnp.exp2(s_ref[st] - r_ref[st])
            if diag:
                w = jnp.where(strictly_past, w, 0.0)
            acc_ref[st] += _dot(vt_ref[st, :, rows], w.astype(BF16))

    def step(cur_ref, next_ref, j, j_next, diag):
        for group in stream_groups:
            scores(next_ref, j_next, group)
            softplus_sums(cur_ref, diag, group)
        for group in stream_groups:
            weigh_values(cur_ref, j, diag, group)

    def next_distance(i):
        lowest = cs_ref[0]
        for st in range(1, len(streams)):
            lowest = jnp.minimum(lowest, cs_ref[st])
        return jnp.where(jnp.min(lowest) > SB_EXIT, qi + 1, i + 1)

    acc_ref[...] = jnp.zeros(acc_ref.shape, F32)
    cs_ref[...] = jnp.zeros(cs_ref.shape, F32)
    for group in stream_groups:
        scores(s_even_ref, qi, group)
    step(s_even_ref, s_odd_ref, qi, jnp.maximum(qi - 1, 0), True)

    def body(i):
        j = qi - i
        j_next = jnp.maximum(j - 1, 0)

        @pl.when(i % 2 == 1)
        def _():
            step(s_odd_ref, s_even_ref, j, j_next, False)

        @pl.when(i % 2 == 0)
        def _():
            step(s_even_ref, s_odd_ref, j, j_next, False)

        return next_distance(i)

    lax.while_loop(lambda i: i <= qi, body, next_distance(0))

    for bi in range(nb):
        halves = [acc_ref[bi * HEADS_PER_BLOCK + hh] for hh in range(HEADS_PER_BLOCK)]
        o_ref[bi, q_rows, :] = jnp.concatenate(halves, axis=0).T.astype(o_ref.dtype)


def _sb_attention(proj3, d):
    b, s, _ = proj3.shape
    t = ATT_TILE
    nb = FLASH_BATCH if b % FLASH_BATCH == 0 else 1
    n_streams = nb * HEADS_PER_BLOCK
    q_rows = min(Q_TILES_PER_STEP * t, s)
    in_specs, out_spec = _mixer_specs(0, nb, q_rows, s, d)
    return pl.pallas_call(
        functools.partial(_sb_kernel, nb=nb, t=t),
        grid=(b // nb, N_HEAD_BLOCKS, s // q_rows),
        in_specs=in_specs,
        out_specs=out_spec,
        out_shape=jax.ShapeDtypeStruct((b, s, BRANCH_WIDTH), BF16),
        scratch_shapes=[pltpu.VMEM((n_streams, HEAD_DIM, s), BF16),
                        pltpu.VMEM((n_streams, HEAD_DIM, t), F32),
                        pltpu.VMEM((n_streams, 1, t), F32),
                        pltpu.VMEM((n_streams, t, t), F32),
                        pltpu.VMEM((n_streams, t, t), F32),
                        pltpu.VMEM((n_streams, t, t), F32)],
        compiler_params=_cparams(("parallel", "parallel", "arbitrary")),
        name="stickbreak_attn",
    )(proj3, proj3, proj3)


def _flash_kernel(*refs, moba, nb, t):
    q_tiles = refs[0].shape[1] // t

    def q_tile(sub, carry):
        q_rows = pl.ds(pl.multiple_of(sub * t, t), t)
        first = (pl.program_id(2) == 0) & (sub == 0)
        _flash_q_tile(pl.program_id(2) * q_tiles + sub, q_rows, first, *refs, moba=moba, nb=nb, t=t)
        return carry

    lax.fori_loop(0, q_tiles, q_tile, 0)


def _flash_q_tile(qi, q_rows, first, *refs, moba, nb, t):
    if moba:
        (q_ref, k_ref, v_ref, o_ref, kaug_ref, vt_ref, acc_ref, m_ref, s0_ref, s1_ref, s2_ref, s3_ref,
         bmax_ref, knorm_ref, kmean_ref, sel_ref) = refs
    else:
        (q_ref, k_ref, v_ref, e_ref, o_ref, kaug_ref, vt_ref, acc_ref, m_ref, s0_ref, s1_ref, s2_ref,
         s3_ref, bmax_ref, knorm_ref) = refs
    s_bufs = (s0_ref, s1_ref, s2_ref, s3_ref)
    n_blocks = k_ref.shape[1] // t
    hp = pl.program_id(1)
    lane = lax.broadcasted_iota(jnp.int32, (1, LANES), 1)
    streams = [(bi, hh) for bi in range(nb) for hh in range(HEADS_PER_BLOCK)]

    def piece_lanes(hh):
        e0 = _bias_lane(hh)
        return (lane >= e0) & (lane < e0 + N_BIAS_PIECES)

    head_of_lane = jnp.where(lax.broadcasted_iota(jnp.int32, (LANES, LANES), 0) >= HEAD_DIM, 1, 0)
    head_sum = jnp.where(head_of_lane == lax.broadcasted_iota(jnp.int32, (LANES, LANES), 1), 1.0, 0.0).astype(BF16)

    def head_sq_norm_max(x32):
        hi, lo = _split2(x32 * x32)
        return jnp.max(_dot(hi, head_sum) + _dot(lo, head_sum), axis=0, keepdims=True)

    def head_col(row, hh):
        return jnp.sum(jnp.where(lane == hh, row, 0.0), axis=1, keepdims=True)

    @pl.when(first)
    def _build():
        bmax_ref[...] = jnp.full(bmax_ref.shape, NEG_INF, F32)
        knorm_ref[...] = jnp.zeros(knorm_ref.shape, F32)
        tail = jnp.where(lax.broadcasted_iota(jnp.int32, (V_ROWS - HEAD_DIM, t), 0) == 0, 1.0, 0.0)

        def chunk(c, carry):
            start = pl.multiple_of(c * t, t)
            rows = pl.ds(start, t)
            extras = []
            if moba:
                pos = (lax.broadcasted_iota(jnp.int32, (t, LANES), 0) + start).astype(F32)
                for hh in range(HEADS_PER_BLOCK):
                    h = hp * HEADS_PER_BLOCK + hh
                    slope = pltpu.bitcast(jnp.full((1, LANES), (126 - h) << 23, jnp.int32), F32)
                    extra = jnp.zeros((t, LANES), F32)
                    for i, piece in enumerate(_split3(pos * slope * LOG2E)):
                        extra = jnp.where(lane == _bias_lane(hh) + i, piece.astype(F32), extra)
                    extras.append(extra.astype(BF16))
            for bi in range(nb):
                k2 = k_ref[bi, rows, :]
                k32 = k2.astype(F32)
                v2t = v_ref[bi, rows, :].astype(F32).T
                if moba:
                    kmean_ref[bi, pl.ds(c, 1), :] = jnp.sum(k32, axis=0, keepdims=True) * (1.0 / t)
                for hh in range(HEADS_PER_BLOCK):
                    st = bi * HEADS_PER_BLOCK + hh
                    extra = extras[hh] if moba else e_ref[bi, rows, :]
                    kaug_ref[st, rows, :] = jnp.where(_head_lane_mask(hh), k2, extra)
                    v_h = v2t[hh * HEAD_DIM:(hh + 1) * HEAD_DIM, :]
                    vt_ref[st, :, rows] = jnp.concatenate([v_h, tail], axis=0).astype(BF16)
                    piece_max = jnp.max(extra.astype(F32), axis=0, keepdims=True)
                    bias_max = jnp.sum(jnp.where(piece_lanes(hh), piece_max, 0.0), axis=1, keepdims=True)
                    run = jnp.maximum(bmax_ref[st, pl.ds(jnp.maximum(c - 1, 0), 1), :], bias_max)
                    bmax_ref[st, pl.ds(c, 1), :] = run
                knorm_ref[bi] = jnp.maximum(knorm_ref[bi], head_sq_norm_max(k32))
            return carry

        lax.fori_loop(0, n_blocks, chunk, 0)

    q_aug = []
    reach = []
    for bi in range(nb):
        q2 = q_ref[bi, q_rows, :]
        qk_sq = head_sq_norm_max(q2.astype(F32)) * knorm_ref[bi]
        for hh in range(HEADS_PER_BLOCK):
            ones = jnp.where(piece_lanes(hh), 1.0, 0.0).astype(BF16)
            q_aug.append(_transposed(jnp.where(_head_lane_mask(hh), q2, jnp.broadcast_to(ones, q2.shape))))
            reach.append(jnp.sqrt(head_col(qk_sq, hh)) * NORM_SLACK)

    if moba:
        blk = lax.broadcasted_iota(jnp.int32, (n_blocks, 1), 0)
        blk_f = blk.astype(F32)
        for st, (bi, hh) in enumerate(streams):
            km = jnp.where(_head_lane_mask(hh), kmean_ref[bi], 0.0)
            pieces = _dot(jnp.concatenate(_split3(km), axis=0), q_aug[st])
            route = pieces[:n_blocks] + pieces[n_blocks:2 * n_blocks] + pieces[2 * n_blocks:]
            route = jnp.where(blk < qi, route, NEG_INF)
            sel = jnp.zeros((n_blocks, t), F32)
            for _ in range(MOBA_TOPK):
                top = jnp.max(route, axis=0, keepdims=True)
                first = jnp.min(jnp.where(route == top, blk_f, float(n_blocks)), axis=0, keepdims=True)
                hit = blk_f == first
                sel = jnp.where(hit, 1.0, sel)
                route = jnp.where(hit, -jnp.inf, route)
            sel_ref[st] = jnp.where(blk < qi, sel, 0.0)

    valid = _tile_rel(t) <= 0

    all_streams = range(len(streams))
    stream_groups = [range(g, g + HEADS_PER_BLOCK) for g in range(0, len(streams), HEADS_PER_BLOCK)]

    def scores(s_ref, j, group=all_streams):
        rows = pl.ds(pl.multiple_of(j * t, t), t)
        for st in group:
            s_ref[st] = _dot(kaug_ref[st, rows, :], q_aug[st])

    def accumulate(s_ref, j, diag, group=all_streams):
        rows = pl.ds(pl.multiple_of(j * t, t), t)
        for st in group:
            s_t = s_ref[st]
            if diag:
                s_t = jnp.where(valid, s_t, NEG_INF)
            m_old = m_ref[st]
            m_new = jnp.maximum(m_old, jnp.max(s_t, axis=0, keepdims=True))
            m_sub = m_new
            if moba and not diag:
                picked = sel_ref[st, pl.ds(j, 1), :] > 0.0
                m_new = jnp.where(picked, m_new, m_old)
                m_sub = jnp.where(picked, m_new, -NEG_INF)
            alpha = jnp.exp2(m_old - m_new)
            p = jnp.exp2(s_t - m_sub)
            acc_ref[st] = alpha * acc_ref[st] + _dot(vt_ref[st, :, rows], p.astype(BF16))
            m_ref[st] = m_new

    def tile_at(dist):
        return jnp.maximum(qi - dist, 0)

    m_ref[...] = jnp.full(m_ref.shape, NEG_INF, F32)
    acc_ref[...] = jnp.zeros(acc_ref.shape, F32)
    scores(s_bufs[0], qi)
    for group in stream_groups:
        scores(s_bufs[1], tile_at(1), group)
        accumulate(s_bufs[0], qi, True, group)
        scores(s_bufs[2], tile_at(2), group)

    blk_col = lax.broadcasted_iota(jnp.int32, (n_blocks, 1), 0)
    fewest_dead = None
    for st in range(len(streams)):
        thr = jnp.min(m_ref[st], axis=1, keepdims=True) - FLASH_SKIP - reach[st]
        dead = (bmax_ref[st] < thr) & (blk_col < qi)
        n_dead = jnp.sum(jnp.where(dead, 1.0, 0.0), axis=0, keepdims=True)
        fewest_dead = n_dead if fewest_dead is None else jnp.minimum(fewest_dead, n_dead)
    n_past = qi - jnp.max(fewest_dead).astype(jnp.int32)

    def pair(first, cur, nxt):
        for k in range(2):
            for group in stream_groups:
                scores(s_bufs[nxt[k]], tile_at(first + 2 + k), group)
                accumulate(s_bufs[cur[k]], qi - first - k, False, group)

    def body(p, carry):
        first = 2 * p + 1

        @pl.when(p % 2 == 0)
        def _():
            pair(first, (1, 2), (3, 0))

        @pl.when(p % 2 == 1)
        def _():
            pair(first, (3, 0), (1, 2))

        return carry

    lax.fori_loop(0, n_past // 2, body, 0)

    for slot in (1, 3):
        @pl.when(n_past % 4 == slot)
        def _(slot=slot):
            accumulate(s_bufs[slot], qi - n_past, False)

    for bi in range(nb):
        halves = []
        for hh in range(HEADS_PER_BLOCK):
            acc = acc_ref[bi * HEADS_PER_BLOCK + hh]
            halves.append(acc[:HEAD_DIM] * (1.0 / acc[HEAD_DIM:HEAD_DIM + 1]))
        o_ref[bi, q_rows, :] = jnp.concatenate(halves, axis=0).T.astype(o_ref.dtype)


def _flash_attention(proj3, key_bias3, d, moba):
    b, s, _ = proj3.shape
    t = ATT_TILE
    nb = FLASH_BATCH if b % FLASH_BATCH == 0 else 1
    n_blocks = s // t
    n_streams = nb * HEADS_PER_BLOCK
    q_rows = min(Q_TILES_PER_STEP * t, s)
    in_specs, out_spec = _mixer_specs(2 if moba else 1, nb, q_rows, s, d)
    operands = [proj3, proj3, proj3]
    scratch = [pltpu.VMEM((n_streams, s, LANES), BF16),
               pltpu.VMEM((n_streams, V_ROWS, s), BF16),
               pltpu.VMEM((n_streams, V_ROWS, t), F32),
               pltpu.VMEM((n_streams, 1, t), F32),
               pltpu.VMEM((n_streams, t, t), F32),
               pltpu.VMEM((n_streams, t, t), F32),
               pltpu.VMEM((n_streams, t, t), F32),
               pltpu.VMEM((n_streams, t, t), F32),
               pltpu.VMEM((n_streams, n_blocks, 1), F32),
               pltpu.VMEM((nb, 1, LANES), F32)]
    if moba:
        scratch += [pltpu.VMEM((nb, n_blocks, LANES), F32),
                    pltpu.VMEM((n_streams, n_blocks, t), F32)]
    else:
        in_specs.append(pl.BlockSpec((nb, s, LANES), lambda bg, hp, qi: (bg, 0, hp)))
        operands.append(key_bias3)
    return pl.pallas_call(
        functools.partial(_flash_kernel, moba=moba, nb=nb, t=t),
        grid=(b // nb, N_HEAD_BLOCKS, s // q_rows),
        in_specs=in_specs,
        out_specs=out_spec,
        out_shape=jax.ShapeDtypeStruct((b, s, BRANCH_WIDTH), BF16),
        scratch_shapes=scratch,
        compiler_params=_cparams(("parallel", "parallel", "arbitrary")),
        name="moba_attn" if moba else "forgetting_attn",
    )(*operands)


def _mix_kernel(x_ref, oa_ref, ob_ref, oc_ref, g_ref, wb_ref, wo_ref, y_ref):
    d = x_ref.shape[1]
    mixed = None
    for n, o_ref in enumerate((oa_ref, ob_ref, oc_ref)):
        gate = jax.nn.sigmoid(g_ref[:, n * d:(n + 1) * d].astype(F32))
        term = gate * _dot(o_ref[...], wb_ref[n])
        mixed = term if mixed is None else mixed + term
    y_ref[...] = x_ref[...] + _dot(mixed.astype(BF16), wo_ref[...])


def _mix(x, o_a, o_b, o_c, proj, wb, wo, layer, tm):
    n, d = x.shape
    o_spec = pl.BlockSpec((tm, BRANCH_WIDTH), lambda i: (i, 0))
    return pl.pallas_call(
        _mix_kernel,
        grid=(n // tm,),
        in_specs=[
            pl.BlockSpec((tm, d), lambda i: (i, 0)),
            o_spec, o_spec, o_spec,
            pl.BlockSpec((tm, N_BRANCH * d), lambda i: (i, 0)),
            pl.BlockSpec((None, N_BRANCH, BRANCH_WIDTH, d), lambda i: (layer, 0, 0, 0)),
            pl.BlockSpec((None, d, d), lambda i: (layer, 0, 0)),
        ],
        out_specs=pl.BlockSpec((tm, d), lambda i: (i, 0)),
        out_shape=jax.ShapeDtypeStruct((n, d), F32),
        compiler_params=_cparams(("parallel",)),
        name="gated_mix_out_proj",
    )(x, o_a, o_b, o_c, proj, wb, wo)


def _mlp_kernel(x_ref, g_ref, wu_ref, wd_ref, gf_ref, y_ref, h_ref, acc_ref, *, final_norm):
    f = pl.program_id(1)

    @pl.when(f == 0)
    def _():
        h_ref[...] = _rms(x_ref[...], g_ref[...]).astype(BF16)
        acc_ref[...] = jnp.zeros_like(acc_ref)

    hid = jnp.square(jnp.maximum(_dot(h_ref[...], wu_ref[...]), 0.0))
    acc_ref[...] += _dot(hid.astype(BF16), wd_ref[...])

    @pl.when(f == pl.num_programs(1) - 1)
    def _():
        y = x_ref[...] + acc_ref[...]
        if final_norm:
            y = _rms(y, gf_ref[...])
        y_ref[...] = y


def _mlp(x, g, wu, wd, g_final, layer, final_norm, tm, tf):
    n, d = x.shape
    d_ff = wu.shape[2]
    return pl.pallas_call(
        functools.partial(_mlp_kernel, final_norm=final_norm),
        grid=(n // tm, d_ff // tf),
        in_specs=[
            pl.BlockSpec((tm, d), lambda i, f: (i, 0)),
            pl.BlockSpec((1, d), lambda i, f: (0, 0)),
            pl.BlockSpec((None, d, tf), lambda i, f: (layer, 0, f)),
            pl.BlockSpec((None, tf, d), lambda i, f: (layer, f, 0)),
            pl.BlockSpec((1, d), lambda i, f: (0, 0)),
        ],
        out_specs=pl.BlockSpec((tm, d), lambda i, f: (i, 0)),
        out_shape=jax.ShapeDtypeStruct((n, d), F32),
        scratch_shapes=[pltpu.VMEM((tm, d), BF16), pltpu.VMEM((tm, d), F32)],
        compiler_params=_cparams(("parallel", "arbitrary")),
        name="relu2_mlp",
    )(x, g, wu, wd, g_final)


REPACK_ROWS = 512


def _repack_kernel(w_ref, wf_ref, main_ref, f_ref):
    i = pl.program_id(1)
    q_chunks = BRANCH_WIDTH // REPACK_ROWS
    n_gate_chunks = N_BRANCH * w_ref.shape[2] // REPACK_ROWS
    mixer_chunk = jnp.maximum(i - n_gate_chunks, 0) % (3 * q_chunks)
    is_q = (i >= n_gate_chunks) & (mixer_chunk < q_chunks)
    scale = jnp.where(is_q, HEAD_DIM ** -0.5 * LOG2E, 1.0)
    main_ref[...] = (w_ref[0] * scale).astype(BF16)

    @pl.when(i == 0)
    def _():
        pad = jnp.zeros((f_ref.shape[0] - N_HEADS, f_ref.shape[1]), F32)
        f_ref[...] = jnp.concatenate([wf_ref[0], pad], axis=0).astype(BF16)


def _prep_in_proj(w_in):
    depth, d, cols = w_in.shape
    w_t = jnp.swapaxes(w_in, 1, 2)
    bw = BRANCH_WIDTH
    r = REPACK_ROWS
    f_lo = 6 * bw
    c_lo = f_lo + N_HEADS
    g_lo = c_lo + 3 * bw
    main_rows = cols - N_HEADS
    n_gate = (cols - g_lo) // r
    n_ab = f_lo // r

    def src_row(l, i):
        row = jnp.where(i < n_gate, g_lo + r * i,
                        jnp.where(i < n_gate + n_ab, r * (i - n_gate), c_lo + r * (i - n_gate - n_ab)))
        return l, pl.multiple_of(row, N_HEADS), 0

    return pl.pallas_call(
        _repack_kernel,
        grid=(depth, main_rows // r),
        in_specs=[pl.BlockSpec((pl.Element(1), pl.Element(r), pl.Element(d)), src_row),
                  pl.BlockSpec((pl.Element(1), pl.Element(N_HEADS), pl.Element(d)), lambda l, i: (l, f_lo, 0))],
        out_specs=[pl.BlockSpec((None, r, d), lambda l, i: (l, i, 0)),
                   pl.BlockSpec((None, LANES, d), lambda l, i: (l, 0, 0))],
        out_shape=[jax.ShapeDtypeStruct((depth, main_rows, d), BF16),
                   jax.ShapeDtypeStruct((depth, LANES, d), BF16)],
        compiler_params=_cparams(("parallel", "arbitrary")),
        name="repack_w_in",
    )(w_t, w_t)


def kernel(x, norm_mix, w_in, b_forget, w_branch, w_out, norm_mlp, w_up, w_down, norm_final):
    b, s, d = x.shape
    depth = w_in.shape[0]
    n = b * s
    assert s % ATT_TILE == 0 and d % LANES == 0

    w_main, w_f = _prep_in_proj(w_in)
    wb = w_branch.astype(BF16)
    wo = w_out.astype(BF16)
    wu = w_up.astype(BF16)
    wd = w_down.astype(BF16)
    b_f = jnp.pad(b_forget, ((0, 0), (0, LANES - N_HEADS)))[:, None, :]

    tm = min(1024, s)
    xf = x.reshape(n, d)
    for l in range(depth):
        g_mix = norm_mix[l][None, :]
        proj, key_bias = _in_proj(xf, g_mix, w_main, w_f, b_f[l], l, s, tm, 1536, 256)
        proj3 = proj.reshape(b, s, -1)
        o_a = _sb_attention(proj3, d).reshape(n, -1)
        o_b = _flash_attention(proj3, key_bias.reshape(b, s, -1), d, False).reshape(n, -1)
        o_c = _flash_attention(proj3, None, d, True).reshape(n, -1)
        xf = _mix(xf, o_a, o_b, o_c, proj, wb, wo, l, min(512, n))
        xf = _mlp(xf, norm_mlp[l][None, :], wu, wd, norm_final[None, :],
                  l, l == depth - 1, min(1024, n), 1024)
    return xf.reshape(b, s, d)
```

```python
import functools
import math

import jax
import jax.numpy as jnp
import numpy as np
from jax import lax
from jax.experimental import pallas as pl
from jax.experimental.pallas import tpu as pltpu

F32 = jnp.float32
BF16 = jnp.bfloat16

HEAD_DIM = 64
N_HEADS = 8
BRANCH_WIDTH = N_HEADS * HEAD_DIM
N_BRANCH = 3
MOBA_BLOCK = 256
MOBA_TOPK = 3
RMS_EPS = 1e-6
NEG_INF = -1e30
LOG2E = math.log2(math.e)
LANES = 128
HEADS_PER_BLOCK = LANES // HEAD_DIM
N_HEAD_BLOCKS = BRANCH_WIDTH // LANES
N_BIAS_PIECES = 3
ATT_TILE = MOBA_BLOCK
Q_TILES_PER_STEP = 4
MIXER_BATCH = 4
SB_EXIT = 160.0
FLASH_SKIP = 160.0
SP2_CLAMP = 120.0
NORM_SLACK = 1.001
V_ROWS = HEAD_DIM + 16
VMEM_LIMIT = 60000 * 1024


def _cparams(sem):
    return pltpu.CompilerParams(dimension_semantics=sem, vmem_limit_bytes=VMEM_LIMIT)


def _rms(x, g):
    ms = jnp.mean(x * x, axis=-1, keepdims=True)
    return x * lax.rsqrt(ms + RMS_EPS) * g


def _dot(a, b):
    return jnp.dot(a, b, preferred_element_type=F32)


def _dot_nt(a, b):
    return lax.dot_general(a, b, (((1,), (1,)), ((), ())), preferred_element_type=F32)


def _transposed(x):
    return x.astype(F32).T.astype(BF16)


def _split2(x):
    hi = x.astype(BF16)
    lo = (x - hi.astype(F32)).astype(BF16)
    return hi, lo


def _split3(x):
    hi = x.astype(BF16)
    r = x - hi.astype(F32)
    mid = r.astype(BF16)
    lo = (r - mid.astype(F32)).astype(BF16)
    return hi, mid, lo


def _bias_lane(hh):
    return (1 - hh) * HEAD_DIM


def _in_proj_kernel(x_ref, g_ref, w_ref, wf_ref, b_ref, tri_ref, place_ref,
                    o_ref, e_ref, h_ref, carry_ref, *, tiles_per_seq):
    @pl.when(pl.program_id(1) == 0)
    def _():
        h = _rms(x_ref[...], g_ref[...]).astype(BF16)
        h_ref[...] = h

        @pl.when(pl.program_id(0) % tiles_per_seq == 0)
        def _():
            carry_ref[...] = jnp.zeros_like(carry_ref)

        y = _dot_nt(h, wf_ref[...]) + b_ref[...]
        logf = jnp.minimum(y, 0.0) - jnp.log(1.0 + jnp.exp(-jnp.abs(y)))
        pieces = _split3(logf)
        tri = tri_ref[...]
        tc = tri.shape[0]
        cum = carry_ref[...]
        for c in range(h.shape[0] // tc):
            rows = slice(c * tc, (c + 1) * tc)
            cum = cum[-1:, :]
            for piece in pieces:
                cum = cum + _dot(tri, piece[rows])
            bias_pieces = jnp.concatenate(_split3(cum * (-LOG2E)), axis=1)
            e_ref[rows, :] = _dot(bias_pieces, place_ref[...]).astype(BF16)
        carry_ref[...] = cum[-1:, :]

    o_ref[...] = _dot_nt(h_ref[...], w_ref[...]).astype(o_ref.dtype)


def _bias_placement():
    place = np.zeros((N_BIAS_PIECES, LANES, N_HEAD_BLOCKS * LANES), np.float32)
    for h in range(N_HEADS):
        hp, hh = divmod(h, HEADS_PER_BLOCK)
        for i in range(N_BIAS_PIECES):
            place[i, h, hp * LANES + _bias_lane(hh) + i] = 1.0
    return jnp.asarray(place.reshape(N_BIAS_PIECES * LANES, -1), BF16)


def _in_proj(x, g, w, w_f, b_f, layer, s, tm, tn, tc):
    n, d = x.shape
    cols = w.shape[1]
    tri = jnp.asarray(np.tril(np.ones((tc, tc), np.float32)), BF16)
    place = _bias_placement()
    e_cols = place.shape[1]
    const2 = lambda i, j: (0, 0)
    return pl.pallas_call(
        functools.partial(_in_proj_kernel, tiles_per_seq=s // tm),
        grid=(n // tm, cols // tn),
        in_specs=[
            pl.BlockSpec((tm, d), lambda i, j: (i, 0)),
            pl.BlockSpec((1, d), const2),
            pl.BlockSpec((None, tn, d), lambda i, j: (layer, j, 0)),
            pl.BlockSpec((None, LANES, d), lambda i, j: (layer, 0, 0)),
            pl.BlockSpec((1, LANES), const2),
            pl.BlockSpec((tc, tc), const2),
            pl.BlockSpec(place.shape, const2),
        ],
        out_specs=[pl.BlockSpec((tm, tn), lambda i, j: (i, j)),
                   pl.BlockSpec((tm, e_cols), lambda i, j: (i, 0))],
        out_shape=[jax.ShapeDtypeStruct((n, cols), BF16),
                   jax.ShapeDtypeStruct((n, e_cols), BF16)],
        scratch_shapes=[pltpu.VMEM((tm, d), BF16), pltpu.VMEM((1, LANES), F32)],
        compiler_params=_cparams(("arbitrary", "arbitrary")),
        name="norm_in_proj",
    )(x, g, w, w_f, b_f, tri, place)


def _head_lane_mask(hh):
    lane = lax.broadcasted_iota(jnp.int32, (1, LANES), 1)
    return (lane >= hh * HEAD_DIM) & (lane < (hh + 1) * HEAD_DIM)


def _tile_rel(t):
    return (lax.broadcasted_iota(jnp.int32, (t, t), 0)
            - lax.broadcasted_iota(jnp.int32, (t, t), 1))


def _mixer_specs(mixer, nb, t, s, d):
    qoff = N_BRANCH * d // LANES + (3 * mixer) * N_HEAD_BLOCKS
    koff = qoff + N_HEAD_BLOCKS
    voff = koff + N_HEAD_BLOCKS
    return [
        pl.BlockSpec((nb, t, LANES), lambda bg, hp, qi: (bg, qi, qoff + hp)),
        pl.BlockSpec((nb, s, LANES), lambda bg, hp, qi: (bg, 0, koff + hp)),
        pl.BlockSpec((nb, s, LANES), lambda bg, hp, qi: (bg, 0, voff + hp)),
    ], pl.BlockSpec((nb, t, LANES), lambda bg, hp, qi: (bg, qi, hp))


def _sb_kernel(q_ref, k_ref, v_ref, o_ref, vt_ref, *scratch, nb, t):
    n_blocks = k_ref.shape[1] // t
    q_tiles = q_ref.shape[1] // t
    streams = [(bi, hh) for bi in range(nb) for hh in range(HEADS_PER_BLOCK)]

    @pl.when(pl.program_id(2) == 0)
    def _build():
        def chunk(c, carry):
            rows = pl.ds(pl.multiple_of(c * t, t), t)
            for st, (bi, hh) in enumerate(streams):
                v2t = v_ref[bi, rows, :].astype(F32).T
                vt_ref[st, :, rows] = v2t[hh * HEAD_DIM:(hh + 1) * HEAD_DIM, :].astype(BF16)
            return carry

        lax.fori_loop(0, n_blocks, chunk, 0)

    def q_tile(sub, carry):
        q_rows = pl.ds(pl.multiple_of(sub * t, t), t)
        _sb_q_tile(pl.program_id(2) * q_tiles + sub, q_rows, q_ref, k_ref, o_ref, vt_ref, *scratch, nb=nb, t=t)
        return carry

    lax.fori_loop(0, q_tiles, q_tile, 0)


def _sb_q_tile(qi, q_rows, q_ref, k_ref, o_ref, vt_ref, acc_ref, cs_ref, r_ref, s_even_ref, s_odd_ref, *, nb, t):
    streams = [(bi, hh) for bi in range(nb) for hh in range(HEADS_PER_BLOCK)]
    q_m = []
    for bi, hh in streams:
        q2 = q_ref[bi, q_rows, :]
        q_m.append(_transposed(jnp.where(_head_lane_mask(hh), q2, jnp.zeros_like(q2))))

    rel = _tile_rel(t)
    strictly_past = rel < 0
    upper = jnp.where(rel <= 0, 1.0, 0.0).astype(BF16)

    stream_groups = [range(g, g + HEADS_PER_BLOCK) for g in range(0, len(streams), HEADS_PER_BLOCK)]

    def scores(s_ref, j, group):
        rows = pl.ds(pl.multiple_of(j * t, t), t)
        for st in group:
            s_ref[st] = _dot(k_ref[streams[st][0], rows, :], q_m[st])

    def softplus_sums(s_ref, diag, group):
        for st in group:
            z = s_ref[st]
            sp = jnp.maximum(z, jnp.log2(1.0 + jnp.exp2(jnp.minimum(z, SP2_CLAMP))))
            if diag:
                sp = jnp.where(strictly_past, sp, 0.0)
            r = _dot(upper, sp.astype(BF16)) + cs_ref[st]
            r_ref[st] = r
            cs_ref[st] = r[0:1, :]

    def weigh_values(s_ref, j, diag, group):
        rows = pl.ds(pl.multiple_of(j * t, t), t)
        for st in group:
            w = jnp.exp2(s_ref[st] - r_ref[st])
            if diag:
                w = jnp.where(strictly_past, w, 0.0)
            acc_ref[st] += _dot(vt_ref[st, :, rows], w.astype(BF16))

    def step(cur_ref, next_ref, j, j_next, diag):
        for group in stream_groups:
            scores(next_ref, j_next, group)
            softplus_sums(cur_ref, diag, group)
        for group in stream_groups:
            weigh_values(cur_ref, j, diag, group)

    def next_distance(i):
        lowest = cs_ref[0]
        for st in range(1, len(streams)):
            lowest = jnp.minimum(lowest, cs_ref[st])
        return jnp.where(jnp.min(lowest) > SB_EXIT, qi + 1, i + 1)

    acc_ref[...] = jnp.zeros(acc_ref.shape, F32)
    cs_ref[...] = jnp.zeros(cs_ref.shape, F32)
    for group in stream_groups:
        scores(s_even_ref, qi, group)
    step(s_even_ref, s_odd_ref, qi, jnp.maximum(qi - 1, 0), True)

    def body(i):
        j = qi - i
        j_next = jnp.maximum(j - 1, 0)

        @pl.when(i % 2 == 1)
        def _():
            step(s_odd_ref, s_even_ref, j, j_next, False)

        @pl.when(i % 2 == 0)
        def _():
            step(s_even_ref, s_odd_ref, j, j_next, False)

        return next_distance(i)

    lax.while_loop(lambda i: i <= qi, body, next_distance(0))

    for bi in range(nb):
        halves = [acc_ref[bi * HEADS_PER_BLOCK + hh] for hh in range(HEADS_PER_BLOCK)]
        o_ref[bi, q_rows, :] = jnp.concatenate(halves, axis=0).T.astype(o_ref.dtype)


def _sb_attention(proj3, d):
    b, s, _ = proj3.shape
    t = ATT_TILE
    nb = MIXER_BATCH if b % MIXER_BATCH == 0 else 1
    n_streams = nb * HEADS_PER_BLOCK
    q_rows = min(Q_TILES_PER_STEP * t, s)
    in_specs, out_spec = _mixer_specs(0, nb, q_rows, s, d)
    return pl.pallas_call(
        functools.partial(_sb_kernel, nb=nb, t=t),
        grid=(b // nb, N_HEAD_BLOCKS, s // q_rows),
        in_specs=in_specs,
        out_specs=out_spec,
        out_shape=jax.ShapeDtypeStruct((b, s, BRANCH_WIDTH), BF16),
        scratch_shapes=[pltpu.VMEM((n_streams, HEAD_DIM, s), BF16),
                        pltpu.VMEM((n_streams, HEAD_DIM, t), F32),
                        pltpu.VMEM((n_streams, 1, t), F32),
                        pltpu.VMEM((n_streams, t, t), F32),
                        pltpu.VMEM((n_streams, t, t), F32),
                        pltpu.VMEM((n_streams, t, t), F32)],
        compiler_params=_cparams(("parallel", "parallel", "arbitrary")),
        name="stickbreak_attn",
    )(proj3, proj3, proj3)


def _flash_kernel(*refs, moba, nb, t):
    q_tiles = refs[0].shape[1] // t

    def q_tile(sub, carry):
        q_rows = pl.ds(pl.multiple_of(sub * t, t), t)
        first = (pl.program_id(2) == 0) & (sub == 0)
        _flash_q_tile(pl.program_id(2) * q_tiles + sub, q_rows, first, *refs, moba=moba, nb=nb, t=t)
        return carry

    lax.fori_loop(0, q_tiles, q_tile, 0)


def _flash_q_tile(qi, q_rows, first, *refs, moba, nb, t):
    if moba:
        (q_ref, k_ref, v_ref, o_ref, kaug_ref, vt_ref, acc_ref, m_ref, s0_ref, s1_ref, s2_ref, s3_ref,
         bmax_ref, knorm_ref, kmean_ref, sel_ref) = refs
    else:
        (q_ref, k_ref, v_ref, e_ref, o_ref, kaug_ref, vt_ref, acc_ref, m_ref, s0_ref, s1_ref, s2_ref,
         s3_ref, bmax_ref, knorm_ref) = refs
    s_bufs = (s0_ref, s1_ref, s2_ref, s3_ref)
    n_blocks = k_ref.shape[1] // t
    hp = pl.program_id(1)
    lane = lax.broadcasted_iota(jnp.int32, (1, LANES), 1)
    streams = [(bi, hh) for bi in range(nb) for hh in range(HEADS_PER_BLOCK)]

    def piece_lanes(hh):
        e0 = _bias_lane(hh)
        return (lane >= e0) & (lane < e0 + N_BIAS_PIECES)

    head_of_lane = jnp.where(lax.broadcasted_iota(jnp.int32, (LANES, LANES), 0) >= HEAD_DIM, 1, 0)
    head_sum = jnp.where(head_of_lane == lax.broadcasted_iota(jnp.int32, (LANES, LANES), 1), 1.0, 0.0).astype(BF16)

    def head_sq_norm_max(x32):
        hi, lo = _split2(x32 * x32)
        return jnp.max(_dot(hi, head_sum) + _dot(lo, head_sum), axis=0, keepdims=True)

    def head_col(row, hh):
        return jnp.sum(jnp.where(lane == hh, row, 0.0), axis=1, keepdims=True)

    @pl.when(first)
    def _build():
        bmax_ref[...] = jnp.full(bmax_ref.shape, NEG_INF, F32)
        knorm_ref[...] = jnp.zeros(knorm_ref.shape, F32)
        tail = jnp.where(lax.broadcasted_iota(jnp.int32, (V_ROWS - HEAD_DIM, t), 0) == 0, 1.0, 0.0)

        def chunk(c, carry):
            start = pl.multiple_of(c * t, t)
            rows = pl.ds(start, t)
            extras = []
            if moba:
                pos = (lax.broadcasted_iota(jnp.int32, (t, LANES), 0) + start).astype(F32)
                for hh in range(HEADS_PER_BLOCK):
                    h = hp * HEADS_PER_BLOCK + hh
                    slope = pltpu.bitcast(jnp.full((1, LANES), (126 - h) << 23, jnp.int32), F32)
                    extra = jnp.zeros((t, LANES), F32)
                    for i, piece in enumerate(_split3(pos * slope * LOG2E)):
                        extra = jnp.where(lane == _bias_lane(hh) + i, piece.astype(F32), extra)
                    extras.append(extra.astype(BF16))
            for bi in range(nb):
                k2 = k_ref[bi, rows, :]
                k32 = k2.astype(F32)
                v2t = v_ref[bi, rows, :].astype(F32).T
                if moba:
                    kmean_ref[bi, pl.ds(c, 1), :] = jnp.sum(k32, axis=0, keepdims=True) * (1.0 / t)
                for hh in range(HEADS_PER_BLOCK):
                    st = bi * HEADS_PER_BLOCK + hh
                    extra = extras[hh] if moba else e_ref[bi, rows, :]
                    kaug_ref[st, rows, :] = jnp.where(_head_lane_mask(hh), k2, extra)
                    v_h = v2t[hh * HEAD_DIM:(hh + 1) * HEAD_DIM, :]
                    vt_ref[st, :, rows] = jnp.concatenate([v_h, tail], axis=0).astype(BF16)
                    piece_max = jnp.max(extra.astype(F32), axis=0, keepdims=True)
                    bias_max = jnp.sum(jnp.where(piece_lanes(hh), piece_max, 0.0), axis=1, keepdims=True)
                    run = jnp.maximum(bmax_ref[st, pl.ds(jnp.maximum(c - 1, 0), 1), :], bias_max)
                    bmax_ref[st, pl.ds(c, 1), :] = run
                knorm_ref[bi] = jnp.maximum(knorm_ref[bi], head_sq_norm_max(k32))
            return carry

        lax.fori_loop(0, n_blocks, chunk, 0)

    q_aug = []
    reach = []
    for bi in range(nb):
        q2 = q_ref[bi, q_rows, :]
        qk_sq = head_sq_norm_max(q2.astype(F32)) * knorm_ref[bi]
        for hh in range(HEADS_PER_BLOCK):
            ones = jnp.where(piece_lanes(hh), 1.0, 0.0).astype(BF16)
            q_aug.append(_transposed(jnp.where(_head_lane_mask(hh), q2, jnp.broadcast_to(ones, q2.shape))))
            reach.append(jnp.sqrt(head_col(qk_sq, hh)) * NORM_SLACK)

    if moba:
        blk = lax.broadcasted_iota(jnp.int32, (n_blocks, 1), 0)
        blk_f = blk.astype(F32)
        for st, (bi, hh) in enumerate(streams):
            km = jnp.where(_head_lane_mask(hh), kmean_ref[bi], 0.0)
            pieces = _dot(jnp.concatenate(_split3(km), axis=0), q_aug[st])
            route = pieces[:n_blocks] + pieces[n_blocks:2 * n_blocks] + pieces[2 * n_blocks:]
            route = jnp.where(blk < qi, route, NEG_INF)
            sel = jnp.zeros((n_blocks, t), F32)
            for _ in range(MOBA_TOPK):
                top = jnp.max(route, axis=0, keepdims=True)
                first = jnp.min(jnp.where(route == top, blk_f, float(n_blocks)), axis=0, keepdims=True)
                hit = blk_f == first
                sel = jnp.where(hit, 1.0, sel)
                route = jnp.where(hit, -jnp.inf, route)
            sel_ref[st] = jnp.where(blk < qi, sel, 0.0)

    valid = _tile_rel(t) <= 0

    all_streams = range(len(streams))
    stream_groups = [range(g, g + HEADS_PER_BLOCK) for g in range(0, len(streams), HEADS_PER_BLOCK)]

    def scores(s_ref, j, group=all_streams):
        rows = pl.ds(pl.multiple_of(j * t, t), t)
        for st in group:
            s_ref[st] = _dot(kaug_ref[st, rows, :], q_aug[st])

    def accumulate(s_ref, j, diag, group=all_streams):
        rows = pl.ds(pl.multiple_of(j * t, t), t)
        for st in group:
            s_t = s_ref[st]
            if diag:
                s_t = jnp.where(valid, s_t, NEG_INF)
            m_old = m_ref[st]
            m_new = jnp.maximum(m_old, jnp.max(s_t, axis=0, keepdims=True))
            m_sub = m_new
            if moba and not diag:
                picked = sel_ref[st, pl.ds(j, 1), :] > 0.0
                m_new = jnp.where(picked, m_new, m_old)
                m_sub = jnp.where(picked, m_new, -NEG_INF)
            alpha = jnp.exp2(m_old - m_new)
            p = jnp.exp2(s_t - m_sub)
            acc_ref[st] = alpha * acc_ref[st] + _dot(vt_ref[st, :, rows], p.astype(BF16))
            m_ref[st] = m_new

    def tile_at(dist):
        return jnp.maximum(qi - dist, 0)

    m_ref[...] = jnp.full(m_ref.shape, NEG_INF, F32)
    acc_ref[...] = jnp.zeros(acc_ref.shape, F32)
    scores(s_bufs[0], qi)
    for group in stream_groups:
        scores(s_bufs[1], tile_at(1), group)
        accumulate(s_bufs[0], qi, True, group)
        scores(s_bufs[2], tile_at(2), group)

    blk_col = lax.broadcasted_iota(jnp.int32, (n_blocks, 1), 0)
    fewest_dead = None
    for st in range(len(streams)):
        thr = jnp.min(m_ref[st], axis=1, keepdims=True) - FLASH_SKIP - reach[st]
        dead = (bmax_ref[st] < thr) & (blk_col < qi)
        n_dead = jnp.sum(jnp.where(dead, 1.0, 0.0), axis=0, keepdims=True)
        fewest_dead = n_dead if fewest_dead is None else jnp.minimum(fewest_dead, n_dead)
    n_past = qi - jnp.max(fewest_dead).astype(jnp.int32)

    def pair(first, cur, nxt):
        for k in range(2):
            for group in stream_groups:
                scores(s_bufs[nxt[k]], tile_at(first + 2 + k), group)
                accumulate(s_bufs[cur[k]], qi - first - k, False, group)

    def body(p, carry):
        first = 2 * p + 1

        @pl.when(p % 2 == 0)
        def _():
            pair(first, (1, 2), (3, 0))

        @pl.when(p % 2 == 1)
        def _():
            pair(first, (3, 0), (1, 2))

        return carry

    lax.fori_loop(0, n_past // 2, body, 0)

    for slot in (1, 3):
        @pl.when(n_past % 4 == slot)
        def _(slot=slot):
            accumulate(s_bufs[slot], qi - n_past, False)

    for bi in range(nb):
        halves = []
        for hh in range(HEADS_PER_BLOCK):
            acc = acc_ref[bi * HEADS_PER_BLOCK + hh]
            halves.append(acc[:HEAD_DIM] * (1.0 / acc[HEAD_DIM:HEAD_DIM + 1]))
        o_ref[bi, q_rows, :] = jnp.concatenate(halves, axis=0).T.astype(o_ref.dtype)


def _flash_attention(proj3, key_bias3, d, moba):
    b, s, _ = proj3.shape
    t = ATT_TILE
    nb = MIXER_BATCH if b % MIXER_BATCH == 0 else 1
    n_blocks = s // t
    n_streams = nb * HEADS_PER_BLOCK
    q_rows = min(Q_TILES_PER_STEP * t, s)
    in_specs, out_spec = _mixer_specs(2 if moba else 1, nb, q_rows, s, d)
    operands = [proj3, proj3, proj3]
    scratch = [pltpu.VMEM((n_streams, s, LANES), BF16),
               pltpu.VMEM((n_streams, V_ROWS, s), BF16),
               pltpu.VMEM((n_streams, V_ROWS, t), F32),
               pltpu.VMEM((n_streams, 1, t), F32),
               pltpu.VMEM((n_streams, t, t), F32),
               pltpu.VMEM((n_streams, t, t), F32),
               pltpu.VMEM((n_streams, t, t), F32),
               pltpu.VMEM((n_streams, t, t), F32),
               pltpu.VMEM((n_streams, n_blocks, 1), F32),
               pltpu.VMEM((nb, 1, LANES), F32)]
    if moba:
        scratch += [pltpu.VMEM((nb, n_blocks, LANES), F32),
                    pltpu.VMEM((n_streams, n_blocks, t), F32)]
    else:
        in_specs.append(pl.BlockSpec((nb, s, LANES), lambda bg, hp, qi: (bg, 0, hp)))
        operands.append(key_bias3)
    return pl.pallas_call(
        functools.partial(_flash_kernel, moba=moba, nb=nb, t=t),
        grid=(b // nb, N_HEAD_BLOCKS, s // q_rows),
        in_specs=in_specs,
        out_specs=out_spec,
        out_shape=jax.ShapeDtypeStruct((b, s, BRANCH_WIDTH), BF16),
        scratch_shapes=scratch,
        compiler_params=_cparams(("parallel", "parallel", "arbitrary")),
        name="moba_attn" if moba else "forgetting_attn",
    )(*operands)


def _mix_kernel(x_ref, oa_ref, ob_ref, oc_ref, g_ref, wb_ref, wo_ref, y_ref):
    d = x_ref.shape[1]
    mixed = None
    for n, o_ref in enumerate((oa_ref, ob_ref, oc_ref)):
        gate = jax.nn.sigmoid(g_ref[:, n * d:(n + 1) * d].astype(F32))
        term = gate * _dot(o_ref[...], wb_ref[n])
        mixed = term if mixed is None else mixed + term
    y_ref[...] = x_ref[...] + _dot(mixed.astype(BF16), wo_ref[...])


def _mix(x, o_a, o_b, o_c, proj, wb, wo, layer, tm):
    n, d = x.shape
    o_spec = pl.BlockSpec((tm, BRANCH_WIDTH), lambda i: (i, 0))
    return pl.pallas_call(
        _mix_kernel,
        grid=(n // tm,),
        in_specs=[
            pl.BlockSpec((tm, d), lambda i: (i, 0)),
            o_spec, o_spec, o_spec,
            pl.BlockSpec((tm, N_BRANCH * d), lambda i: (i, 0)),
            pl.BlockSpec((None, N_BRANCH, BRANCH_WIDTH, d), lambda i: (layer, 0, 0, 0)),
            pl.BlockSpec((None, d, d), lambda i: (layer, 0, 0)),
        ],
        out_specs=pl.BlockSpec((tm, d), lambda i: (i, 0)),
        out_shape=jax.ShapeDtypeStruct((n, d), F32),
        compiler_params=_cparams(("parallel",)),
        name="gated_mix_out_proj",
    )(x, o_a, o_b, o_c, proj, wb, wo)


def _mlp_kernel(x_ref, g_ref, wu_ref, wd_ref, gf_ref, y_ref, h_ref, acc_ref, *, final_norm):
    f = pl.program_id(1)

    @pl.when(f == 0)
    def _():
        h_ref[...] = _rms(x_ref[...], g_ref[...]).astype(BF16)
        acc_ref[...] = jnp.zeros_like(acc_ref)

    hid = jnp.square(jnp.maximum(_dot(h_ref[...], wu_ref[...]), 0.0))
    acc_ref[...] += _dot(hid.astype(BF16), wd_ref[...])

    @pl.when(f == pl.num_programs(1) - 1)
    def _():
        y = x_ref[...] + acc_ref[...]
        if final_norm:
            y = _rms(y, gf_ref[...])
        y_ref[...] = y


def _mlp(x, g, wu, wd, g_final, layer, final_norm, tm, tf):
    n, d = x.shape
    d_ff = wu.shape[2]
    return pl.pallas_call(
        functools.partial(_mlp_kernel, final_norm=final_norm),
        grid=(n // tm, d_ff // tf),
        in_specs=[
            pl.BlockSpec((tm, d), lambda i, f: (i, 0)),
            pl.BlockSpec((1, d), lambda i, f: (0, 0)),
            pl.BlockSpec((None, d, tf), lambda i, f: (layer, 0, f)),
            pl.BlockSpec((None, tf, d), lambda i, f: (layer, f, 0)),
            pl.BlockSpec((1, d), lambda i, f: (0, 0)),
        ],
        out_specs=pl.BlockSpec((tm, d), lambda i, f: (i, 0)),
        out_shape=jax.ShapeDtypeStruct((n, d), F32),
        scratch_shapes=[pltpu.VMEM((tm, d), BF16), pltpu.VMEM((tm, d), F32)],
        compiler_params=_cparams(("parallel", "arbitrary")),
        name="relu2_mlp",
    )(x, g, wu, wd, g_final)


REPACK_ROWS = 512


def _repack_kernel(w_ref, wf_ref, main_ref, f_ref):
    i = pl.program_id(1)
    q_chunks = BRANCH_WIDTH // REPACK_ROWS
    n_gate_chunks = N_BRANCH * w_ref.shape[2] // REPACK_ROWS
    mixer_chunk = jnp.maximum(i - n_gate_chunks, 0) % (3 * q_chunks)
    is_q = (i >= n_gate_chunks) & (mixer_chunk < q_chunks)
    scale = jnp.where(is_q, HEAD_DIM ** -0.5 * LOG2E, 1.0)
    main_ref[...] = (w_ref[0] * scale).astype(BF16)

    @pl.when(i == 0)
    def _():
        pad = jnp.zeros((f_ref.shape[0] - N_HEADS, f_ref.shape[1]), F32)
        f_ref[...] = jnp.concatenate([wf_ref[0], pad], axis=0).astype(BF16)


def _prep_in_proj(w_in):
    depth, d, cols = w_in.shape
    w_t = jnp.swapaxes(w_in, 1, 2)
    bw = BRANCH_WIDTH
    r = REPACK_ROWS
    f_lo = 6 * bw
    c_lo = f_lo + N_HEADS
    g_lo = c_lo + 3 * bw
    main_rows = cols - N_HEADS
    n_gate = (cols - g_lo) // r
    n_ab = f_lo // r

    def src_row(l, i):
        row = jnp.where(i < n_gate, g_lo + r * i,
                        jnp.where(i < n_gate + n_ab, r * (i - n_gate), c_lo + r * (i - n_gate - n_ab)))
        return l, pl.multiple_of(row, N_HEADS), 0

    return pl.pallas_call(
        _repack_kernel,
        grid=(depth, main_rows // r),
        in_specs=[pl.BlockSpec((pl.Element(1), pl.Element(r), pl.Element(d)), src_row),
                  pl.BlockSpec((pl.Element(1), pl.Element(N_HEADS), pl.Element(d)), lambda l, i: (l, f_lo, 0))],
        out_specs=[pl.BlockSpec((None, r, d), lambda l, i: (l, i, 0)),
                   pl.BlockSpec((None, LANES, d), lambda l, i: (l, 0, 0))],
        out_shape=[jax.ShapeDtypeStruct((depth, main_rows, d), BF16),
                   jax.ShapeDtypeStruct((depth, LANES, d), BF16)],
        compiler_params=_cparams(("parallel", "arbitrary")),
        name="repack_w_in",
    )(w_t, w_t)


def kernel(x, norm_mix, w_in, b_forget, w_branch, w_out, norm_mlp, w_up, w_down, norm_final):
    b, s, d = x.shape
    depth = w_in.shape[0]
    n = b * s
    assert s % ATT_TILE == 0 and d % LANES == 0

    w_main, w_f = _prep_in_proj(w_in)
    wb = w_branch.astype(BF16)
    wo = w_out.astype(BF16)
    wu = w_up.astype(BF16)
    wd = w_down.astype(BF16)
    b_f = jnp.pad(b_forget, ((0, 0), (0, LANES - N_HEADS)))[:, None, :]

    tm = min(1024, s)
    xf = x.reshape(n, d)
    for l in range(depth):
        g_mix = norm_mix[l][None, :]
        proj, key_bias = _in_proj(xf, g_mix, w_main, w_f, b_f[l], l, s, tm, 2560, 256)
        proj3 = proj.reshape(b, s, -1)
        o_a = _sb_attention(proj3, d).reshape(n, -1)
        o_b = _flash_attention(proj3, key_bias.reshape(b, s, -1), d, False).reshape(n, -1)
        o_c = _flash_attention(proj3, None, d, True).reshape(n, -1)
        xf = _mix(xf, o_a, o_b, o_c, proj, wb, wo, l, min(512, n))
        xf = _mlp(xf, norm_mlp[l][None, :], wu, wd, norm_final[None, :],
                  l, l == depth - 1, min(1024, n), 1024)
    return xf.reshape(b, s, d)
```

```python
import functools
import math

import jax
import jax.numpy as jnp
import numpy as np
from jax import lax
from jax.experimental import pallas as pl
from jax.experimental.pallas import tpu as pltpu

F32 = jnp.float32
BF16 = jnp.bfloat16

HEAD_DIM = 64
N_HEADS = 8
BRANCH_WIDTH = N_HEADS * HEAD_DIM
N_BRANCH = 3
MOBA_BLOCK = 256
MOBA_TOPK = 3
RMS_EPS = 1e-6
NEG_INF = -1e30
LOG2E = math.log2(math.e)
LANES = 128
HEADS_PER_BLOCK = LANES // HEAD_DIM
N_HEAD_BLOCKS = BRANCH_WIDTH // LANES
N_BIAS_PIECES = 3
ATT_TILE = MOBA_BLOCK
Q_TILES_PER_STEP = 4
MIXER_BATCH = 4
SB_EXIT = 160.0
FLASH_SKIP = 160.0
SP2_CLAMP = 120.0
NORM_SLACK = 1.001
V_ROWS = HEAD_DIM + 16
VMEM_LIMIT = 60000 * 1024


def _cparams(sem):
    return pltpu.CompilerParams(dimension_semantics=sem, vmem_limit_bytes=VMEM_LIMIT)


def _rms(x, g):
    ms = jnp.mean(x * x, axis=-1, keepdims=True)
    return x * lax.rsqrt(ms + RMS_EPS) * g


def _dot(a, b):
    return jnp.dot(a, b, preferred_element_type=F32)


def _dot_nt(a, b):
    return lax.dot_general(a, b, (((1,), (1,)), ((), ())), preferred_element_type=F32)


def _transposed(x):
    return x.astype(F32).T.astype(BF16)


def _split2(x):
    hi = x.astype(BF16)
    lo = (x - hi.astype(F32)).astype(BF16)
    return hi, lo


def _split3(x):
    hi = x.astype(BF16)
    r = x - hi.astype(F32)
    mid = r.astype(BF16)
    lo = (r - mid.astype(F32)).astype(BF16)
    return hi, mid, lo


def _bias_lane(hh):
    return (1 - hh) * HEAD_DIM


def _in_proj_kernel(x_ref, g_ref, w_ref, wf_ref, b_ref, tri_ref, place_ref,
                    o_ref, e_ref, h_ref, carry_ref, *, tiles_per_seq):
    @pl.when(pl.program_id(1) == 0)
    def _():
        h = _rms(x_ref[...], g_ref[...]).astype(BF16)
        h_ref[...] = h

        @pl.when(pl.program_id(0) % tiles_per_seq == 0)
        def _():
            carry_ref[...] = jnp.zeros_like(carry_ref)

        y = _dot_nt(h, wf_ref[...]) + b_ref[...]
        logf = jnp.minimum(y, 0.0) - jnp.log(1.0 + jnp.exp(-jnp.abs(y)))
        pieces = _split3(logf)
        tri = tri_ref[...]
        tc = tri.shape[0]
        cum = carry_ref[...]
        for c in range(h.shape[0] // tc):
            rows = slice(c * tc, (c + 1) * tc)
            cum = cum[-1:, :]
            for piece in pieces:
                cum = cum + _dot(tri, piece[rows])
            bias_pieces = jnp.concatenate(_split3(cum * (-LOG2E)), axis=1)
            e_ref[rows, :] = _dot(bias_pieces, place_ref[...]).astype(BF16)
        carry_ref[...] = cum[-1:, :]

    o_ref[...] = _dot_nt(h_ref[...], w_ref[...]).astype(o_ref.dtype)


def _bias_placement():
    place = np.zeros((N_BIAS_PIECES, LANES, N_HEAD_BLOCKS * LANES), np.float32)
    for h in range(N_HEADS):
        hp, hh = divmod(h, HEADS_PER_BLOCK)
        for i in range(N_BIAS_PIECES):
            place[i, h, hp * LANES + _bias_lane(hh) + i] = 1.0
    return jnp.asarray(place.reshape(N_BIAS_PIECES * LANES, -1), BF16)


def _in_proj(x, g, w, w_f, b_f, layer, s, tm, tn, tc):
    n, d = x.shape
    cols = w.shape[1]
    tri = jnp.asarray(np.tril(np.ones((tc, tc), np.float32)), BF16)
    place = _bias_placement()
    e_cols = place.shape[1]
    const2 = lambda i, j: (0, 0)
    return pl.pallas_call(
        functools.partial(_in_proj_kernel, tiles_per_seq=s // tm),
        grid=(n // tm, cols // tn),
        in_specs=[
            pl.BlockSpec((tm, d), lambda i, j: (i, 0)),
            pl.BlockSpec((1, d), const2),
            pl.BlockSpec((None, tn, d), lambda i, j: (layer, j, 0)),
            pl.BlockSpec((None, LANES, d), lambda i, j: (layer, 0, 0)),
            pl.BlockSpec((1, LANES), const2),
            pl.BlockSpec((tc, tc), const2),
            pl.BlockSpec(place.shape, const2),
        ],
        out_specs=[pl.BlockSpec((tm, tn), lambda i, j: (i, j)),
                   pl.BlockSpec((tm, e_cols), lambda i, j: (i, 0))],
        out_shape=[jax.ShapeDtypeStruct((n, cols), BF16),
                   jax.ShapeDtypeStruct((n, e_cols), BF16)],
        scratch_shapes=[pltpu.VMEM((tm, d), BF16), pltpu.VMEM((1, LANES), F32)],
        compiler_params=_cparams(("arbitrary", "arbitrary")),
        name="norm_in_proj",
    )(x, g, w, w_f, b_f, tri, place)


def _head_lane_mask(hh):
    lane = lax.broadcasted_iota(jnp.int32, (1, LANES), 1)
    return (lane >= hh * HEAD_DIM) & (lane < (hh + 1) * HEAD_DIM)


def _tile_rel(t):
    return (lax.broadcasted_iota(jnp.int32, (t, t), 0)
            - lax.broadcasted_iota(jnp.int32, (t, t), 1))


def _mixer_specs(mixer, nb, t, s, d):
    qoff = N_BRANCH * d // LANES + (3 * mixer) * N_HEAD_BLOCKS
    koff = qoff + N_HEAD_BLOCKS
    voff = koff + N_HEAD_BLOCKS
    return [
        pl.BlockSpec((nb, t, LANES), lambda bg, hp, qi: (bg, qi, qoff + hp)),
        pl.BlockSpec((nb, s, LANES), lambda bg, hp, qi: (bg, 0, koff + hp)),
        pl.BlockSpec((nb, s, LANES), lambda bg, hp, qi: (bg, 0, voff + hp)),
    ], pl.BlockSpec((nb, t, LANES), lambda bg, hp, qi: (bg, qi, hp))


def _sb_kernel(q_ref, k_ref, v_ref, o_ref, vt_ref, *scratch, nb, t):
    n_blocks = k_ref.shape[1] // t
    q_tiles = q_ref.shape[1] // t
    streams = [(bi, hh) for bi in range(nb) for hh in range(HEADS_PER_BLOCK)]

    @pl.when(pl.program_id(2) == 0)
    def _build():
        def chunk(c, carry):
            rows = pl.ds(pl.multiple_of(c * t, t), t)
            for st, (bi, hh) in enumerate(streams):
                v2t = v_ref[bi, rows, :].astype(F32).T
                vt_ref[st, :, rows] = v2t[hh * HEAD_DIM:(hh + 1) * HEAD_DIM, :].astype(BF16)
            return carry

        lax.fori_loop(0, n_blocks, chunk, 0)

    def q_tile(sub, carry):
        q_rows = pl.ds(pl.multiple_of(sub * t, t), t)
        _sb_q_tile(pl.program_id(2) * q_tiles + sub, q_rows, q_ref, k_ref, o_ref, vt_ref, *scratch, nb=nb, t=t)
        return carry

    lax.fori_loop(0, q_tiles, q_tile, 0)


def _sb_q_tile(qi, q_rows, q_ref, k_ref, o_ref, vt_ref, acc_ref, cs_ref, r_ref, s_even_ref, s_odd_ref, *, nb, t):
    streams = [(bi, hh) for bi in range(nb) for hh in range(HEADS_PER_BLOCK)]
    q_m = []
    for bi, hh in streams:
        q2 = q_ref[bi, q_rows, :]
        q_m.append(_transposed(jnp.where(_head_lane_mask(hh), q2, jnp.zeros_like(q2))))

    rel = _tile_rel(t)
    strictly_past = rel < 0
    upper = jnp.where(rel <= 0, 1.0, 0.0).astype(BF16)

    stream_groups = [range(g, g + HEADS_PER_BLOCK) for g in range(0, len(streams), HEADS_PER_BLOCK)]

    def scores(s_ref, j, group):
        rows = pl.ds(pl.multiple_of(j * t, t), t)
        for st in group:
            s_ref[st] = _dot(k_ref[streams[st][0], rows, :], q_m[st])

    def softplus_sums(s_ref, diag, group):
        for st in group:
            z = s_ref[st]
            sp = jnp.maximum(z, jnp.log2(1.0 + jnp.exp2(jnp.minimum(z, SP2_CLAMP))))
            if diag:
                sp = jnp.where(strictly_past, sp, 0.0)
            r = _dot(upper, sp.astype(BF16)) + cs_ref[st]
            r_ref[st] = r
            cs_ref[st] = r[0:1, :]

    def weigh_values(s_ref, j, diag, group):
        rows = pl.ds(pl.multiple_of(j * t, t), t)
        for st in group:
            w = jnp.exp2(s_ref[st] - r_ref[st])
            if diag:
                w = jnp.where(strictly_past, w, 0.0)
            acc_ref[st] += _dot(vt_ref[st, :, rows], w.astype(BF16))

    def step(cur_ref, next_ref, j, j_next, diag):
        for group in stream_groups:
            scores(next_ref, j_next, group)
            softplus_sums(cur_ref, diag, group)
        for group in stream_groups:
            weigh_values(cur_ref, j, diag, group)

    def next_distance(i):
        lowest = cs_ref[0]
        for st in range(1, len(streams)):
            lowest = jnp.minimum(lowest, cs_ref[st])
        return jnp.where(jnp.min(lowest) > SB_EXIT, qi + 1, i + 1)

    acc_ref[...] = jnp.zeros(acc_ref.shape, F32)
    cs_ref[...] = jnp.zeros(cs_ref.shape, F32)
    for group in stream_groups:
        scores(s_even_ref, qi, group)
    step(s_even_ref, s_odd_ref, qi, jnp.maximum(qi - 1, 0), True)

    def body(i):
        j = qi - i
        j_next = jnp.maximum(j - 1, 0)

        @pl.when(i % 2 == 1)
        def _():
            step(s_odd_ref, s_even_ref, j, j_next, False)

        @pl.when(i % 2 == 0)
        def _():
            step(s_even_ref, s_odd_ref, j, j_next, False)

        return next_distance(i)

    lax.while_loop(lambda i: i <= qi, body, next_distance(0))

    for bi in range(nb):
        halves = [acc_ref[bi * HEADS_PER_BLOCK + hh] for hh in range(HEADS_PER_BLOCK)]
        o_ref[bi, q_rows, :] = jnp.concatenate(halves, axis=0).T.astype(o_ref.dtype)


def _sb_attention(proj3, d):
    b, s, _ = proj3.shape
    t = ATT_TILE
    nb = MIXER_BATCH if b % MIXER_BATCH == 0 else 1
    n_streams = nb * HEADS_PER_BLOCK
    q_rows = min(Q_TILES_PER_STEP * t, s)
    in_specs, out_spec = _mixer_specs(0, nb, q_rows, s, d)
    return pl.pallas_call(
        functools.partial(_sb_kernel, nb=nb, t=t),
        grid=(b // nb, N_HEAD_BLOCKS, s // q_rows),
        in_specs=in_specs,
        out_specs=out_spec,
        out_shape=jax.ShapeDtypeStruct((b, s, BRANCH_WIDTH), BF16),
        scratch_shapes=[pltpu.VMEM((n_streams, HEAD_DIM, s), BF16),
                        pltpu.VMEM((n_streams, HEAD_DIM, t), F32),
                        pltpu.VMEM((n_streams, 1, t), F32),
                        pltpu.VMEM((n_streams, t, t), F32),
                        pltpu.VMEM((n_streams, t, t), F32),
                        pltpu.VMEM((n_streams, t, t), F32)],
        compiler_params=_cparams(("parallel", "parallel", "arbitrary")),
        name="stickbreak_attn",
    )(proj3, proj3, proj3)


def _flash_kernel(*refs, moba, nb, t):
    q_tiles = refs[0].shape[1] // t

    def q_tile(sub, carry):
        q_rows = pl.ds(pl.multiple_of(sub * t, t), t)
        first = (pl.program_id(2) == 0) & (sub == 0)
        _flash_q_tile(pl.program_id(2) * q_tiles + sub, q_rows, first, *refs, moba=moba, nb=nb, t=t)
        return carry

    lax.fori_loop(0, q_tiles, q_tile, 0)


def _flash_q_tile(qi, q_rows, first, *refs, moba, nb, t):
    if moba:
        (q_ref, k_ref, v_ref, o_ref, kaug_ref, vt_ref, acc_ref, m_ref, s0_ref, s1_ref, s2_ref, s3_ref,
         bmax_ref, knorm_ref, kmean_ref, sel_ref) = refs
    else:
        (q_ref, k_ref, v_ref, e_ref, o_ref, kaug_ref, vt_ref, acc_ref, m_ref, s0_ref, s1_ref, s2_ref,
         s3_ref, bmax_ref, knorm_ref) = refs
    s_bufs = (s0_ref, s1_ref, s2_ref, s3_ref)
    n_blocks = k_ref.shape[1] // t
    hp = pl.program_id(1)
    lane = lax.broadcasted_iota(jnp.int32, (1, LANES), 1)
    streams = [(bi, hh) for bi in range(nb) for hh in range(HEADS_PER_BLOCK)]

    def piece_lanes(hh):
        e0 = _bias_lane(hh)
        return (lane >= e0) & (lane < e0 + N_BIAS_PIECES)

    head_of_lane = jnp.where(lax.broadcasted_iota(jnp.int32, (LANES, LANES), 0) >= HEAD_DIM, 1, 0)
    head_sum = jnp.where(head_of_lane == lax.broadcasted_iota(jnp.int32, (LANES, LANES), 1), 1.0, 0.0).astype(BF16)

    def head_sq_norm_max(x32):
        hi, lo = _split2(x32 * x32)
        return jnp.max(_dot(hi, head_sum) + _dot(lo, head_sum), axis=0, keepdims=True)

    def head_col(row, hh):
        return jnp.sum(jnp.where(lane == hh, row, 0.0), axis=1, keepdims=True)

    @pl.when(first)
    def _build():
        bmax_ref[...] = jnp.full(bmax_ref.shape, NEG_INF, F32)
        knorm_ref[...] = jnp.zeros(knorm_ref.shape, F32)
        tail = jnp.where(lax.broadcasted_iota(jnp.int32, (V_ROWS - HEAD_DIM, t), 0) == 0, 1.0, 0.0)

        def chunk(c, carry):
            start = pl.multiple_of(c * t, t)
            rows = pl.ds(start, t)
            extras = []
            if moba:
                pos = (lax.broadcasted_iota(jnp.int32, (t, LANES), 0) + start).astype(F32)
                for hh in range(HEADS_PER_BLOCK):
                    h = hp * HEADS_PER_BLOCK + hh
                    slope = pltpu.bitcast(jnp.full((1, LANES), (126 - h) << 23, jnp.int32), F32)
                    extra = jnp.zeros((t, LANES), F32)
                    for i, piece in enumerate(_split3(pos * slope * LOG2E)):
                        extra = jnp.where(lane == _bias_lane(hh) + i, piece.astype(F32), extra)
                    extras.append(extra.astype(BF16))
            for bi in range(nb):
                k2 = k_ref[bi, rows, :]
                k32 = k2.astype(F32)
                v2t = v_ref[bi, rows, :].astype(F32).T
                if moba:
                    kmean_ref[bi, pl.ds(c, 1), :] = jnp.sum(k32, axis=0, keepdims=True) * (1.0 / t)
                for hh in range(HEADS_PER_BLOCK):
                    st = bi * HEADS_PER_BLOCK + hh
                    extra = extras[hh] if moba else e_ref[bi, rows, :]
                    kaug_ref[st, rows, :] = jnp.where(_head_lane_mask(hh), k2, extra)
                    v_h = v2t[hh * HEAD_DIM:(hh + 1) * HEAD_DIM, :]
                    vt_ref[st, :, rows] = jnp.concatenate([v_h, tail], axis=0).astype(BF16)
                    piece_max = jnp.max(extra.astype(F32), axis=0, keepdims=True)
                    bias_max = jnp.sum(jnp.where(piece_lanes(hh), piece_max, 0.0), axis=1, keepdims=True)
                    run = jnp.maximum(bmax_ref[st, pl.ds(jnp.maximum(c - 1, 0), 1), :], bias_max)
                    bmax_ref[st, pl.ds(c, 1), :] = run
                knorm_ref[bi] = jnp.maximum(knorm_ref[bi], head_sq_norm_max(k32))
            return carry

        lax.fori_loop(0, n_blocks, chunk, 0)

    q_aug = []
    reach = []
    for bi in range(nb):
        q2 = q_ref[bi, q_rows, :]
        qk_sq = head_sq_norm_max(q2.astype(F32)) * knorm_ref[bi]
        for hh in range(HEADS_PER_BLOCK):
            ones = jnp.where(piece_lanes(hh), 1.0, 0.0).astype(BF16)
            q_aug.append(_transposed(jnp.where(_head_lane_mask(hh), q2, jnp.broadcast_to(ones, q2.shape))))
            reach.append(jnp.sqrt(head_col(qk_sq, hh)) * NORM_SLACK)

    if moba:
        blk = lax.broadcasted_iota(jnp.int32, (n_blocks, 1), 0)
        blk_f = blk.astype(F32)
        for st, (bi, hh) in enumerate(streams):
            km = jnp.where(_head_lane_mask(hh), kmean_ref[bi], 0.0)
            pieces = _dot(jnp.concatenate(_split3(km), axis=0), q_aug[st])
            route = pieces[:n_blocks] + pieces[n_blocks:2 * n_blocks] + pieces[2 * n_blocks:]
            route = jnp.where(blk < qi, route, NEG_INF)
            sel = jnp.zeros((n_blocks, t), F32)
            for _ in range(MOBA_TOPK):
                top = jnp.max(route, axis=0, keepdims=True)
                first = jnp.min(jnp.where(route == top, blk_f, float(n_blocks)), axis=0, keepdims=True)
                hit = blk_f == first
                sel = jnp.where(hit, 1.0, sel)
                route = jnp.where(hit, -jnp.inf, route)
            sel_ref[st] = jnp.where(blk < qi, sel, 0.0)

    valid = _tile_rel(t) <= 0

    all_streams = range(len(streams))
    stream_groups = [range(g, g + HEADS_PER_BLOCK) for g in range(0, len(streams), HEADS_PER_BLOCK)]

    def scores(s_ref, j, group=all_streams):
        rows = pl.ds(pl.multiple_of(j * t, t), t)
        for st in group:
            s_ref[st] = _dot(kaug_ref[st, rows, :], q_aug[st])

    def accumulate(s_ref, j, diag, group=all_streams):
        rows = pl.ds(pl.multiple_of(j * t, t), t)
        for st in group:
            s_t = s_ref[st]
            if diag:
                s_t = jnp.where(valid, s_t, NEG_INF)
            m_old = m_ref[st]
            m_new = jnp.maximum(m_old, jnp.max(s_t, axis=0, keepdims=True))
            m_sub = m_new
            if moba and not diag:
                picked = sel_ref[st, pl.ds(j, 1), :] > 0.0
                m_new = jnp.where(picked, m_new, m_old)
                m_sub = jnp.where(picked, m_new, -NEG_INF)
            alpha = jnp.exp2(m_old - m_new)
            p = jnp.exp2(s_t - m_sub)
            acc_ref[st] = alpha * acc_ref[st] + _dot(vt_ref[st, :, rows], p.astype(BF16))
            m_ref[st] = m_new

    def tile_at(dist):
        return jnp.maximum(qi - dist, 0)

    m_ref[...] = jnp.full(m_ref.shape, NEG_INF, F32)
    acc_ref[...] = jnp.zeros(acc_ref.shape, F32)
    scores(s_bufs[0], qi)
    for group in stream_groups:
        scores(s_bufs[1], tile_at(1), group)
        accumulate(s_bufs[0], qi, True, group)
        scores(s_bufs[2], tile_at(2), group)

    blk_col = lax.broadcasted_iota(jnp.int32, (n_blocks, 1), 0)
    fewest_dead = None
    for st in range(len(streams)):
        thr = jnp.min(m_ref[st], axis=1, keepdims=True) - FLASH_SKIP - reach[st]
        dead = (bmax_ref[st] < thr) & (blk_col < qi)
        n_dead = jnp.sum(jnp.where(dead, 1.0, 0.0), axis=0, keepdims=True)
        fewest_dead = n_dead if fewest_dead is None else jnp.minimum(fewest_dead, n_dead)
    n_past = qi - jnp.max(fewest_dead).astype(jnp.int32)

    def pair(first, cur, nxt):
        for k in range(2):
            for group in stream_groups:
                scores(s_bufs[nxt[k]], tile_at(first + 2 + k), group)
                accumulate(s_bufs[cur[k]], qi - first - k, False, group)

    def body(p, carry):
        first = 2 * p + 1

        @pl.when(p % 2 == 0)
        def _():
            pair(first, (1, 2), (3, 0))

        @pl.when(p % 2 == 1)
        def _():
            pair(first, (3, 0), (1, 2))

        return carry

    lax.fori_loop(0, n_past // 2, body, 0)

    for slot in (1, 3):
        @pl.when(n_past % 4 == slot)
        def _(slot=slot):
            accumulate(s_bufs[slot], qi - n_past, False)

    for bi in range(nb):
        halves = []
        for hh in range(HEADS_PER_BLOCK):
            acc = acc_ref[bi * HEADS_PER_BLOCK + hh]
            halves.append(acc[:HEAD_DIM] * (1.0 / acc[HEAD_DIM:HEAD_DIM + 1]))
        o_ref[bi, q_rows, :] = jnp.concatenate(halves, axis=0).T.astype(o_ref.dtype)


def _flash_attention(proj3, key_bias3, d, moba):
    b, s, _ = proj3.shape
    t = ATT_TILE
    nb = MIXER_BATCH if b % MIXER_BATCH == 0 else 1
    n_blocks = s // t
    n_streams = nb * HEADS_PER_BLOCK
    q_rows = min(Q_TILES_PER_STEP * t, s)
    in_specs, out_spec = _mixer_specs(2 if moba else 1, nb, q_rows, s, d)
    operands = [proj3, proj3, proj3]
    scratch = [pltpu.VMEM((n_streams, s, LANES), BF16),
               pltpu.VMEM((n_streams, V_ROWS, s), BF16),
               pltpu.VMEM((n_streams, V_ROWS, t), F32),
               pltpu.VMEM((n_streams, 1, t), F32),
               pltpu.VMEM((n_streams, t, t), F32),
               pltpu.VMEM((n_streams, t, t), F32),
               pltpu.VMEM((n_streams, t, t), F32),
               pltpu.VMEM((n_streams, t, t), F32),
               pltpu.VMEM((n_streams, n_blocks, 1), F32),
               pltpu.VMEM((nb, 1, LANES), F32)]
    if moba:
        scratch += [pltpu.VMEM((nb, n_blocks, LANES), F32),
                    pltpu.VMEM((n_streams, n_blocks, t), F32)]
    else:
        in_specs.append(pl.BlockSpec((nb, s, LANES), lambda bg, hp, qi: (bg, 0, hp)))
        operands.append(key_bias3)
    return pl.pallas_call(
        functools.partial(_flash_kernel, moba=moba, nb=nb, t=t),
        grid=(b // nb, N_HEAD_BLOCKS, s // q_rows),
        in_specs=in_specs,
        out_specs=out_spec,
        out_shape=jax.ShapeDtypeStruct((b, s, BRANCH_WIDTH), BF16),
        scratch_shapes=scratch,
        compiler_params=_cparams(("parallel", "parallel", "arbitrary")),
        name="moba_attn" if moba else "forgetting_attn",
    )(*operands)


def _mix_kernel(x_ref, oa_ref, ob_ref, oc_ref, g_ref, wb_ref, wo_ref, y_ref):
    d = x_ref.shape[1]
    mixed = None
    for n, o_ref in enumerate((oa_ref, ob_ref, oc_ref)):
        gate = jax.nn.sigmoid(g_ref[:, n * d:(n + 1) * d].astype(F32))
        term = gate * _dot(o_ref[...], wb_ref[n])
        mixed = term if mixed is None else mixed + term
    y_ref[...] = x_ref[...] + _dot(mixed.astype(BF16), wo_ref[...])


def _mix(x, o_a, o_b, o_c, proj, wb, wo, layer, tm):
    n, d = x.shape
    o_spec = pl.BlockSpec((tm, BRANCH_WIDTH), lambda i: (i, 0))
    return pl.pallas_call(
        _mix_kernel,
        grid=(n // tm,),
        in_specs=[
            pl.BlockSpec((tm, d), lambda i: (i, 0)),
            o_spec, o_spec, o_spec,
            pl.BlockSpec((tm, N_BRANCH * d), lambda i: (i, 0)),
            pl.BlockSpec((None, N_BRANCH, BRANCH_WIDTH, d), lambda i: (layer, 0, 0, 0)),
            pl.BlockSpec((None, d, d), lambda i: (layer, 0, 0)),
        ],
        out_specs=pl.BlockSpec((tm, d), lambda i: (i, 0)),
        out_shape=jax.ShapeDtypeStruct((n, d), F32),
        compiler_params=_cparams(("parallel",)),
        name="gated_mix_out_proj",
    )(x, o_a, o_b, o_c, proj, wb, wo)


def _mlp_kernel(x_ref, g_ref, wu_ref, wd_ref, gf_ref, y_ref, h_ref, acc_ref, *, final_norm):
    f = pl.program_id(1)

    @pl.when(f == 0)
    def _():
        h_ref[...] = _rms(x_ref[...], g_ref[...]).astype(BF16)
        acc_ref[...] = jnp.zeros_like(acc_ref)

    hid = jnp.square(jnp.maximum(_dot(h_ref[...], wu_ref[...]), 0.0))
    acc_ref[...] += _dot(hid.astype(BF16), wd_ref[...])

    @pl.when(f == pl.num_programs(1) - 1)
    def _():
        y = x_ref[...] + acc_ref[...]
        if final_norm:
            y = _rms(y, gf_ref[...])
        y_ref[...] = y


def _mlp(x, g, wu, wd, g_final, layer, final_norm, tm, tf):
    n, d = x.shape
    d_ff = wu.shape[2]
    return pl.pallas_call(
        functools.partial(_mlp_kernel, final_norm=final_norm),
        grid=(n // tm, d_ff // tf),
        in_specs=[
            pl.BlockSpec((tm, d), lambda i, f: (i, 0)),
            pl.BlockSpec((1, d), lambda i, f: (0, 0)),
            pl.BlockSpec((None, d, tf), lambda i, f: (layer, 0, f)),
            pl.BlockSpec((None, tf, d), lambda i, f: (layer, f, 0)),
            pl.BlockSpec((1, d), lambda i, f: (0, 0)),
        ],
        out_specs=pl.BlockSpec((tm, d), lambda i, f: (i, 0)),
        out_shape=jax.ShapeDtypeStruct((n, d), F32),
        scratch_shapes=[pltpu.VMEM((tm, d), BF16), pltpu.VMEM((tm, d), F32)],
        compiler_params=_cparams(("parallel", "arbitrary")),
        name="relu2_mlp",
    )(x, g, wu, wd, g_final)


REPACK_ROWS = 512


def _repack_kernel(w_ref, wf_ref, main_ref, f_ref):
    i = pl.program_id(1)
    q_chunks = BRANCH_WIDTH // REPACK_ROWS
    n_gate_chunks = N_BRANCH * w_ref.shape[2] // REPACK_ROWS
    mixer_chunk = jnp.maximum(i - n_gate_chunks, 0) % (3 * q_chunks)
    is_q = (i >= n_gate_chunks) & (mixer_chunk < q_chunks)
    scale = jnp.where(is_q, HEAD_DIM ** -0.5 * LOG2E, 1.0)
    main_ref[...] = (w_ref[0] * scale).astype(BF16)

    @pl.when(i == 0)
    def _():
        pad = jnp.zeros((f_ref.shape[0] - N_HEADS, f_ref.shape[1]), F32)
        f_ref[...] = jnp.concatenate([wf_ref[0], pad], axis=0).astype(BF16)


def _prep_in_proj(w_in):
    depth, d, cols = w_in.shape
    w_t = jnp.swapaxes(w_in, 1, 2)
    bw = BRANCH_WIDTH
    r = REPACK_ROWS
    f_lo = 6 * bw
    c_lo = f_lo + N_HEADS
    g_lo = c_lo + 3 * bw
    main_rows = cols - N_HEADS
    n_gate = (cols - g_lo) // r
    n_ab = f_lo // r

    def src_row(l, i):
        row = jnp.where(i < n_gate, g_lo + r * i,
                        jnp.where(i < n_gate + n_ab, r * (i - n_gate), c_lo + r * (i - n_gate - n_ab)))
        return l, pl.multiple_of(row, N_HEADS), 0

    return pl.pallas_call(
        _repack_kernel,
        grid=(depth, main_rows // r),
        in_specs=[pl.BlockSpec((pl.Element(1), pl.Element(r), pl.Element(d)), src_row),
                  pl.BlockSpec((pl.Element(1), pl.Element(N_HEADS), pl.Element(d)), lambda l, i: (l, f_lo, 0))],
        out_specs=[pl.BlockSpec((None, r, d), lambda l, i: (l, i, 0)),
                   pl.BlockSpec((None, LANES, d), lambda l, i: (l, 0, 0))],
        out_shape=[jax.ShapeDtypeStruct((depth, main_rows, d), BF16),
                   jax.ShapeDtypeStruct((depth, LANES, d), BF16)],
        compiler_params=_cparams(("parallel", "arbitrary")),
        name="repack_w_in",
    )(w_t, w_t)


def kernel(x, norm_mix, w_in, b_forget, w_branch, w_out, norm_mlp, w_up, w_down, norm_final):
    b, s, d = x.shape
    depth = w_in.shape[0]
    n = b * s
    assert s % ATT_TILE == 0 and d % LANES == 0

    w_main, w_f = _prep_in_proj(w_in)
    wb = w_branch.astype(BF16)
    wo = w_out.astype(BF16)
    wu = w_up.astype(BF16)
    wd = w_down.astype(BF16)
    b_f = jnp.pad(b_forget, ((0, 0), (0, LANES - N_HEADS)))[:, None, :]

    tm = min(1024, s)
    xf = x.reshape(n, d)
    for l in range(depth):
        g_mix = norm_mix[l][None, :]
        proj, key_bias = _in_proj(xf, g_mix, w_main, w_f, b_f[l], l, s, tm, 3840, 256)
        proj3 = proj.reshape(b, s, -1)
        o_a = _sb_attention(proj3, d).reshape(n, -1)
        o_b = _flash_attention(proj3, key_bias.reshape(b, s, -1), d, False).reshape(n, -1)
        o_c = _flash_attention(proj3, None, d, True).reshape(n, -1)
        xf = _mix(xf, o_a, o_b, o_c, proj, wb, wo, l, min(512, n))
        xf = _mlp(xf, norm_mlp[l][None, :], wu, wd, norm_final[None, :],
                  l, l == depth - 1, min(1024, n), 1024)
    return xf.reshape(b, s, d)
```

```python
import functools
import math

import jax
import jax.numpy as jnp
import numpy as np
from jax import lax
from jax.experimental import pallas as pl
from jax.experimental.pallas import tpu as pltpu

F32 = jnp.float32
BF16 = jnp.bfloat16

HEAD_DIM = 64
N_HEADS = 8
BRANCH_WIDTH = N_HEADS * HEAD_DIM
N_BRANCH = 3
MOBA_BLOCK = 256
MOBA_TOPK = 3
RMS_EPS = 1e-6
NEG_INF = -1e30
LOG2E = math.log2(math.e)
LANES = 128
HEADS_PER_BLOCK = LANES // HEAD_DIM
N_HEAD_BLOCKS = BRANCH_WIDTH // LANES
N_BIAS_PIECES = 3
ATT_TILE = MOBA_BLOCK
Q_TILES_PER_STEP = 4
MIXER_BATCH = 4
SB_EXIT = 160.0
FLASH_SKIP = 160.0
SP2_CLAMP = 120.0
NORM_SLACK = 1.001
V_ROWS = HEAD_DIM + 16
VMEM_LIMIT = 60000 * 1024


def _cparams(sem):
    return pltpu.CompilerParams(dimension_semantics=sem, vmem_limit_bytes=VMEM_LIMIT)


def _rms(x, g):
    ms = jnp.mean(x * x, axis=-1, keepdims=True)
    return x * lax.rsqrt(ms + RMS_EPS) * g


def _dot(a, b):
    return jnp.dot(a, b, preferred_element_type=F32)


def _dot_nt(a, b):
    return lax.dot_general(a, b, (((1,), (1,)), ((), ())), preferred_element_type=F32)


def _transposed(x):
    return x.astype(F32).T.astype(BF16)


def _split2(x):
    hi = x.astype(BF16)
    lo = (x - hi.astype(F32)).astype(BF16)
    return hi, lo


def _split3(x):
    hi = x.astype(BF16)
    r = x - hi.astype(F32)
    mid = r.astype(BF16)
    lo = (r - mid.astype(F32)).astype(BF16)
    return hi, mid, lo


def _bias_lane(hh):
    return (1 - hh) * HEAD_DIM


def _in_proj_kernel(x_ref, g_ref, w_ref, wf_ref, b_ref, tri_ref, place_ref,
                    o_ref, e_ref, h_ref, carry_ref, *, tiles_per_seq):
    @pl.when(pl.program_id(1) == 0)
    def _():
        h = _rms(x_ref[...], g_ref[...]).astype(BF16)
        h_ref[...] = h

        @pl.when(pl.program_id(0) % tiles_per_seq == 0)
        def _():
            carry_ref[...] = jnp.zeros_like(carry_ref)

        y = _dot_nt(h, wf_ref[...]) + b_ref[...]
        logf = jnp.minimum(y, 0.0) - jnp.log(1.0 + jnp.exp(-jnp.abs(y)))
        pieces = _split3(logf)
        tri = tri_ref[...]
        tc = tri.shape[0]
        cum = carry_ref[...]
        for c in range(h.shape[0] // tc):
            rows = slice(c * tc, (c + 1) * tc)
            cum = cum[-1:, :]
            for piece in pieces:
                cum = cum + _dot(tri, piece[rows])
            bias_pieces = jnp.concatenate(_split3(cum * (-LOG2E)), axis=1)
            e_ref[rows, :] = _dot(bias_pieces, place_ref[...]).astype(BF16)
        carry_ref[...] = cum[-1:, :]

    o_ref[...] = _dot_nt(h_ref[...], w_ref[...]).astype(o_ref.dtype)


def _bias_placement():
    place = np.zeros((N_BIAS_PIECES, LANES, N_HEAD_BLOCKS * LANES), np.float32)
    for h in range(N_HEADS):
        hp, hh = divmod(h, HEADS_PER_BLOCK)
        for i in range(N_BIAS_PIECES):
            place[i, h, hp * LANES + _bias_lane(hh) + i] = 1.0
    return jnp.asarray(place.reshape(N_BIAS_PIECES * LANES, -1), BF16)


def _in_proj(x, g, w, w_f, b_f, layer, s, tm, tn, tc):
    n, d = x.shape
    cols = w.shape[1]
    tri = jnp.asarray(np.tril(np.ones((tc, tc), np.float32)), BF16)
    place = _bias_placement()
    e_cols = place.shape[1]
    const2 = lambda i, j: (0, 0)
    return pl.pallas_call(
        functools.partial(_in_proj_kernel, tiles_per_seq=s // tm),
        grid=(n // tm, cols // tn),
        in_specs=[
            pl.BlockSpec((tm, d), lambda i, j: (i, 0)),
            pl.BlockSpec((1, d), const2),
            pl.BlockSpec((None, tn, d), lambda i, j: (layer, j, 0)),
            pl.BlockSpec((None, LANES, d), lambda i, j: (layer, 0, 0)),
            pl.BlockSpec((1, LANES), const2),
            pl.BlockSpec((tc, tc), const2),
            pl.BlockSpec(place.shape, const2),
        ],
        out_specs=[pl.BlockSpec((tm, tn), lambda i, j: (i, j)),
                   pl.BlockSpec((tm, e_cols), lambda i, j: (i, 0))],
        out_shape=[jax.ShapeDtypeStruct((n, cols), BF16),
                   jax.ShapeDtypeStruct((n, e_cols), BF16)],
        scratch_shapes=[pltpu.VMEM((tm, d), BF16), pltpu.VMEM((1, LANES), F32)],
        compiler_params=_cparams(("arbitrary", "arbitrary")),
        name="norm_in_proj",
    )(x, g, w, w_f, b_f, tri, place)


def _head_lane_mask(hh):
    lane = lax.broadcasted_iota(jnp.int32, (1, LANES), 1)
    return (lane >= hh * HEAD_DIM) & (lane < (hh + 1) * HEAD_DIM)


def _tile_rel(t):
    return (lax.broadcasted_iota(jnp.int32, (t, t), 0)
            - lax.broadcasted_iota(jnp.int32, (t, t), 1))


def _mixer_specs(mixer, nb, t, s, d):
    qoff = N_BRANCH * d // LANES + (3 * mixer) * N_HEAD_BLOCKS
    koff = qoff + N_HEAD_BLOCKS
    voff = koff + N_HEAD_BLOCKS
    return [
        pl.BlockSpec((nb, t, LANES), lambda bg, hp, qi: (bg, qi, qoff + hp)),
        pl.BlockSpec((nb, s, LANES), lambda bg, hp, qi: (bg, 0, koff + hp)),
        pl.BlockSpec((nb, s, LANES), lambda bg, hp, qi: (bg, 0, voff + hp)),
    ], pl.BlockSpec((nb, t, LANES), lambda bg, hp, qi: (bg, qi, hp))


def _sb_kernel(q_ref, k_ref, v_ref, o_ref, vt_ref, *scratch, nb, t):
    n_blocks = k_ref.shape[1] // t
    q_tiles = q_ref.shape[1] // t
    streams = [(bi, hh) for bi in range(nb) for hh in range(HEADS_PER_BLOCK)]

    @pl.when(pl.program_id(2) == 0)
    def _build():
        def chunk(c, carry):
            rows = pl.ds(pl.multiple_of(c * t, t), t)
            for st, (bi, hh) in enumerate(streams):
                v2t = v_ref[bi, rows, :].astype(F32).T
                vt_ref[st, :, rows] = v2t[hh * HEAD_DIM:(hh + 1) * HEAD_DIM, :].astype(BF16)
            return carry

        lax.fori_loop(0, n_blocks, chunk, 0)

    def q_tile(sub, carry):
        q_rows = pl.ds(pl.multiple_of(sub * t, t), t)
        _sb_q_tile(pl.program_id(2) * q_tiles + sub, q_rows, q_ref, k_ref, o_ref, vt_ref, *scratch, nb=nb, t=t)
        return carry

    lax.fori_loop(0, q_tiles, q_tile, 0)


def _sb_q_tile(qi, q_rows, q_ref, k_ref, o_ref, vt_ref, acc_ref, cs_ref, r_ref, s_even_ref, s_odd_ref, *, nb, t):
    streams = [(bi, hh) for bi in range(nb) for hh in range(HEADS_PER_BLOCK)]
    q_m = []
    for bi, hh in streams:
        q2 = q_ref[bi, q_rows, :]
        q_m.append(_transposed(jnp.where(_head_lane_mask(hh), q2, jnp.zeros_like(q2))))

    rel = _tile_rel(t)
    strictly_past = rel < 0
    upper = jnp.where(rel <= 0, 1.0, 0.0).astype(BF16)

    stream_groups = [range(g, g + HEADS_PER_BLOCK) for g in range(0, len(streams), HEADS_PER_BLOCK)]

    def scores(s_ref, j, group):
        rows = pl.ds(pl.multiple_of(j * t, t), t)
        for st in group:
            s_ref[st] = _dot(k_ref[streams[st][0], rows, :], q_m[st])

    def softplus_sums(s_ref, diag, group):
        for st in group:
            z = s_ref[st]
            sp = jnp.maximum(z, jnp.log2(1.0 + jnp.exp2(jnp.minimum(z, SP2_CLAMP))))
            if diag:
                sp = jnp.where(strictly_past, sp, 0.0)
            r = _dot(upper, sp.astype(BF16)) + cs_ref[st]
            r_ref[st] = r
            cs_ref[st] = r[0:1, :]

    def weigh_values(s_ref, j, diag, group):
        rows = pl.ds(pl.multiple_of(j * t, t), t)
        for st in group:
            w = jnp.exp2(s_ref[st] - r_ref[st])
            if diag:
                w = jnp.where(strictly_past, w, 0.0)
            acc_ref[st] += _dot(vt_ref[st, :, rows], w.astype(BF16))

    def step(cur_ref, next_ref, j, j_next, diag):
        for group in stream_groups:
            scores(next_ref, j_next, group)
            softplus_sums(cur_ref, diag, group)
        for group in stream_groups:
            weigh_values(cur_ref, j, diag, group)

    def next_distance(i):
        lowest = cs_ref[0]
        for st in range(1, len(streams)):
            lowest = jnp.minimum(lowest, cs_ref[st])
        return jnp.where(jnp.min(lowest) > SB_EXIT, qi + 1, i + 1)

    acc_ref[...] = jnp.zeros(acc_ref.shape, F32)
    cs_ref[...] = jnp.zeros(cs_ref.shape, F32)
    for group in stream_groups:
        scores(s_even_ref, qi, group)
    step(s_even_ref, s_odd_ref, qi, jnp.maximum(qi - 1, 0), True)

    def body(i):
        j = qi - i
        j_next = jnp.maximum(j - 1, 0)

        @pl.when(i % 2 == 1)
        def _():
            step(s_odd_ref, s_even_ref, j, j_next, False)

        @pl.when(i % 2 == 0)
        def _():
            step(s_even_ref, s_odd_ref, j, j_next, False)

        return next_distance(i)

    lax.while_loop(lambda i: i <= qi, body, next_distance(0))

    for bi in range(nb):
        halves = [acc_ref[bi * HEADS_PER_BLOCK + hh] for hh in range(HEADS_PER_BLOCK)]
        o_ref[bi, q_rows, :] = jnp.concatenate(halves, axis=0).T.astype(o_ref.dtype)


def _sb_attention(proj3, d):
    b, s, _ = proj3.shape
    t = ATT_TILE
    nb = MIXER_BATCH if b % MIXER_BATCH == 0 else 1
    n_streams = nb * HEADS_PER_BLOCK
    q_rows = min(Q_TILES_PER_STEP * t, s)
    in_specs, out_spec = _mixer_specs(0, nb, q_rows, s, d)
    return pl.pallas_call(
        functools.partial(_sb_kernel, nb=nb, t=t),
        grid=(b // nb, N_HEAD_BLOCKS, s // q_rows),
        in_specs=in_specs,
        out_specs=out_spec,
        out_shape=jax.ShapeDtypeStruct((b, s, BRANCH_WIDTH), BF16),
        scratch_shapes=[pltpu.VMEM((n_streams, HEAD_DIM, s), BF16),
                        pltpu.VMEM((n_streams, HEAD_DIM, t), F32),
                        pltpu.VMEM((n_streams, 1, t), F32),
                        pltpu.VMEM((n_streams, t, t), F32),
                        pltpu.VMEM((n_streams, t, t), F32),
                        pltpu.VMEM((n_streams, t, t), F32)],
        compiler_params=_cparams(("parallel", "parallel", "arbitrary")),
        name="stickbreak_attn",
    )(proj3, proj3, proj3)


def _flash_kernel(*refs, moba, nb, t):
    q_tiles = refs[0].shape[1] // t

    def q_tile(sub, carry):
        q_rows = pl.ds(pl.multiple_of(sub * t, t), t)
        first = (pl.program_id(2) == 0) & (sub == 0)
        _flash_q_tile(pl.program_id(2) * q_tiles + sub, q_rows, first, *refs, moba=moba, nb=nb, t=t)
        return carry

    lax.fori_loop(0, q_tiles, q_tile, 0)


def _flash_q_tile(qi, q_rows, first, *refs, moba, nb, t):
    if moba:
        (q_ref, k_ref, v_ref, o_ref, kaug_ref, vt_ref, acc_ref, m_ref, s0_ref, s1_ref, s2_ref, s3_ref,
         bmax_ref, knorm_ref, kmean_ref, sel_ref) = refs
    else:
        (q_ref, k_ref, v_ref, e_ref, o_ref, kaug_ref, vt_ref, acc_ref, m_ref, s0_ref, s1_ref, s2_ref,
         s3_ref, bmax_ref, knorm_ref) = refs
    s_bufs = (s0_ref, s1_ref, s2_ref, s3_ref)
    n_blocks = k_ref.shape[1] // t
    hp = pl.program_id(1)
    lane = lax.broadcasted_iota(jnp.int32, (1, LANES), 1)
    streams = [(bi, hh) for bi in range(nb) for hh in range(HEADS_PER_BLOCK)]

    def piece_lanes(hh):
        e0 = _bias_lane(hh)
        return (lane >= e0) & (lane < e0 + N_BIAS_PIECES)

    head_of_lane = jnp.where(lax.broadcasted_iota(jnp.int32, (LANES, LANES), 0) >= HEAD_DIM, 1, 0)
    head_sum = jnp.where(head_of_lane == lax.broadcasted_iota(jnp.int32, (LANES, LANES), 1), 1.0, 0.0).astype(BF16)

    def head_sq_norm_max(x32):
        hi, lo = _split2(x32 * x32)
        return jnp.max(_dot(hi, head_sum) + _dot(lo, head_sum), axis=0, keepdims=True)

    def head_col(row, hh):
        return jnp.sum(jnp.where(lane == hh, row, 0.0), axis=1, keepdims=True)

    @pl.when(first)
    def _build():
        bmax_ref[...] = jnp.full(bmax_ref.shape, NEG_INF, F32)
        knorm_ref[...] = jnp.zeros(knorm_ref.shape, F32)
        tail = jnp.where(lax.broadcasted_iota(jnp.int32, (V_ROWS - HEAD_DIM, t), 0) == 0, 1.0, 0.0)

        def chunk(c, carry):
            start = pl.multiple_of(c * t, t)
            rows = pl.ds(start, t)
            extras = []
            if moba:
                pos = (lax.broadcasted_iota(jnp.int32, (t, LANES), 0) + start).astype(F32)
                for hh in range(HEADS_PER_BLOCK):
                    h = hp * HEADS_PER_BLOCK + hh
                    slope = pltpu.bitcast(jnp.full((1, LANES), (126 - h) << 23, jnp.int32), F32)
                    extra = jnp.zeros((t, LANES), F32)
                    for i, piece in enumerate(_split3(pos * slope * LOG2E)):
                        extra = jnp.where(lane == _bias_lane(hh) + i, piece.astype(F32), extra)
                    extras.append(extra.astype(BF16))
            for bi in range(nb):
                k2 = k_ref[bi, rows, :]
                k32 = k2.astype(F32)
                v2t = v_ref[bi, rows, :].astype(F32).T
                if moba:
                    kmean_ref[bi, pl.ds(c, 1), :] = jnp.sum(k32, axis=0, keepdims=True) * (1.0 / t)
                for hh in range(HEADS_PER_BLOCK):
                    st = bi * HEADS_PER_BLOCK + hh
                    extra = extras[hh] if moba else e_ref[bi, rows, :]
                    kaug_ref[st, rows, :] = jnp.where(_head_lane_mask(hh), k2, extra)
                    v_h = v2t[hh * HEAD_DIM:(hh + 1) * HEAD_DIM, :]
                    vt_ref[st, :, rows] = jnp.concatenate([v_h, tail], axis=0).astype(BF16)
                    piece_max = jnp.max(extra.astype(F32), axis=0, keepdims=True)
                    bias_max = jnp.sum(jnp.where(piece_lanes(hh), piece_max, 0.0), axis=1, keepdims=True)
                    run = jnp.maximum(bmax_ref[st, pl.ds(jnp.maximum(c - 1, 0), 1), :], bias_max)
                    bmax_ref[st, pl.ds(c, 1), :] = run
                knorm_ref[bi] = jnp.maximum(knorm_ref[bi], head_sq_norm_max(k32))
            return carry

        lax.fori_loop(0, n_blocks, chunk, 0)

    q_aug = []
    reach = []
    for bi in range(nb):
        q2 = q_ref[bi, q_rows, :]
        qk_sq = head_sq_norm_max(q2.astype(F32)) * knorm_ref[bi]
        for hh in range(HEADS_PER_BLOCK):
            ones = jnp.where(piece_lanes(hh), 1.0, 0.0).astype(BF16)
            q_aug.append(_transposed(jnp.where(_head_lane_mask(hh), q2, jnp.broadcast_to(ones, q2.shape))))
            reach.append(jnp.sqrt(head_col(qk_sq, hh)) * NORM_SLACK)

    if moba:
        blk = lax.broadcasted_iota(jnp.int32, (n_blocks, 1), 0)
        blk_f = blk.astype(F32)
        for st, (bi, hh) in enumerate(streams):
            km = jnp.where(_head_lane_mask(hh), kmean_ref[bi], 0.0)
            pieces = _dot(jnp.concatenate(_split3(km), axis=0), q_aug[st])
            route = pieces[:n_blocks] + pieces[n_blocks:2 * n_blocks] + pieces[2 * n_blocks:]
            route = jnp.where(blk < qi, route, NEG_INF)
            sel = jnp.zeros((n_blocks, t), F32)
            for _ in range(MOBA_TOPK):
                top = jnp.max(route, axis=0, keepdims=True)
                first = jnp.min(jnp.where(route == top, blk_f, float(n_blocks)), axis=0, keepdims=True)
                hit = blk_f == first
                sel = jnp.where(hit, 1.0, sel)
                route = jnp.where(hit, -jnp.inf, route)
            sel_ref[st] = jnp.where(blk < qi, sel, 0.0)

    valid = _tile_rel(t) <= 0

    all_streams = range(len(streams))
    stream_groups = [range(g, g + HEADS_PER_BLOCK) for g in range(0, len(streams), HEADS_PER_BLOCK)]

    def scores(s_ref, j, group=all_streams):
        rows = pl.ds(pl.multiple_of(j * t, t), t)
        for st in group:
            s_ref[st] = _dot(kaug_ref[st, rows, :], q_aug[st])

    def accumulate(s_ref, j, diag, group=all_streams):
        rows = pl.ds(pl.multiple_of(j * t, t), t)
        for st in group:
            s_t = s_ref[st]
            if diag:
                s_t = jnp.where(valid, s_t, NEG_INF)
            m_old = m_ref[st]
            m_new = jnp.maximum(m_old, jnp.max(s_t, axis=0, keepdims=True))
            m_sub = m_new
            if moba and not diag:
                picked = sel_ref[st, pl.ds(j, 1), :] > 0.0
                m_new = jnp.where(picked, m_new, m_old)
                m_sub = jnp.where(picked, m_new, -NEG_INF)
            alpha = jnp.exp2(m_old - m_new)
            p = jnp.exp2(s_t - m_sub)
            acc_ref[st] = alpha * acc_ref[st] + _dot(vt_ref[st, :, rows], p.astype(BF16))
            m_ref[st] = m_new

    def tile_at(dist):
        return jnp.maximum(qi - dist, 0)

    m_ref[...] = jnp.full(m_ref.shape, NEG_INF, F32)
    acc_ref[...] = jnp.zeros(acc_ref.shape, F32)
    scores(s_bufs[0], qi)
    for group in stream_groups:
        scores(s_bufs[1], tile_at(1), group)
        accumulate(s_bufs[0], qi, True, group)
        scores(s_bufs[2], tile_at(2), group)

    blk_col = lax.broadcasted_iota(jnp.int32, (n_blocks, 1), 0)
    fewest_dead = None
    for st in range(len(streams)):
        thr = jnp.min(m_ref[st], axis=1, keepdims=True) - FLASH_SKIP - reach[st]
        dead = (bmax_ref[st] < thr) & (blk_col < qi)
        n_dead = jnp.sum(jnp.where(dead, 1.0, 0.0), axis=0, keepdims=True)
        fewest_dead = n_dead if fewest_dead is None else jnp.minimum(fewest_dead, n_dead)
    n_past = qi - jnp.max(fewest_dead).astype(jnp.int32)

    def pair(first, cur, nxt):
        for k in range(2):
            for group in stream_groups:
                scores(s_bufs[nxt[k]], tile_at(first + 2 + k), group)
                accumulate(s_bufs[cur[k]], qi - first - k, False, group)

    def body(p, carry):
        first = 2 * p + 1

        @pl.when(p % 2 == 0)
        def _():
            pair(first, (1, 2), (3, 0))

        @pl.when(p % 2 == 1)
        def _():
            pair(first, (3, 0), (1, 2))

        return carry

    lax.fori_loop(0, n_past // 2, body, 0)

    for slot in (1, 3):
        @pl.when(n_past % 4 == slot)
        def _(slot=slot):
            accumulate(s_bufs[slot], qi - n_past, False)

    for bi in range(nb):
        halves = []
        for hh in range(HEADS_PER_BLOCK):
            acc = acc_ref[bi * HEADS_PER_BLOCK + hh]
            halves.append(acc[:HEAD_DIM] * (1.0 / acc[HEAD_DIM:HEAD_DIM + 1]))
        o_ref[bi, q_rows, :] = jnp.concatenate(halves, axis=0).T.astype(o_ref.dtype)


def _flash_attention(proj3, key_bias3, d, moba):
    b, s, _ = proj3.shape
    t = ATT_TILE
    nb = MIXER_BATCH if b % MIXER_BATCH == 0 else 1
    n_blocks = s // t
    n_streams = nb * HEADS_PER_BLOCK
    q_rows = min(Q_TILES_PER_STEP * t, s)
    in_specs, out_spec = _mixer_specs(2 if moba else 1, nb, q_rows, s, d)
    operands = [proj3, proj3, proj3]
    scratch = [pltpu.VMEM((n_streams, s, LANES), BF16),
               pltpu.VMEM((n_streams, V_ROWS, s), BF16),
               pltpu.VMEM((n_streams, V_ROWS, t), F32),
               pltpu.VMEM((n_streams, 1, t), F32),
               pltpu.VMEM((n_streams, t, t), F32),
               pltpu.VMEM((n_streams, t, t), F32),
               pltpu.VMEM((n_streams, t, t), F32),
               pltpu.VMEM((n_streams, t, t), F32),
               pltpu.VMEM((n_streams, n_blocks, 1), F32),
               pltpu.VMEM((nb, 1, LANES), F32)]
    if moba:
        scratch += [pltpu.VMEM((nb, n_blocks, LANES), F32),
                    pltpu.VMEM((n_streams, n_blocks, t), F32)]
    else:
        in_specs.append(pl.BlockSpec((nb, s, LANES), lambda bg, hp, qi: (bg, 0, hp)))
        operands.append(key_bias3)
    return pl.pallas_call(
        functools.partial(_flash_kernel, moba=moba, nb=nb, t=t),
        grid=(b // nb, N_HEAD_BLOCKS, s // q_rows),
        in_specs=in_specs,
        out_specs=out_spec,
        out_shape=jax.ShapeDtypeStruct((b, s, BRANCH_WIDTH), BF16),
        scratch_shapes=scratch,
        compiler_params=_cparams(("parallel", "parallel", "arbitrary")),
        name="moba_attn" if moba else "forgetting_attn",
    )(*operands)


def _mix_kernel(x_ref, oa_ref, ob_ref, oc_ref, g_ref, wb_ref, wo_ref, y_ref):
    d = x_ref.shape[1]
    mixed = None
    for n, o_ref in enumerate((oa_ref, ob_ref, oc_ref)):
        gate = jax.nn.sigmoid(g_ref[:, n * d:(n + 1) * d].astype(F32))
        term = gate * _dot(o_ref[...], wb_ref[n])
        mixed = term if mixed is None else mixed + term
    y_ref[...] = x_ref[...] + _dot(mixed.astype(BF16), wo_ref[...])


def _mix(x, o_a, o_b, o_c, proj, wb, wo, layer, tm):
    n, d = x.shape
    o_spec = pl.BlockSpec((tm, BRANCH_WIDTH), lambda i: (i, 0))
    return pl.pallas_call(
        _mix_kernel,
        grid=(n // tm,),
        in_specs=[
            pl.BlockSpec((tm, d), lambda i: (i, 0)),
            o_spec, o_spec, o_spec,
            pl.BlockSpec((tm, N_BRANCH * d), lambda i: (i, 0)),
            pl.BlockSpec((None, N_BRANCH, BRANCH_WIDTH, d), lambda i: (layer, 0, 0, 0)),
            pl.BlockSpec((None, d, d), lambda i: (layer, 0, 0)),
        ],
        out_specs=pl.BlockSpec((tm, d), lambda i: (i, 0)),
        out_shape=jax.ShapeDtypeStruct((n, d), F32),
        compiler_params=_cparams(("parallel",)),
        name="gated_mix_out_proj",
    )(x, o_a, o_b, o_c, proj, wb, wo)


def _mlp_kernel(x_ref, g_ref, wu_ref, wd_ref, gf_ref, y_ref, h_ref, acc_ref, *, final_norm):
    f = pl.program_id(1)

    @pl.when(f == 0)
    def _():
        h_ref[...] = _rms(x_ref[...], g_ref[...]).astype(BF16)
        acc_ref[...] = jnp.zeros_like(acc_ref)

    hid = jnp.square(jnp.maximum(_dot(h_ref[...], wu_ref[...]), 0.0))
    acc_ref[...] += _dot(hid.astype(BF16), wd_ref[...])

    @pl.when(f == pl.num_programs(1) - 1)
    def _():
        y = x_ref[...] + acc_ref[...]
        if final_norm:
            y = _rms(y, gf_ref[...])
        y_ref[...] = y


def _mlp(x, g, wu, wd, g_final, layer, final_norm, tm, tf):
    n, d = x.shape
    d_ff = wu.shape[2]
    return pl.pallas_call(
        functools.partial(_mlp_kernel, final_norm=final_norm),
        grid=(n // tm, d_ff // tf),
        in_specs=[
            pl.BlockSpec((tm, d), lambda i, f: (i, 0)),
            pl.BlockSpec((1, d), lambda i, f: (0, 0)),
            pl.BlockSpec((None, d, tf), lambda i, f: (layer, 0, f)),
            pl.BlockSpec((None, tf, d), lambda i, f: (layer, f, 0)),
            pl.BlockSpec((1, d), lambda i, f: (0, 0)),
        ],
        out_specs=pl.BlockSpec((tm, d), lambda i, f: (i, 0)),
        out_shape=jax.ShapeDtypeStruct((n, d), F32),
        scratch_shapes=[pltpu.VMEM((tm, d), BF16), pltpu.VMEM((tm, d), F32)],
        compiler_params=_cparams(("parallel", "arbitrary")),
        name="relu2_mlp",
    )(x, g, wu, wd, g_final)


REPACK_ROWS = 512


def _repack_kernel(w_ref, wf_ref, main_ref, f_ref):
    i = pl.program_id(1)
    q_chunks = BRANCH_WIDTH // REPACK_ROWS
    n_gate_chunks = N_BRANCH * w_ref.shape[2] // REPACK_ROWS
    mixer_chunk = jnp.maximum(i - n_gate_chunks, 0) % (3 * q_chunks)
    is_q = (i >= n_gate_chunks) & (mixer_chunk < q_chunks)
    scale = jnp.where(is_q, HEAD_DIM ** -0.5 * LOG2E, 1.0)
    main_ref[...] = (w_ref[0] * scale).astype(BF16)

    @pl.when(i == 0)
    def _():
        pad = jnp.zeros((f_ref.shape[0] - N_HEADS, f_ref.shape[1]), F32)
        f_ref[...] = jnp.concatenate([wf_ref[0], pad], axis=0).astype(BF16)


def _prep_in_proj(w_in):
    depth, d, cols = w_in.shape
    w_t = jnp.swapaxes(w_in, 1, 2)
    bw = BRANCH_WIDTH
    r = REPACK_ROWS
    f_lo = 6 * bw
    c_lo = f_lo + N_HEADS
    g_lo = c_lo + 3 * bw
    main_rows = cols - N_HEADS
    n_gate = (cols - g_lo) // r
    n_ab = f_lo // r

    def src_row(l, i):
        row = jnp.where(i < n_gate, g_lo + r * i,
                        jnp.where(i < n_gate + n_ab, r * (i - n_gate), c_lo + r * (i - n_gate - n_ab)))
        return l, pl.multiple_of(row, N_HEADS), 0

    return pl.pallas_call(
        _repack_kernel,
        grid=(depth, main_rows // r),
        in_specs=[pl.BlockSpec((pl.Element(1), pl.Element(r), pl.Element(d)), src_row),
                  pl.BlockSpec((pl.Element(1), pl.Element(N_HEADS), pl.Element(d)), lambda l, i: (l, f_lo, 0))],
        out_specs=[pl.BlockSpec((None, r, d), lambda l, i: (l, i, 0)),
                   pl.BlockSpec((None, LANES, d), lambda l, i: (l, 0, 0))],
        out_shape=[jax.ShapeDtypeStruct((depth, main_rows, d), BF16),
                   jax.ShapeDtypeStruct((depth, LANES, d), BF16)],
        compiler_params=_cparams(("parallel", "arbitrary")),
        name="repack_w_in",
    )(w_t, w_t)


def kernel(x, norm_mix, w_in, b_forget, w_branch, w_out, norm_mlp, w_up, w_down, norm_final):
    b, s, d = x.shape
    depth = w_in.shape[0]
    n = b * s
    assert s % ATT_TILE == 0 and d % LANES == 0

    w_main, w_f = _prep_in_proj(w_in)
    wb = w_branch.astype(BF16)
    wo = w_out.astype(BF16)
    wu = w_up.astype(BF16)
    wd = w_down.astype(BF16)
    b_f = jnp.pad(b_forget, ((0, 0), (0, LANES - N_HEADS)))[:, None, :]

    tm = min(1024, s)
    xf = x.reshape(n, d)
    for l in range(depth):
        g_mix = norm_mix[l][None, :]
        proj, key_bias = _in_proj(xf, g_mix, w_main, w_f, b_f[l], l, s, tm, 3840, 256)
        proj3 = proj.reshape(b, s, -1)
        o_a = _sb_attention(proj3, d).reshape(n, -1)
        o_b = _flash_attention(proj3, key_bias.reshape(b, s, -1), d, False).reshape(n, -1)
        o_c = _flash_attention(proj3, None, d, True).reshape(n, -1)
        xf = _mix(xf, o_a, o_b, o_c, proj, wb, wo, l, min(1024, n))
        xf = _mlp(xf, norm_mlp[l][None, :], wu, wd, norm_final[None, :],
                  l, l == depth - 1, min(1024, n), 1024)
    return xf.reshape(b, s, d)
```

```python
import functools
import math

import jax
import jax.numpy as jnp
import numpy as np
from jax import lax
from jax.experimental import pallas as pl
from jax.experimental.pallas import tpu as pltpu

F32 = jnp.float32
BF16 = jnp.bfloat16

HEAD_DIM = 64
N_HEADS = 8
BRANCH_WIDTH = N_HEADS * HEAD_DIM
N_BRANCH = 3
MOBA_BLOCK = 256
MOBA_TOPK = 3
RMS_EPS = 1e-6
NEG_INF = -1e30
LOG2E = math.log2(math.e)
LANES = 128
HEADS_PER_BLOCK = LANES // HEAD_DIM
N_HEAD_BLOCKS = BRANCH_WIDTH // LANES
N_BIAS_PIECES = 3
ATT_TILE = MOBA_BLOCK
Q_TILES_PER_STEP = 4
MIXER_BATCH = 4
SB_EXIT = 152.0
FLASH_SKIP = 152.0
SP2_CLAMP = 120.0
NORM_SLACK = 1.001
V_ROWS = HEAD_DIM + 16
VMEM_LIMIT = 60000 * 1024


def _cparams(sem):
    return pltpu.CompilerParams(dimension_semantics=sem, vmem_limit_bytes=VMEM_LIMIT)


def _rms(x, g):
    ms = jnp.mean(x * x, axis=-1, keepdims=True)
    return x * lax.rsqrt(ms + RMS_EPS) * g


def _dot(a, b):
    return jnp.dot(a, b, preferred_element_type=F32)


def _dot_nt(a, b):
    return lax.dot_general(a, b, (((1,), (1,)), ((), ())), preferred_element_type=F32)


def _transposed(x):
    return x.astype(F32).T.astype(BF16)


def _split2(x):
    hi = x.astype(BF16)
    lo = (x - hi.astype(F32)).astype(BF16)
    return hi, lo


def _split3(x):
    hi = x.astype(BF16)
    r = x - hi.astype(F32)
    mid = r.astype(BF16)
    lo = (r - mid.astype(F32)).astype(BF16)
    return hi, mid, lo


def _bias_lane(hh):
    return (1 - hh) * HEAD_DIM


def _in_proj_kernel(x_ref, g_ref, w_ref, wf_ref, b_ref, tri_ref, place_ref,
                    o_ref, e_ref, h_ref, carry_ref, *, tiles_per_seq):
    @pl.when(pl.program_id(1) == 0)
    def _():
        h = _rms(x_ref[...], g_ref[...]).astype(BF16)
        h_ref[...] = h

        @pl.when(pl.program_id(0) % tiles_per_seq == 0)
        def _():
            carry_ref[...] = jnp.zeros_like(carry_ref)

        y = _dot_nt(h, wf_ref[...]) + b_ref[...]
        logf = jnp.minimum(y, 0.0) - jnp.log(1.0 + jnp.exp(-jnp.abs(y)))
        pieces = _split3(logf)
        tri = tri_ref[...]
        tc = tri.shape[0]
        cum = carry_ref[...]
        for c in range(h.shape[0] // tc):
            rows = slice(c * tc, (c + 1) * tc)
            cum = cum[-1:, :]
            for piece in pieces:
                cum = cum + _dot(tri, piece[rows])
            bias_pieces = jnp.concatenate(_split3(cum * (-LOG2E)), axis=1)
            e_ref[rows, :] = _dot(bias_pieces, place_ref[...]).astype(BF16)
        carry_ref[...] = cum[-1:, :]

    o_ref[...] = _dot_nt(h_ref[...], w_ref[...]).astype(o_ref.dtype)


def _bias_placement():
    place = np.zeros((N_BIAS_PIECES, LANES, N_HEAD_BLOCKS * LANES), np.float32)
    for h in range(N_HEADS):
        hp, hh = divmod(h, HEADS_PER_BLOCK)
        for i in range(N_BIAS_PIECES):
            place[i, h, hp * LANES + _bias_lane(hh) + i] = 1.0
    return jnp.asarray(place.reshape(N_BIAS_PIECES * LANES, -1), BF16)


def _in_proj(x, g, w, w_f, b_f, layer, s, tm, tn, tc):
    n, d = x.shape
    cols = w.shape[1]
    tri = jnp.asarray(np.tril(np.ones((tc, tc), np.float32)), BF16)
    place = _bias_placement()
    e_cols = place.shape[1]
    const2 = lambda i, j: (0, 0)
    return pl.pallas_call(
        functools.partial(_in_proj_kernel, tiles_per_seq=s // tm),
        grid=(n // tm, cols // tn),
        in_specs=[
            pl.BlockSpec((tm, d), lambda i, j: (i, 0)),
            pl.BlockSpec((1, d), const2),
            pl.BlockSpec((None, tn, d), lambda i, j: (layer, j, 0)),
            pl.BlockSpec((None, LANES, d), lambda i, j: (layer, 0, 0)),
            pl.BlockSpec((1, LANES), const2),
            pl.BlockSpec((tc, tc), const2),
            pl.BlockSpec(place.shape, const2),
        ],
        out_specs=[pl.BlockSpec((tm, tn), lambda i, j: (i, j)),
                   pl.BlockSpec((tm, e_cols), lambda i, j: (i, 0))],
        out_shape=[jax.ShapeDtypeStruct((n, cols), BF16),
                   jax.ShapeDtypeStruct((n, e_cols), BF16)],
        scratch_shapes=[pltpu.VMEM((tm, d), BF16), pltpu.VMEM((1, LANES), F32)],
        compiler_params=_cparams(("arbitrary", "arbitrary")),
        name="norm_in_proj",
    )(x, g, w, w_f, b_f, tri, place)


def _head_lane_mask(hh):
    lane = lax.broadcasted_iota(jnp.int32, (1, LANES), 1)
    return (lane >= hh * HEAD_DIM) & (lane < (hh + 1) * HEAD_DIM)


def _tile_rel(t):
    return (lax.broadcasted_iota(jnp.int32, (t, t), 0)
            - lax.broadcasted_iota(jnp.int32, (t, t), 1))


def _mixer_specs(mixer, nb, t, s, d):
    qoff = N_BRANCH * d // LANES + (3 * mixer) * N_HEAD_BLOCKS
    koff = qoff + N_HEAD_BLOCKS
    voff = koff + N_HEAD_BLOCKS
    return [
        pl.BlockSpec((nb, t, LANES), lambda bg, hp, qi: (bg, qi, qoff + hp)),
        pl.BlockSpec((nb, s, LANES), lambda bg, hp, qi: (bg, 0, koff + hp)),
        pl.BlockSpec((nb, s, LANES), lambda bg, hp, qi: (bg, 0, voff + hp)),
    ], pl.BlockSpec((nb, t, LANES), lambda bg, hp, qi: (bg, qi, hp))


def _sb_kernel(q_ref, k_ref, v_ref, o_ref, vt_ref, *scratch, nb, t):
    n_blocks = k_ref.shape[1] // t
    q_tiles = q_ref.shape[1] // t
    streams = [(bi, hh) for bi in range(nb) for hh in range(HEADS_PER_BLOCK)]

    @pl.when(pl.program_id(2) == 0)
    def _build():
        def chunk(c, carry):
            rows = pl.ds(pl.multiple_of(c * t, t), t)
            for st, (bi, hh) in enumerate(streams):
                v2t = v_ref[bi, rows, :].astype(F32).T
                vt_ref[st, :, rows] = v2t[hh * HEAD_DIM:(hh + 1) * HEAD_DIM, :].astype(BF16)
            return carry

        lax.fori_loop(0, n_blocks, chunk, 0)

    def q_tile(sub, carry):
        q_rows = pl.ds(pl.multiple_of(sub * t, t), t)
        _sb_q_tile(pl.program_id(2) * q_tiles + sub, q_rows, q_ref, k_ref, o_ref, vt_ref, *scratch, nb=nb, t=t)
        return carry

    lax.fori_loop(0, q_tiles, q_tile, 0)


def _sb_q_tile(qi, q_rows, q_ref, k_ref, o_ref, vt_ref, acc_ref, cs_ref, r_ref, s_even_ref, s_odd_ref, *, nb, t):
    streams = [(bi, hh) for bi in range(nb) for hh in range(HEADS_PER_BLOCK)]
    q_m = []
    for bi, hh in streams:
        q2 = q_ref[bi, q_rows, :]
        q_m.append(_transposed(jnp.where(_head_lane_mask(hh), q2, jnp.zeros_like(q2))))

    rel = _tile_rel(t)
    strictly_past = rel < 0
    upper = jnp.where(rel <= 0, 1.0, 0.0).astype(BF16)

    stream_groups = [range(g, g + HEADS_PER_BLOCK) for g in range(0, len(streams), HEADS_PER_BLOCK)]

    def scores(s_ref, j, group):
        rows = pl.ds(pl.multiple_of(j * t, t), t)
        for st in group:
            s_ref[st] = _dot(k_ref[streams[st][0], rows, :], q_m[st])

    def softplus_sums(s_ref, diag, group):
        for st in group:
            z = s_ref[st]
            sp = jnp.maximum(z, jnp.log2(1.0 + jnp.exp2(jnp.minimum(z, SP2_CLAMP))))
            if diag:
                sp = jnp.where(strictly_past, sp, 0.0)
            r = _dot(upper, sp.astype(BF16)) + cs_ref[st]
            r_ref[st] = r
            cs_ref[st] = r[0:1, :]

    def weigh_values(s_ref, j, diag, group):
        rows = pl.ds(pl.multiple_of(j * t, t), t)
        for st in group:
            w = jnp.exp2(s_ref[st] - r_ref[st])
            if diag:
                w = jnp.where(strictly_past, w, 0.0)
            acc_ref[st] += _dot(vt_ref[st, :, rows], w.astype(BF16))

    def step(cur_ref, next_ref, j, j_next, diag):
        for group in stream_groups:
            scores(next_ref, j_next, group)
            softplus_sums(cur_ref, diag, group)
        for group in stream_groups:
            weigh_values(cur_ref, j, diag, group)

    def next_distance(i):
        lowest = cs_ref[0]
        for st in range(1, len(streams)):
            lowest = jnp.minimum(lowest, cs_ref[st])
        return jnp.where(jnp.min(lowest) > SB_EXIT, qi + 1, i + 1)

    acc_ref[...] = jnp.zeros(acc_ref.shape, F32)
    cs_ref[...] = jnp.zeros(cs_ref.shape, F32)
    for group in stream_groups:
        scores(s_even_ref, qi, group)
    step(s_even_ref, s_odd_ref, qi, jnp.maximum(qi - 1, 0), True)

    def body(i):
        j = qi - i
        j_next = jnp.maximum(j - 1, 0)

        @pl.when(i % 2 == 1)
        def _():
            step(s_odd_ref, s_even_ref, j, j_next, False)

        @pl.when(i % 2 == 0)
        def _():
            step(s_even_ref, s_odd_ref, j, j_next, False)

        return next_distance(i)

    lax.while_loop(lambda i: i <= qi, body, jnp.int32(1))

    for bi in range(nb):
        halves = [acc_ref[bi * HEADS_PER_BLOCK + hh] for hh in range(HEADS_PER_BLOCK)]
        o_ref[bi, q_rows, :] = jnp.concatenate(halves, axis=0).T.astype(o_ref.dtype)


def _sb_attention(proj3, d):
    b, s, _ = proj3.shape
    t = ATT_TILE
    nb = MIXER_BATCH if b % MIXER_BATCH == 0 else 1
    n_streams = nb * HEADS_PER_BLOCK
    q_rows = min(Q_TILES_PER_STEP * t, s)
    in_specs, out_spec = _mixer_specs(0, nb, q_rows, s, d)
    return pl.pallas_call(
        functools.partial(_sb_kernel, nb=nb, t=t),
        grid=(b // nb, N_HEAD_BLOCKS, s // q_rows),
        in_specs=in_specs,
        out_specs=out_spec,
        out_shape=jax.ShapeDtypeStruct((b, s, BRANCH_WIDTH), BF16),
        scratch_shapes=[pltpu.VMEM((n_streams, HEAD_DIM, s), BF16),
                        pltpu.VMEM((n_streams, HEAD_DIM, t), F32),
                        pltpu.VMEM((n_streams, 1, t), F32),
                        pltpu.VMEM((n_streams, t, t), F32),
                        pltpu.VMEM((n_streams, t, t), F32),
                        pltpu.VMEM((n_streams, t, t), F32)],
        compiler_params=_cparams(("parallel", "parallel", "arbitrary")),
        name="stickbreak_attn",
    )(proj3, proj3, proj3)


def _flash_kernel(*refs, moba, nb, t):
    q_tiles = refs[0].shape[1] // t

    def q_tile(sub, carry):
        q_rows = pl.ds(pl.multiple_of(sub * t, t), t)
        first = (pl.program_id(2) == 0) & (sub == 0)
        _flash_q_tile(pl.program_id(2) * q_tiles + sub, q_rows, first, *refs, moba=moba, nb=nb, t=t)
        return carry

    lax.fori_loop(0, q_tiles, q_tile, 0)


def _flash_q_tile(qi, q_rows, first, *refs, moba, nb, t):
    if moba:
        (q_ref, k_ref, v_ref, o_ref, kaug_ref, vt_ref, acc_ref, m_ref, s0_ref, s1_ref, s2_ref, s3_ref,
         bmax_ref, knorm_ref, kmean_ref, sel_ref) = refs
    else:
        (q_ref, k_ref, v_ref, e_ref, o_ref, kaug_ref, vt_ref, acc_ref, m_ref, s0_ref, s1_ref, s2_ref,
         s3_ref, bmax_ref, knorm_ref) = refs
    s_bufs = (s0_ref, s1_ref, s2_ref, s3_ref)
    n_blocks = k_ref.shape[1] // t
    hp = pl.program_id(1)
    lane = lax.broadcasted_iota(jnp.int32, (1, LANES), 1)
    streams = [(bi, hh) for bi in range(nb) for hh in range(HEADS_PER_BLOCK)]

    def piece_lanes(hh):
        e0 = _bias_lane(hh)
        return (lane >= e0) & (lane < e0 + N_BIAS_PIECES)

    head_of_lane = jnp.where(lax.broadcasted_iota(jnp.int32, (LANES, LANES), 0) >= HEAD_DIM, 1, 0)
    head_sum = jnp.where(head_of_lane == lax.broadcasted_iota(jnp.int32, (LANES, LANES), 1), 1.0, 0.0).astype(BF16)

    def head_sq_norm_max(x32):
        hi, lo = _split2(x32 * x32)
        return jnp.max(_dot(hi, head_sum) + _dot(lo, head_sum), axis=0, keepdims=True)

    def head_col(row, hh):
        return jnp.sum(jnp.where(lane == hh, row, 0.0), axis=1, keepdims=True)

    @pl.when(first)
    def _build():
        bmax_ref[...] = jnp.full(bmax_ref.shape, NEG_INF, F32)
        knorm_ref[...] = jnp.zeros(knorm_ref.shape, F32)
        tail = jnp.where(lax.broadcasted_iota(jnp.int32, (V_ROWS - HEAD_DIM, t), 0) == 0, 1.0, 0.0)

        def chunk(c, carry):
            start = pl.multiple_of(c * t, t)
            rows = pl.ds(start, t)
            extras = []
            if moba:
                pos = (lax.broadcasted_iota(jnp.int32, (t, LANES), 0) + start).astype(F32)
                for hh in range(HEADS_PER_BLOCK):
                    h = hp * HEADS_PER_BLOCK + hh
                    slope = pltpu.bitcast(jnp.full((1, LANES), (126 - h) << 23, jnp.int32), F32)
                    extra = jnp.zeros((t, LANES), F32)
                    for i, piece in enumerate(_split3(pos * slope * LOG2E)):
                        extra = jnp.where(lane == _bias_lane(hh) + i, piece.astype(F32), extra)
                    extras.append(extra.astype(BF16))
            for bi in range(nb):
                k2 = k_ref[bi, rows, :]
                k32 = k2.astype(F32)
                v2t = v_ref[bi, rows, :].astype(F32).T
                if moba:
                    kmean_ref[bi, pl.ds(c, 1), :] = jnp.sum(k32, axis=0, keepdims=True) * (1.0 / t)
                for hh in range(HEADS_PER_BLOCK):
                    st = bi * HEADS_PER_BLOCK + hh
                    extra = extras[hh] if moba else e_ref[bi, rows, :]
                    kaug_ref[st, rows, :] = jnp.where(_head_lane_mask(hh), k2, extra)
                    v_h = v2t[hh * HEAD_DIM:(hh + 1) * HEAD_DIM, :]
                    vt_ref[st, :, rows] = jnp.concatenate([v_h, tail], axis=0).astype(BF16)
                    piece_max = jnp.max(extra.astype(F32), axis=0, keepdims=True)
                    bias_max = jnp.sum(jnp.where(piece_lanes(hh), piece_max, 0.0), axis=1, keepdims=True)
                    run = jnp.maximum(bmax_ref[st, pl.ds(jnp.maximum(c - 1, 0), 1), :], bias_max)
                    bmax_ref[st, pl.ds(c, 1), :] = run
                knorm_ref[bi] = jnp.maximum(knorm_ref[bi], head_sq_norm_max(k32))
            return carry

        lax.fori_loop(0, n_blocks, chunk, 0)

    q_aug = []
    reach = []
    for bi in range(nb):
        q2 = q_ref[bi, q_rows, :]
        qk_sq = head_sq_norm_max(q2.astype(F32)) * knorm_ref[bi]
        for hh in range(HEADS_PER_BLOCK):
            ones = jnp.where(piece_lanes(hh), 1.0, 0.0).astype(BF16)
            q_aug.append(_transposed(jnp.where(_head_lane_mask(hh), q2, jnp.broadcast_to(ones, q2.shape))))
            reach.append(jnp.sqrt(head_col(qk_sq, hh)) * NORM_SLACK)

    if moba:
        blk = lax.broadcasted_iota(jnp.int32, (n_blocks, 1), 0)
        blk_f = blk.astype(F32)
        for st, (bi, hh) in enumerate(streams):
            km = jnp.where(_head_lane_mask(hh), kmean_ref[bi], 0.0)
            pieces = _dot(jnp.concatenate(_split3(km), axis=0), q_aug[st])
            route = pieces[:n_blocks] + pieces[n_blocks:2 * n_blocks] + pieces[2 * n_blocks:]
            route = jnp.where(blk < qi, route, NEG_INF)
            sel = jnp.zeros((n_blocks, t), F32)
            for _ in range(MOBA_TOPK):
                top = jnp.max(route, axis=0, keepdims=True)
                first = jnp.min(jnp.where(route == top, blk_f, float(n_blocks)), axis=0, keepdims=True)
                hit = blk_f == first
                sel = jnp.where(hit, 1.0, sel)
                route = jnp.where(hit, -jnp.inf, route)
            sel_ref[st] = jnp.where(blk < qi, sel, 0.0)

    valid = _tile_rel(t) <= 0

    all_streams = range(len(streams))
    stream_groups = [range(g, g + HEADS_PER_BLOCK) for g in range(0, len(streams), HEADS_PER_BLOCK)]

    def scores(s_ref, j, group=all_streams):
        rows = pl.ds(pl.multiple_of(j * t, t), t)
        for st in group:
            s_ref[st] = _dot(kaug_ref[st, rows, :], q_aug[st])

    def accumulate(s_ref, j, diag, group=all_streams):
        rows = pl.ds(pl.multiple_of(j * t, t), t)
        for st in group:
            s_t = s_ref[st]
            if diag:
                s_t = jnp.where(valid, s_t, NEG_INF)
            m_old = m_ref[st]
            m_new = jnp.maximum(m_old, jnp.max(s_t, axis=0, keepdims=True))
            m_sub = m_new
            if moba and not diag:
                picked = sel_ref[st, pl.ds(j, 1), :] > 0.0
                m_new = jnp.where(picked, m_new, m_old)
                m_sub = jnp.where(picked, m_new, -NEG_INF)
            alpha = jnp.exp2(m_old - m_new)
            p = jnp.exp2(s_t - m_sub)
            acc_ref[st] = alpha * acc_ref[st] + _dot(vt_ref[st, :, rows], p.astype(BF16))
            m_ref[st] = m_new

    def tile_at(dist):
        return jnp.maximum(qi - dist, 0)

    m_ref[...] = jnp.full(m_ref.shape, NEG_INF, F32)
    acc_ref[...] = jnp.zeros(acc_ref.shape, F32)
    scores(s_bufs[0], qi)
    for group in stream_groups:
        scores(s_bufs[1], tile_at(1), group)
        accumulate(s_bufs[0], qi, True, group)
        scores(s_bufs[2], tile_at(2), group)

    blk_col = lax.broadcasted_iota(jnp.int32, (n_blocks, 1), 0)
    fewest_dead = None
    for st in range(len(streams)):
        thr = jnp.min(m_ref[st], axis=1, keepdims=True) - FLASH_SKIP - reach[st]
        dead = (bmax_ref[st] < thr) & (blk_col < qi)
        n_dead = jnp.sum(jnp.where(dead, 1.0, 0.0), axis=0, keepdims=True)
        fewest_dead = n_dead if fewest_dead is None else jnp.minimum(fewest_dead, n_dead)
    n_past = qi - jnp.max(fewest_dead).astype(jnp.int32)

    def pair(first, cur, nxt):
        for k in range(2):
            for group in stream_groups:
                scores(s_bufs[nxt[k]], tile_at(first + 2 + k), group)
                accumulate(s_bufs[cur[k]], qi - first - k, False, group)

    def body(p, carry):
        first = 2 * p + 1

        @pl.when(p % 2 == 0)
        def _():
            pair(first, (1, 2), (3, 0))

        @pl.when(p % 2 == 1)
        def _():
            pair(first, (3, 0), (1, 2))

        return carry

    lax.fori_loop(0, n_past // 2, body, 0)

    for slot in (1, 3):
        @pl.when(n_past % 4 == slot)
        def _(slot=slot):
            accumulate(s_bufs[slot], qi - n_past, False)

    for bi in range(nb):
        halves = []
        for hh in range(HEADS_PER_BLOCK):
            acc = acc_ref[bi * HEADS_PER_BLOCK + hh]
            halves.append(acc[:HEAD_DIM] * (1.0 / acc[HEAD_DIM:HEAD_DIM + 1]))
        o_ref[bi, q_rows, :] = jnp.concatenate(halves, axis=0).T.astype(o_ref.dtype)


def _flash_attention(proj3, key_bias3, d, moba):
    b, s, _ = proj3.shape
    t = ATT_TILE
    nb = MIXER_BATCH if b % MIXER_BATCH == 0 else 1
    n_blocks = s // t
    n_streams = nb * HEADS_PER_BLOCK
    q_rows = min(Q_TILES_PER_STEP * t, s)
    in_specs, out_spec = _mixer_specs(2 if moba else 1, nb, q_rows, s, d)
    operands = [proj3, proj3, proj3]
    scratch = [pltpu.VMEM((n_streams, s, LANES), BF16),
               pltpu.VMEM((n_streams, V_ROWS, s), BF16),
               pltpu.VMEM((n_streams, V_ROWS, t), F32),
               pltpu.VMEM((n_streams, 1, t), F32),
               pltpu.VMEM((n_streams, t, t), F32),
               pltpu.VMEM((n_streams, t, t), F32),
               pltpu.VMEM((n_streams, t, t), F32),
               pltpu.VMEM((n_streams, t, t), F32),
               pltpu.VMEM((n_streams, n_blocks, 1), F32),
               pltpu.VMEM((nb, 1, LANES), F32)]
    if moba:
        scratch += [pltpu.VMEM((nb, n_blocks, LANES), F32),
                    pltpu.VMEM((n_streams, n_blocks, t), F32)]
    else:
        in_specs.append(pl.BlockSpec((nb, s, LANES), lambda bg, hp, qi: (bg, 0, hp)))
        operands.append(key_bias3)
    return pl.pallas_call(
        functools.partial(_flash_kernel, moba=moba, nb=nb, t=t),
        grid=(b // nb, N_HEAD_BLOCKS, s // q_rows),
        in_specs=in_specs,
        out_specs=out_spec,
        out_shape=jax.ShapeDtypeStruct((b, s, BRANCH_WIDTH), BF16),
        scratch_shapes=scratch,
        compiler_params=_cparams(("parallel", "parallel", "arbitrary")),
        name="moba_attn" if moba else "forgetting_attn",
    )(*operands)


def _mix_kernel(x_ref, oa_ref, ob_ref, oc_ref, g_ref, wb_ref, wo_ref, y_ref):
    d = x_ref.shape[1]
    mixed = None
    for n, o_ref in enumerate((oa_ref, ob_ref, oc_ref)):
        gate = jax.nn.sigmoid(g_ref[:, n * d:(n + 1) * d].astype(F32))
        term = gate * _dot(o_ref[...], wb_ref[n])
        mixed = term if mixed is None else mixed + term
    y_ref[...] = x_ref[...] + _dot(mixed.astype(BF16), wo_ref[...])


def _mix(x, o_a, o_b, o_c, proj, wb, wo, layer, tm):
    n, d = x.shape
    o_spec = pl.BlockSpec((tm, BRANCH_WIDTH), lambda i: (i, 0))
    return pl.pallas_call(
        _mix_kernel,
        grid=(n // tm,),
        in_specs=[
            pl.BlockSpec((tm, d), lambda i: (i, 0)),
            o_spec, o_spec, o_spec,
            pl.BlockSpec((tm, N_BRANCH * d), lambda i: (i, 0)),
            pl.BlockSpec((None, N_BRANCH, BRANCH_WIDTH, d), lambda i: (layer, 0, 0, 0)),
            pl.BlockSpec((None, d, d), lambda i: (layer, 0, 0)),
        ],
        out_specs=pl.BlockSpec((tm, d), lambda i: (i, 0)),
        out_shape=jax.ShapeDtypeStruct((n, d), F32),
        compiler_params=_cparams(("parallel",)),
        name="gated_mix_out_proj",
    )(x, o_a, o_b, o_c, proj, wb, wo)


def _mlp_kernel(x_ref, g_ref, wu_ref, wd_ref, gf_ref, y_ref, h_ref, acc_ref, *, final_norm):
    f = pl.program_id(1)

    @pl.when(f == 0)
    def _():
        h_ref[...] = _rms(x_ref[...], g_ref[...]).astype(BF16)
        acc_ref[...] = jnp.zeros_like(acc_ref)

    hid = jnp.square(jnp.maximum(_dot(h_ref[...], wu_ref[...]), 0.0))
    acc_ref[...] += _dot(hid.astype(BF16), wd_ref[...])

    @pl.when(f == pl.num_programs(1) - 1)
    def _():
        y = x_ref[...] + acc_ref[...]
        if final_norm:
            y = _rms(y, gf_ref[...])
        y_ref[...] = y


def _mlp(x, g, wu, wd, g_final, layer, final_norm, tm, tf):
    n, d = x.shape
    d_ff = wu.shape[2]
    return pl.pallas_call(
        functools.partial(_mlp_kernel, final_norm=final_norm),
        grid=(n // tm, d_ff // tf),
        in_specs=[
            pl.BlockSpec((tm, d), lambda i, f: (i, 0)),
            pl.BlockSpec((1, d), lambda i, f: (0, 0)),
            pl.BlockSpec((None, d, tf), lambda i, f: (layer, 0, f)),
            pl.BlockSpec((None, tf, d), lambda i, f: (layer, f, 0)),
            pl.BlockSpec((1, d), lambda i, f: (0, 0)),
        ],
        out_specs=pl.BlockSpec((tm, d), lambda i, f: (i, 0)),
        out_shape=jax.ShapeDtypeStruct((n, d), F32),
        scratch_shapes=[pltpu.VMEM((tm, d), BF16), pltpu.VMEM((tm, d), F32)],
        compiler_params=_cparams(("parallel", "arbitrary")),
        name="relu2_mlp",
    )(x, g, wu, wd, g_final)


REPACK_ROWS = 512


def _repack_kernel(w_ref, wf_ref, main_ref, f_ref):
    i = pl.program_id(1)
    q_chunks = BRANCH_WIDTH // REPACK_ROWS
    n_gate_chunks = N_BRANCH * w_ref.shape[2] // REPACK_ROWS
    mixer_chunk = jnp.maximum(i - n_gate_chunks, 0) % (3 * q_chunks)
    is_q = (i >= n_gate_chunks) & (mixer_chunk < q_chunks)
    scale = jnp.where(is_q, HEAD_DIM ** -0.5 * LOG2E, 1.0)
    main_ref[...] = (w_ref[0] * scale).astype(BF16)

    @pl.when(i == 0)
    def _():
        pad = jnp.zeros((f_ref.shape[0] - N_HEADS, f_ref.shape[1]), F32)
        f_ref[...] = jnp.concatenate([wf_ref[0], pad], axis=0).astype(BF16)


def _prep_in_proj(w_in):
    depth, d, cols = w_in.shape
    w_t = jnp.swapaxes(w_in, 1, 2)
    bw = BRANCH_WIDTH
    r = REPACK_ROWS
    f_lo = 6 * bw
    c_lo = f_lo + N_HEADS
    g_lo = c_lo + 3 * bw
    main_rows = cols - N_HEADS
    n_gate = (cols - g_lo) // r
    n_ab = f_lo // r

    def src_row(l, i):
        row = jnp.where(i < n_gate, g_lo + r * i,
                        jnp.where(i < n_gate + n_ab, r * (i - n_gate), c_lo + r * (i - n_gate - n_ab)))
        return l, pl.multiple_of(row, N_HEADS), 0

    return pl.pallas_call(
        _repack_kernel,
        grid=(depth, main_rows // r),
        in_specs=[pl.BlockSpec((pl.Element(1), pl.Element(r), pl.Element(d)), src_row),
                  pl.BlockSpec((pl.Element(1), pl.Element(N_HEADS), pl.Element(d)), lambda l, i: (l, f_lo, 0))],
        out_specs=[pl.BlockSpec((None, r, d), lambda l, i: (l, i, 0)),
                   pl.BlockSpec((None, LANES, d), lambda l, i: (l, 0, 0))],
        out_shape=[jax.ShapeDtypeStruct((depth, main_rows, d), BF16),
                   jax.ShapeDtypeStruct((depth, LANES, d), BF16)],
        compiler_params=_cparams(("parallel", "arbitrary")),
        name="repack_w_in",
    )(w_t, w_t)


def kernel(x, norm_mix, w_in, b_forget, w_branch, w_out, norm_mlp, w_up, w_down, norm_final):
    b, s, d = x.shape
    depth = w_in.shape[0]
    n = b * s
    assert s % ATT_TILE == 0 and d % LANES == 0

    w_main, w_f = _prep_in_proj(w_in)
    wb = w_branch.astype(BF16)
    wo = w_out.astype(BF16)
    wu = w_up.astype(BF16)
    wd = w_down.astype(BF16)
    b_f = jnp.pad(b_forget, ((0, 0), (0, LANES - N_HEADS)))[:, None, :]

    tm = min(1024, s)
    xf = x.reshape(n, d)
    for l in range(depth):
        g_mix = norm_mix[l][None, :]
        proj, key_bias = _in_proj(xf, g_mix, w_main, w_f, b_f[l], l, s, tm, 3840, 256)
        proj3 = proj.reshape(b, s, -1)
        o_a = _sb_attention(proj3, d).reshape(n, -1)
        o_b = _flash_attention(proj3, key_bias.reshape(b, s, -1), d, False).reshape(n, -1)
        o_c = _flash_attention(proj3, None, d, True).reshape(n, -1)
        xf = _mix(xf, o_a, o_b, o_c, proj, wb, wo, l, min(1024, n))
        xf = _mlp(xf, norm_mlp[l][None, :], wu, wd, norm_final[None, :],
                  l, l == depth - 1, min(1024, n), 1024)
    return xf.reshape(b, s, d)
```

```python
import functools
import math

import jax
import jax.numpy as jnp
import numpy as np
from jax import lax
from jax.experimental import pallas as pl
from jax.experimental.pallas import tpu as pltpu

F32 = jnp.float32
BF16 = jnp.bfloat16

HEAD_DIM = 64
N_HEADS = 8
BRANCH_WIDTH = N_HEADS * HEAD_DIM
N_BRANCH = 3
MOBA_BLOCK = 256
MOBA_TOPK = 3
RMS_EPS = 1e-6
NEG_INF = -1e30
LOG2E = math.log2(math.e)
LANES = 128
HEADS_PER_BLOCK = LANES // HEAD_DIM
N_HEAD_BLOCKS = BRANCH_WIDTH // LANES
N_BIAS_PIECES = 3
ATT_TILE = MOBA_BLOCK
Q_TILES_PER_STEP = 4
MIXER_BATCH = 4
SB_EXIT = 152.0
FLASH_SKIP = 152.0
SP2_CLAMP = 120.0
NORM_SLACK = 1.001
V_ROWS = HEAD_DIM + 16
VMEM_LIMIT = 60000 * 1024


def _cparams(sem):
    return pltpu.CompilerParams(dimension_semantics=sem, vmem_limit_bytes=VMEM_LIMIT)


def _rms(x, g):
    ms = jnp.mean(x * x, axis=-1, keepdims=True)
    return x * lax.rsqrt(ms + RMS_EPS) * g


def _dot(a, b):
    return jnp.dot(a, b, preferred_element_type=F32)


def _dot_nt(a, b):
    return lax.dot_general(a, b, (((1,), (1,)), ((), ())), preferred_element_type=F32)


def _transposed(x):
    return x.astype(F32).T.astype(BF16)


def _split2(x):
    hi = x.astype(BF16)
    lo = (x - hi.astype(F32)).astype(BF16)
    return hi, lo


def _split3(x):
    hi = x.astype(BF16)
    r = x - hi.astype(F32)
    mid = r.astype(BF16)
    lo = (r - mid.astype(F32)).astype(BF16)
    return hi, mid, lo


def _bias_lane(hh):
    return (1 - hh) * HEAD_DIM


def _in_proj_kernel(x_ref, g_ref, w_ref, wf_ref, b_ref, tri_ref, place_ref,
                    o_ref, e_ref, h_ref, carry_ref, *, tiles_per_seq):
    @pl.when(pl.program_id(1) == 0)
    def _():
        h = _rms(x_ref[...], g_ref[...]).astype(BF16)
        h_ref[...] = h

        @pl.when(pl.program_id(0) % tiles_per_seq == 0)
        def _():
            carry_ref[...] = jnp.zeros_like(carry_ref)

        y = _dot_nt(h, wf_ref[...]) + b_ref[...]
        logf = jnp.minimum(y, 0.0) - jnp.log(1.0 + jnp.exp(-jnp.abs(y)))
        pieces = _split3(logf)
        tri = tri_ref[...]
        tc = tri.shape[0]
        cum = carry_ref[...]
        for c in range(h.shape[0] // tc):
            rows = slice(c * tc, (c + 1) * tc)
            cum = cum[-1:, :]
            for piece in pieces:
                cum = cum + _dot(tri, piece[rows])
            bias_pieces = jnp.concatenate(_split3(cum * (-LOG2E)), axis=1)
            e_ref[rows, :] = _dot(bias_pieces, place_ref[...]).astype(BF16)
        carry_ref[...] = cum[-1:, :]

    o_ref[...] = _dot_nt(h_ref[...], w_ref[...]).astype(o_ref.dtype)


def _bias_placement():
    place = np.zeros((N_BIAS_PIECES, LANES, N_HEAD_BLOCKS * LANES), np.float32)
    for h in range(N_HEADS):
        hp, hh = divmod(h, HEADS_PER_BLOCK)
        for i in range(N_BIAS_PIECES):
            place[i, h, hp * LANES + _bias_lane(hh) + i] = 1.0
    return jnp.asarray(place.reshape(N_BIAS_PIECES * LANES, -1), BF16)


def _in_proj(x, g, w, w_f, b_f, layer, s, tm, tn, tc):
    n, d = x.shape
    cols = w.shape[1]
    tri = jnp.asarray(np.tril(np.ones((tc, tc), np.float32)), BF16)
    place = _bias_placement()
    e_cols = place.shape[1]
    const2 = lambda i, j: (0, 0)
    return pl.pallas_call(
        functools.partial(_in_proj_kernel, tiles_per_seq=s // tm),
        grid=(n // tm, cols // tn),
        in_specs=[
            pl.BlockSpec((tm, d), lambda i, j: (i, 0)),
            pl.BlockSpec((1, d), const2),
            pl.BlockSpec((None, tn, d), lambda i, j: (layer, j, 0)),
            pl.BlockSpec((None, LANES, d), lambda i, j: (layer, 0, 0)),
            pl.BlockSpec((1, LANES), const2),
            pl.BlockSpec((tc, tc), const2),
            pl.BlockSpec(place.shape, const2),
        ],
        out_specs=[pl.BlockSpec((tm, tn), lambda i, j: (i, j)),
                   pl.BlockSpec((tm, e_cols), lambda i, j: (i, 0))],
        out_shape=[jax.ShapeDtypeStruct((n, cols), BF16),
                   jax.ShapeDtypeStruct((n, e_cols), BF16)],
        scratch_shapes=[pltpu.VMEM((tm, d), BF16), pltpu.VMEM((1, LANES), F32)],
        compiler_params=_cparams(("arbitrary", "arbitrary")),
        name="norm_in_proj",
    )(x, g, w, w_f, b_f, tri, place)


def _head_lane_mask(hh):
    lane = lax.broadcasted_iota(jnp.int32, (1, LANES), 1)
    return (lane >= hh * HEAD_DIM) & (lane < (hh + 1) * HEAD_DIM)


def _tile_rel(t):
    return (lax.broadcasted_iota(jnp.int32, (t, t), 0)
            - lax.broadcasted_iota(jnp.int32, (t, t), 1))


def _mixer_specs(mixer, nb, t, s, d):
    qoff = N_BRANCH * d // LANES + (3 * mixer) * N_HEAD_BLOCKS
    koff = qoff + N_HEAD_BLOCKS
    voff = koff + N_HEAD_BLOCKS
    return [
        pl.BlockSpec((nb, t, LANES), lambda bg, hp, qi: (bg, qi, qoff + hp)),
        pl.BlockSpec((nb, s, LANES), lambda bg, hp, qi: (bg, 0, koff + hp)),
        pl.BlockSpec((nb, s, LANES), lambda bg, hp, qi: (bg, 0, voff + hp)),
    ], pl.BlockSpec((nb, t, LANES), lambda bg, hp, qi: (bg, qi, hp))


def _sb_kernel(q_ref, k_ref, v_ref, o_ref, vt_ref, *scratch, nb, t):
    n_blocks = k_ref.shape[1] // t
    q_tiles = q_ref.shape[1] // t
    streams = [(bi, hh) for bi in range(nb) for hh in range(HEADS_PER_BLOCK)]

    @pl.when(pl.program_id(2) == 0)
    def _build():
        def chunk(c, carry):
            rows = pl.ds(pl.multiple_of(c * t, t), t)
            for st, (bi, hh) in enumerate(streams):
                v2t = v_ref[bi, rows, :].astype(F32).T
                vt_ref[st, :, rows] = v2t[hh * HEAD_DIM:(hh + 1) * HEAD_DIM, :].astype(BF16)
            return carry

        lax.fori_loop(0, n_blocks, chunk, 0)

    def q_tile(sub, carry):
        q_rows = pl.ds(pl.multiple_of(sub * t, t), t)
        _sb_q_tile(pl.program_id(2) * q_tiles + sub, q_rows, q_ref, k_ref, o_ref, vt_ref, *scratch, nb=nb, t=t)
        return carry

    lax.fori_loop(0, q_tiles, q_tile, 0)


def _sb_q_tile(qi, q_rows, q_ref, k_ref, o_ref, vt_ref, acc_ref, cs_ref, r_ref, s_even_ref, s_odd_ref, *, nb, t):
    streams = [(bi, hh) for bi in range(nb) for hh in range(HEADS_PER_BLOCK)]
    q_m = []
    for bi, hh in streams:
        q2 = q_ref[bi, q_rows, :]
        q_m.append(_transposed(jnp.where(_head_lane_mask(hh), q2, jnp.zeros_like(q2))))

    rel = _tile_rel(t)
    strictly_past = rel < 0
    upper = jnp.where(rel <= 0, 1.0, 0.0).astype(BF16)

    stream_groups = [range(g, g + HEADS_PER_BLOCK) for g in range(0, len(streams), HEADS_PER_BLOCK)]

    def scores(s_ref, j, group):
        rows = pl.ds(pl.multiple_of(j * t, t), t)
        for st in group:
            s_ref[st] = _dot(k_ref[streams[st][0], rows, :], q_m[st])

    def softplus_sums(s_ref, diag, group):
        for st in group:
            z = s_ref[st]
            sp = jnp.maximum(z, jnp.log2(1.0 + jnp.exp2(jnp.minimum(z, SP2_CLAMP))))
            if diag:
                sp = jnp.where(strictly_past, sp, 0.0)
            r = _dot(upper, sp.astype(BF16)) + cs_ref[st]
            r_ref[st] = r
            cs_ref[st] = r[0:1, :]

    def weigh_values(s_ref, j, diag, group):
        rows = pl.ds(pl.multiple_of(j * t, t), t)
        for st in group:
            w = jnp.exp2(s_ref[st] - r_ref[st])
            if diag:
                w = jnp.where(strictly_past, w, 0.0)
            acc_ref[st] += _dot(vt_ref[st, :, rows], w.astype(BF16))

    def step(cur_ref, next_ref, j, j_next, diag):
        for group in stream_groups:
            scores(next_ref, j_next, group)
            softplus_sums(cur_ref, diag, group)
        for group in stream_groups:
            weigh_values(cur_ref, j, diag, group)

    def next_distance(i):
        lowest = cs_ref[0]
        for st in range(1, len(streams)):
            lowest = jnp.minimum(lowest, cs_ref[st])
        return jnp.where(jnp.min(lowest) > SB_EXIT, qi + 1, i + 1)

    acc_ref[...] = jnp.zeros(acc_ref.shape, F32)
    cs_ref[...] = jnp.zeros(cs_ref.shape, F32)
    for group in stream_groups:
        scores(s_even_ref, qi, group)
    step(s_even_ref, s_odd_ref, qi, jnp.maximum(qi - 1, 0), True)

    def body(i):
        j = qi - i
        j_next = jnp.maximum(j - 1, 0)

        @pl.when(i % 2 == 1)
        def _():
            step(s_odd_ref, s_even_ref, j, j_next, False)

        @pl.when(i % 2 == 0)
        def _():
            step(s_even_ref, s_odd_ref, j, j_next, False)

        return next_distance(i)

    lax.while_loop(lambda i: i <= qi, body, jnp.int32(1))

    for bi in range(nb):
        halves = [acc_ref[bi * HEADS_PER_BLOCK + hh] for hh in range(HEADS_PER_BLOCK)]
        o_ref[bi, q_rows, :] = jnp.concatenate(halves, axis=0).T.astype(o_ref.dtype)


def _sb_attention(proj3, d):
    b, s, _ = proj3.shape
    t = ATT_TILE
    nb = MIXER_BATCH if b % MIXER_BATCH == 0 else 1
    n_streams = nb * HEADS_PER_BLOCK
    q_rows = min(Q_TILES_PER_STEP * t, s)
    in_specs, out_spec = _mixer_specs(0, nb, q_rows, s, d)
    return pl.pallas_call(
        functools.partial(_sb_kernel, nb=nb, t=t),
        grid=(b // nb, N_HEAD_BLOCKS, s // q_rows),
        in_specs=in_specs,
        out_specs=out_spec,
        out_shape=jax.ShapeDtypeStruct((b, s, BRANCH_WIDTH), BF16),
        scratch_shapes=[pltpu.VMEM((n_streams, HEAD_DIM, s), BF16),
                        pltpu.VMEM((n_streams, HEAD_DIM, t), F32),
                        pltpu.VMEM((n_streams, 1, t), F32),
                        pltpu.VMEM((n_streams, t, t), F32),
                        pltpu.VMEM((n_streams, t, t), F32),
                        pltpu.VMEM((n_streams, t, t), F32)],
        compiler_params=_cparams(("parallel", "parallel", "arbitrary")),
        name="stickbreak_attn",
    )(proj3, proj3, proj3)


def _flash_kernel(*refs, moba, nb, t):
    q_tiles = refs[0].shape[1] // t

    def q_tile(sub, carry):
        q_rows = pl.ds(pl.multiple_of(sub * t, t), t)
        first = (pl.program_id(2) == 0) & (sub == 0)
        _flash_q_tile(pl.program_id(2) * q_tiles + sub, q_rows, first, *refs, moba=moba, nb=nb, t=t)
        return carry

    lax.fori_loop(0, q_tiles, q_tile, 0)


def _flash_q_tile(qi, q_rows, first, *refs, moba, nb, t):
    if moba:
        (q_ref, k_ref, v_ref, o_ref, kaug_ref, vt_ref, acc_ref, m_ref, s0_ref, s1_ref, s2_ref, s3_ref,
         bmax_ref, knorm_ref, kmean_ref, sel_ref) = refs
    else:
        (q_ref, k_ref, v_ref, e_ref, o_ref, kaug_ref, vt_ref, acc_ref, m_ref, s0_ref, s1_ref, s2_ref,
         s3_ref, bmax_ref, knorm_ref) = refs
    s_bufs = (s0_ref, s1_ref, s2_ref, s3_ref)
    n_blocks = k_ref.shape[1] // t
    hp = pl.program_id(1)
    lane = lax.broadcasted_iota(jnp.int32, (1, LANES), 1)
    streams = [(bi, hh) for bi in range(nb) for hh in range(HEADS_PER_BLOCK)]

    def piece_lanes(hh):
        e0 = _bias_lane(hh)
        return (lane >= e0) & (lane < e0 + N_BIAS_PIECES)

    head_of_lane = jnp.where(lax.broadcasted_iota(jnp.int32, (LANES, LANES), 0) >= HEAD_DIM, 1, 0)
    head_sum = jnp.where(head_of_lane == lax.broadcasted_iota(jnp.int32, (LANES, LANES), 1), 1.0, 0.0).astype(BF16)

    def head_sq_norm_max(x32):
        hi, lo = _split2(x32 * x32)
        return jnp.max(_dot(hi, head_sum) + _dot(lo, head_sum), axis=0, keepdims=True)

    def head_col(row, hh):
        return jnp.sum(jnp.where(lane == hh, row, 0.0), axis=1, keepdims=True)

    @pl.when(first)
    def _build():
        kaug_ref[:, 0:t, :] = jnp.zeros((len(streams), t, LANES), BF16)
        bmax_ref[...] = jnp.full(bmax_ref.shape, NEG_INF, F32)
        knorm_ref[...] = jnp.zeros(knorm_ref.shape, F32)
        tail = jnp.where(lax.broadcasted_iota(jnp.int32, (V_ROWS - HEAD_DIM, t), 0) == 0, 1.0, 0.0)

        def chunk(c, carry):
            start = pl.multiple_of(c * t, t)
            rows = pl.ds(start, t)
            extras = []
            if moba:
                pos = (lax.broadcasted_iota(jnp.int32, (t, LANES), 0) + start).astype(F32)
                for hh in range(HEADS_PER_BLOCK):
                    h = hp * HEADS_PER_BLOCK + hh
                    slope = pltpu.bitcast(jnp.full((1, LANES), (126 - h) << 23, jnp.int32), F32)
                    extra = jnp.zeros((t, LANES), F32)
                    for i, piece in enumerate(_split3(pos * slope * LOG2E)):
                        extra = jnp.where(lane == _bias_lane(hh) + i, piece.astype(F32), extra)
                    extras.append(extra.astype(BF16))
            for bi in range(nb):
                k2 = k_ref[bi, rows, :]
                k32 = k2.astype(F32)
                v2t = v_ref[bi, rows, :].astype(F32).T
                if moba:
                    kmean_ref[bi, pl.ds(c, 1), :] = jnp.sum(k32, axis=0, keepdims=True) * (1.0 / t)
                for hh in range(HEADS_PER_BLOCK):
                    st = bi * HEADS_PER_BLOCK + hh
                    extra = extras[hh] if moba else e_ref[bi, rows, :]
                    kaug_ref[st, pl.ds(start + t, t), :] = jnp.where(_head_lane_mask(hh), k2, extra)
                    v_h = v2t[hh * HEAD_DIM:(hh + 1) * HEAD_DIM, :]
                    vt_ref[st, :, rows] = jnp.concatenate([v_h, tail], axis=0).astype(BF16)
                    piece_max = jnp.max(extra.astype(F32), axis=0, keepdims=True)
                    bias_max = jnp.sum(jnp.where(piece_lanes(hh), piece_max, 0.0), axis=1, keepdims=True)
                    run = jnp.maximum(bmax_ref[st, pl.ds(jnp.maximum(c - 1, 0), 1), :], bias_max)
                    bmax_ref[st, pl.ds(c, 1), :] = run
                knorm_ref[bi] = jnp.maximum(knorm_ref[bi], head_sq_norm_max(k32))
            return carry

        lax.fori_loop(0, n_blocks, chunk, 0)

    q_aug = []
    reach = []
    for bi in range(nb):
        q2 = q_ref[bi, q_rows, :]
        qk_sq = head_sq_norm_max(q2.astype(F32)) * knorm_ref[bi]
        for hh in range(HEADS_PER_BLOCK):
            ones = jnp.where(piece_lanes(hh), 1.0, 0.0).astype(BF16)
            q_aug.append(_transposed(jnp.where(_head_lane_mask(hh), q2, jnp.broadcast_to(ones, q2.shape))))
            reach.append(jnp.sqrt(head_col(qk_sq, hh)) * NORM_SLACK)

    if moba:
        blk = lax.broadcasted_iota(jnp.int32, (n_blocks, 1), 0)
        blk_f = blk.astype(F32)
        for st, (bi, hh) in enumerate(streams):
            km = jnp.where(_head_lane_mask(hh), kmean_ref[bi], 0.0)
            pieces = _dot(jnp.concatenate(_split3(km), axis=0), q_aug[st])
            route = pieces[:n_blocks] + pieces[n_blocks:2 * n_blocks] + pieces[2 * n_blocks:]
            route = jnp.where(blk < qi, route, NEG_INF)
            sel = jnp.zeros((n_blocks, t), F32)
            for _ in range(MOBA_TOPK):
                top = jnp.max(route, axis=0, keepdims=True)
                first = jnp.min(jnp.where(route == top, blk_f, float(n_blocks)), axis=0, keepdims=True)
                hit = blk_f == first
                sel = jnp.where(hit, 1.0, sel)
                route = jnp.where(hit, -jnp.inf, route)
            sel_ref[st] = jnp.where(blk < qi, sel, 0.0)

    valid = _tile_rel(t) <= 0

    all_streams = range(len(streams))
    stream_groups = [range(g, g + HEADS_PER_BLOCK) for g in range(0, len(streams), HEADS_PER_BLOCK)]

    def scores(s_ref, j, group=all_streams):
        rows = pl.ds(pl.multiple_of((j + 1) * t, t), t)
        for st in group:
            s_ref[st] = _dot(kaug_ref[st, rows, :], q_aug[st])

    def scores_pair(near_ref, far_ref, near_dist, group):
        far = jnp.maximum(qi - near_dist - 1, -1)
        rows = pl.ds(pl.multiple_of((far + 1) * t, t), 2 * t)
        for st in group:
            both = _dot(kaug_ref[st, rows, :], q_aug[st])
            far_ref[st] = both[:t]
            near_ref[st] = both[t:]

    def accumulate(s_ref, j, diag, group=all_streams):
        rows = pl.ds(pl.multiple_of(j * t, t), t)
        for st in group:
            s_t = s_ref[st]
            if diag:
                s_t = jnp.where(valid, s_t, NEG_INF)
            m_old = m_ref[st]
            m_new = jnp.maximum(m_old, jnp.max(s_t, axis=0, keepdims=True))
            m_sub = m_new
            if moba and not diag:
                picked = sel_ref[st, pl.ds(j, 1), :] > 0.0
                m_new = jnp.where(picked, m_new, m_old)
                m_sub = jnp.where(picked, m_new, -NEG_INF)
            alpha = jnp.exp2(m_old - m_new)
            p = jnp.exp2(s_t - m_sub)
            acc_ref[st] = alpha * acc_ref[st] + _dot(vt_ref[st, :, rows], p.astype(BF16))
            m_ref[st] = m_new

    m_ref[...] = jnp.full(m_ref.shape, NEG_INF, F32)
    acc_ref[...] = jnp.zeros(acc_ref.shape, F32)
    scores(s_bufs[0], qi)
    for group in stream_groups:
        scores_pair(s_bufs[1], s_bufs[2], 1, group)
        accumulate(s_bufs[0], qi, True, group)

    blk_col = lax.broadcasted_iota(jnp.int32, (n_blocks, 1), 0)
    fewest_dead = None
    for st in range(len(streams)):
        thr = jnp.min(m_ref[st], axis=1, keepdims=True) - FLASH_SKIP - reach[st]
        dead = (bmax_ref[st] < thr) & (blk_col < qi)
        n_dead = jnp.sum(jnp.where(dead, 1.0, 0.0), axis=0, keepdims=True)
        fewest_dead = n_dead if fewest_dead is None else jnp.minimum(fewest_dead, n_dead)
    n_past = qi - jnp.max(fewest_dead).astype(jnp.int32)

    def pair(first, cur, nxt):
        for group in stream_groups:
            scores_pair(s_bufs[nxt[0]], s_bufs[nxt[1]], first + 2, group)
            accumulate(s_bufs[cur[0]], qi - first, False, group)
            accumulate(s_bufs[cur[1]], qi - first - 1, False, group)

    def body(p, carry):
        first = 2 * p + 1

        @pl.when(p % 2 == 0)
        def _():
            pair(first, (1, 2), (3, 0))

        @pl.when(p % 2 == 1)
        def _():
            pair(first, (3, 0), (1, 2))

        return carry

    lax.fori_loop(0, n_past // 2, body, 0)

    for slot in (1, 3):
        @pl.when(n_past % 4 == slot)
        def _(slot=slot):
            accumulate(s_bufs[slot], qi - n_past, False)

    for bi in range(nb):
        halves = []
        for hh in range(HEADS_PER_BLOCK):
            acc = acc_ref[bi * HEADS_PER_BLOCK + hh]
            halves.append(acc[:HEAD_DIM] * (1.0 / acc[HEAD_DIM:HEAD_DIM + 1]))
        o_ref[bi, q_rows, :] = jnp.concatenate(halves, axis=0).T.astype(o_ref.dtype)


def _flash_attention(proj3, key_bias3, d, moba):
    b, s, _ = proj3.shape
    t = ATT_TILE
    nb = MIXER_BATCH if b % MIXER_BATCH == 0 else 1
    n_blocks = s // t
    n_streams = nb * HEADS_PER_BLOCK
    q_rows = min(Q_TILES_PER_STEP * t, s)
    in_specs, out_spec = _mixer_specs(2 if moba else 1, nb, q_rows, s, d)
    operands = [proj3, proj3, proj3]
    scratch = [pltpu.VMEM((n_streams, s + t, LANES), BF16),
               pltpu.VMEM((n_streams, V_ROWS, s), BF16),
               pltpu.VMEM((n_streams, V_ROWS, t), F32),
               pltpu.VMEM((n_streams, 1, t), F32),
               pltpu.VMEM((n_streams, t, t), F32),
               pltpu.VMEM((n_streams, t, t), F32),
               pltpu.VMEM((n_streams, t, t), F32),
               pltpu.VMEM((n_streams, t, t), F32),
               pltpu.VMEM((n_streams, n_blocks, 1), F32),
               pltpu.VMEM((nb, 1, LANES), F32)]
    if moba:
        scratch += [pltpu.VMEM((nb, n_blocks, LANES), F32),
                    pltpu.VMEM((n_streams, n_blocks, t), F32)]
    else:
        in_specs.append(pl.BlockSpec((nb, s, LANES), lambda bg, hp, qi: (bg, 0, hp)))
        operands.append(key_bias3)
    return pl.pallas_call(
        functools.partial(_flash_kernel, moba=moba, nb=nb, t=t),
        grid=(b // nb, N_HEAD_BLOCKS, s // q_rows),
        in_specs=in_specs,
        out_specs=out_spec,
        out_shape=jax.ShapeDtypeStruct((b, s, BRANCH_WIDTH), BF16),
        scratch_shapes=scratch,
        compiler_params=_cparams(("parallel", "parallel", "arbitrary")),
        name="moba_attn" if moba else "forgetting_attn",
    )(*operands)


def _mix_kernel(x_ref, oa_ref, ob_ref, oc_ref, g_ref, wb_ref, wo_ref, y_ref):
    d = x_ref.shape[1]
    mixed = None
    for n, o_ref in enumerate((oa_ref, ob_ref, oc_ref)):
        gate = jax.nn.sigmoid(g_ref[:, n * d:(n + 1) * d].astype(F32))
        term = gate * _dot(o_ref[...], wb_ref[n])
        mixed = term if mixed is None else mixed + term
    y_ref[...] = x_ref[...] + _dot(mixed.astype(BF16), wo_ref[...])


def _mix(x, o_a, o_b, o_c, proj, wb, wo, layer, tm):
    n, d = x.shape
    o_spec = pl.BlockSpec((tm, BRANCH_WIDTH), lambda i: (i, 0))
    return pl.pallas_call(
        _mix_kernel,
        grid=(n // tm,),
        in_specs=[
            pl.BlockSpec((tm, d), lambda i: (i, 0)),
            o_spec, o_spec, o_spec,
            pl.BlockSpec((tm, N_BRANCH * d), lambda i: (i, 0)),
            pl.BlockSpec((None, N_BRANCH, BRANCH_WIDTH, d), lambda i: (layer, 0, 0, 0)),
            pl.BlockSpec((None, d, d), lambda i: (layer, 0, 0)),
        ],
        out_specs=pl.BlockSpec((tm, d), lambda i: (i, 0)),
        out_shape=jax.ShapeDtypeStruct((n, d), F32),
        compiler_params=_cparams(("parallel",)),
        name="gated_mix_out_proj",
    )(x, o_a, o_b, o_c, proj, wb, wo)


def _mlp_kernel(x_ref, g_ref, wu_ref, wd_ref, gf_ref, y_ref, h_ref, acc_ref, *, final_norm):
    f = pl.program_id(1)

    @pl.when(f == 0)
    def _():
        h_ref[...] = _rms(x_ref[...], g_ref[...]).astype(BF16)
        acc_ref[...] = jnp.zeros_like(acc_ref)

    hid = jnp.square(jnp.maximum(_dot(h_ref[...], wu_ref[...]), 0.0))
    acc_ref[...] += _dot(hid.astype(BF16), wd_ref[...])

    @pl.when(f == pl.num_programs(1) - 1)
    def _():
        y = x_ref[...] + acc_ref[...]
        if final_norm:
            y = _rms(y, gf_ref[...])
        y_ref[...] = y


def _mlp(x, g, wu, wd, g_final, layer, final_norm, tm, tf):
    n, d = x.shape
    d_ff = wu.shape[2]
    return pl.pallas_call(
        functools.partial(_mlp_kernel, final_norm=final_norm),
        grid=(n // tm, d_ff // tf),
        in_specs=[
            pl.BlockSpec((tm, d), lambda i, f: (i, 0)),
            pl.BlockSpec((1, d), lambda i, f: (0, 0)),
            pl.BlockSpec((None, d, tf), lambda i, f: (layer, 0, f)),
            pl.BlockSpec((None, tf, d), lambda i, f: (layer, f, 0)),
            pl.BlockSpec((1, d), lambda i, f: (0, 0)),
        ],
        out_specs=pl.BlockSpec((tm, d), lambda i, f: (i, 0)),
        out_shape=jax.ShapeDtypeStruct((n, d), F32),
        scratch_shapes=[pltpu.VMEM((tm, d), BF16), pltpu.VMEM((tm, d), F32)],
        compiler_params=_cparams(("parallel", "arbitrary")),
        name="relu2_mlp",
    )(x, g, wu, wd, g_final)


REPACK_ROWS = 512


def _repack_kernel(w_ref, wf_ref, main_ref, f_ref):
    i = pl.program_id(1)
    q_chunks = BRANCH_WIDTH // REPACK_ROWS
    n_gate_chunks = N_BRANCH * w_ref.shape[2] // REPACK_ROWS
    mixer_chunk = jnp.maximum(i - n_gate_chunks, 0) % (3 * q_chunks)
    is_q = (i >= n_gate_chunks) & (mixer_chunk < q_chunks)
    scale = jnp.where(is_q, HEAD_DIM ** -0.5 * LOG2E, 1.0)
    main_ref[...] = (w_ref[0] * scale).astype(BF16)

    @pl.when(i == 0)
    def _():
        pad = jnp.zeros((f_ref.shape[0] - N_HEADS, f_ref.shape[1]), F32)
        f_ref[...] = jnp.concatenate([wf_ref[0], pad], axis=0).astype(BF16)


def _prep_in_proj(w_in):
    depth, d, cols = w_in.shape
    w_t = jnp.swapaxes(w_in, 1, 2)
    bw = BRANCH_WIDTH
    r = REPACK_ROWS
    f_lo = 6 * bw
    c_lo = f_lo + N_HEADS
    g_lo = c_lo + 3 * bw
    main_rows = cols - N_HEADS
    n_gate = (cols - g_lo) // r
    n_ab = f_lo // r

    def src_row(l, i):
        row = jnp.where(i < n_gate, g_lo + r * i,
                        jnp.where(i < n_gate + n_ab, r * (i - n_gate), c_lo + r * (i - n_gate - n_ab)))
        return l, pl.multiple_of(row, N_HEADS), 0

    return pl.pallas_call(
        _repack_kernel,
        grid=(depth, main_rows // r),
        in_specs=[pl.BlockSpec((pl.Element(1), pl.Element(r), pl.Element(d)), src_row),
                  pl.BlockSpec((pl.Element(1), pl.Element(N_HEADS), pl.Element(d)), lambda l, i: (l, f_lo, 0))],
        out_specs=[pl.BlockSpec((None, r, d), lambda l, i: (l, i, 0)),
                   pl.BlockSpec((None, LANES, d), lambda l, i: (l, 0, 0))],
        out_shape=[jax.ShapeDtypeStruct((depth, main_rows, d), BF16),
                   jax.ShapeDtypeStruct((depth, LANES, d), BF16)],
        compiler_params=_cparams(("parallel", "arbitrary")),
        name="repack_w_in",
    )(w_t, w_t)


def kernel(x, norm_mix, w_in, b_forget, w_branch, w_out, norm_mlp, w_up, w_down, norm_final):
    b, s, d = x.shape
    depth = w_in.shape[0]
    n = b * s
    assert s % ATT_TILE == 0 and d % LANES == 0

    w_main, w_f = _prep_in_proj(w_in)
    wb = w_branch.astype(BF16)
    wo = w_out.astype(BF16)
    wu = w_up.astype(BF16)
    wd = w_down.astype(BF16)
    b_f = jnp.pad(b_forget, ((0, 0), (0, LANES - N_HEADS)))[:, None, :]

    tm = min(1024, s)
    xf = x.reshape(n, d)
    for l in range(depth):
        g_mix = norm_mix[l][None, :]
        proj, key_bias = _in_proj(xf, g_mix, w_main, w_f, b_f[l], l, s, tm, 3840, 256)
        proj3 = proj.reshape(b, s, -1)
        o_a = _sb_attention(proj3, d).reshape(n, -1)
        o_b = _flash_attention(proj3, key_bias.reshape(b, s, -1), d, False).reshape(n, -1)
        o_c = _flash_attention(proj3, None, d, True).reshape(n, -1)
        xf = _mix(xf, o_a, o_b, o_c, proj, wb, wo, l, min(1024, n))
        xf = _mlp(xf, norm_mlp[l][None, :], wu, wd, norm_final[None, :],
                  l, l == depth - 1, min(1024, n), 1024)
    return xf.reshape(b, s, d)
```

```python
import functools
import math

import jax
import jax.numpy as jnp
import numpy as np
from jax import lax
from jax.experimental import pallas as pl
from jax.experimental.pallas import tpu as pltpu

F32 = jnp.float32
BF16 = jnp.bfloat16

HEAD_DIM = 64
N_HEADS = 8
BRANCH_WIDTH = N_HEADS * HEAD_DIM
N_BRANCH = 3
MOBA_BLOCK = 256
MOBA_TOPK = 3
RMS_EPS = 1e-6
NEG_INF = -1e30
LOG2E = math.log2(math.e)
LANES = 128
HEADS_PER_BLOCK = LANES // HEAD_DIM
N_HEAD_BLOCKS = BRANCH_WIDTH // LANES
N_BIAS_PIECES = 3
ATT_TILE = MOBA_BLOCK
Q_TILES_PER_STEP = 4
MIXER_BATCH = 4
SB_EXIT = 152.0
FLASH_SKIP = 152.0
SP2_CLAMP = 120.0
NORM_SLACK = 1.001
V_ROWS = HEAD_DIM + 16
VMEM_LIMIT = 60000 * 1024


def _cparams(sem):
    return pltpu.CompilerParams(dimension_semantics=sem, vmem_limit_bytes=VMEM_LIMIT)


def _rms(x, g):
    ms = jnp.mean(x * x, axis=-1, keepdims=True)
    return x * lax.rsqrt(ms + RMS_EPS) * g


def _dot(a, b):
    return jnp.dot(a, b, preferred_element_type=F32)


def _dot_nt(a, b):
    return lax.dot_general(a, b, (((1,), (1,)), ((), ())), preferred_element_type=F32)


def _transposed(x):
    return x.astype(F32).T.astype(BF16)


def _split2(x):
    hi = x.astype(BF16)
    lo = (x - hi.astype(F32)).astype(BF16)
    return hi, lo


def _split3(x):
    hi = x.astype(BF16)
    r = x - hi.astype(F32)
    mid = r.astype(BF16)
    lo = (r - mid.astype(F32)).astype(BF16)
    return hi, mid, lo


def _bias_lane(hh):
    return (1 - hh) * HEAD_DIM


def _in_proj_kernel(x_ref, g_ref, w_ref, wf_ref, b_ref, tri_ref, place_ref,
                    o_ref, e_ref, h_ref, carry_ref, *, tiles_per_seq):
    @pl.when(pl.program_id(1) == 0)
    def _():
        h = _rms(x_ref[...], g_ref[...]).astype(BF16)
        h_ref[...] = h

        @pl.when(pl.program_id(0) % tiles_per_seq == 0)
        def _():
            carry_ref[...] = jnp.zeros_like(carry_ref)

        y = _dot_nt(h, wf_ref[...]) + b_ref[...]
        logf = jnp.minimum(y, 0.0) - jnp.log(1.0 + jnp.exp(-jnp.abs(y)))
        pieces = _split3(logf)
        tri = tri_ref[...]
        tc = tri.shape[0]
        cum = carry_ref[...]
        for c in range(h.shape[0] // tc):
            rows = slice(c * tc, (c + 1) * tc)
            cum = cum[-1:, :]
            for piece in pieces:
                cum = cum + _dot(tri, piece[rows])
            bias_pieces = jnp.concatenate(_split3(cum * (-LOG2E)), axis=1)
            e_ref[rows, :] = _dot(bias_pieces, place_ref[...]).astype(BF16)
        carry_ref[...] = cum[-1:, :]

    o_ref[...] = _dot_nt(h_ref[...], w_ref[...]).astype(o_ref.dtype)


def _bias_placement():
    place = np.zeros((N_BIAS_PIECES, LANES, N_HEAD_BLOCKS * LANES), np.float32)
    for h in range(N_HEADS):
        hp, hh = divmod(h, HEADS_PER_BLOCK)
        for i in range(N_BIAS_PIECES):
            place[i, h, hp * LANES + _bias_lane(hh) + i] = 1.0
    return jnp.asarray(place.reshape(N_BIAS_PIECES * LANES, -1), BF16)


def _in_proj(x, g, w, w_f, b_f, layer, s, tm, tn, tc):
    n, d = x.shape
    cols = w.shape[1]
    tri = jnp.asarray(np.tril(np.ones((tc, tc), np.float32)), BF16)
    place = _bias_placement()
    e_cols = place.shape[1]
    const2 = lambda i, j: (0, 0)
    return pl.pallas_call(
        functools.partial(_in_proj_kernel, tiles_per_seq=s // tm),
        grid=(n // tm, cols // tn),
        in_specs=[
            pl.BlockSpec((tm, d), lambda i, j: (i, 0)),
            pl.BlockSpec((1, d), const2),
            pl.BlockSpec((None, tn, d), lambda i, j: (layer, j, 0)),
            pl.BlockSpec((None, LANES, d), lambda i, j: (layer, 0, 0)),
            pl.BlockSpec((1, LANES), const2),
            pl.BlockSpec((tc, tc), const2),
            pl.BlockSpec(place.shape, const2),
        ],
        out_specs=[pl.BlockSpec((tm, tn), lambda i, j: (i, j)),
                   pl.BlockSpec((tm, e_cols), lambda i, j: (i, 0))],
        out_shape=[jax.ShapeDtypeStruct((n, cols), BF16),
                   jax.ShapeDtypeStruct((n, e_cols), BF16)],
        scratch_shapes=[pltpu.VMEM((tm, d), BF16), pltpu.VMEM((1, LANES), F32)],
        compiler_params=_cparams(("arbitrary", "arbitrary")),
        name="norm_in_proj",
    )(x, g, w, w_f, b_f, tri, place)


def _head_lane_mask(hh):
    lane = lax.broadcasted_iota(jnp.int32, (1, LANES), 1)
    return (lane >= hh * HEAD_DIM) & (lane < (hh + 1) * HEAD_DIM)


def _tile_rel(t):
    return (lax.broadcasted_iota(jnp.int32, (t, t), 0)
            - lax.broadcasted_iota(jnp.int32, (t, t), 1))


def _mixer_specs(mixer, nb, t, s, d):
    qoff = N_BRANCH * d // LANES + (3 * mixer) * N_HEAD_BLOCKS
    koff = qoff + N_HEAD_BLOCKS
    voff = koff + N_HEAD_BLOCKS
    return [
        pl.BlockSpec((nb, t, LANES), lambda bg, hp, qi: (bg, qi, qoff + hp)),
        pl.BlockSpec((nb, s, LANES), lambda bg, hp, qi: (bg, 0, koff + hp)),
        pl.BlockSpec((nb, s, LANES), lambda bg, hp, qi: (bg, 0, voff + hp)),
    ], pl.BlockSpec((nb, t, LANES), lambda bg, hp, qi: (bg, qi, hp))


def _sb_kernel(q_ref, k_ref, v_ref, o_ref, vt_ref, *scratch, nb, t):
    n_blocks = k_ref.shape[1] // t
    q_tiles = q_ref.shape[1] // t
    streams = [(bi, hh) for bi in range(nb) for hh in range(HEADS_PER_BLOCK)]

    @pl.when(pl.program_id(2) == 0)
    def _build():
        def chunk(c, carry):
            rows = pl.ds(pl.multiple_of(c * t, t), t)
            for st, (bi, hh) in enumerate(streams):
                v2t = v_ref[bi, rows, :].astype(F32).T
                vt_ref[st, :, rows] = v2t[hh * HEAD_DIM:(hh + 1) * HEAD_DIM, :].astype(BF16)
            return carry

        lax.fori_loop(0, n_blocks, chunk, 0)

    def q_tile(sub, carry):
        q_rows = pl.ds(pl.multiple_of(sub * t, t), t)
        _sb_q_tile(pl.program_id(2) * q_tiles + sub, q_rows, q_ref, k_ref, o_ref, vt_ref, *scratch, nb=nb, t=t)
        return carry

    lax.fori_loop(0, q_tiles, q_tile, 0)


def _sb_q_tile(qi, q_rows, q_ref, k_ref, o_ref, vt_ref, acc_ref, cs_ref, r_ref, s_even_ref, s_odd_ref, *, nb, t):
    streams = [(bi, hh) for bi in range(nb) for hh in range(HEADS_PER_BLOCK)]
    q_m = []
    for bi, hh in streams:
        q2 = q_ref[bi, q_rows, :]
        q_m.append(_transposed(jnp.where(_head_lane_mask(hh), q2, jnp.zeros_like(q2))))

    rel = _tile_rel(t)
    strictly_past = rel < 0
    upper = jnp.where(rel <= 0, 1.0, 0.0).astype(BF16)

    stream_groups = [range(g, g + HEADS_PER_BLOCK) for g in range(0, len(streams), HEADS_PER_BLOCK)]

    def scores(s_ref, j, group):
        rows = pl.ds(pl.multiple_of(j * t, t), t)
        for st in group:
            s_ref[st] = _dot(k_ref[streams[st][0], rows, :], q_m[st])

    def softplus_sums(s_ref, diag, group):
        for st in group:
            z = s_ref[st]
            sp = jnp.maximum(z, jnp.log2(1.0 + jnp.exp2(jnp.minimum(z, SP2_CLAMP))))
            if diag:
                sp = jnp.where(strictly_past, sp, 0.0)
            r = _dot(upper, sp.astype(BF16)) + cs_ref[st]
            r_ref[st] = r
            cs_ref[st] = r[0:1, :]

    def weigh_values(s_ref, j, diag, group):
        rows = pl.ds(pl.multiple_of(j * t, t), t)
        for st in group:
            w = jnp.exp2(s_ref[st] - r_ref[st])
            if diag:
                w = jnp.where(strictly_past, w, 0.0)
            acc_ref[st] += _dot(vt_ref[st, :, rows], w.astype(BF16))

    def step(cur_ref, next_ref, j, j_next, diag):
        for group in stream_groups:
            scores(next_ref, j_next, group)
            softplus_sums(cur_ref, diag, group)
        for group in stream_groups:
            weigh_values(cur_ref, j, diag, group)

    def next_distance(i):
        lowest = cs_ref[0]
        for st in range(1, len(streams)):
            lowest = jnp.minimum(lowest, cs_ref[st])
        return jnp.where(jnp.min(lowest) > SB_EXIT, qi + 1, i + 1)

    acc_ref[...] = jnp.zeros(acc_ref.shape, F32)
    cs_ref[...] = jnp.zeros(cs_ref.shape, F32)
    for group in stream_groups:
        scores(s_even_ref, qi, group)
    step(s_even_ref, s_odd_ref, qi, jnp.maximum(qi - 1, 0), True)

    def body(i):
        j = qi - i
        j_next = jnp.maximum(j - 1, 0)

        @pl.when(i % 2 == 1)
        def _():
            step(s_odd_ref, s_even_ref, j, j_next, False)

        @pl.when(i % 2 == 0)
        def _():
            step(s_even_ref, s_odd_ref, j, j_next, False)

        return next_distance(i)

    lax.while_loop(lambda i: i <= qi, body, jnp.int32(1))

    for bi in range(nb):
        halves = [acc_ref[bi * HEADS_PER_BLOCK + hh] for hh in range(HEADS_PER_BLOCK)]
        o_ref[bi, q_rows, :] = jnp.concatenate(halves, axis=0).T.astype(o_ref.dtype)


def _sb_attention(proj3, d):
    b, s, _ = proj3.shape
    t = ATT_TILE
    nb = MIXER_BATCH if b % MIXER_BATCH == 0 else 1
    n_streams = nb * HEADS_PER_BLOCK
    q_rows = min(Q_TILES_PER_STEP * t, s)
    in_specs, out_spec = _mixer_specs(0, nb, q_rows, s, d)
    return pl.pallas_call(
        functools.partial(_sb_kernel, nb=nb, t=t),
        grid=(b // nb, N_HEAD_BLOCKS, s // q_rows),
        in_specs=in_specs,
        out_specs=out_spec,
        out_shape=jax.ShapeDtypeStruct((b, s, BRANCH_WIDTH), BF16),
        scratch_shapes=[pltpu.VMEM((n_streams, HEAD_DIM, s), BF16),
                        pltpu.VMEM((n_streams, HEAD_DIM, t), F32),
                        pltpu.VMEM((n_streams, 1, t), F32),
                        pltpu.VMEM((n_streams, t, t), F32),
                        pltpu.VMEM((n_streams, t, t), F32),
                        pltpu.VMEM((n_streams, t, t), F32)],
        compiler_params=_cparams(("parallel", "parallel", "arbitrary")),
        name="stickbreak_attn",
    )(proj3, proj3, proj3)


def _flash_kernel(*refs, moba, nb, t):
    q_tiles = refs[0].shape[1] // t

    def q_tile(sub, carry):
        q_rows = pl.ds(pl.multiple_of(sub * t, t), t)
        first = (pl.program_id(2) == 0) & (sub == 0)
        _flash_q_tile(pl.program_id(2) * q_tiles + sub, q_rows, first, *refs, moba=moba, nb=nb, t=t)
        return carry

    lax.fori_loop(0, q_tiles, q_tile, 0)


def _flash_q_tile(qi, q_rows, first, *refs, moba, nb, t):
    if moba:
        (q_ref, k_ref, v_ref, o_ref, kaug_ref, vt_ref, acc_ref, m_ref, s0_ref, s1_ref, s2_ref, s3_ref,
         bmax_ref, knorm_ref, kmean_ref, sel_ref) = refs
    else:
        (q_ref, k_ref, v_ref, e_ref, o_ref, kaug_ref, vt_ref, acc_ref, m_ref, s0_ref, s1_ref, s2_ref,
         s3_ref, bmax_ref, knorm_ref) = refs
    s_bufs = (s0_ref, s1_ref, s2_ref, s3_ref)
    n_blocks = k_ref.shape[1] // t
    hp = pl.program_id(1)
    lane = lax.broadcasted_iota(jnp.int32, (1, LANES), 1)
    streams = [(bi, hh) for bi in range(nb) for hh in range(HEADS_PER_BLOCK)]

    def piece_lanes(hh):
        e0 = _bias_lane(hh)
        return (lane >= e0) & (lane < e0 + N_BIAS_PIECES)

    head_of_lane = jnp.where(lax.broadcasted_iota(jnp.int32, (LANES, LANES), 0) >= HEAD_DIM, 1, 0)
    head_sum = jnp.where(head_of_lane == lax.broadcasted_iota(jnp.int32, (LANES, LANES), 1), 1.0, 0.0).astype(BF16)

    def head_sq_norm_max(x32):
        hi, lo = _split2(x32 * x32)
        return jnp.max(_dot(hi, head_sum) + _dot(lo, head_sum), axis=0, keepdims=True)

    def head_col(row, hh):
        return jnp.sum(jnp.where(lane == hh, row, 0.0), axis=1, keepdims=True)

    @pl.when(first)
    def _build():
        kaug_ref[:, 0:t, :] = jnp.zeros((len(streams), t, LANES), BF16)
        bmax_ref[...] = jnp.full(bmax_ref.shape, NEG_INF, F32)
        knorm_ref[...] = jnp.zeros(knorm_ref.shape, F32)
        tail = jnp.where(lax.broadcasted_iota(jnp.int32, (V_ROWS - HEAD_DIM, t), 0) == 0, 1.0, 0.0)

        def chunk(c, carry):
            start = pl.multiple_of(c * t, t)
            rows = pl.ds(start, t)
            extras = []
            if moba:
                pos = (lax.broadcasted_iota(jnp.int32, (t, LANES), 0) + start).astype(F32)
                for hh in range(HEADS_PER_BLOCK):
                    h = hp * HEADS_PER_BLOCK + hh
                    slope = pltpu.bitcast(jnp.full((1, LANES), (126 - h) << 23, jnp.int32), F32)
                    extra = jnp.zeros((t, LANES), F32)
                    for i, piece in enumerate(_split3(pos * slope * LOG2E)):
                        extra = jnp.where(lane == _bias_lane(hh) + i, piece.astype(F32), extra)
                    extras.append(extra.astype(BF16))
            for bi in range(nb):
                k2 = k_ref[bi, rows, :]
                k32 = k2.astype(F32)
                v2t = v_ref[bi, rows, :].astype(F32).T
                if moba:
                    kmean_ref[bi, pl.ds(c, 1), :] = jnp.sum(k32, axis=0, keepdims=True) * (1.0 / t)
                for hh in range(HEADS_PER_BLOCK):
                    st = bi * HEADS_PER_BLOCK + hh
                    extra = extras[hh] if moba else e_ref[bi, rows, :]
                    kaug_ref[st, pl.ds(start + t, t), :] = jnp.where(_head_lane_mask(hh), k2, extra)
                    v_h = v2t[hh * HEAD_DIM:(hh + 1) * HEAD_DIM, :]
                    vt_ref[st, :, rows] = jnp.concatenate([v_h, tail], axis=0).astype(BF16)
                    piece_max = jnp.max(extra.astype(F32), axis=0, keepdims=True)
                    bias_max = jnp.sum(jnp.where(piece_lanes(hh), piece_max, 0.0), axis=1, keepdims=True)
                    run = jnp.maximum(bmax_ref[st, pl.ds(jnp.maximum(c - 1, 0), 1), :], bias_max)
                    bmax_ref[st, pl.ds(c, 1), :] = run
                knorm_ref[bi] = jnp.maximum(knorm_ref[bi], head_sq_norm_max(k32))
            return carry

        lax.fori_loop(0, n_blocks, chunk, 0)

    q_aug = []
    reach = []
    for bi in range(nb):
        q2 = q_ref[bi, q_rows, :]
        qk_sq = head_sq_norm_max(q2.astype(F32)) * knorm_ref[bi]
        for hh in range(HEADS_PER_BLOCK):
            ones = jnp.where(piece_lanes(hh), 1.0, 0.0).astype(BF16)
            q_aug.append(_transposed(jnp.where(_head_lane_mask(hh), q2, jnp.broadcast_to(ones, q2.shape))))
            reach.append(jnp.sqrt(head_col(qk_sq, hh)) * NORM_SLACK)

    if moba:
        blk = lax.broadcasted_iota(jnp.int32, (n_blocks, 1), 0)
        blk_f = blk.astype(F32)
        for st, (bi, hh) in enumerate(streams):
            km = jnp.where(_head_lane_mask(hh), kmean_ref[bi], 0.0)
            pieces = _dot(jnp.concatenate(_split3(km), axis=0), q_aug[st])
            route = pieces[:n_blocks] + pieces[n_blocks:2 * n_blocks] + pieces[2 * n_blocks:]
            route = jnp.where(blk < qi, route, NEG_INF)
            sel = jnp.zeros((n_blocks, t), F32)
            for _ in range(MOBA_TOPK):
                top = jnp.max(route, axis=0, keepdims=True)
                first = jnp.min(jnp.where(route == top, blk_f, float(n_blocks)), axis=0, keepdims=True)
                hit = blk_f == first
                sel = jnp.where(hit, 1.0, sel)
                route = jnp.where(hit, -jnp.inf, route)
            sel_ref[st] = jnp.where(blk < qi, sel, 0.0)

    valid = _tile_rel(t) <= 0

    all_streams = range(len(streams))
    stream_groups = [range(g, g + HEADS_PER_BLOCK) for g in range(0, len(streams), HEADS_PER_BLOCK)]

    def scores(s_ref, j, group=all_streams):
        rows = pl.ds(pl.multiple_of((j + 1) * t, t), t)
        for st in group:
            s_ref[st] = _dot(kaug_ref[st, rows, :], q_aug[st])

    def scores_pair(near_ref, far_ref, near_dist, group):
        far = jnp.maximum(qi - near_dist - 1, -1)
        rows = pl.ds(pl.multiple_of((far + 1) * t, t), 2 * t)
        for st in group:
            both = _dot(kaug_ref[st, rows, :], q_aug[st])
            far_ref[st] = both[:t]
            near_ref[st] = both[t:]

    def accumulate(s_ref, j, diag, group=all_streams):
        rows = pl.ds(pl.multiple_of(j * t, t), t)
        for st in group:
            s_t = s_ref[st]
            if diag:
                s_t = jnp.where(valid, s_t, NEG_INF)
            m_old = m_ref[st]
            m_new = jnp.maximum(m_old, jnp.max(s_t, axis=0, keepdims=True))
            m_sub = m_new
            if moba and not diag:
                picked = sel_ref[st, pl.ds(j, 1), :] > 0.0
                m_new = jnp.where(picked, m_new, m_old)
                m_sub = jnp.where(picked, m_new, -NEG_INF)
            alpha = jnp.exp2(m_old - m_new)
            p = jnp.exp2(s_t - m_sub)
            acc_ref[st] = alpha * acc_ref[st] + _dot(vt_ref[st, :, rows], p.astype(BF16))
            m_ref[st] = m_new

    def accumulate_pair(near_ref, far_ref, j, group):
        rows = pl.ds(pl.multiple_of((j - 1) * t, t), 2 * t)
        for st in group:
            s_near = near_ref[st]
            s_far = far_ref[st]
            m_old = m_ref[st]
            max_near = jnp.max(s_near, axis=0, keepdims=True)
            max_far = jnp.max(s_far, axis=0, keepdims=True)
            if moba:
                pick_near = sel_ref[st, pl.ds(j, 1), :] > 0.0
                pick_far = sel_ref[st, pl.ds(j - 1, 1), :] > 0.0
                max_near = jnp.where(pick_near, max_near, NEG_INF)
                max_far = jnp.where(pick_far, max_far, NEG_INF)
            m_new = jnp.maximum(m_old, jnp.maximum(max_near, max_far))
            sub_near = sub_far = m_new
            if moba:
                sub_near = jnp.where(pick_near, m_new, -NEG_INF)
                sub_far = jnp.where(pick_far, m_new, -NEG_INF)
            alpha = jnp.exp2(m_old - m_new)
            p = jnp.concatenate([jnp.exp2(s_far - sub_far).astype(BF16),
                                 jnp.exp2(s_near - sub_near).astype(BF16)], axis=0)
            acc_ref[st] = alpha * acc_ref[st] + _dot(vt_ref[st, :, rows], p)
            m_ref[st] = m_new

    m_ref[...] = jnp.full(m_ref.shape, NEG_INF, F32)
    acc_ref[...] = jnp.zeros(acc_ref.shape, F32)
    scores(s_bufs[0], qi)
    for group in stream_groups:
        scores_pair(s_bufs[1], s_bufs[2], 1, group)
        accumulate(s_bufs[0], qi, True, group)

    blk_col = lax.broadcasted_iota(jnp.int32, (n_blocks, 1), 0)
    fewest_dead = None
    for st in range(len(streams)):
        thr = jnp.min(m_ref[st], axis=1, keepdims=True) - FLASH_SKIP - reach[st]
        dead = (bmax_ref[st] < thr) & (blk_col < qi)
        n_dead = jnp.sum(jnp.where(dead, 1.0, 0.0), axis=0, keepdims=True)
        fewest_dead = n_dead if fewest_dead is None else jnp.minimum(fewest_dead, n_dead)
    n_past = qi - jnp.max(fewest_dead).astype(jnp.int32)

    def pair(first, cur, nxt):
        for group in stream_groups:
            scores_pair(s_bufs[nxt[0]], s_bufs[nxt[1]], first + 2, group)
            accumulate_pair(s_bufs[cur[0]], s_bufs[cur[1]], qi - first, group)

    def body(p, carry):
        first = 2 * p + 1

        @pl.when(p % 2 == 0)
        def _():
            pair(first, (1, 2), (3, 0))

        @pl.when(p % 2 == 1)
        def _():
            pair(first, (3, 0), (1, 2))

        return carry

    lax.fori_loop(0, n_past // 2, body, 0)

    for slot in (1, 3):
        @pl.when(n_past % 4 == slot)
        def _(slot=slot):
            accumulate(s_bufs[slot], qi - n_past, False)

    for bi in range(nb):
        halves = []
        for hh in range(HEADS_PER_BLOCK):
            acc = acc_ref[bi * HEADS_PER_BLOCK + hh]
            halves.append(acc[:HEAD_DIM] * (1.0 / acc[HEAD_DIM:HEAD_DIM + 1]))
        o_ref[bi, q_rows, :] = jnp.concatenate(halves, axis=0).T.astype(o_ref.dtype)


def _flash_attention(proj3, key_bias3, d, moba):
    b, s, _ = proj3.shape
    t = ATT_TILE
    nb = MIXER_BATCH if b % MIXER_BATCH == 0 else 1
    n_blocks = s // t
    n_streams = nb * HEADS_PER_BLOCK
    q_rows = min(Q_TILES_PER_STEP * t, s)
    in_specs, out_spec = _mixer_specs(2 if moba else 1, nb, q_rows, s, d)
    operands = [proj3, proj3, proj3]
    scratch = [pltpu.VMEM((n_streams, s + t, LANES), BF16),
               pltpu.VMEM((n_streams, V_ROWS, s), BF16),
               pltpu.VMEM((n_streams, V_ROWS, t), F32),
               pltpu.VMEM((n_streams, 1, t), F32),
               pltpu.VMEM((n_streams, t, t), F32),
               pltpu.VMEM((n_streams, t, t), F32),
               pltpu.VMEM((n_streams, t, t), F32),
               pltpu.VMEM((n_streams, t, t), F32),
               pltpu.VMEM((n_streams, n_blocks, 1), F32),
               pltpu.VMEM((nb, 1, LANES), F32)]
    if moba:
        scratch += [pltpu.VMEM((nb, n_blocks, LANES), F32),
                    pltpu.VMEM((n_streams, n_blocks, t), F32)]
    else:
        in_specs.append(pl.BlockSpec((nb, s, LANES), lambda bg, hp, qi: (bg, 0, hp)))
        operands.append(key_bias3)
    return pl.pallas_call(
        functools.partial(_flash_kernel, moba=moba, nb=nb, t=t),
        grid=(b // nb, N_HEAD_BLOCKS, s // q_rows),
        in_specs=in_specs,
        out_specs=out_spec,
        out_shape=jax.ShapeDtypeStruct((b, s, BRANCH_WIDTH), BF16),
        scratch_shapes=scratch,
        compiler_params=_cparams(("parallel", "parallel", "arbitrary")),
        name="moba_attn" if moba else "forgetting_attn",
    )(*operands)


def _mix_kernel(x_ref, oa_ref, ob_ref, oc_ref, g_ref, wb_ref, wo_ref, y_ref):
    d = x_ref.shape[1]
    mixed = None
    for n, o_ref in enumerate((oa_ref, ob_ref, oc_ref)):
        gate = jax.nn.sigmoid(g_ref[:, n * d:(n + 1) * d].astype(F32))
        term = gate * _dot(o_ref[...], wb_ref[n])
        mixed = term if mixed is None else mixed + term
    y_ref[...] = x_ref[...] + _dot(mixed.astype(BF16), wo_ref[...])


def _mix(x, o_a, o_b, o_c, proj, wb, wo, layer, tm):
    n, d = x.shape
    o_spec = pl.BlockSpec((tm, BRANCH_WIDTH), lambda i: (i, 0))
    return pl.pallas_call(
        _mix_kernel,
        grid=(n // tm,),
        in_specs=[
            pl.BlockSpec((tm, d), lambda i: (i, 0)),
            o_spec, o_spec, o_spec,
            pl.BlockSpec((tm, N_BRANCH * d), lambda i: (i, 0)),
            pl.BlockSpec((None, N_BRANCH, BRANCH_WIDTH, d), lambda i: (layer, 0, 0, 0)),
            pl.BlockSpec((None, d, d), lambda i: (layer, 0, 0)),
        ],
        out_specs=pl.BlockSpec((tm, d), lambda i: (i, 0)),
        out_shape=jax.ShapeDtypeStruct((n, d), F32),
        compiler_params=_cparams(("parallel",)),
        name="gated_mix_out_proj",
    )(x, o_a, o_b, o_c, proj, wb, wo)


def _mlp_kernel(x_ref, g_ref, wu_ref, wd_ref, gf_ref, y_ref, h_ref, acc_ref, *, final_norm):
    f = pl.program_id(1)

    @pl.when(f == 0)
    def _():
        h_ref[...] = _rms(x_ref[...], g_ref[...]).astype(BF16)
        acc_ref[...] = jnp.zeros_like(acc_ref)

    hid = jnp.square(jnp.maximum(_dot(h_ref[...], wu_ref[...]), 0.0))
    acc_ref[...] += _dot(hid.astype(BF16), wd_ref[...])

    @pl.when(f == pl.num_programs(1) - 1)
    def _():
        y = x_ref[...] + acc_ref[...]
        if final_norm:
            y = _rms(y, gf_ref[...])
        y_ref[...] = y


def _mlp(x, g, wu, wd, g_final, layer, final_norm, tm, tf):
    n, d = x.shape
    d_ff = wu.shape[2]
    return pl.pallas_call(
        functools.partial(_mlp_kernel, final_norm=final_norm),
        grid=(n // tm, d_ff // tf),
        in_specs=[
            pl.BlockSpec((tm, d), lambda i, f: (i, 0)),
            pl.BlockSpec((1, d), lambda i, f: (0, 0)),
            pl.BlockSpec((None, d, tf), lambda i, f: (layer, 0, f)),
            pl.BlockSpec((None, tf, d), lambda i, f: (layer, f, 0)),
            pl.BlockSpec((1, d), lambda i, f: (0, 0)),
        ],
        out_specs=pl.BlockSpec((tm, d), lambda i, f: (i, 0)),
        out_shape=jax.ShapeDtypeStruct((n, d), F32),
        scratch_shapes=[pltpu.VMEM((tm, d), BF16), pltpu.VMEM((tm, d), F32)],
        compiler_params=_cparams(("parallel", "arbitrary")),
        name="relu2_mlp",
    )(x, g, wu, wd, g_final)


REPACK_ROWS = 512


def _repack_kernel(w_ref, wf_ref, main_ref, f_ref):
    i = pl.program_id(1)
    q_chunks = BRANCH_WIDTH // REPACK_ROWS
    n_gate_chunks = N_BRANCH * w_ref.shape[2] // REPACK_ROWS
    mixer_chunk = jnp.maximum(i - n_gate_chunks, 0) % (3 * q_chunks)
    is_q = (i >= n_gate_chunks) & (mixer_chunk < q_chunks)
    scale = jnp.where(is_q, HEAD_DIM ** -0.5 * LOG2E, 1.0)
    main_ref[...] = (w_ref[0] * scale).astype(BF16)

    @pl.when(i == 0)
    def _():
        pad = jnp.zeros((f_ref.shape[0] - N_HEADS, f_ref.shape[1]), F32)
        f_ref[...] = jnp.concatenate([wf_ref[0], pad], axis=0).astype(BF16)


def _prep_in_proj(w_in):
    depth, d, cols = w_in.shape
    w_t = jnp.swapaxes(w_in, 1, 2)
    bw = BRANCH_WIDTH
    r = REPACK_ROWS
    f_lo = 6 * bw
    c_lo = f_lo + N_HEADS
    g_lo = c_lo + 3 * bw
    main_rows = cols - N_HEADS
    n_gate = (cols - g_lo) // r
    n_ab = f_lo // r

    def src_row(l, i):
        row = jnp.where(i < n_gate, g_lo + r * i,
                        jnp.where(i < n_gate + n_ab, r * (i - n_gate), c_lo + r * (i - n_gate - n_ab)))
        return l, pl.multiple_of(row, N_HEADS), 0

    return pl.pallas_call(
        _repack_kernel,
        grid=(depth, main_rows // r),
        in_specs=[pl.BlockSpec((pl.Element(1), pl.Element(r), pl.Element(d)), src_row),
                  pl.BlockSpec((pl.Element(1), pl.Element(N_HEADS), pl.Element(d)), lambda l, i: (l, f_lo, 0))],
        out_specs=[pl.BlockSpec((None, r, d), lambda l, i: (l, i, 0)),
                   pl.BlockSpec((None, LANES, d), lambda l, i: (l, 0, 0))],
        out_shape=[jax.ShapeDtypeStruct((depth, main_rows, d), BF16),
                   jax.ShapeDtypeStruct((depth, LANES, d), BF16)],
        compiler_params=_cparams(("parallel", "arbitrary")),
        name="repack_w_in",
    )(w_t, w_t)


def kernel(x, norm_mix, w_in, b_forget, w_branch, w_out, norm_mlp, w_up, w_down, norm_final):
    b, s, d = x.shape
    depth = w_in.shape[0]
    n = b * s
    assert s % ATT_TILE == 0 and d % LANES == 0

    w_main, w_f = _prep_in_proj(w_in)
    wb = w_branch.astype(BF16)
    wo = w_out.astype(BF16)
    wu = w_up.astype(BF16)
    wd = w_down.astype(BF16)
    b_f = jnp.pad(b_forget, ((0, 0), (0, LANES - N_HEADS)))[:, None, :]

    tm = min(1024, s)
    xf = x.reshape(n, d)
    for l in range(depth):
        g_mix = norm_mix[l][None, :]
        proj, key_bias = _in_proj(xf, g_mix, w_main, w_f, b_f[l], l, s, tm, 3840, 256)
        proj3 = proj.reshape(b, s, -1)
        o_a = _sb_attention(proj3, d).reshape(n, -1)
        o_b = _flash_attention(proj3, key_bias.reshape(b, s, -1), d, False).reshape(n, -1)
        o_c = _flash_attention(proj3, None, d, True).reshape(n, -1)
        xf = _mix(xf, o_a, o_b, o_c, proj, wb, wo, l, min(1024, n))
        xf = _mlp(xf, norm_mlp[l][None, :], wu, wd, norm_final[None, :],
                  l, l == depth - 1, min(1024, n), 1024)
    return xf.reshape(b, s, d)
```

```python
import functools
import math

import jax
import jax.numpy as jnp
import numpy as np
from jax import lax
from jax.experimental import pallas as pl
from jax.experimental.pallas import tpu as pltpu

F32 = jnp.float32
BF16 = jnp.bfloat16

HEAD_DIM = 64
N_HEADS = 8
BRANCH_WIDTH = N_HEADS * HEAD_DIM
N_BRANCH = 3
MOBA_BLOCK = 256
MOBA_TOPK = 3
RMS_EPS = 1e-6
NEG_INF = -1e30
LOG2E = math.log2(math.e)
LANES = 128
HEADS_PER_BLOCK = LANES // HEAD_DIM
N_HEAD_BLOCKS = BRANCH_WIDTH // LANES
N_BIAS_PIECES = 3
ATT_TILE = MOBA_BLOCK
Q_TILES_PER_STEP = 4
MIXER_BATCH = 4
SB_EXIT = 152.0
FLASH_SKIP = 152.0
SP2_CLAMP = 120.0
NORM_SLACK = 1.001
V_ROWS = HEAD_DIM + 16
VMEM_LIMIT = 60000 * 1024


def _cparams(sem):
    return pltpu.CompilerParams(dimension_semantics=sem, vmem_limit_bytes=VMEM_LIMIT)


def _rms(x, g):
    ms = jnp.mean(x * x, axis=-1, keepdims=True)
    return x * lax.rsqrt(ms + RMS_EPS) * g


def _dot(a, b):
    return jnp.dot(a, b, preferred_element_type=F32)


def _dot_nt(a, b):
    return lax.dot_general(a, b, (((1,), (1,)), ((), ())), preferred_element_type=F32)


def _transposed(x):
    return x.astype(F32).T.astype(BF16)


def _split2(x):
    hi = x.astype(BF16)
    lo = (x - hi.astype(F32)).astype(BF16)
    return hi, lo


def _split3(x):
    hi = x.astype(BF16)
    r = x - hi.astype(F32)
    mid = r.astype(BF16)
    lo = (r - mid.astype(F32)).astype(BF16)
    return hi, mid, lo


def _bias_lane(hh):
    return (1 - hh) * HEAD_DIM


def _in_proj_kernel(x_ref, g_ref, w_ref, wf_ref, b_ref, tri_ref, place_ref,
                    o_ref, e_ref, h_ref, carry_ref, *, tiles_per_seq):
    @pl.when(pl.program_id(1) == 0)
    def _():
        h = _rms(x_ref[...], g_ref[...]).astype(BF16)
        h_ref[...] = h

        @pl.when(pl.program_id(0) % tiles_per_seq == 0)
        def _():
            carry_ref[...] = jnp.zeros_like(carry_ref)

        y = _dot_nt(h, wf_ref[...]) + b_ref[...]
        logf = jnp.minimum(y, 0.0) - jnp.log(1.0 + jnp.exp(-jnp.abs(y)))
        pieces = _split3(logf)
        tri = tri_ref[...]
        tc = tri.shape[0]
        cum = carry_ref[...]
        for c in range(h.shape[0] // tc):
            rows = slice(c * tc, (c + 1) * tc)
            cum = cum[-1:, :]
            for piece in pieces:
                cum = cum + _dot(tri, piece[rows])
            bias_pieces = jnp.concatenate(_split3(cum * (-LOG2E)), axis=1)
            e_ref[rows, :] = _dot(bias_pieces, place_ref[...]).astype(BF16)
        carry_ref[...] = cum[-1:, :]

    o_ref[...] = _dot_nt(h_ref[...], w_ref[...]).astype(o_ref.dtype)


def _bias_placement():
    place = np.zeros((N_BIAS_PIECES, LANES, N_HEAD_BLOCKS * LANES), np.float32)
    for h in range(N_HEADS):
        hp, hh = divmod(h, HEADS_PER_BLOCK)
        for i in range(N_BIAS_PIECES):
            place[i, h, hp * LANES + _bias_lane(hh) + i] = 1.0
    return jnp.asarray(place.reshape(N_BIAS_PIECES * LANES, -1), BF16)


def _in_proj(x, g, w, w_f, b_f, layer, s, tm, tn, tc):
    n, d = x.shape
    cols = w.shape[1]
    tri = jnp.asarray(np.tril(np.ones((tc, tc), np.float32)), BF16)
    place = _bias_placement()
    e_cols = place.shape[1]
    const2 = lambda i, j: (0, 0)
    return pl.pallas_call(
        functools.partial(_in_proj_kernel, tiles_per_seq=s // tm),
        grid=(n // tm, cols // tn),
        in_specs=[
            pl.BlockSpec((tm, d), lambda i, j: (i, 0)),
            pl.BlockSpec((1, d), const2),
            pl.BlockSpec((None, tn, d), lambda i, j: (layer, j, 0)),
            pl.BlockSpec((None, LANES, d), lambda i, j: (layer, 0, 0)),
            pl.BlockSpec((1, LANES), const2),
            pl.BlockSpec((tc, tc), const2),
            pl.BlockSpec(place.shape, const2),
        ],
        out_specs=[pl.BlockSpec((tm, tn), lambda i, j: (i, j)),
                   pl.BlockSpec((tm, e_cols), lambda i, j: (i, 0))],
        out_shape=[jax.ShapeDtypeStruct((n, cols), BF16),
                   jax.ShapeDtypeStruct((n, e_cols), BF16)],
        scratch_shapes=[pltpu.VMEM((tm, d), BF16), pltpu.VMEM((1, LANES), F32)],
        compiler_params=_cparams(("arbitrary", "arbitrary")),
        name="norm_in_proj",
    )(x, g, w, w_f, b_f, tri, place)


def _head_lane_mask(hh):
    lane = lax.broadcasted_iota(jnp.int32, (1, LANES), 1)
    return (lane >= hh * HEAD_DIM) & (lane < (hh + 1) * HEAD_DIM)


def _tile_rel(t):
    return (lax.broadcasted_iota(jnp.int32, (t, t), 0)
            - lax.broadcasted_iota(jnp.int32, (t, t), 1))


def _mixer_specs(mixer, nb, t, s, d):
    qoff = N_BRANCH * d // LANES + (3 * mixer) * N_HEAD_BLOCKS
    koff = qoff + N_HEAD_BLOCKS
    voff = koff + N_HEAD_BLOCKS
    return [
        pl.BlockSpec((nb, t, LANES), lambda bg, hp, qi: (bg, qi, qoff + hp)),
        pl.BlockSpec((nb, s, LANES), lambda bg, hp, qi: (bg, 0, koff + hp)),
        pl.BlockSpec((nb, s, LANES), lambda bg, hp, qi: (bg, 0, voff + hp)),
    ], pl.BlockSpec((nb, t, LANES), lambda bg, hp, qi: (bg, qi, hp))


def _sb_kernel(q_ref, k_ref, v_ref, o_ref, vt_ref, *scratch, nb, t):
    n_blocks = k_ref.shape[1] // t
    q_tiles = q_ref.shape[1] // t
    streams = [(bi, hh) for bi in range(nb) for hh in range(HEADS_PER_BLOCK)]

    @pl.when(pl.program_id(2) == 0)
    def _build():
        def chunk(c, carry):
            rows = pl.ds(pl.multiple_of(c * t, t), t)
            for st, (bi, hh) in enumerate(streams):
                v2t = v_ref[bi, rows, :].astype(F32).T
                vt_ref[st, :, rows] = v2t[hh * HEAD_DIM:(hh + 1) * HEAD_DIM, :].astype(BF16)
            return carry

        lax.fori_loop(0, n_blocks, chunk, 0)

    def q_tile(sub, carry):
        q_rows = pl.ds(pl.multiple_of(sub * t, t), t)
        _sb_q_tile(pl.program_id(2) * q_tiles + sub, q_rows, q_ref, k_ref, o_ref, vt_ref, *scratch, nb=nb, t=t)
        return carry

    lax.fori_loop(0, q_tiles, q_tile, 0)


def _sb_q_tile(qi, q_rows, q_ref, k_ref, o_ref, vt_ref, acc_ref, cs_ref, r_ref, s_even_ref, s_odd_ref, *, nb, t):
    streams = [(bi, hh) for bi in range(nb) for hh in range(HEADS_PER_BLOCK)]
    q_m = []
    for bi, hh in streams:
        q2 = q_ref[bi, q_rows, :]
        q_m.append(_transposed(jnp.where(_head_lane_mask(hh), q2, jnp.zeros_like(q2))))

    rel = _tile_rel(t)
    strictly_past = rel < 0
    upper = jnp.where(rel <= 0, 1.0, 0.0).astype(BF16)

    stream_groups = [range(g, g + HEADS_PER_BLOCK) for g in range(0, len(streams), HEADS_PER_BLOCK)]

    def scores(s_ref, j, group):
        rows = pl.ds(pl.multiple_of(j * t, t), t)
        for st in group:
            s_ref[st] = _dot(k_ref[streams[st][0], rows, :], q_m[st])

    def softplus_sums(s_ref, diag, group):
        for st in group:
            z = s_ref[st]
            sp = jnp.maximum(z, jnp.log2(1.0 + jnp.exp2(jnp.minimum(z, SP2_CLAMP))))
            if diag:
                sp = jnp.where(strictly_past, sp, 0.0)
            r = _dot(upper, sp.astype(BF16)) + cs_ref[st]
            r_ref[st] = r
            cs_ref[st] = r[0:1, :]

    def weigh_values(s_ref, j, diag, group):
        rows = pl.ds(pl.multiple_of(j * t, t), t)
        for st in group:
            w = jnp.exp2(s_ref[st] - r_ref[st])
            if diag:
                w = jnp.where(strictly_past, w, 0.0)
            acc_ref[st] += _dot(vt_ref[st, :, rows], w.astype(BF16))

    def step(cur_ref, j, diag, next_ref=None, j_next=None):
        for group in stream_groups:
            if next_ref is not None:
                scores(next_ref, j_next, group)
            softplus_sums(cur_ref, diag, group)
        for group in stream_groups:
            weigh_values(cur_ref, j, diag, group)

    def next_distance(i):
        lowest = cs_ref[0]
        for st in range(1, len(streams)):
            lowest = jnp.minimum(lowest, cs_ref[st])
        return jnp.where(jnp.min(lowest) > SB_EXIT, qi + 1, i + 1)

    acc_ref[...] = jnp.zeros(acc_ref.shape, F32)
    cs_ref[...] = jnp.zeros(cs_ref.shape, F32)
    for group in stream_groups:
        scores(s_even_ref, qi, group)
    step(s_even_ref, qi, True, s_odd_ref, jnp.maximum(qi - 1, 0))

    def body(i):
        j = qi - i

        @pl.when(i == 1)
        def _():
            step(s_odd_ref, j, False)

        @pl.when(i > 1)
        def _():
            for group in stream_groups:
                scores(s_even_ref, j, group)
            step(s_even_ref, j, False)

        return next_distance(i)

    lax.while_loop(lambda i: i <= qi, body, jnp.int32(1))

    for bi in range(nb):
        halves = [acc_ref[bi * HEADS_PER_BLOCK + hh] for hh in range(HEADS_PER_BLOCK)]
        o_ref[bi, q_rows, :] = jnp.concatenate(halves, axis=0).T.astype(o_ref.dtype)


def _sb_attention(proj3, d):
    b, s, _ = proj3.shape
    t = ATT_TILE
    nb = MIXER_BATCH if b % MIXER_BATCH == 0 else 1
    n_streams = nb * HEADS_PER_BLOCK
    q_rows = min(Q_TILES_PER_STEP * t, s)
    in_specs, out_spec = _mixer_specs(0, nb, q_rows, s, d)
    return pl.pallas_call(
        functools.partial(_sb_kernel, nb=nb, t=t),
        grid=(b // nb, N_HEAD_BLOCKS, s // q_rows),
        in_specs=in_specs,
        out_specs=out_spec,
        out_shape=jax.ShapeDtypeStruct((b, s, BRANCH_WIDTH), BF16),
        scratch_shapes=[pltpu.VMEM((n_streams, HEAD_DIM, s), BF16),
                        pltpu.VMEM((n_streams, HEAD_DIM, t), F32),
                        pltpu.VMEM((n_streams, 1, t), F32),
                        pltpu.VMEM((n_streams, t, t), F32),
                        pltpu.VMEM((n_streams, t, t), F32),
                        pltpu.VMEM((n_streams, t, t), F32)],
        compiler_params=_cparams(("parallel", "parallel", "arbitrary")),
        name="stickbreak_attn",
    )(proj3, proj3, proj3)


def _flash_kernel(*refs, moba, nb, t):
    q_tiles = refs[0].shape[1] // t

    def q_tile(sub, carry):
        q_rows = pl.ds(pl.multiple_of(sub * t, t), t)
        first = (pl.program_id(2) == 0) & (sub == 0)
        _flash_q_tile(pl.program_id(2) * q_tiles + sub, q_rows, first, *refs, moba=moba, nb=nb, t=t)
        return carry

    lax.fori_loop(0, q_tiles, q_tile, 0)


def _flash_q_tile(qi, q_rows, first, *refs, moba, nb, t):
    if moba:
        (q_ref, k_ref, v_ref, o_ref, kaug_ref, vt_ref, acc_ref, m_ref, s0_ref, s1_ref, s2_ref, s3_ref,
         bmax_ref, knorm_ref, kmean_ref, sel_ref) = refs
    else:
        (q_ref, k_ref, v_ref, e_ref, o_ref, kaug_ref, vt_ref, acc_ref, m_ref, s0_ref, s1_ref, s2_ref,
         s3_ref, bmax_ref, knorm_ref) = refs
    s_bufs = (s0_ref, s1_ref, s2_ref, s3_ref)
    n_blocks = k_ref.shape[1] // t
    hp = pl.program_id(1)
    lane = lax.broadcasted_iota(jnp.int32, (1, LANES), 1)
    streams = [(bi, hh) for bi in range(nb) for hh in range(HEADS_PER_BLOCK)]

    def piece_lanes(hh):
        e0 = _bias_lane(hh)
        return (lane >= e0) & (lane < e0 + N_BIAS_PIECES)

    head_of_lane = jnp.where(lax.broadcasted_iota(jnp.int32, (LANES, LANES), 0) >= HEAD_DIM, 1, 0)
    head_sum = jnp.where(head_of_lane == lax.broadcasted_iota(jnp.int32, (LANES, LANES), 1), 1.0, 0.0).astype(BF16)

    def head_sq_norm_max(x32):
        hi, lo = _split2(x32 * x32)
        return jnp.max(_dot(hi, head_sum) + _dot(lo, head_sum), axis=0, keepdims=True)

    def head_col(row, hh):
        return jnp.sum(jnp.where(lane == hh, row, 0.0), axis=1, keepdims=True)

    @pl.when(first)
    def _build():
        kaug_ref[:, 0:t, :] = jnp.zeros((len(streams), t, LANES), BF16)
        bmax_ref[...] = jnp.full(bmax_ref.shape, NEG_INF, F32)
        knorm_ref[...] = jnp.zeros(knorm_ref.shape, F32)
        tail = jnp.where(lax.broadcasted_iota(jnp.int32, (V_ROWS - HEAD_DIM, t), 0) == 0, 1.0, 0.0)

        def chunk(c, carry):
            start = pl.multiple_of(c * t, t)
            rows = pl.ds(start, t)
            extras = []
            if moba:
                pos = (lax.broadcasted_iota(jnp.int32, (t, LANES), 0) + start).astype(F32)
                for hh in range(HEADS_PER_BLOCK):
                    h = hp * HEADS_PER_BLOCK + hh
                    slope = pltpu.bitcast(jnp.full((1, LANES), (126 - h) << 23, jnp.int32), F32)
                    extra = jnp.zeros((t, LANES), F32)
                    for i, piece in enumerate(_split3(pos * slope * LOG2E)):
                        extra = jnp.where(lane == _bias_lane(hh) + i, piece.astype(F32), extra)
                    extras.append(extra.astype(BF16))
            for bi in range(nb):
                k2 = k_ref[bi, rows, :]
                k32 = k2.astype(F32)
                v2t = v_ref[bi, rows, :].astype(F32).T
                if moba:
                    kmean_ref[bi, pl.ds(c, 1), :] = jnp.sum(k32, axis=0, keepdims=True) * (1.0 / t)
                for hh in range(HEADS_PER_BLOCK):
                    st = bi * HEADS_PER_BLOCK + hh
                    extra = extras[hh] if moba else e_ref[bi, rows, :]
                    kaug_ref[st, pl.ds(start + t, t), :] = jnp.where(_head_lane_mask(hh), k2, extra)
                    v_h = v2t[hh * HEAD_DIM:(hh + 1) * HEAD_DIM, :]
                    vt_ref[st, :, rows] = jnp.concatenate([v_h, tail], axis=0).astype(BF16)
                    piece_max = jnp.max(extra.astype(F32), axis=0, keepdims=True)
                    bias_max = jnp.sum(jnp.where(piece_lanes(hh), piece_max, 0.0), axis=1, keepdims=True)
                    run = jnp.maximum(bmax_ref[st, pl.ds(jnp.maximum(c - 1, 0), 1), :], bias_max)
                    bmax_ref[st, pl.ds(c, 1), :] = run
                knorm_ref[bi] = jnp.maximum(knorm_ref[bi], head_sq_norm_max(k32))
            return carry

        lax.fori_loop(0, n_blocks, chunk, 0)

    q_aug = []
    reach = []
    for bi in range(nb):
        q2 = q_ref[bi, q_rows, :]
        qk_sq = head_sq_norm_max(q2.astype(F32)) * knorm_ref[bi]
        for hh in range(HEADS_PER_BLOCK):
            ones = jnp.where(piece_lanes(hh), 1.0, 0.0).astype(BF16)
            q_aug.append(_transposed(jnp.where(_head_lane_mask(hh), q2, jnp.broadcast_to(ones, q2.shape))))
            reach.append(jnp.sqrt(head_col(qk_sq, hh)) * NORM_SLACK)

    if moba:
        blk = lax.broadcasted_iota(jnp.int32, (n_blocks, 1), 0)
        blk_f = blk.astype(F32)
        for st, (bi, hh) in enumerate(streams):
            km = jnp.where(_head_lane_mask(hh), kmean_ref[bi], 0.0)
            pieces = _dot(jnp.concatenate(_split3(km), axis=0), q_aug[st])
            route = pieces[:n_blocks] + pieces[n_blocks:2 * n_blocks] + pieces[2 * n_blocks:]
            route = jnp.where(blk < qi, route, NEG_INF)
            sel = jnp.zeros((n_blocks, t), F32)
            for _ in range(MOBA_TOPK):
                top = jnp.max(route, axis=0, keepdims=True)
                first = jnp.min(jnp.where(route == top, blk_f, float(n_blocks)), axis=0, keepdims=True)
                hit = blk_f == first
                sel = jnp.where(hit, 1.0, sel)
                route = jnp.where(hit, -jnp.inf, route)
            sel_ref[st] = jnp.where(blk < qi, sel, 0.0)

    valid = _tile_rel(t) <= 0

    all_streams = range(len(streams))
    stream_groups = [range(g, g + HEADS_PER_BLOCK) for g in range(0, len(streams), HEADS_PER_BLOCK)]

    def scores(s_ref, j, group=all_streams):
        rows = pl.ds(pl.multiple_of((j + 1) * t, t), t)
        for st in group:
            s_ref[st] = _dot(kaug_ref[st, rows, :], q_aug[st])

    def scores_pair(near_ref, far_ref, near_dist, group):
        far = jnp.maximum(qi - near_dist - 1, -1)
        rows = pl.ds(pl.multiple_of((far + 1) * t, t), 2 * t)
        for st in group:
            both = _dot(kaug_ref[st, rows, :], q_aug[st])
            far_ref[st] = both[:t]
            near_ref[st] = both[t:]

    def accumulate(s_ref, j, diag, group=all_streams):
        rows = pl.ds(pl.multiple_of(j * t, t), t)
        for st in group:
            s_t = s_ref[st]
            if diag:
                s_t = jnp.where(valid, s_t, NEG_INF)
            m_old = m_ref[st]
            m_new = jnp.maximum(m_old, jnp.max(s_t, axis=0, keepdims=True))
            m_sub = m_new
            if moba and not diag:
                picked = sel_ref[st, pl.ds(j, 1), :] > 0.0
                m_new = jnp.where(picked, m_new, m_old)
                m_sub = jnp.where(picked, m_new, -NEG_INF)
            alpha = jnp.exp2(m_old - m_new)
            p = jnp.exp2(s_t - m_sub)
            acc_ref[st] = alpha * acc_ref[st] + _dot(vt_ref[st, :, rows], p.astype(BF16))
            m_ref[st] = m_new

    def accumulate_pair(near_ref, far_ref, j, group):
        rows = pl.ds(pl.multiple_of((j - 1) * t, t), 2 * t)
        for st in group:
            s_near = near_ref[st]
            s_far = far_ref[st]
            m_old = m_ref[st]
            max_near = jnp.max(s_near, axis=0, keepdims=True)
            max_far = jnp.max(s_far, axis=0, keepdims=True)
            if moba:
                pick_near = sel_ref[st, pl.ds(j, 1), :] > 0.0
                pick_far = sel_ref[st, pl.ds(j - 1, 1), :] > 0.0
                max_near = jnp.where(pick_near, max_near, NEG_INF)
                max_far = jnp.where(pick_far, max_far, NEG_INF)
            m_new = jnp.maximum(m_old, jnp.maximum(max_near, max_far))
            sub_near = sub_far = m_new
            if moba:
                sub_near = jnp.where(pick_near, m_new, -NEG_INF)
                sub_far = jnp.where(pick_far, m_new, -NEG_INF)
            alpha = jnp.exp2(m_old - m_new)
            p = jnp.concatenate([jnp.exp2(s_far - sub_far).astype(BF16),
                                 jnp.exp2(s_near - sub_near).astype(BF16)], axis=0)
            acc_ref[st] = alpha * acc_ref[st] + _dot(vt_ref[st, :, rows], p)
            m_ref[st] = m_new

    m_ref[...] = jnp.full(m_ref.shape, NEG_INF, F32)
    acc_ref[...] = jnp.zeros(acc_ref.shape, F32)
    scores(s_bufs[0], qi)
    for group in stream_groups:
        scores_pair(s_bufs[1], s_bufs[2], 1, group)
        accumulate(s_bufs[0], qi, True, group)

    blk_col = lax.broadcasted_iota(jnp.int32, (n_blocks, 1), 0)
    fewest_dead = None
    for st in range(len(streams)):
        thr = jnp.min(m_ref[st], axis=1, keepdims=True) - FLASH_SKIP - reach[st]
        dead = (bmax_ref[st] < thr) & (blk_col < qi)
        n_dead = jnp.sum(jnp.where(dead, 1.0, 0.0), axis=0, keepdims=True)
        fewest_dead = n_dead if fewest_dead is None else jnp.minimum(fewest_dead, n_dead)
    n_past = qi - jnp.max(fewest_dead).astype(jnp.int32)

    def pair(first, cur, nxt):
        for group in stream_groups:
            scores_pair(s_bufs[nxt[0]], s_bufs[nxt[1]], first + 2, group)
            accumulate_pair(s_bufs[cur[0]], s_bufs[cur[1]], qi - first, group)

    def body(p, carry):
        first = 2 * p + 1

        @pl.when(p % 2 == 0)
        def _():
            pair(first, (1, 2), (3, 0))

        @pl.when(p % 2 == 1)
        def _():
            pair(first, (3, 0), (1, 2))

        return carry

    lax.fori_loop(0, n_past // 2, body, 0)

    for slot in (1, 3):
        @pl.when(n_past % 4 == slot)
        def _(slot=slot):
            accumulate(s_bufs[slot], qi - n_past, False)

    for bi in range(nb):
        halves = []
        for hh in range(HEADS_PER_BLOCK):
            acc = acc_ref[bi * HEADS_PER_BLOCK + hh]
            halves.append(acc[:HEAD_DIM] * (1.0 / acc[HEAD_DIM:HEAD_DIM + 1]))
        o_ref[bi, q_rows, :] = jnp.concatenate(halves, axis=0).T.astype(o_ref.dtype)


def _flash_attention(proj3, key_bias3, d, moba):
    b, s, _ = proj3.shape
    t = ATT_TILE
    nb = MIXER_BATCH if b % MIXER_BATCH == 0 else 1
    n_blocks = s // t
    n_streams = nb * HEADS_PER_BLOCK
    q_rows = min(Q_TILES_PER_STEP * t, s)
    in_specs, out_spec = _mixer_specs(2 if moba else 1, nb, q_rows, s, d)
    operands = [proj3, proj3, proj3]
    scratch = [pltpu.VMEM((n_streams, s + t, LANES), BF16),
               pltpu.VMEM((n_streams, V_ROWS, s), BF16),
               pltpu.VMEM((n_streams, V_ROWS, t), F32),
               pltpu.VMEM((n_streams, 1, t), F32),
               pltpu.VMEM((n_streams, t, t), F32),
               pltpu.VMEM((n_streams, t, t), F32),
               pltpu.VMEM((n_streams, t, t), F32),
               pltpu.VMEM((n_streams, t, t), F32),
               pltpu.VMEM((n_streams, n_blocks, 1), F32),
               pltpu.VMEM((nb, 1, LANES), F32)]
    if moba:
        scratch += [pltpu.VMEM((nb, n_blocks, LANES), F32),
                    pltpu.VMEM((n_streams, n_blocks, t), F32)]
    else:
        in_specs.append(pl.BlockSpec((nb, s, LANES), lambda bg, hp, qi: (bg, 0, hp)))
        operands.append(key_bias3)
    return pl.pallas_call(
        functools.partial(_flash_kernel, moba=moba, nb=nb, t=t),
        grid=(b // nb, N_HEAD_BLOCKS, s // q_rows),
        in_specs=in_specs,
        out_specs=out_spec,
        out_shape=jax.ShapeDtypeStruct((b, s, BRANCH_WIDTH), BF16),
        scratch_shapes=scratch,
        compiler_params=_cparams(("parallel", "parallel", "arbitrary")),
        name="moba_attn" if moba else "forgetting_attn",
    )(*operands)


def _mix_kernel(x_ref, oa_ref, ob_ref, oc_ref, g_ref, wb_ref, wo_ref, y_ref):
    d = x_ref.shape[1]
    mixed = None
    for n, o_ref in enumerate((oa_ref, ob_ref, oc_ref)):
        gate = jax.nn.sigmoid(g_ref[:, n * d:(n + 1) * d].astype(F32))
        term = gate * _dot(o_ref[...], wb_ref[n])
        mixed = term if mixed is None else mixed + term
    y_ref[...] = x_ref[...] + _dot(mixed.astype(BF16), wo_ref[...])


def _mix(x, o_a, o_b, o_c, proj, wb, wo, layer, tm):
    n, d = x.shape
    o_spec = pl.BlockSpec((tm, BRANCH_WIDTH), lambda i: (i, 0))
    return pl.pallas_call(
        _mix_kernel,
        grid=(n // tm,),
        in_specs=[
            pl.BlockSpec((tm, d), lambda i: (i, 0)),
            o_spec, o_spec, o_spec,
            pl.BlockSpec((tm, N_BRANCH * d), lambda i: (i, 0)),
            pl.BlockSpec((None, N_BRANCH, BRANCH_WIDTH, d), lambda i: (layer, 0, 0, 0)),
            pl.BlockSpec((None, d, d), lambda i: (layer, 0, 0)),
        ],
        out_specs=pl.BlockSpec((tm, d), lambda i: (i, 0)),
        out_shape=jax.ShapeDtypeStruct((n, d), F32),
        compiler_params=_cparams(("parallel",)),
        name="gated_mix_out_proj",
    )(x, o_a, o_b, o_c, proj, wb, wo)


def _mlp_kernel(x_ref, g_ref, wu_ref, wd_ref, gf_ref, y_ref, h_ref, acc_ref, *, final_norm):
    f = pl.program_id(1)

    @pl.when(f == 0)
    def _():
        h_ref[...] = _rms(x_ref[...], g_ref[...]).astype(BF16)
        acc_ref[...] = jnp.zeros_like(acc_ref)

    hid = jnp.square(jnp.maximum(_dot(h_ref[...], wu_ref[...]), 0.0))
    acc_ref[...] += _dot(hid.astype(BF16), wd_ref[...])

    @pl.when(f == pl.num_programs(1) - 1)
    def _():
        y = x_ref[...] + acc_ref[...]
        if final_norm:
            y = _rms(y, gf_ref[...])
        y_ref[...] = y


def _mlp(x, g, wu, wd, g_final, layer, final_norm, tm, tf):
    n, d = x.shape
    d_ff = wu.shape[2]
    return pl.pallas_call(
        functools.partial(_mlp_kernel, final_norm=final_norm),
        grid=(n // tm, d_ff // tf),
        in_specs=[
            pl.BlockSpec((tm, d), lambda i, f: (i, 0)),
            pl.BlockSpec((1, d), lambda i, f: (0, 0)),
            pl.BlockSpec((None, d, tf), lambda i, f: (layer, 0, f)),
            pl.BlockSpec((None, tf, d), lambda i, f: (layer, f, 0)),
            pl.BlockSpec((1, d), lambda i, f: (0, 0)),
        ],
        out_specs=pl.BlockSpec((tm, d), lambda i, f: (i, 0)),
        out_shape=jax.ShapeDtypeStruct((n, d), F32),
        scratch_shapes=[pltpu.VMEM((tm, d), BF16), pltpu.VMEM((tm, d), F32)],
        compiler_params=_cparams(("parallel", "arbitrary")),
        name="relu2_mlp",
    )(x, g, wu, wd, g_final)


REPACK_ROWS = 512


def _repack_kernel(w_ref, wf_ref, main_ref, f_ref):
    i = pl.program_id(1)
    q_chunks = BRANCH_WIDTH // REPACK_ROWS
    n_gate_chunks = N_BRANCH * w_ref.shape[2] // REPACK_ROWS
    mixer_chunk = jnp.maximum(i - n_gate_chunks, 0) % (3 * q_chunks)
    is_q = (i >= n_gate_chunks) & (mixer_chunk < q_chunks)
    scale = jnp.where(is_q, HEAD_DIM ** -0.5 * LOG2E, 1.0)
    main_ref[...] = (w_ref[0] * scale).astype(BF16)

    @pl.when(i == 0)
    def _():
        pad = jnp.zeros((f_ref.shape[0] - N_HEADS, f_ref.shape[1]), F32)
        f_ref[...] = jnp.concatenate([wf_ref[0], pad], axis=0).astype(BF16)


def _prep_in_proj(w_in):
    depth, d, cols = w_in.shape
    w_t = jnp.swapaxes(w_in, 1, 2)
    bw = BRANCH_WIDTH
    r = REPACK_ROWS
    f_lo = 6 * bw
    c_lo = f_lo + N_HEADS
    g_lo = c_lo + 3 * bw
    main_rows = cols - N_HEADS
    n_gate = (cols - g_lo) // r
    n_ab = f_lo // r

    def src_row(l, i):
        row = jnp.where(i < n_gate, g_lo + r * i,
                        jnp.where(i < n_gate + n_ab, r * (i - n_gate), c_lo + r * (i - n_gate - n_ab)))
        return l, pl.multiple_of(row, N_HEADS), 0

    return pl.pallas_call(
        _repack_kernel,
        grid=(depth, main_rows // r),
        in_specs=[pl.BlockSpec((pl.Element(1), pl.Element(r), pl.Element(d)), src_row),
                  pl.BlockSpec((pl.Element(1), pl.Element(N_HEADS), pl.Element(d)), lambda l, i: (l, f_lo, 0))],
        out_specs=[pl.BlockSpec((None, r, d), lambda l, i: (l, i, 0)),
                   pl.BlockSpec((None, LANES, d), lambda l, i: (l, 0, 0))],
        out_shape=[jax.ShapeDtypeStruct((depth, main_rows, d), BF16),
                   jax.ShapeDtypeStruct((depth, LANES, d), BF16)],
        compiler_params=_cparams(("parallel", "arbitrary")),
        name="repack_w_in",
    )(w_t, w_t)


def kernel(x, norm_mix, w_in, b_forget, w_branch, w_out, norm_mlp, w_up, w_down, norm_final):
    b, s, d = x.shape
    depth = w_in.shape[0]
    n = b * s
    assert s % ATT_TILE == 0 and d % LANES == 0

    w_main, w_f = _prep_in_proj(w_in)
    wb = w_branch.astype(BF16)
    wo = w_out.astype(BF16)
    wu = w_up.astype(BF16)
    wd = w_down.astype(BF16)
    b_f = jnp.pad(b_forget, ((0, 0), (0, LANES - N_HEADS)))[:, None, :]

    tm = min(1024, s)
    xf = x.reshape(n, d)
    for l in range(depth):
        g_mix = norm_mix[l][None, :]
        proj, key_bias = _in_proj(xf, g_mix, w_main, w_f, b_f[l], l, s, tm, 3840, 256)
        proj3 = proj.reshape(b, s, -1)
        o_a = _sb_attention(proj3, d).reshape(n, -1)
        o_b = _flash_attention(proj3, key_bias.reshape(b, s, -1), d, False).reshape(n, -1)
        o_c = _flash_attention(proj3, None, d, True).reshape(n, -1)
        xf = _mix(xf, o_a, o_b, o_c, proj, wb, wo, l, min(1024, n))
        xf = _mlp(xf, norm_mlp[l][None, :], wu, wd, norm_final[None, :],
                  l, l == depth - 1, min(1024, n), 1024)
    return xf.reshape(b, s, d)
```

```python
import functools
import math

import jax
import jax.numpy as jnp
import numpy as np
from jax import lax
from jax.experimental import pallas as pl
from jax.experimental.pallas import tpu as pltpu

F32 = jnp.float32
BF16 = jnp.bfloat16

HEAD_DIM = 64
N_HEADS = 8
BRANCH_WIDTH = N_HEADS * HEAD_DIM
N_BRANCH = 3
MOBA_BLOCK = 256
MOBA_TOPK = 3
RMS_EPS = 1e-6
NEG_INF = -1e30
LOG2E = math.log2(math.e)
LANES = 128
HEADS_PER_BLOCK = LANES // HEAD_DIM
N_HEAD_BLOCKS = BRANCH_WIDTH // LANES
N_BIAS_PIECES = 3
ATT_TILE = MOBA_BLOCK
Q_TILES_PER_STEP = 4
MIXER_BATCH = 4
SB_EXIT = 152.0
FLASH_SKIP = 152.0
SP2_CLAMP = 120.0
NORM_SLACK = 1.001
V_ROWS = HEAD_DIM + 16
VMEM_LIMIT = 60000 * 1024


def _cparams(sem):
    return pltpu.CompilerParams(dimension_semantics=sem, vmem_limit_bytes=VMEM_LIMIT)


def _rms(x, g):
    ms = jnp.mean(x * x, axis=-1, keepdims=True)
    return x * lax.rsqrt(ms + RMS_EPS) * g


def _dot(a, b):
    return jnp.dot(a, b, preferred_element_type=F32)


def _dot_nt(a, b):
    return lax.dot_general(a, b, (((1,), (1,)), ((), ())), preferred_element_type=F32)


def _transposed(x):
    return x.astype(F32).T.astype(BF16)


def _split2(x):
    hi = x.astype(BF16)
    lo = (x - hi.astype(F32)).astype(BF16)
    return hi, lo


def _split3(x):
    hi = x.astype(BF16)
    r = x - hi.astype(F32)
    mid = r.astype(BF16)
    lo = (r - mid.astype(F32)).astype(BF16)
    return hi, mid, lo


def _bias_lane(hh):
    return (1 - hh) * HEAD_DIM


def _in_proj_kernel(x_ref, g_ref, w_ref, wf_ref, b_ref, tri_ref, place_ref,
                    o_ref, e_ref, h_ref, carry_ref, *, tiles_per_seq):
    @pl.when(pl.program_id(1) == 0)
    def _():
        h = _rms(x_ref[...], g_ref[...]).astype(BF16)
        h_ref[...] = h

        @pl.when(pl.program_id(0) % tiles_per_seq == 0)
        def _():
            carry_ref[...] = jnp.zeros_like(carry_ref)

        y = _dot_nt(h, wf_ref[...]) + b_ref[...]
        logf = jnp.minimum(y, 0.0) - jnp.log(1.0 + jnp.exp(-jnp.abs(y)))
        pieces = _split3(logf)
        tri = tri_ref[...]
        tc = tri.shape[0]
        cum = carry_ref[...]
        for c in range(h.shape[0] // tc):
            rows = slice(c * tc, (c + 1) * tc)
            cum = cum[-1:, :]
            for piece in pieces:
                cum = cum + _dot(tri, piece[rows])
            bias_pieces = jnp.concatenate(_split3(cum * (-LOG2E)), axis=1)
            e_ref[rows, :] = _dot(bias_pieces, place_ref[...]).astype(BF16)
        carry_ref[...] = cum[-1:, :]

    o_ref[...] = _dot_nt(h_ref[...], w_ref[...]).astype(o_ref.dtype)


def _bias_placement():
    place = np.zeros((N_BIAS_PIECES, LANES, N_HEAD_BLOCKS * LANES), np.float32)
    for h in range(N_HEADS):
        hp, hh = divmod(h, HEADS_PER_BLOCK)
        for i in range(N_BIAS_PIECES):
            place[i, h, hp * LANES + _bias_lane(hh) + i] = 1.0
    return jnp.asarray(place.reshape(N_BIAS_PIECES * LANES, -1), BF16)


def _in_proj(x, g, w, w_f, b_f, layer, s, tm, tn, tc):
    n, d = x.shape
    cols = w.shape[1]
    tri = jnp.asarray(np.tril(np.ones((tc, tc), np.float32)), BF16)
    place = _bias_placement()
    e_cols = place.shape[1]
    const2 = lambda i, j: (0, 0)
    return pl.pallas_call(
        functools.partial(_in_proj_kernel, tiles_per_seq=s // tm),
        grid=(n // tm, cols // tn),
        in_specs=[
            pl.BlockSpec((tm, d), lambda i, j: (i, 0)),
            pl.BlockSpec((1, d), const2),
            pl.BlockSpec((None, tn, d), lambda i, j: (layer, j, 0)),
            pl.BlockSpec((None, LANES, d), lambda i, j: (layer, 0, 0)),
            pl.BlockSpec((1, LANES), const2),
            pl.BlockSpec((tc, tc), const2),
            pl.BlockSpec(place.shape, const2),
        ],
        out_specs=[pl.BlockSpec((tm, tn), lambda i, j: (i, j)),
                   pl.BlockSpec((tm, e_cols), lambda i, j: (i, 0))],
        out_shape=[jax.ShapeDtypeStruct((n, cols), BF16),
                   jax.ShapeDtypeStruct((n, e_cols), BF16)],
        scratch_shapes=[pltpu.VMEM((tm, d), BF16), pltpu.VMEM((1, LANES), F32)],
        compiler_params=_cparams(("arbitrary", "arbitrary")),
        name="norm_in_proj",
    )(x, g, w, w_f, b_f, tri, place)


def _head_lane_mask(hh):
    lane = lax.broadcasted_iota(jnp.int32, (1, LANES), 1)
    return (lane >= hh * HEAD_DIM) & (lane < (hh + 1) * HEAD_DIM)


def _tile_rel(t):
    return (lax.broadcasted_iota(jnp.int32, (t, t), 0)
            - lax.broadcasted_iota(jnp.int32, (t, t), 1))


def _mixer_specs(mixer, nb, t, s, d):
    qoff = N_BRANCH * d // LANES + (3 * mixer) * N_HEAD_BLOCKS
    koff = qoff + N_HEAD_BLOCKS
    voff = koff + N_HEAD_BLOCKS
    return [
        pl.BlockSpec((nb, t, LANES), lambda bg, hp, qi: (bg, qi, qoff + hp)),
        pl.BlockSpec((nb, s, LANES), lambda bg, hp, qi: (bg, 0, koff + hp)),
        pl.BlockSpec((nb, s, LANES), lambda bg, hp, qi: (bg, 0, voff + hp)),
    ], pl.BlockSpec((nb, t, LANES), lambda bg, hp, qi: (bg, qi, hp))


def _sb_kernel(q_ref, k_ref, v_ref, o_ref, vt_ref, *scratch, nb, t):
    n_blocks = k_ref.shape[1] // t
    q_tiles = q_ref.shape[1] // t
    streams = [(bi, hh) for bi in range(nb) for hh in range(HEADS_PER_BLOCK)]

    @pl.when(pl.program_id(2) == 0)
    def _build():
        def chunk(c, carry):
            rows = pl.ds(pl.multiple_of(c * t, t), t)
            for st, (bi, hh) in enumerate(streams):
                v2t = v_ref[bi, rows, :].astype(F32).T
                vt_ref[st, :, rows] = v2t[hh * HEAD_DIM:(hh + 1) * HEAD_DIM, :].astype(BF16)
            return carry

        lax.fori_loop(0, n_blocks, chunk, 0)

    def q_tile(sub, carry):
        q_rows = pl.ds(pl.multiple_of(sub * t, t), t)
        _sb_q_tile(pl.program_id(2) * q_tiles + sub, q_rows, q_ref, k_ref, o_ref, vt_ref, *scratch, nb=nb, t=t)
        return carry

    lax.fori_loop(0, q_tiles, q_tile, 0)


def _sb_q_tile(qi, q_rows, q_ref, k_ref, o_ref, vt_ref, acc_ref, cs_ref, r_ref, s_even_ref, s_odd_ref, *, nb, t):
    streams = [(bi, hh) for bi in range(nb) for hh in range(HEADS_PER_BLOCK)]
    q_m = []
    for bi, hh in streams:
        q2 = q_ref[bi, q_rows, :]
        q_m.append(_transposed(jnp.where(_head_lane_mask(hh), q2, jnp.zeros_like(q2))))

    rel = _tile_rel(t)
    strictly_past = rel < 0
    upper = jnp.where(rel <= 0, 1.0, 0.0).astype(BF16)

    stream_groups = [range(g, g + HEADS_PER_BLOCK) for g in range(0, len(streams), HEADS_PER_BLOCK)]

    def scores(s_ref, j, group):
        rows = pl.ds(pl.multiple_of(j * t, t), t)
        for st in group:
            s_ref[st] = _dot(k_ref[streams[st][0], rows, :], q_m[st])

    def softplus_sums(s_ref, diag, group):
        for st in group:
            z = s_ref[st]
            sp = jnp.maximum(z, jnp.log2(1.0 + jnp.exp2(jnp.minimum(z, SP2_CLAMP))))
            if diag:
                sp = jnp.where(strictly_past, sp, 0.0)
            r = _dot(upper, sp.astype(BF16)) + cs_ref[st]
            r_ref[st] = r
            cs_ref[st] = r[0:1, :]

    def weigh_values(s_ref, j, diag, group):
        rows = pl.ds(pl.multiple_of(j * t, t), t)
        for st in group:
            w = jnp.exp2(s_ref[st] - r_ref[st])
            if diag:
                w = jnp.where(strictly_past, w, 0.0)
            acc_ref[st] += _dot(vt_ref[st, :, rows], w.astype(BF16))

    def step(cur_ref, next_ref, j, j_next, diag):
        for group in stream_groups:
            scores(next_ref, j_next, group)
            softplus_sums(cur_ref, diag, group)
        for group in stream_groups:
            weigh_values(cur_ref, j, diag, group)

    def next_distance(i):
        lowest = cs_ref[0]
        for st in range(1, len(streams)):
            lowest = jnp.minimum(lowest, cs_ref[st])
        return jnp.where(jnp.min(lowest) > SB_EXIT, qi + 1, i + 1)

    acc_ref[...] = jnp.zeros(acc_ref.shape, F32)
    cs_ref[...] = jnp.zeros(cs_ref.shape, F32)
    for group in stream_groups:
        scores(s_even_ref, qi, group)
    step(s_even_ref, s_odd_ref, qi, jnp.maximum(qi - 1, 0), True)

    def body(i):
        j = qi - i
        j_next = jnp.maximum(j - 1, 0)

        @pl.when(i % 2 == 1)
        def _():
            step(s_odd_ref, s_even_ref, j, j_next, False)

        @pl.when(i % 2 == 0)
        def _():
            step(s_even_ref, s_odd_ref, j, j_next, False)

        return next_distance(i)

    lax.while_loop(lambda i: i <= qi, body, jnp.int32(1))

    for bi in range(nb):
        halves = [acc_ref[bi * HEADS_PER_BLOCK + hh] for hh in range(HEADS_PER_BLOCK)]
        o_ref[bi, q_rows, :] = jnp.concatenate(halves, axis=0).T.astype(o_ref.dtype)


def _sb_attention(proj3, d):
    b, s, _ = proj3.shape
    t = ATT_TILE
    nb = MIXER_BATCH if b % MIXER_BATCH == 0 else 1
    n_streams = nb * HEADS_PER_BLOCK
    q_rows = min(Q_TILES_PER_STEP * t, s)
    in_specs, out_spec = _mixer_specs(0, nb, q_rows, s, d)
    return pl.pallas_call(
        functools.partial(_sb_kernel, nb=nb, t=t),
        grid=(b // nb, N_HEAD_BLOCKS, s // q_rows),
        in_specs=in_specs,
        out_specs=out_spec,
        out_shape=jax.ShapeDtypeStruct((b, s, BRANCH_WIDTH), BF16),
        scratch_shapes=[pltpu.VMEM((n_streams, HEAD_DIM, s), BF16),
                        pltpu.VMEM((n_streams, HEAD_DIM, t), F32),
                        pltpu.VMEM((n_streams, 1, t), F32),
                        pltpu.VMEM((n_streams, t, t), F32),
                        pltpu.VMEM((n_streams, t, t), F32),
                        pltpu.VMEM((n_streams, t, t), F32)],
        compiler_params=_cparams(("parallel", "parallel", "arbitrary")),
        name="stickbreak_attn",
    )(proj3, proj3, proj3)


def _flash_kernel(*refs, moba, nb, t):
    q_tiles = refs[0].shape[1] // t

    def q_tile(sub, carry):
        q_rows = pl.ds(pl.multiple_of(sub * t, t), t)
        first = (pl.program_id(2) == 0) & (sub == 0)
        _flash_q_tile(pl.program_id(2) * q_tiles + sub, q_rows, first, *refs, moba=moba, nb=nb, t=t)
        return carry

    lax.fori_loop(0, q_tiles, q_tile, 0)


def _flash_q_tile(qi, q_rows, first, *refs, moba, nb, t):
    if moba:
        (q_ref, k_ref, v_ref, e_ref, o_ref, kaug_ref, vt_ref, acc_ref, m_ref, s0_ref, s1_ref, s2_ref, s3_ref,
         bmax_ref, knorm_ref, kmean_ref, sel_ref) = refs
    else:
        (q_ref, k_ref, v_ref, e_ref, o_ref, kaug_ref, vt_ref, acc_ref, m_ref, s0_ref, s1_ref, s2_ref,
         s3_ref, bmax_ref, knorm_ref) = refs
    s_bufs = (s0_ref, s1_ref, s2_ref, s3_ref)
    n_blocks = k_ref.shape[1] // t
    hp = pl.program_id(1)
    lane = lax.broadcasted_iota(jnp.int32, (1, LANES), 1)
    streams = [(bi, hh) for bi in range(nb) for hh in range(HEADS_PER_BLOCK)]

    def piece_lanes(hh):
        e0 = _bias_lane(hh)
        return (lane >= e0) & (lane < e0 + N_BIAS_PIECES)

    head_of_lane = jnp.where(lax.broadcasted_iota(jnp.int32, (LANES, LANES), 0) >= HEAD_DIM, 1, 0)
    head_sum = jnp.where(head_of_lane == lax.broadcasted_iota(jnp.int32, (LANES, LANES), 1), 1.0, 0.0).astype(BF16)

    def head_sq_norm_max(x32):
        hi, lo = _split2(x32 * x32)
        return jnp.max(_dot(hi, head_sum) + _dot(lo, head_sum), axis=0, keepdims=True)

    def head_col(row, hh):
        return jnp.sum(jnp.where(lane == hh, row, 0.0), axis=1, keepdims=True)

    @pl.when(first)
    def _build():
        kaug_ref[:, 0:t, :] = jnp.zeros((len(streams), t, LANES), BF16)
        bmax_ref[...] = jnp.full(bmax_ref.shape, NEG_INF, F32)
        knorm_ref[...] = jnp.zeros(knorm_ref.shape, F32)
        tail = jnp.where(lax.broadcasted_iota(jnp.int32, (V_ROWS - HEAD_DIM, t), 0) == 0, 1.0, 0.0)

        def chunk(c, carry):
            start = pl.multiple_of(c * t, t)
            rows = pl.ds(start, t)
            for bi in range(nb):
                k2 = k_ref[bi, rows, :]
                k32 = k2.astype(F32)
                v2t = v_ref[bi, rows, :].astype(F32).T
                if moba:
                    kmean_ref[bi, pl.ds(c, 1), :] = jnp.sum(k32, axis=0, keepdims=True) * (1.0 / t)
                for hh in range(HEADS_PER_BLOCK):
                    st = bi * HEADS_PER_BLOCK + hh
                    extra = e_ref[0 if moba else bi, rows, :]
                    kaug_ref[st, pl.ds(start + t, t), :] = jnp.where(_head_lane_mask(hh), k2, extra)
                    v_h = v2t[hh * HEAD_DIM:(hh + 1) * HEAD_DIM, :]
                    vt_ref[st, :, rows] = jnp.concatenate([v_h, tail], axis=0).astype(BF16)
                    piece_max = jnp.max(extra.astype(F32), axis=0, keepdims=True)
                    bias_max = jnp.sum(jnp.where(piece_lanes(hh), piece_max, 0.0), axis=1, keepdims=True)
                    run = jnp.maximum(bmax_ref[st, pl.ds(jnp.maximum(c - 1, 0), 1), :], bias_max)
                    bmax_ref[st, pl.ds(c, 1), :] = run
                knorm_ref[bi] = jnp.maximum(knorm_ref[bi], head_sq_norm_max(k32))
            return carry

        lax.fori_loop(0, n_blocks, chunk, 0)

    q_aug = []
    reach = []
    for bi in range(nb):
        q2 = q_ref[bi, q_rows, :]
        qk_sq = head_sq_norm_max(q2.astype(F32)) * knorm_ref[bi]
        for hh in range(HEADS_PER_BLOCK):
            ones = jnp.where(piece_lanes(hh), 1.0, 0.0).astype(BF16)
            q_aug.append(_transposed(jnp.where(_head_lane_mask(hh), q2, jnp.broadcast_to(ones, q2.shape))))
            reach.append(jnp.sqrt(head_col(qk_sq, hh)) * NORM_SLACK)

    if moba:
        blk = lax.broadcasted_iota(jnp.int32, (n_blocks, 1), 0)
        blk_f = blk.astype(F32)
        for st, (bi, hh) in enumerate(streams):
            km = jnp.where(_head_lane_mask(hh), kmean_ref[bi], 0.0)
            pieces = _dot(jnp.concatenate(_split3(km), axis=0), q_aug[st])
            route = pieces[:n_blocks] + pieces[n_blocks:2 * n_blocks] + pieces[2 * n_blocks:]
            route = jnp.where(blk < qi, route, NEG_INF)
            sel = jnp.zeros((n_blocks, t), F32)
            for _ in range(MOBA_TOPK):
                top = jnp.max(route, axis=0, keepdims=True)
                first = jnp.min(jnp.where(route == top, blk_f, float(n_blocks)), axis=0, keepdims=True)
                hit = blk_f == first
                sel = jnp.where(hit, 1.0, sel)
                route = jnp.where(hit, -jnp.inf, route)
            sel_ref[st] = jnp.where(blk < qi, sel, 0.0)

    valid = _tile_rel(t) <= 0

    all_streams = range(len(streams))
    stream_groups = [range(g, g + HEADS_PER_BLOCK) for g in range(0, len(streams), HEADS_PER_BLOCK)]

    def scores(s_ref, j, group=all_streams):
        rows = pl.ds(pl.multiple_of((j + 1) * t, t), t)
        for st in group:
            s_ref[st] = _dot(kaug_ref[st, rows, :], q_aug[st])

    def scores_pair(near_ref, far_ref, near_dist, group):
        far = jnp.maximum(qi - near_dist - 1, -1)
        rows = pl.ds(pl.multiple_of((far + 1) * t, t), 2 * t)
        for st in group:
            both = _dot(kaug_ref[st, rows, :], q_aug[st])
            far_ref[st] = both[:t]
            near_ref[st] = both[t:]

    def accumulate(s_ref, j, diag, group=all_streams):
        rows = pl.ds(pl.multiple_of(j * t, t), t)
        for st in group:
            s_t = s_ref[st]
            if diag:
                s_t = jnp.where(valid, s_t, NEG_INF)
            m_old = m_ref[st]
            m_new = jnp.maximum(m_old, jnp.max(s_t, axis=0, keepdims=True))
            m_sub = m_new
            if moba and not diag:
                picked = sel_ref[st, pl.ds(j, 1), :] > 0.0
                m_new = jnp.where(picked, m_new, m_old)
                m_sub = jnp.where(picked, m_new, -NEG_INF)
            alpha = jnp.exp2(m_old - m_new)
            p = jnp.exp2(s_t - m_sub)
            acc_ref[st] = alpha * acc_ref[st] + _dot(vt_ref[st, :, rows], p.astype(BF16))
            m_ref[st] = m_new

    def accumulate_pair(near_ref, far_ref, j, group):
        rows = pl.ds(pl.multiple_of((j - 1) * t, t), 2 * t)
        for st in group:
            s_near = near_ref[st]
            s_far = far_ref[st]
            m_old = m_ref[st]
            max_near = jnp.max(s_near, axis=0, keepdims=True)
            max_far = jnp.max(s_far, axis=0, keepdims=True)
            if moba:
                pick_near = sel_ref[st, pl.ds(j, 1), :] > 0.0
                pick_far = sel_ref[st, pl.ds(j - 1, 1), :] > 0.0
                max_near = jnp.where(pick_near, max_near, NEG_INF)
                max_far = jnp.where(pick_far, max_far, NEG_INF)
            m_new = jnp.maximum(m_old, jnp.maximum(max_near, max_far))
            sub_near = sub_far = m_new
            if moba:
                sub_near = jnp.where(pick_near, m_new, -NEG_INF)
                sub_far = jnp.where(pick_far, m_new, -NEG_INF)
            alpha = jnp.exp2(m_old - m_new)
            p = jnp.concatenate([jnp.exp2(s_far - sub_far).astype(BF16),
                                 jnp.exp2(s_near - sub_near).astype(BF16)], axis=0)
            acc_ref[st] = alpha * acc_ref[st] + _dot(vt_ref[st, :, rows], p)
            m_ref[st] = m_new

    m_ref[...] = jnp.full(m_ref.shape, NEG_INF, F32)
    acc_ref[...] = jnp.zeros(acc_ref.shape, F32)
    scores(s_bufs[0], qi)
    for group in stream_groups:
        scores_pair(s_bufs[1], s_bufs[2], 1, group)
        accumulate(s_bufs[0], qi, True, group)

    blk_col = lax.broadcasted_iota(jnp.int32, (n_blocks, 1), 0)
    fewest_dead = None
    for st in range(len(streams)):
        thr = jnp.min(m_ref[st], axis=1, keepdims=True) - FLASH_SKIP - reach[st]
        dead = (bmax_ref[st] < thr) & (blk_col < qi)
        n_dead = jnp.sum(jnp.where(dead, 1.0, 0.0), axis=0, keepdims=True)
        fewest_dead = n_dead if fewest_dead is None else jnp.minimum(fewest_dead, n_dead)
    n_past = qi - jnp.max(fewest_dead).astype(jnp.int32)

    def pair(first, cur, nxt):
        for group in stream_groups:
            scores_pair(s_bufs[nxt[0]], s_bufs[nxt[1]], first + 2, group)
            accumulate_pair(s_bufs[cur[0]], s_bufs[cur[1]], qi - first, group)

    def body(p, carry):
        first = 2 * p + 1

        @pl.when(p % 2 == 0)
        def _():
            pair(first, (1, 2), (3, 0))

        @pl.when(p % 2 == 1)
        def _():
            pair(first, (3, 0), (1, 2))

        return carry

    lax.fori_loop(0, n_past // 2, body, 0)

    for slot in (1, 3):
        @pl.when(n_past % 4 == slot)
        def _(slot=slot):
            accumulate(s_bufs[slot], qi - n_past, False)

    for bi in range(nb):
        halves = []
        for hh in range(HEADS_PER_BLOCK):
            acc = acc_ref[bi * HEADS_PER_BLOCK + hh]
            halves.append(acc[:HEAD_DIM] * (1.0 / acc[HEAD_DIM:HEAD_DIM + 1]))
        o_ref[bi, q_rows, :] = jnp.concatenate(halves, axis=0).T.astype(o_ref.dtype)


def _alibi_key_bias(s):
    pos = np.arange(s, dtype=np.float32)
    table = np.zeros((s, N_HEAD_BLOCKS * LANES), np.float32)
    for h in range(N_HEADS):
        hp, hh = divmod(h, HEADS_PER_BLOCK)
        rest = pos * np.float32(2.0 ** -(h + 1)) * np.float32(LOG2E)
        for i in range(N_BIAS_PIECES):
            piece = rest.astype(BF16).astype(np.float32)
            table[:, hp * LANES + _bias_lane(hh) + i] = piece
            rest = rest - piece
    return jnp.asarray(table[None], BF16)


def _flash_attention(proj3, key_bias3, d, moba):
    b, s, _ = proj3.shape
    t = ATT_TILE
    nb = MIXER_BATCH if b % MIXER_BATCH == 0 else 1
    n_blocks = s // t
    n_streams = nb * HEADS_PER_BLOCK
    q_rows = min(Q_TILES_PER_STEP * t, s)
    in_specs, out_spec = _mixer_specs(2 if moba else 1, nb, q_rows, s, d)
    operands = [proj3, proj3, proj3]
    scratch = [pltpu.VMEM((n_streams, s + t, LANES), BF16),
               pltpu.VMEM((n_streams, V_ROWS, s), BF16),
               pltpu.VMEM((n_streams, V_ROWS, t), F32),
               pltpu.VMEM((n_streams, 1, t), F32),
               pltpu.VMEM((n_streams, t, t), F32),
               pltpu.VMEM((n_streams, t, t), F32),
               pltpu.VMEM((n_streams, t, t), F32),
               pltpu.VMEM((n_streams, t, t), F32),
               pltpu.VMEM((n_streams, n_blocks, 1), F32),
               pltpu.VMEM((nb, 1, LANES), F32)]
    if moba:
        scratch += [pltpu.VMEM((nb, n_blocks, LANES), F32),
                    pltpu.VMEM((n_streams, n_blocks, t), F32)]
        in_specs.append(pl.BlockSpec((1, s, LANES), lambda bg, hp, qi: (0, 0, hp)))
    else:
        in_specs.append(pl.BlockSpec((nb, s, LANES), lambda bg, hp, qi: (bg, 0, hp)))
    operands.append(key_bias3)
    return pl.pallas_call(
        functools.partial(_flash_kernel, moba=moba, nb=nb, t=t),
        grid=(b // nb, N_HEAD_BLOCKS, s // q_rows),
        in_specs=in_specs,
        out_specs=out_spec,
        out_shape=jax.ShapeDtypeStruct((b, s, BRANCH_WIDTH), BF16),
        scratch_shapes=scratch,
        compiler_params=_cparams(("parallel", "parallel", "arbitrary")),
        name="moba_attn" if moba else "forgetting_attn",
    )(*operands)


def _mix_kernel(x_ref, oa_ref, ob_ref, oc_ref, g_ref, wb_ref, wo_ref, y_ref):
    d = x_ref.shape[1]
    mixed = None
    for n, o_ref in enumerate((oa_ref, ob_ref, oc_ref)):
        gate = jax.nn.sigmoid(g_ref[:, n * d:(n + 1) * d].astype(F32))
        term = gate * _dot(o_ref[...], wb_ref[n])
        mixed = term if mixed is None else mixed + term
    y_ref[...] = x_ref[...] + _dot(mixed.astype(BF16), wo_ref[...])


def _mix(x, o_a, o_b, o_c, proj, wb, wo, layer, tm):
    n, d = x.shape
    o_spec = pl.BlockSpec((tm, BRANCH_WIDTH), lambda i: (i, 0))
    return pl.pallas_call(
        _mix_kernel,
        grid=(n // tm,),
        in_specs=[
            pl.BlockSpec((tm, d), lambda i: (i, 0)),
            o_spec, o_spec, o_spec,
            pl.BlockSpec((tm, N_BRANCH * d), lambda i: (i, 0)),
            pl.BlockSpec((None, N_BRANCH, BRANCH_WIDTH, d), lambda i: (layer, 0, 0, 0)),
            pl.BlockSpec((None, d, d), lambda i: (layer, 0, 0)),
        ],
        out_specs=pl.BlockSpec((tm, d), lambda i: (i, 0)),
        out_shape=jax.ShapeDtypeStruct((n, d), F32),
        compiler_params=_cparams(("parallel",)),
        name="gated_mix_out_proj",
    )(x, o_a, o_b, o_c, proj, wb, wo)


def _mlp_kernel(x_ref, g_ref, wu_ref, wd_ref, gf_ref, y_ref, h_ref, acc_ref, *, final_norm):
    f = pl.program_id(1)

    @pl.when(f == 0)
    def _():
        h_ref[...] = _rms(x_ref[...], g_ref[...]).astype(BF16)
        acc_ref[...] = jnp.zeros_like(acc_ref)

    hid = jnp.square(jnp.maximum(_dot(h_ref[...], wu_ref[...]), 0.0))
    acc_ref[...] += _dot(hid.astype(BF16), wd_ref[...])

    @pl.when(f == pl.num_programs(1) - 1)
    def _():
        y = x_ref[...] + acc_ref[...]
        if final_norm:
            y = _rms(y, gf_ref[...])
        y_ref[...] = y


def _mlp(x, g, wu, wd, g_final, layer, final_norm, tm, tf):
    n, d = x.shape
    d_ff = wu.shape[2]
    return pl.pallas_call(
        functools.partial(_mlp_kernel, final_norm=final_norm),
        grid=(n // tm, d_ff // tf),
        in_specs=[
            pl.BlockSpec((tm, d), lambda i, f: (i, 0)),
            pl.BlockSpec((1, d), lambda i, f: (0, 0)),
            pl.BlockSpec((None, d, tf), lambda i, f: (layer, 0, f)),
            pl.BlockSpec((None, tf, d), lambda i, f: (layer, f, 0)),
            pl.BlockSpec((1, d), lambda i, f: (0, 0)),
        ],
        out_specs=pl.BlockSpec((tm, d), lambda i, f: (i, 0)),
        out_shape=jax.ShapeDtypeStruct((n, d), F32),
        scratch_shapes=[pltpu.VMEM((tm, d), BF16), pltpu.VMEM((tm, d), F32)],
        compiler_params=_cparams(("parallel", "arbitrary")),
        name="relu2_mlp",
    )(x, g, wu, wd, g_final)


REPACK_ROWS = 512


def _repack_kernel(w_ref, wf_ref, main_ref, f_ref):
    i = pl.program_id(1)
    q_chunks = BRANCH_WIDTH // REPACK_ROWS
    n_gate_chunks = N_BRANCH * w_ref.shape[2] // REPACK_ROWS
    mixer_chunk = jnp.maximum(i - n_gate_chunks, 0) % (3 * q_chunks)
    is_q = (i >= n_gate_chunks) & (mixer_chunk < q_chunks)
    scale = jnp.where(is_q, HEAD_DIM ** -0.5 * LOG2E, 1.0)
    main_ref[...] = (w_ref[0] * scale).astype(BF16)

    @pl.when(i == 0)
    def _():
        pad = jnp.zeros((f_ref.shape[0] - N_HEADS, f_ref.shape[1]), F32)
        f_ref[...] = jnp.concatenate([wf_ref[0], pad], axis=0).astype(BF16)


def _prep_in_proj(w_in):
    depth, d, cols = w_in.shape
    w_t = jnp.swapaxes(w_in, 1, 2)
    bw = BRANCH_WIDTH
    r = REPACK_ROWS
    f_lo = 6 * bw
    c_lo = f_lo + N_HEADS
    g_lo = c_lo + 3 * bw
    main_rows = cols - N_HEADS
    n_gate = (cols - g_lo) // r
    n_ab = f_lo // r

    def src_row(l, i):
        row = jnp.where(i < n_gate, g_lo + r * i,
                        jnp.where(i < n_gate + n_ab, r * (i - n_gate), c_lo + r * (i - n_gate - n_ab)))
        return l, pl.multiple_of(row, N_HEADS), 0

    return pl.pallas_call(
        _repack_kernel,
        grid=(depth, main_rows // r),
        in_specs=[pl.BlockSpec((pl.Element(1), pl.Element(r), pl.Element(d)), src_row),
                  pl.BlockSpec((pl.Element(1), pl.Element(N_HEADS), pl.Element(d)), lambda l, i: (l, f_lo, 0))],
        out_specs=[pl.BlockSpec((None, r, d), lambda l, i: (l, i, 0)),
                   pl.BlockSpec((None, LANES, d), lambda l, i: (l, 0, 0))],
        out_shape=[jax.ShapeDtypeStruct((depth, main_rows, d), BF16),
                   jax.ShapeDtypeStruct((depth, LANES, d), BF16)],
        compiler_params=_cparams(("parallel", "arbitrary")),
        name="repack_w_in",
    )(w_t, w_t)


def kernel(x, norm_mix, w_in, b_forget, w_branch, w_out, norm_mlp, w_up, w_down, norm_final):
    b, s, d = x.shape
    depth = w_in.shape[0]
    n = b * s
    assert s % ATT_TILE == 0 and d % LANES == 0

    w_main, w_f = _prep_in_proj(w_in)
    wb = w_branch.astype(BF16)
    wo = w_out.astype(BF16)
    wu = w_up.astype(BF16)
    wd = w_down.astype(BF16)
    b_f = jnp.pad(b_forget, ((0, 0), (0, LANES - N_HEADS)))[:, None, :]

    alibi_bias = _alibi_key_bias(s)
    tm = min(1024, s)
    xf = x.reshape(n, d)
    for l in range(depth):
        g_mix = norm_mix[l][None, :]
        proj, key_bias = _in_proj(xf, g_mix, w_main, w_f, b_f[l], l, s, tm, 3840, 256)
        proj3 = proj.reshape(b, s, -1)
        o_a = _sb_attention(proj3, d).reshape(n, -1)
        o_b = _flash_attention(proj3, key_bias.reshape(b, s, -1), d, False).reshape(n, -1)
        o_c = _flash_attention(proj3, alibi_bias, d, True).reshape(n, -1)
        xf = _mix(xf, o_a, o_b, o_c, proj, wb, wo, l, min(1024, n))
        xf = _mlp(xf, norm_mlp[l][None, :], wu, wd, norm_final[None, :],
                  l, l == depth - 1, min(1024, n), 1024)
    return xf.reshape(b, s, d)
```
